```python
import jax, jax.numpy as jnp
from jax import lax
import numpy as np

D_MODEL = 1024
BATCH = 8
SEQ = 2048
DEPTH = 1
DEC_BATCH = 16
DEC_SEQ = 32
PAST_LEN = 4096

CHUNK = 64
EPS = 1e-6
N_MOD = 6
RWKV_HEAD_DIM = 64
RWKV_DIM = D_MODEL
RWKV_HEADS = RWKV_DIM // RWKV_HEAD_DIM
DECAY_LORA = 64
AAA_LORA = 64
GATE_LORA = 128
RWKV_PROJ = 3 * RWKV_DIM + DECAY_LORA + AAA_LORA + GATE_LORA
LN_X_EPS = 64e-5
GLA_HEADS = 4
GLA_DK_TOTAL = D_MODEL // 2
GLA_DV_TOTAL = D_MODEL
GLA_DK = GLA_DK_TOTAL // GLA_HEADS
GLA_DV = GLA_DV_TOTAL // GLA_HEADS
GLA_GATE_RANK = 16
GLA_GATE_NORM = 16.0
GLA_PROJ = 2 * GLA_DK_TOTAL + 2 * GLA_DV_TOTAL + GLA_GATE_RANK
GATE_PROJ = 2 * D_MODEL
IN_PROJ = RWKV_PROJ + GLA_PROJ + GATE_PROJ
N_GROUPS = 4
EXPERTS_PER_GROUP = 8
N_EXPERTS = N_GROUPS * EXPERTS_PER_GROUP
TOP_K = 2
EXPERT_FF = 512

kernel_name = 'hybrid_rwkv7_gla_hmoe_stream_step'


def _rmsnorm(x, g):
    x32 = x.astype(jnp.float32)
    y = x32 * lax.rsqrt(jnp.mean(x32 * x32, axis=-1, keepdims=True) + EPS)
    return (y * g.astype(jnp.float32)).astype(x.dtype)


def _rwkv7(xr, s0, w0, decay_up, a0, aaa_up, gate_up, k_k, k_a, r_k, ln_g, ln_b):
    B, T, _ = xr.shape
    f32 = jnp.float32
    o1, o2, o3 = RWKV_DIM, 2 * RWKV_DIM, 3 * RWKV_DIM
    o4, o5 = o3 + DECAY_LORA, o3 + DECAY_LORA + AAA_LORA
    x32 = xr.astype(f32)
    r, k, v = x32[..., :o1], x32[..., o1:o2], x32[..., o2:o3]
    xw, xa, xg = x32[..., o3:o4], x32[..., o4:o5], x32[..., o5:]
    w_log = -jax.nn.softplus(-(w0.astype(f32) + jnp.tanh(xw) @ decay_up.astype(f32))) - 0.5
    decay = jnp.exp(-jnp.exp(w_log))
    a = jax.nn.sigmoid(a0.astype(f32) + xa @ aaa_up.astype(f32))
    g = jax.nn.sigmoid(xg) @ gate_up.astype(f32)
    hd = lambda t: t.reshape(B, T, RWKV_HEADS, RWKV_HEAD_DIM)
    kk = hd(k * k_k.astype(f32))
    kk = kk * lax.rsqrt(jnp.maximum(jnp.sum(kk * kk, axis=-1, keepdims=True), 1e-24))
    k = k * (1.0 + (a - 1.0) * k_a.astype(f32))
    r_h, k_h, v_h, w_h, a_h = hd(r), hd(k), hd(v), hd(decay), hd(a)
    tm = lambda t: jnp.transpose(t, (1, 0, 2, 3))

    def step(S, inp):
        r_t, k_t, v_t, w_t, kk_t, a_t = inp
        sa = jnp.einsum('bhij,bhj->bhi', S, -kk_t)
        S = S * w_t[:, :, None, :] + sa[..., None] * (kk_t * a_t)[:, :, None, :] + v_t[..., None] * k_t[:, :, None, :]
        return S, jnp.einsum('bhij,bhj->bhi', S, r_t)

    s_new, ys = lax.scan(step, s0.astype(f32), (tm(r_h), tm(k_h), tm(v_h), tm(w_h), tm(kk), tm(a_h)))
    y = jnp.transpose(ys, (1, 0, 2, 3))
    mu = jnp.mean(y, axis=-1, keepdims=True)
    var = jnp.mean(jnp.square(y - mu), axis=-1, keepdims=True)
    yn = ((y - mu) * lax.rsqrt(var + LN_X_EPS)).reshape(B, T, RWKV_DIM) * ln_g.astype(f32) + ln_b.astype(f32)
    bonus = (jnp.sum(r_h * k_h * r_k.astype(f32), axis=-1, keepdims=True) * v_h).reshape(B, T, RWKV_DIM)
    out = (yn + bonus) * g
    return out.astype(xr.dtype), s_new.astype(s0.dtype)


def _gla(xg, s0, alpha_up, alpha_b, norm_g):
    B, T, _ = xg.shape
    f32 = jnp.float32
    o1, o2, o3, o4 = GLA_DK_TOTAL, 2 * GLA_DK_TOTAL, 2 * GLA_DK_TOTAL + GLA_DV_TOTAL, 2 * GLA_DK_TOTAL + GLA_DV_TOTAL + GLA_GATE_RANK
    x32 = xg.astype(f32)
    q = x32[..., :o1] * GLA_DK ** -0.5
    k, v, xa, r = x32[..., o1:o2], x32[..., o2:o3], x32[..., o3:o4], x32[..., o4:]
    log_alpha = jax.nn.log_sigmoid(xa @ alpha_up.astype(f32) + alpha_b.astype(f32)) / GLA_GATE_NORM
    C = min(CHUNK, T)
    n = T // C

    def to_chunks(t):
        return t.reshape(B, n, C, GLA_HEADS, -1).transpose(1, 0, 3, 2, 4)

    mask = jnp.tril(jnp.ones((C, C), dtype=bool))[None, None, :, :, None]

    def chunk_step(S, inp):
        q_c, k_c, v_c, g_c = inp
        b = jnp.cumsum(g_c, axis=2)
        dmat = jnp.exp(jnp.where(mask, b[:, :, :, None, :] - b[:, :, None, :, :], -jnp.inf))
        att = jnp.einsum('bhtd,bhtsd,bhsd->bhts', q_c, dmat, k_c)
        o = jnp.einsum('bhts,bhsv->bhtv', att, v_c) + jnp.einsum('bhtd,bhdv->bhtv', q_c * jnp.exp(b), S)
        b_last = b[:, :, -1:, :]
        S = jnp.exp(b_last[:, :, 0, :])[..., None] * S + jnp.einsum('bhsd,bhsv->bhdv', k_c * jnp.exp(b_last - b), v_c)
        return S, o

    s_new, o = lax.scan(chunk_step, s0.astype(f32), (to_chunks(q), to_chunks(k), to_chunks(v), to_chunks(log_alpha)))
    o = o.transpose(1, 0, 3, 2, 4).reshape(B, T, GLA_HEADS, GLA_DV)
    o = o * lax.rsqrt(jnp.mean(o * o, axis=-1, keepdims=True) + EPS) * norm_g.astype(f32)
    out = o.reshape(B, T, GLA_DV_TOTAL) * jax.nn.silu(r)
    return out.astype(xg.dtype), s_new.astype(s0.dtype)


def _hier_moe(h, router_group, router_expert, e_gate, e_up, e_down):
    B, T, D = h.shape
    ht = h.reshape(B * T, D)
    f32 = jnp.float32
    group_logits = (ht @ router_group).astype(f32)
    p_group = jax.nn.softmax(group_logits, axis=-1)
    _, g_idx = lax.top_k(group_logits, 1)
    p_g = jnp.take_along_axis(p_group, g_idx, axis=-1)
    expert_logits = (ht @ router_expert).astype(f32).reshape(-1, N_GROUPS, EXPERTS_PER_GROUP)
    within = jnp.take_along_axis(expert_logits, g_idx[:, :, None], axis=1)[:, 0]
    top_vals, top_idx = lax.top_k(within, TOP_K)
    w_top = jax.nn.softmax(top_vals, axis=-1) * p_g
    expert_ids = g_idx * EXPERTS_PER_GROUP + top_idx
    combine = jnp.sum(jax.nn.one_hot(expert_ids, N_EXPERTS, dtype=f32) * w_top[..., None], axis=1).astype(h.dtype)
    out = jnp.zeros_like(ht)
    for e in range(N_EXPERTS):
        hid = jax.nn.silu(ht @ e_gate[e]) * (ht @ e_up[e])
        out = out + combine[:, e:e + 1] * (hid @ e_down[e])
    return out.reshape(B, T, D)


def _layer(x, c, shift_prev, s_rwkv, s_gla, lw):
    (w_ada, b_ada, norm1_g, w_in, rwkv_mu, rwkv_w0, rwkv_decay_up, rwkv_a0, rwkv_aaa_up, rwkv_gate_up,
     rwkv_k_k, rwkv_k_a, rwkv_r_k, rwkv_ln_g, rwkv_ln_b, gla_alpha_up, gla_alpha_b, gla_norm_g,
     w_branch_rwkv, w_branch_gla, w_out, norm2_g, router_group, router_expert,
     expert_w_gate, expert_w_up, expert_w_down) = lw
    mod = (jax.nn.silu(c) @ w_ada + b_ada)[:, None, :]
    sh1, sc1, gt1, sh2, sc2, gt2 = jnp.split(mod, N_MOD, axis=-1)
    h = _rmsnorm(x, norm1_g) * (1.0 + sc1) + sh1
    proj = h @ w_in
    p_rwkv = proj[..., :RWKV_PROJ]
    p_gla = proj[..., RWKV_PROJ:RWKV_PROJ + GLA_PROJ]
    gate_r, gate_g = jnp.split(jax.nn.sigmoid(proj[..., RWKV_PROJ + GLA_PROJ:]), 2, axis=-1)
    prev = jnp.concatenate([shift_prev, p_rwkv[:, :-1]], axis=1)
    x_rwkv = p_rwkv + (prev - p_rwkv) * rwkv_mu
    y_r, s_rwkv_new = _rwkv7(x_rwkv, s_rwkv, rwkv_w0, rwkv_decay_up, rwkv_a0, rwkv_aaa_up, rwkv_gate_up,
                             rwkv_k_k, rwkv_k_a, rwkv_r_k, rwkv_ln_g, rwkv_ln_b)
    y_g, s_gla_new = _gla(p_gla, s_gla, gla_alpha_up, gla_alpha_b, gla_norm_g)
    merged = gate_r * (y_r @ w_branch_rwkv) + gate_g * (y_g @ w_branch_gla)
    x = x + gt1 * (merged @ w_out)
    h2 = _rmsnorm(x, norm2_g) * (1.0 + sc2) + sh2
    x = x + gt2 * _hier_moe(h2, router_group, router_expert, expert_w_gate, expert_w_up, expert_w_down)
    return x, p_rwkv[:, -1:], s_rwkv_new, s_gla_new


def _trunk(x, c, shift0, srwkv0, sgla0, weights, final_norm_g):
    shifts, srs, sgs = [], [], []
    for l in range(DEPTH):
        lw = tuple(w[l] for w in weights)
        x, sh, sr, sg = _layer(x, c, shift0[l], srwkv0[l], sgla0[l], lw)
        shifts.append(sh)
        srs.append(sr)
        sgs.append(sg)
    return _rmsnorm(x, final_norm_g), jnp.stack(shifts), jnp.stack(srs), jnp.stack(sgs)


def setup_inputs(seed: int = 0) -> dict:
    key = jax.random.key(seed)
    ks = iter(jax.random.split(key, 40))
    nrm = lambda shape, s: jax.random.normal(next(ks), shape, jnp.float32) * s
    L = DEPTH
    d = {}
    d['x_prompt'] = nrm((BATCH, SEQ, D_MODEL), 1.0)
    d['x_sample'] = nrm((DEC_BATCH, DEC_SEQ, D_MODEL), 1.0)
    d['c_prompt'] = nrm((BATCH, D_MODEL), 1.0)
    d['c_sample'] = nrm((DEC_BATCH, D_MODEL), 1.0)
    d['state_rwkv_shift'] = nrm((L, DEC_BATCH, 1, RWKV_PROJ), 1.0)
    d['state_rwkv'] = nrm((L, DEC_BATCH, RWKV_HEADS, RWKV_HEAD_DIM, RWKV_HEAD_DIM), 0.5)
    d['state_gla'] = nrm((L, DEC_BATCH, GLA_HEADS, GLA_DK, GLA_DV), 0.5)
    d['w_ada'] = nrm((L, D_MODEL, N_MOD * D_MODEL), 0.5 * D_MODEL ** -0.5)
    d['b_ada'] = nrm((L, N_MOD * D_MODEL), 0.01)
    d['norm1_g'] = 1.0 + nrm((L, D_MODEL), 0.02)
    d['w_in'] = nrm((L, D_MODEL, IN_PROJ), D_MODEL ** -0.5)
    d['rwkv_mu'] = jax.random.uniform(next(ks), (L, RWKV_PROJ), jnp.float32)
    d['rwkv_w0'] = nrm((L, RWKV_DIM), 0.5)
    d['rwkv_decay_up'] = nrm((L, DECAY_LORA, RWKV_DIM), 0.3 * DECAY_LORA ** -0.5)
    d['rwkv_a0'] = nrm((L, RWKV_DIM), 0.1)
    d['rwkv_aaa_up'] = nrm((L, AAA_LORA, RWKV_DIM), 0.3 * AAA_LORA ** -0.5)
    d['rwkv_gate_up'] = nrm((L, GATE_LORA, RWKV_DIM), GATE_LORA ** -0.5)
    d['rwkv_k_k'] = 0.85 + nrm((L, RWKV_DIM), 0.02)
    d['rwkv_k_a'] = 1.0 + nrm((L, RWKV_DIM), 0.02)
    d['rwkv_r_k'] = nrm((L, RWKV_HEADS, RWKV_HEAD_DIM), 0.1)
    d['rwkv_ln_g'] = 1.0 + nrm((L, RWKV_DIM), 0.02)
    d['rwkv_ln_b'] = nrm((L, RWKV_DIM), 0.01)
    d['gla_alpha_up'] = nrm((L, GLA_GATE_RANK, GLA_DK_TOTAL), GLA_GATE_RANK ** -0.5)
    d['gla_alpha_b'] = nrm((L, GLA_DK_TOTAL), 0.1)
    d['gla_norm_g'] = 1.0 + nrm((L, GLA_DV), 0.02)
    d['w_branch_rwkv'] = nrm((L, RWKV_DIM, D_MODEL), RWKV_DIM ** -0.5)
    d['w_branch_gla'] = nrm((L, GLA_DV_TOTAL, D_MODEL), GLA_DV_TOTAL ** -0.5)
    d['w_out'] = nrm((L, D_MODEL, D_MODEL), D_MODEL ** -0.5)
    d['norm2_g'] = 1.0 + nrm((L, D_MODEL), 0.02)
    d['router_group'] = nrm((L, D_MODEL, N_GROUPS), D_MODEL ** -0.5)
    d['router_expert'] = nrm((L, D_MODEL, N_EXPERTS), D_MODEL ** -0.5)
    d['expert_w_gate'] = nrm((L, N_EXPERTS, D_MODEL, EXPERT_FF), D_MODEL ** -0.5)
    d['expert_w_up'] = nrm((L, N_EXPERTS, D_MODEL, EXPERT_FF), D_MODEL ** -0.5)
    d['expert_w_down'] = nrm((L, N_EXPERTS, EXPERT_FF, D_MODEL), EXPERT_FF ** -0.5)
    d['final_norm_g'] = 1.0 + nrm((D_MODEL,), 0.02)
    return d


def reference(x_prompt, x_sample, c_prompt, c_sample, state_rwkv_shift, state_rwkv, state_gla,
              w_ada, b_ada, norm1_g, w_in, rwkv_mu, rwkv_w0, rwkv_decay_up, rwkv_a0, rwkv_aaa_up,
              rwkv_gate_up, rwkv_k_k, rwkv_k_a, rwkv_r_k, rwkv_ln_g, rwkv_ln_b, gla_alpha_up, gla_alpha_b,
              gla_norm_g, w_branch_rwkv, w_branch_gla, w_out, norm2_g, router_group, router_expert,
              expert_w_gate, expert_w_up, expert_w_down, final_norm_g):
    weights = (w_ada, b_ada, norm1_g, w_in, rwkv_mu, rwkv_w0, rwkv_decay_up, rwkv_a0, rwkv_aaa_up,
               rwkv_gate_up, rwkv_k_k, rwkv_k_a, rwkv_r_k, rwkv_ln_g, rwkv_ln_b, gla_alpha_up, gla_alpha_b,
               gla_norm_g, w_branch_rwkv, w_branch_gla, w_out, norm2_g, router_group, router_expert,
               expert_w_gate, expert_w_up, expert_w_down)
    Bp = x_prompt.shape[0]
    dt = x_prompt.dtype
    shift0 = jnp.zeros((DEPTH, Bp, 1, RWKV_PROJ), dt)
    srwkv0 = jnp.zeros((DEPTH, Bp, RWKV_HEADS, RWKV_HEAD_DIM, RWKV_HEAD_DIM), state_rwkv.dtype)
    sgla0 = jnp.zeros((DEPTH, Bp, GLA_HEADS, GLA_DK, GLA_DV), state_gla.dtype)
    y_prompt, p_shift, p_rwkv, p_gla = _trunk(x_prompt, c_prompt, shift0, srwkv0, sgla0, weights, final_norm_g)
    y_sample, s_shift, s_rwkv, s_gla = _trunk(x_sample, c_sample, state_rwkv_shift, state_rwkv, state_gla, weights, final_norm_g)
    return (y_prompt, y_sample, p_shift, p_rwkv, p_gla, s_shift, s_rwkv, s_gla)
```

```python
import functools
import math

import jax
import jax.numpy as jnp
from jax import lax
from jax.experimental import pallas as pl
from jax.experimental.pallas import tpu as pltpu

F32 = jnp.float32
BF16 = jnp.bfloat16

D_MODEL = 1024
N_MOD = 6
EPS = 1e-6
RWKV_HEAD_DIM = 64
RWKV_HEADS = 16
RWKV_PAIRS = RWKV_HEADS // 2
DECAY_LORA = 64
AAA_LORA = 64
GATE_LORA = 128
RWKV_PROJ = 3 * D_MODEL + DECAY_LORA + AAA_LORA + GATE_LORA
LN_X_EPS = 64e-5
GLA_HEADS = 4
GLA_DK = 128
GLA_DV = 256
GLA_DK_TOTAL = GLA_HEADS * GLA_DK
GLA_DV_TOTAL = GLA_HEADS * GLA_DV
GLA_GATE_RANK = 16
GLA_GATE_NORM = 16.0
GLA_PROJ = 2 * GLA_DK_TOTAL + 2 * GLA_DV_TOTAL + GLA_GATE_RANK
LANES = 128
GLA_PROJ_PAD = 2 * GLA_DK_TOTAL + 2 * GLA_DV_TOTAL + LANES
GATE_PROJ = 2 * D_MODEL
N_GROUPS = 4
EXPERTS_PER_GROUP = 8
N_EXPERTS = 32
EXPERT_FF = 512

CHUNK = 64
VMEM_LIMIT = 56 * 1024 * 1024


def _bdot(a, b):
    return jnp.dot(a.astype(BF16), b.astype(BF16), preferred_element_type=F32)


def _bdot_nt(a, b):
    return lax.dot_general(a.astype(BF16), b.astype(BF16), (((1,), (1,)), ((), ())),
                           preferred_element_type=F32)


def _bdot_tn(a, b):
    return jnp.dot(a.T.astype(BF16), b.astype(BF16), preferred_element_type=F32)


def _split3(x):
    h1 = x.astype(BF16)
    r1 = x - h1.astype(F32)
    h2 = r1.astype(BF16)
    h3 = (r1 - h2.astype(F32)).astype(BF16)
    return h1, h2, h3


def _dot3(a, b):
    a1, a2, _ = _split3(a)
    b1, b2, _ = _split3(b)
    return (jnp.dot(a1, b1, preferred_element_type=F32)
            + jnp.dot(a2, b1, preferred_element_type=F32)
            + jnp.dot(a1, b2, preferred_element_type=F32))


def _dot_exact_lhs(a_bf16, x):
    x1, x2, x3 = _split3(x)
    return (jnp.dot(a_bf16, x1, preferred_element_type=F32)
            + jnp.dot(a_bf16, x2, preferred_element_type=F32)
            + jnp.dot(a_bf16, x3, preferred_element_type=F32))


def _dot_exact_rhs(x, b_bf16):
    x1, x2, _ = _split3(x)
    return (jnp.dot(x1, b_bf16, preferred_element_type=F32)
            + jnp.dot(x2, b_bf16, preferred_element_type=F32))


def _sigmoid(x):
    return 1.0 / (1.0 + jnp.exp(-x))


def _silu(x):
    return x * _sigmoid(x)


def _rms_scale(x):
    return x * lax.rsqrt(jnp.mean(x * x, axis=-1, keepdims=True) + EPS)


def _ada_kernel(c_ref, w_ref, b_ref, o_ref):
    o_ref[...] = _dot3(_silu(c_ref[...]), w_ref[...]) + b_ref[...]


def _ada(c, w_ada, b_ada):
    n_rows = c.shape[0]
    n_out = w_ada.shape[1]
    tn = 1536
    return pl.pallas_call(
        _ada_kernel,
        out_shape=jax.ShapeDtypeStruct((n_rows, n_out), F32),
        grid=(n_out // tn,),
        in_specs=[pl.BlockSpec((n_rows, D_MODEL), lambda j: (0, 0)),
                  pl.BlockSpec((D_MODEL, tn), lambda j: (0, j)),
                  pl.BlockSpec((1, tn), lambda j: (0, j))],
        out_specs=pl.BlockSpec((n_rows, tn), lambda j: (0, j)),
        compiler_params=pltpu.CompilerParams(dimension_semantics=("parallel",),
                                             vmem_limit_bytes=VMEM_LIMIT),
        name="ada",
    )(c, w_ada, b_ada.reshape(1, n_out))


def _norm_proj_kernel(x_ref, mod_ref, g_ref, w_ref, o_ref, *, gate):
    mod = mod_ref[0]
    h = _rms_scale(x_ref[0]) * g_ref[...] * (1.0 + mod[1:2]) + mod[0:1]
    out = _bdot(h, w_ref[...])
    o_ref[0] = _sigmoid(out) if gate else out


def _norm_proj(x, mod, g, w_bf16, *, gate, tm):
    b, t, _ = x.shape
    n = w_bf16.shape[1]
    return pl.pallas_call(
        functools.partial(_norm_proj_kernel, gate=gate),
        out_shape=jax.ShapeDtypeStruct((b, t, n), F32),
        grid=(b, t // tm),
        in_specs=[pl.BlockSpec((1, tm, D_MODEL), lambda i, j: (i, j, 0)),
                  pl.BlockSpec((1, N_MOD, D_MODEL), lambda i, j: (i, 0, 0)),
                  pl.BlockSpec((1, D_MODEL), lambda i, j: (0, 0)),
                  pl.BlockSpec((D_MODEL, n), lambda i, j: (0, 0))],
        out_specs=pl.BlockSpec((1, tm, n), lambda i, j: (i, j, 0)),
        compiler_params=pltpu.CompilerParams(dimension_semantics=("parallel", "parallel"),
                                             vmem_limit_bytes=VMEM_LIMIT),
        name="norm_proj_gate" if gate else "norm_proj",
    )(x, mod, g, w_bf16)


def _blockdiag(x, lo_mask):
    return jnp.concatenate([jnp.where(lo_mask, x, 0.0), jnp.where(lo_mask, 0.0, x)], axis=0)


def _rwkv_kernel(p_ref, shift_ref, s0_ref, mu_ref, w0_ref, dup_ref, a0_ref, aup_ref, gup_ref,
                 kk_ref, ka_ref, rk_ref, lng_ref, lnb_ref, y_ref, sout_ref, s_scr, carry_scr,
                 *, t_valid):
    c = pl.program_id(1)
    n_chunks = pl.num_programs(1)
    C = CHUNK
    N = RWKV_HEAD_DIM

    @pl.when(c == 0)
    def _():
        s_scr[...] = s0_ref[0]
        carry_scr[...] = shift_ref[0]

    p = p_ref[0]
    row_w = lax.broadcasted_iota(jnp.int32, (C, RWKV_PROJ), 0)
    prev = jnp.where(row_w == 0, carry_scr[...], pltpu.roll(p, 1, axis=0))
    carry_scr[...] = p[C - 1:C, :]
    x = p + (prev - p) * mu_ref[...]

    o3 = 3 * D_MODEL
    xw = x[:, o3:o3 + DECAY_LORA]
    xa = x[:, o3 + DECAY_LORA:o3 + DECAY_LORA + AAA_LORA]
    xg = x[:, o3 + DECAY_LORA + AAA_LORA:]
    z = w0_ref[...] + _dot3(jnp.tanh(xw), dup_ref[...])
    logw_all = -math.exp(-0.5) * _sigmoid(z)
    asig_all = _sigmoid(a0_ref[...] + _dot3(xa, aup_ref[...]))
    g_all = _dot3(_sigmoid(xg), gup_ref[...])

    row = lax.broadcasted_iota(jnp.int32, (C, LANES), 0)
    lane = lax.broadcasted_iota(jnp.int32, (C, LANES), 1)
    lane_t = jnp.where(lane < N, lane, lane - N)
    lo = lane < N
    strict = lane_t < row
    incl = lane_t <= row
    eye = (lane_t == row).astype(F32)
    valid = row < t_valid
    tri = (lax.broadcasted_iota(jnp.int32, (C, C), 1)
           <= lax.broadcasted_iota(jnp.int32, (C, C), 0)).astype(BF16)
    sq_r = lax.broadcasted_iota(jnp.int32, (LANES, LANES), 0)
    sq_c = lax.broadcasted_iota(jnp.int32, (LANES, LANES), 1)
    same_head = (sq_r < N) == (sq_c < N)
    head_ones = same_head.astype(BF16)
    diag = sq_r == sq_c

    def hsum(v):
        return _dot_exact_rhs(v, head_ones)

    for pr in range(RWKV_PAIRS):
        cs = slice(pr * LANES, (pr + 1) * LANES)
        r = x[:, cs]
        k = x[:, D_MODEL + pr * LANES:D_MODEL + (pr + 1) * LANES]
        v = x[:, 2 * D_MODEL + pr * LANES:2 * D_MODEL + (pr + 1) * LANES]
        logw = logw_all[:, cs]
        asig = asig_all[:, cs]
        kk = k * kk_ref[:, cs]
        kk = kk * lax.rsqrt(jnp.maximum(hsum(kk * kk), 1e-24))
        k2 = k * (1.0 + (asig - 1.0) * ka_ref[:, cs])
        bonus = hsum(r * k2 * rk_ref[:, cs]) * v
        if t_valid < C:
            logw = jnp.where(valid, logw, 0.0)
            kk = jnp.where(valid, kk, 0.0)
            k2 = jnp.where(valid, k2, 0.0)
            v = jnp.where(valid, v, 0.0)
            r = jnp.where(valid, r, 0.0)
        a_vec = -kk
        b_vec = kk * asig

        cum = _dot_exact_lhs(tri, logw)
        clast = cum[C - 1:C, :]
        e_inv = jnp.exp(-cum)
        e_tail = jnp.exp(clast - cum)
        aq = a_vec * jnp.exp(cum - logw)
        rq = r * jnp.exp(cum)
        bk = b_vec * e_inv
        kx = k2 * e_inv
        bb = b_vec * e_tail
        kb = k2 * e_tail

        g4 = _bdot_nt(jnp.concatenate([aq, rq], axis=0),
                      jnp.concatenate([_blockdiag(bk, lo), _blockdiag(kx, lo)], axis=0))
        m_ab = jnp.where(strict, g4[:C, :LANES], 0.0)
        m_ak = jnp.where(strict, g4[:C, LANES:], 0.0)
        m_rb = jnp.where(incl, g4[C:, :LANES], 0.0)
        m_rk = jnp.where(incl, g4[C:, LANES:], 0.0)

        t_inv = eye + m_ab
        l_pow = _bdot(m_ab, _blockdiag(m_ab, lo))
        n_steps = C.bit_length() - 1
        for step in range(1, n_steps):
            if step < n_steps - 1:
                both = _bdot(l_pow, jnp.concatenate([_blockdiag(t_inv, lo), _blockdiag(l_pow, lo)], axis=1))
                t_inv = t_inv + both[:, :LANES]
                l_pow = both[:, LANES:]
            else:
                t_inv = t_inv + _bdot(l_pow, _blockdiag(t_inv, lo))

        mv = _bdot(jnp.concatenate([m_ak, m_rk], axis=0), _blockdiag(v, lo))
        au = _bdot(t_inv, jnp.concatenate([_blockdiag(aq, lo), _blockdiag(mv[:C], lo)], axis=1))
        a_bar = au[:, :LANES]
        u_bar = au[:, LANES:]
        mau = _bdot(m_rb, jnp.concatenate([_blockdiag(a_bar, lo), _blockdiag(u_bar, lo)], axis=1))
        r_til = rq + mau[:, :LANES]
        y_loc = mau[:, LANES:] + mv[C:]

        p_full = _bdot_tn(a_bar, bb)
        p_bd = jnp.where(same_head, p_full, 0.0) + jnp.where(diag, jnp.exp(clast), 0.0)
        q_full = _bdot_tn(jnp.concatenate([u_bar, v], axis=0), jnp.concatenate([bb, kb], axis=0))
        q_pk = jnp.where(lo[:N], q_full[:N], q_full[N:])

        s_old = s_scr[pr]
        y = _bdot_nt(r_til, _blockdiag(s_old, lo[:N])) + y_loc
        s_scr[pr] = _bdot(s_old, p_bd) + q_pk

        mu_y = hsum(y) * (1.0 / N)
        yc = y - mu_y
        var = hsum(yc * yc) * (1.0 / N)
        yn = yc * lax.rsqrt(var + LN_X_EPS) * lng_ref[:, cs] + lnb_ref[:, cs]
        y_ref[0, :, cs] = (yn + bonus) * g_all[:, cs]

    @pl.when(c == n_chunks - 1)
    def _():
        sout_ref[0] = s_scr[...]


def _rwkv(p_rwkv, shift_prev, s_packed, mu, w0, decay_up, a0, aaa_up, gate_up, k_k, k_a, r_k,
          ln_g, ln_b, *, t_valid):
    b, t, _ = p_rwkv.shape
    row = lambda a: a.reshape(1, -1)
    full = lambda shape: pl.BlockSpec(shape, lambda i, j: (0,) * len(shape))
    return pl.pallas_call(
        functools.partial(_rwkv_kernel, t_valid=t_valid),
        out_shape=(jax.ShapeDtypeStruct((b, t, D_MODEL), F32),
                   jax.ShapeDtypeStruct((b, RWKV_PAIRS, RWKV_HEAD_DIM, LANES), F32)),
        grid=(b, t // CHUNK),
        in_specs=[pl.BlockSpec((1, CHUNK, RWKV_PROJ), lambda i, j: (i, j, 0)),
                  pl.BlockSpec((1, 1, RWKV_PROJ), lambda i, j: (i, 0, 0)),
                  pl.BlockSpec((1, RWKV_PAIRS, RWKV_HEAD_DIM, LANES), lambda i, j: (i, 0, 0, 0)),
                  full((1, RWKV_PROJ)), full((1, D_MODEL)), full((DECAY_LORA, D_MODEL)),
                  full((1, D_MODEL)), full((AAA_LORA, D_MODEL)), full((GATE_LORA, D_MODEL)),
                  full((1, D_MODEL)), full((1, D_MODEL)), full((1, D_MODEL)),
                  full((1, D_MODEL)), full((1, D_MODEL))],
        out_specs=(pl.BlockSpec((1, CHUNK, D_MODEL), lambda i, j: (i, j, 0)),
                   pl.BlockSpec((1, RWKV_PAIRS, RWKV_HEAD_DIM, LANES), lambda i, j: (i, 0, 0, 0))),
        scratch_shapes=[pltpu.VMEM((RWKV_PAIRS, RWKV_HEAD_DIM, LANES), F32),
                        pltpu.VMEM((1, RWKV_PROJ), F32)],
        compiler_params=pltpu.CompilerParams(dimension_semantics=("parallel", "arbitrary"),
                                             vmem_limit_bytes=VMEM_LIMIT),
        name="rwkv_scan",
    )(p_rwkv, shift_prev, s_packed, row(mu), row(w0), decay_up, row(a0), aaa_up, gate_up,
      row(k_k), row(k_a), row(r_k), row(ln_g), row(ln_b))


def _gla_kernel(p_ref, s0_ref, aup_ref, ab_ref, ng_ref, y_ref, sout_ref, s_scr, *, t_valid):
    c = pl.program_id(1)
    n_chunks = pl.num_programs(1)
    C = CHUNK

    @pl.when(c == 0)
    def _():
        s_scr[...] = s0_ref[0]

    p = p_ref[0]
    o_k = GLA_DK_TOTAL
    o_v = 2 * GLA_DK_TOTAL
    o_r = o_v + GLA_DV_TOTAL
    o_a = o_r + GLA_DV_TOTAL
    logit = _dot3(p[:, o_a:], aup_ref[...]) + ab_ref[...]
    g = (jnp.minimum(logit, 0.0) - jnp.log(1.0 + jnp.exp(-jnp.abs(logit)))) * (1.0 / GLA_GATE_NORM)
    row = lax.broadcasted_iota(jnp.int32, (C, GLA_DK_TOTAL), 0)
    if t_valid < C:
        g = jnp.where(row < t_valid, g, 0.0)
    tri = (lax.broadcasted_iota(jnp.int32, (C, C), 1)
           <= lax.broadcasted_iota(jnp.int32, (C, C), 0))
    cum = _dot_exact_lhs(tri.astype(BF16), g)
    clast = cum[C - 1:C, :]
    cmid = cum[C // 2 - 1:C // 2, :]
    valid_k = lax.broadcasted_iota(jnp.int32, (C, GLA_DK), 0) < t_valid
    valid_v = lax.broadcasted_iota(jnp.int32, (C, GLA_DV), 0) < t_valid

    for h in range(GLA_HEADS):
        ks = slice(h * GLA_DK, (h + 1) * GLA_DK)
        vs = slice(h * GLA_DV, (h + 1) * GLA_DV)
        q = p[:, ks] * (GLA_DK ** -0.5)
        k = p[:, o_k + h * GLA_DK:o_k + (h + 1) * GLA_DK]
        v = p[:, o_v + h * GLA_DV:o_v + (h + 1) * GLA_DV]
        r = p[:, o_r + h * GLA_DV:o_r + (h + 1) * GLA_DV]
        if t_valid < C:
            k = jnp.where(valid_k, k, 0.0)
            v = jnp.where(valid_v, v, 0.0)
        b = cum[:, ks]
        att = _bdot_nt(q * jnp.exp(b - cmid[:, ks]), k * jnp.exp(cmid[:, ks] - b))
        att = jnp.where(tri, att, 0.0)
        s_t = s_scr[h]
        o = _bdot(att, v) + _bdot_nt(q * jnp.exp(b), s_t)
        s_scr[h] = s_t * jnp.exp(clast[:, ks]) + _bdot_tn(v, k * jnp.exp(clast[:, ks] - b))
        o = o * lax.rsqrt(jnp.mean(o * o, axis=-1, keepdims=True) + EPS) * ng_ref[...]
        y_ref[0, :, vs] = o * _silu(r)

    @pl.when(c == n_chunks - 1)
    def _():
        sout_ref[0] = s_scr[...]


def _gla(p_gla, s_t, alpha_up_pad, alpha_b, norm_g, *, t_valid):
    b, t, _ = p_gla.shape
    full = lambda shape: pl.BlockSpec(shape, lambda i, j: (0,) * len(shape))
    return pl.pallas_call(
        functools.partial(_gla_kernel, t_valid=t_valid),
        out_shape=(jax.ShapeDtypeStruct((b, t, GLA_DV_TOTAL), F32),
                   jax.ShapeDtypeStruct((b, GLA_HEADS, GLA_DV, GLA_DK), F32)),
        grid=(b, t // CHUNK),
        in_specs=[pl.BlockSpec((1, CHUNK, GLA_PROJ_PAD), lambda i, j: (i, j, 0)),
                  pl.BlockSpec((1, GLA_HEADS, GLA_DV, GLA_DK), lambda i, j: (i, 0, 0, 0)),
                  full((LANES, GLA_DK_TOTAL)), full((1, GLA_DK_TOTAL)), full((1, GLA_DV))],
        out_specs=(pl.BlockSpec((1, CHUNK, GLA_DV_TOTAL), lambda i, j: (i, j, 0)),
                   pl.BlockSpec((1, GLA_HEADS, GLA_DV, GLA_DK), lambda i, j: (i, 0, 0, 0))),
        scratch_shapes=[pltpu.VMEM((GLA_HEADS, GLA_DV, GLA_DK), F32)],
        compiler_params=pltpu.CompilerParams(dimension_semantics=("parallel", "arbitrary"),
                                             vmem_limit_bytes=VMEM_LIMIT),
        name="gla_scan",
    )(p_gla, s_t, alpha_up_pad, alpha_b.reshape(1, -1), norm_g.reshape(1, -1))


def _merge_kernel(yr_ref, yg_ref, gate_ref, x_ref, mod_ref, wbr_ref, wbg_ref, wout_ref, g2_ref,
                  router_ref, x1_ref, h2_ref, comb_ref):
    mod = mod_ref[0]
    gates = gate_ref[0]
    merged = (gates[:, :D_MODEL] * _bdot(yr_ref[0], wbr_ref[...])
              + gates[:, D_MODEL:] * _bdot(yg_ref[0], wbg_ref[...]))
    x1 = x_ref[0] + mod[2:3] * _bdot(merged, wout_ref[...])
    x1_ref[0] = x1
    h2 = _rms_scale(x1) * g2_ref[...] * (1.0 + mod[4:5]) + mod[3:4]
    h2_ref[0] = h2.astype(BF16)

    logits = _dot3(h2, router_ref[...])
    tm = logits.shape[0]
    lane = lax.broadcasted_iota(jnp.int32, (tm, LANES), 1)
    neg = -jnp.inf
    is_group = (lane >= N_EXPERTS) & (lane < N_EXPERTS + N_GROUPS)
    gl = jnp.where(is_group, logits, neg)
    gmax = jnp.max(gl, axis=1, keepdims=True)
    g_idx = jnp.min(jnp.where(gl == gmax, lane, LANES), axis=1, keepdims=True) - N_EXPERTS
    p_g = 1.0 / jnp.sum(jnp.exp(gl - gmax), axis=1, keepdims=True)
    in_group = (lane >= g_idx * EXPERTS_PER_GROUP) & (lane < (g_idx + 1) * EXPERTS_PER_GROUP)
    el = jnp.where(in_group, logits, neg)
    v1 = jnp.max(el, axis=1, keepdims=True)
    i1 = jnp.min(jnp.where(el == v1, lane, LANES), axis=1, keepdims=True)
    el2 = jnp.where(lane == i1, neg, el)
    v2 = jnp.max(el2, axis=1, keepdims=True)
    i2 = jnp.min(jnp.where(el2 == v2, lane, LANES), axis=1, keepdims=True)
    e21 = jnp.exp(v2 - v1)
    w1 = p_g / (1.0 + e21)
    comb_ref[0] = jnp.where(lane == i1, w1, 0.0) + jnp.where(lane == i2, w1 * e21, 0.0)


def _merge(y_r, y_g, gates, x, mod, wbr, wbg, wout, norm2_g, router_cat, *, tm):
    b, t, _ = x.shape
    full = lambda shape: pl.BlockSpec(shape, lambda i, j: (0,) * len(shape))
    tile = lambda n: pl.BlockSpec((1, tm, n), lambda i, j: (i, j, 0))
    return pl.pallas_call(
        _merge_kernel,
        out_shape=(jax.ShapeDtypeStruct((b, t, D_MODEL), F32),
                   jax.ShapeDtypeStruct((b, t, D_MODEL), BF16),
                   jax.ShapeDtypeStruct((b, t, LANES), F32)),
        grid=(b, t // tm),
        in_specs=[tile(D_MODEL), tile(D_MODEL), tile(GATE_PROJ), tile(D_MODEL),
                  pl.BlockSpec((1, N_MOD, D_MODEL), lambda i, j: (i, 0, 0)),
                  full((D_MODEL, D_MODEL)), full((D_MODEL, D_MODEL)), full((D_MODEL, D_MODEL)),
                  full((1, D_MODEL)), full((D_MODEL, LANES))],
        out_specs=(tile(D_MODEL), tile(D_MODEL), tile(LANES)),
        compiler_params=pltpu.CompilerParams(dimension_semantics=("parallel", "parallel"),
                                             vmem_limit_bytes=VMEM_LIMIT),
        name="merge_router",
    )(y_r, y_g, gates, x, mod, wbr, wbg, wout, norm2_g.reshape(1, -1), router_cat)


def _moe_kernel(h2_ref, comb_ref, x1_ref, mod_ref, wg_ref, wu_ref, wd_ref, fg_ref, o_ref, acc_scr):
    e = pl.program_id(2)

    @pl.when(e == 0)
    def _():
        acc_scr[...] = jnp.zeros_like(acc_scr)

    h2 = h2_ref[0]
    comb = comb_ref[0]
    lane = lax.broadcasted_iota(jnp.int32, comb.shape, 1)
    cw = jnp.sum(jnp.where(lane == e, comb, 0.0), axis=1, keepdims=True)
    hid = _silu(jnp.dot(h2, wg_ref[0], preferred_element_type=F32)) * jnp.dot(
        h2, wu_ref[0], preferred_element_type=F32)
    acc_scr[...] += _bdot(hid * cw, wd_ref[0])

    @pl.when(e == pl.num_programs(2) - 1)
    def _():
        x2 = x1_ref[0] + mod_ref[0][5:6] * acc_scr[...]
        o_ref[0] = _rms_scale(x2) * fg_ref[...]


def _moe(h2, comb, x1, mod, wg, wu, wd, final_g, *, tm):
    b, t, _ = x1.shape
    tile = lambda n: pl.BlockSpec((1, tm, n), lambda i, j, e: (i, j, 0))
    return pl.pallas_call(
        _moe_kernel,
        out_shape=jax.ShapeDtypeStruct((b, t, D_MODEL), F32),
        grid=(b, t // tm, N_EXPERTS),
        in_specs=[tile(D_MODEL), tile(LANES), tile(D_MODEL),
                  pl.BlockSpec((1, N_MOD, D_MODEL), lambda i, j, e: (i, 0, 0)),
                  pl.BlockSpec((1, D_MODEL, EXPERT_FF), lambda i, j, e: (e, 0, 0)),
                  pl.BlockSpec((1, D_MODEL, EXPERT_FF), lambda i, j, e: (e, 0, 0)),
                  pl.BlockSpec((1, EXPERT_FF, D_MODEL), lambda i, j, e: (e, 0, 0)),
                  pl.BlockSpec((1, D_MODEL), lambda i, j, e: (0, 0))],
        out_specs=tile(D_MODEL),
        scratch_shapes=[pltpu.VMEM((tm, D_MODEL), F32)],
        compiler_params=pltpu.CompilerParams(
            dimension_semantics=("parallel", "parallel", "arbitrary"),
            vmem_limit_bytes=VMEM_LIMIT),
        name="moe_dense",
    )(h2, comb, x1, mod, wg, wu, wd, final_g.reshape(1, -1))


def _pack_rwkv_state(s):
    b = s.shape[0]
    s = s.reshape(b, RWKV_PAIRS, 2, RWKV_HEAD_DIM, RWKV_HEAD_DIM)
    return s.transpose(0, 1, 3, 2, 4).reshape(b, RWKV_PAIRS, RWKV_HEAD_DIM, LANES)


def _unpack_rwkv_state(s):
    b = s.shape[0]
    s = s.reshape(b, RWKV_PAIRS, RWKV_HEAD_DIM, 2, RWKV_HEAD_DIM)
    return s.transpose(0, 1, 3, 2, 4).reshape(b, RWKV_HEADS, RWKV_HEAD_DIM, RWKV_HEAD_DIM)


def _pad_time(a, t_pad):
    t = a.shape[1]
    return a if t == t_pad else jnp.pad(a, ((0, 0), (0, t_pad - t), (0, 0)))


def _trunk(x, mod, shift0, s_rwkv0, s_gla0, wts):
    b, t, _ = x.shape
    tm = min(t, 512)
    p_rwkv = _norm_proj(x, mod, wts["norm1_g"], wts["w_in_rwkv"], gate=False, tm=tm)
    p_gla = _norm_proj(x, mod, wts["norm1_g"], wts["w_in_gla"], gate=False, tm=tm)
    gates = _norm_proj(x, mod, wts["norm1_g"], wts["w_in_gate"], gate=True, tm=tm)

    t_pad = -(-t // CHUNK) * CHUNK
    y_r, s_rwkv = _rwkv(_pad_time(p_rwkv, t_pad), shift0, _pack_rwkv_state(s_rwkv0),
                        wts["rwkv_mu"], wts["rwkv_w0"], wts["rwkv_decay_up"], wts["rwkv_a0"],
                        wts["rwkv_aaa_up"], wts["rwkv_gate_up"], wts["rwkv_k_k"], wts["rwkv_k_a"],
                        wts["rwkv_r_k"], wts["rwkv_ln_g"], wts["rwkv_ln_b"], t_valid=min(t, CHUNK))
    y_g, s_gla = _gla(_pad_time(p_gla, t_pad), s_gla0.transpose(0, 1, 3, 2), wts["gla_alpha_up"],
                      wts["gla_alpha_b"], wts["gla_norm_g"], t_valid=min(t, CHUNK))
    y_r = y_r[:, :t]
    y_g = y_g[:, :t]

    x1, h2, comb = _merge(y_r, y_g, gates, x, mod, wts["w_branch_rwkv"], wts["w_branch_gla"],
                          wts["w_out"], wts["norm2_g"], wts["router_cat"], tm=tm)
    y = _moe(h2, comb, x1, mod, wts["expert_w_gate"], wts["expert_w_up"], wts["expert_w_down"],
             wts["final_norm_g"], tm=min(t, 1024))
    shift = p_rwkv[:, t - 1:t, :]
    return y, shift[None], _unpack_rwkv_state(s_rwkv)[None], s_gla.transpose(0, 1, 3, 2)[None]


def kernel(x_prompt, x_sample, c_prompt, c_sample, state_rwkv_shift, state_rwkv, state_gla, w_ada, b_ada, norm1_g, w_in, rwkv_mu, rwkv_w0, rwkv_decay_up, rwkv_a0, rwkv_aaa_up, rwkv_gate_up, rwkv_k_k, rwkv_k_a, rwkv_r_k, rwkv_ln_g, rwkv_ln_b, gla_alpha_up, gla_alpha_b, gla_norm_g, w_branch_rwkv, w_branch_gla, w_out, norm2_g, router_group, router_expert, expert_w_gate, expert_w_up, expert_w_down, final_norm_g):
    bp = x_prompt.shape[0]
    bs = x_sample.shape[0]
    w_in0 = w_in[0]
    g0 = RWKV_PROJ
    o1, o2, o3, o4 = (GLA_DK_TOTAL, 2 * GLA_DK_TOTAL, 2 * GLA_DK_TOTAL + GLA_DV_TOTAL,
                      2 * GLA_DK_TOTAL + GLA_DV_TOTAL + GLA_GATE_RANK)
    w_gla = w_in0[:, g0:g0 + GLA_PROJ]
    w_gla = jnp.concatenate([w_gla[:, :o3], w_gla[:, o4:], w_gla[:, o3:o4],
                             jnp.zeros((D_MODEL, LANES - GLA_GATE_RANK), F32)], axis=1)
    wts = dict(
        norm1_g=norm1_g[0].reshape(1, -1),
        w_in_rwkv=w_in0[:, :g0].astype(BF16),
        w_in_gla=w_gla.astype(BF16),
        w_in_gate=w_in0[:, g0 + GLA_PROJ:].astype(BF16),
        rwkv_mu=rwkv_mu[0], rwkv_w0=rwkv_w0[0], rwkv_decay_up=rwkv_decay_up[0], rwkv_a0=rwkv_a0[0],
        rwkv_aaa_up=rwkv_aaa_up[0], rwkv_gate_up=rwkv_gate_up[0], rwkv_k_k=rwkv_k_k[0],
        rwkv_k_a=rwkv_k_a[0], rwkv_r_k=rwkv_r_k[0], rwkv_ln_g=rwkv_ln_g[0], rwkv_ln_b=rwkv_ln_b[0],
        gla_alpha_up=jnp.pad(gla_alpha_up[0], ((0, LANES - GLA_GATE_RANK), (0, 0))),
        gla_alpha_b=gla_alpha_b[0], gla_norm_g=gla_norm_g[0],
        w_branch_rwkv=w_branch_rwkv[0].astype(BF16), w_branch_gla=w_branch_gla[0].astype(BF16),
        w_out=w_out[0].astype(BF16), norm2_g=norm2_g[0],
        router_cat=jnp.pad(jnp.concatenate([router_expert[0], router_group[0]], axis=1),
                           ((0, 0), (0, LANES - N_EXPERTS - N_GROUPS))),
        expert_w_gate=expert_w_gate[0].astype(BF16), expert_w_up=expert_w_up[0].astype(BF16),
        expert_w_down=expert_w_down[0].astype(BF16), final_norm_g=final_norm_g,
    )
    mod = _ada(jnp.concatenate([c_prompt, c_sample], axis=0), w_ada[0], b_ada[0])
    mod = mod.reshape(bp + bs, N_MOD, D_MODEL)

    dt = x_prompt.dtype
    y_p, p_shift, p_rwkv, p_gla = _trunk(
        x_prompt, mod[:bp], jnp.zeros((bp, 1, RWKV_PROJ), dt),
        jnp.zeros((bp, RWKV_HEADS, RWKV_HEAD_DIM, RWKV_HEAD_DIM), state_rwkv.dtype),
        jnp.zeros((bp, GLA_HEADS, GLA_DK, GLA_DV), state_gla.dtype), wts)
    y_s, s_shift, s_rwkv, s_gla = _trunk(
        x_sample, mod[bp:], state_rwkv_shift[0], state_rwkv[0], state_gla[0], wts)
    return (y_p, y_s, p_shift, p_rwkv, p_gla, s_shift, s_rwkv, s_gla)
```

```python
import functools
import math

import jax
import jax.numpy as jnp
from jax import lax
from jax.experimental import pallas as pl
from jax.experimental.pallas import tpu as pltpu

F32 = jnp.float32
BF16 = jnp.bfloat16

D_MODEL = 1024
N_MOD = 6
EPS = 1e-6
RWKV_HEAD_DIM = 64
RWKV_HEADS = 16
RWKV_PAIRS = RWKV_HEADS // 2
DECAY_LORA = 64
AAA_LORA = 64
GATE_LORA = 128
RWKV_PROJ = 3 * D_MODEL + DECAY_LORA + AAA_LORA + GATE_LORA
LN_X_EPS = 64e-5
GLA_HEADS = 4
GLA_DK = 128
GLA_DV = 256
GLA_DK_TOTAL = GLA_HEADS * GLA_DK
GLA_DV_TOTAL = GLA_HEADS * GLA_DV
GLA_GATE_RANK = 16
GLA_GATE_NORM = 16.0
GLA_PROJ = 2 * GLA_DK_TOTAL + 2 * GLA_DV_TOTAL + GLA_GATE_RANK
LANES = 128
GLA_PROJ_PAD = 2 * GLA_DK_TOTAL + 2 * GLA_DV_TOTAL + LANES
GATE_PROJ = 2 * D_MODEL
N_GROUPS = 4
EXPERTS_PER_GROUP = 8
N_EXPERTS = 32
EXPERT_FF = 512

CHUNK = 64
MOE_TILE = 256
RWKV_SEQS_PER_STEP = 2
VMEM_LIMIT = 56 * 1024 * 1024


def _bdot(a, b):
    return jnp.dot(a.astype(BF16), b.astype(BF16), preferred_element_type=F32)


def _bdot_nt(a, b):
    return lax.dot_general(a.astype(BF16), b.astype(BF16), (((1,), (1,)), ((), ())),
                           preferred_element_type=F32)


def _bdot_tn(a, b):
    return jnp.dot(a.T.astype(BF16), b.astype(BF16), preferred_element_type=F32)


def _split3(x):
    h1 = x.astype(BF16)
    r1 = x - h1.astype(F32)
    h2 = r1.astype(BF16)
    h3 = (r1 - h2.astype(F32)).astype(BF16)
    return h1, h2, h3


def _dot3(a, b):
    a1, a2, _ = _split3(a)
    b1, b2, _ = _split3(b)
    return (jnp.dot(a1, b1, preferred_element_type=F32)
            + jnp.dot(a2, b1, preferred_element_type=F32)
            + jnp.dot(a1, b2, preferred_element_type=F32))


def _dot_exact_lhs(a_bf16, x):
    x1, x2, _ = _split3(x)
    return (jnp.dot(a_bf16, x1, preferred_element_type=F32)
            + jnp.dot(a_bf16, x2, preferred_element_type=F32))


def _sigmoid(x):
    return 1.0 / (1.0 + jnp.exp(-x))


def _silu(x):
    return x * _sigmoid(x)


def _rms_scale(x):
    return x * lax.rsqrt(jnp.mean(x * x, axis=-1, keepdims=True) + EPS)


def _ada_kernel(c_ref, w_ref, b_ref, o_ref):
    o_ref[...] = _dot3(_silu(c_ref[...]), w_ref[...]) + b_ref[...]


def _ada(c, w_ada, b_ada):
    n_rows = c.shape[0]
    n_out = w_ada.shape[1]
    tn = 1536
    return pl.pallas_call(
        _ada_kernel,
        out_shape=jax.ShapeDtypeStruct((n_rows, n_out), F32),
        grid=(n_out // tn,),
        in_specs=[pl.BlockSpec((n_rows, D_MODEL), lambda j: (0, 0)),
                  pl.BlockSpec((D_MODEL, tn), lambda j: (0, j)),
                  pl.BlockSpec((1, tn), lambda j: (0, j))],
        out_specs=pl.BlockSpec((n_rows, tn), lambda j: (0, j)),
        compiler_params=pltpu.CompilerParams(dimension_semantics=("parallel",),
                                             vmem_limit_bytes=VMEM_LIMIT),
        name="ada",
    )(c, w_ada, b_ada.reshape(1, n_out))


def _norm_proj_kernel(x_ref, mod_ref, g_ref, w_ref, o_ref, *, gate):
    mod = mod_ref[0]
    h = _rms_scale(x_ref[0]) * g_ref[...] * (1.0 + mod[1:2]) + mod[0:1]
    out = _bdot(h, w_ref[...])
    o_ref[0] = _sigmoid(out) if gate else out


def _norm_proj(x, mod, g, w_bf16, *, gate, tm):
    b, t, _ = x.shape
    n = w_bf16.shape[1]
    return pl.pallas_call(
        functools.partial(_norm_proj_kernel, gate=gate),
        out_shape=jax.ShapeDtypeStruct((b, t, n), F32),
        grid=(b, t // tm),
        in_specs=[pl.BlockSpec((1, tm, D_MODEL), lambda i, j: (i, j, 0)),
                  pl.BlockSpec((1, N_MOD, D_MODEL), lambda i, j: (i, 0, 0)),
                  pl.BlockSpec((1, D_MODEL), lambda i, j: (0, 0)),
                  pl.BlockSpec((D_MODEL, n), lambda i, j: (0, 0))],
        out_specs=pl.BlockSpec((1, tm, n), lambda i, j: (i, j, 0)),
        compiler_params=pltpu.CompilerParams(dimension_semantics=("parallel", "parallel"),
                                             vmem_limit_bytes=VMEM_LIMIT),
        name="norm_proj_gate" if gate else "norm_proj",
    )(x, mod, g, w_bf16)


def _blockdiag(x, lo_mask):
    return jnp.concatenate([jnp.where(lo_mask, x, 0.0), jnp.where(lo_mask, 0.0, x)], axis=0)


def _rwkv_kernel(p_ref, shift_ref, s0_ref, mu_ref, w0_ref, dup_ref, a0_ref, aup_ref, gup_ref,
                 kk_ref, ka_ref, rk_ref, lng_ref, lnb_ref, y_ref, sout_ref, s_scr, carry_scr,
                 *, t_valid, nb):
    c = pl.program_id(1)
    n_chunks = pl.num_programs(1)
    C = CHUNK
    N = RWKV_HEAD_DIM

    @pl.when(c == 0)
    def _():
        s_scr[...] = s0_ref[...]
        carry_scr[...] = shift_ref[...]

    row = lax.broadcasted_iota(jnp.int32, (C, LANES), 0)
    lane = lax.broadcasted_iota(jnp.int32, (C, LANES), 1)
    lane_t = jnp.where(lane < N, lane, lane - N)
    lo = lane < N
    strict = lane_t < row
    incl = lane_t <= row
    eye = (lane_t == row).astype(F32)
    valid = row < t_valid
    tri = (lax.broadcasted_iota(jnp.int32, (C, C), 1)
           <= lax.broadcasted_iota(jnp.int32, (C, C), 0)).astype(BF16)
    sq_r = lax.broadcasted_iota(jnp.int32, (LANES, LANES), 0)
    sq_c = lax.broadcasted_iota(jnp.int32, (LANES, LANES), 1)
    same_head = (sq_r < N) == (sq_c < N)
    diag = sq_r == sq_c
    row_w = lax.broadcasted_iota(jnp.int32, (C, RWKV_PROJ), 0)
    o3 = 3 * D_MODEL

    def hsum(v):
        s_lo = jnp.sum(jnp.where(lo, v, 0.0), axis=1, keepdims=True)
        s_hi = jnp.sum(jnp.where(lo, 0.0, v), axis=1, keepdims=True)
        return jnp.where(lo, s_lo, s_hi)

    def bd(v):
        return _blockdiag(v, lo)

    def cat0(*a):
        return jnp.concatenate(a, axis=0)

    def cat1(*a):
        return jnp.concatenate(a, axis=1)

    valid_w = lax.broadcasted_iota(jnp.int32, (C, D_MODEL), 0) < t_valid
    xs, logw_all, cum_all, asig_all, g_all = [], [], [], [], []
    for bl in range(nb):
        p = p_ref[bl]
        prev = jnp.where(row_w == 0, carry_scr[bl], pltpu.roll(p, 1, axis=0))
        carry_scr[bl] = p[C - 1:C, :]
        x = p + (prev - p) * mu_ref[...]
        xw = x[:, o3:o3 + DECAY_LORA]
        xa = x[:, o3 + DECAY_LORA:o3 + DECAY_LORA + AAA_LORA]
        xg = x[:, o3 + DECAY_LORA + AAA_LORA:]
        z = w0_ref[...] + _bdot(jnp.tanh(xw), dup_ref[...])
        logw_full = -math.exp(-0.5) * _sigmoid(z)
        if t_valid < C:
            logw_full = jnp.where(valid_w, logw_full, 0.0)
        xs.append(x)
        logw_all.append(logw_full)
        cum_all.append(_dot_exact_lhs(tri, logw_full))
        asig_all.append(_sigmoid(a0_ref[...] + _bdot(xa, aup_ref[...])))
        g_all.append(_bdot(_sigmoid(xg), gup_ref[...]))

    units = [(bl, pr) for bl in range(nb) for pr in range(RWKV_PAIRS)]
    U = range(len(units))
    cols = [slice(pr * LANES, (pr + 1) * LANES) for _, pr in units]
    r = [xs[bl][:, pr * LANES:(pr + 1) * LANES] for bl, pr in units]
    k = [xs[bl][:, D_MODEL + pr * LANES:D_MODEL + (pr + 1) * LANES] for bl, pr in units]
    v = [xs[bl][:, 2 * D_MODEL + pr * LANES:2 * D_MODEL + (pr + 1) * LANES] for bl, pr in units]
    logw = [logw_all[bl][:, cols[u]] for u, (bl, _) in enumerate(units)]
    asig = [asig_all[bl][:, cols[u]] for u, (bl, _) in enumerate(units)]
    kk = [k[u] * kk_ref[:, cols[u]] for u in U]
    k2 = [k[u] * (1.0 + (asig[u] - 1.0) * ka_ref[:, cols[u]]) for u in U]
    ss = [hsum(kk[u] * kk[u]) for u in U]
    cum = [cum_all[bl][:, cols[u]] for u, (bl, _) in enumerate(units)]
    bsum = [hsum(r[u] * k2[u] * rk_ref[:, cols[u]]) for u in U]
    kk = [kk[u] * lax.rsqrt(jnp.maximum(ss[u], 1e-24)) for u in U]
    bonus = [bsum[u] * v[u] for u in U]
    if t_valid < C:
        kk = [jnp.where(valid, a, 0.0) for a in kk]
        k2 = [jnp.where(valid, a, 0.0) for a in k2]
        v = [jnp.where(valid, a, 0.0) for a in v]
        r = [jnp.where(valid, a, 0.0) for a in r]
    b_vec = [kk[u] * asig[u] for u in U]
    clast = [cum[u][C - 1:C, :] for u in U]
    e_inv = [jnp.exp(-cum[u]) for u in U]
    e_tail = [jnp.exp(clast[u] - cum[u]) for u in U]
    aq = [-kk[u] * jnp.exp(cum[u] - logw[u]) for u in U]
    rq = [r[u] * jnp.exp(cum[u]) for u in U]
    bk = [b_vec[u] * e_inv[u] for u in U]
    kx = [k2[u] * e_inv[u] for u in U]
    bb = [b_vec[u] * e_tail[u] for u in U]
    kb = [k2[u] * e_tail[u] for u in U]

    g4 = [_bdot_nt(cat0(aq[u], rq[u]), cat0(bd(bk[u]), bd(kx[u]))) for u in U]
    m_ab = [jnp.where(strict, g4[u][:C, :LANES], 0.0) for u in U]
    m_ak = [jnp.where(strict, g4[u][:C, LANES:], 0.0) for u in U]
    m_rb = [jnp.where(incl, g4[u][C:, :LANES], 0.0) for u in U]
    m_rk = [jnp.where(incl, g4[u][C:, LANES:], 0.0) for u in U]

    t_inv = [eye + m_ab[u] for u in U]
    l_pow = [_bdot(m_ab[u], bd(m_ab[u])) for u in U]
    mv = [_bdot(cat0(m_ak[u], m_rk[u]), bd(v[u])) for u in U]
    n_steps = C.bit_length() - 1
    for step in range(1, n_steps):
        if step < n_steps - 1:
            both = [_bdot(l_pow[u], cat1(bd(t_inv[u]), bd(l_pow[u]))) for u in U]
            t_inv = [t_inv[u] + both[u][:, :LANES] for u in U]
            l_pow = [both[u][:, LANES:] for u in U]
        else:
            t_inv = [t_inv[u] + _bdot(l_pow[u], bd(t_inv[u])) for u in U]

    au = [_bdot(t_inv[u], cat1(bd(aq[u]), bd(mv[u][:C]))) for u in U]
    a_bar = [au[u][:, :LANES] for u in U]
    u_bar = [au[u][:, LANES:] for u in U]
    mau = [_bdot(m_rb[u], cat1(bd(a_bar[u]), bd(u_bar[u]))) for u in U]
    p_full = [_bdot_tn(a_bar[u], bb[u]) for u in U]
    q_full = [_bdot_tn(cat0(u_bar[u], v[u]), cat0(bb[u], kb[u])) for u in U]
    r_til = [rq[u] + mau[u][:, :LANES] for u in U]
    y_loc = [mau[u][:, LANES:] + mv[u][C:] for u in U]
    p_bd = [jnp.where(same_head, p_full[u], 0.0) + jnp.where(diag, jnp.exp(clast[u]), 0.0) for u in U]
    q_pk = [jnp.where(lo[:N], q_full[u][:N], q_full[u][N:]) for u in U]

    s_old = [s_scr[bl, pr] for bl, pr in units]
    y = [_bdot_nt(r_til[u], _blockdiag(s_old[u], lo[:N])) + y_loc[u] for u in U]
    s_new = [_bdot(s_old[u], p_bd[u]) + q_pk[u] for u in U]
    for u, (bl, pr) in enumerate(units):
        s_scr[bl, pr] = s_new[u]

    mu_y = [hsum(y[u]) * (1.0 / N) for u in U]
    yc = [y[u] - mu_y[u] for u in U]
    var = [hsum(yc[u] * yc[u]) * (1.0 / N) for u in U]
    for u, (bl, pr) in enumerate(units):
        yn = yc[u] * lax.rsqrt(var[u] + LN_X_EPS) * lng_ref[:, cols[u]] + lnb_ref[:, cols[u]]
        y_ref[bl, :, cols[u]] = (yn + bonus[u]) * g_all[bl][:, cols[u]]

    @pl.when(c == n_chunks - 1)
    def _():
        sout_ref[...] = s_scr[...]


def _rwkv(p_rwkv, shift_prev, s_packed, mu, w0, decay_up, a0, aaa_up, gate_up, k_k, k_a, r_k,
          ln_g, ln_b, *, t_valid, nb):
    b, t, _ = p_rwkv.shape
    row = lambda a: a.reshape(1, -1)
    full = lambda shape: pl.BlockSpec(shape, lambda i, j: (0,) * len(shape))
    state_spec = pl.BlockSpec((nb, RWKV_PAIRS, RWKV_HEAD_DIM, LANES), lambda i, j: (i, 0, 0, 0))
    return pl.pallas_call(
        functools.partial(_rwkv_kernel, t_valid=t_valid, nb=nb),
        out_shape=(jax.ShapeDtypeStruct((b, t, D_MODEL), F32),
                   jax.ShapeDtypeStruct((b, RWKV_PAIRS, RWKV_HEAD_DIM, LANES), F32)),
        grid=(b // nb, t // CHUNK),
        in_specs=[pl.BlockSpec((nb, CHUNK, RWKV_PROJ), lambda i, j: (i, j, 0)),
                  pl.BlockSpec((nb, 1, RWKV_PROJ), lambda i, j: (i, 0, 0)),
                  state_spec,
                  full((1, RWKV_PROJ)), full((1, D_MODEL)), full((DECAY_LORA, D_MODEL)),
                  full((1, D_MODEL)), full((AAA_LORA, D_MODEL)), full((GATE_LORA, D_MODEL)),
                  full((1, D_MODEL)), full((1, D_MODEL)), full((1, D_MODEL)),
                  full((1, D_MODEL)), full((1, D_MODEL))],
        out_specs=(pl.BlockSpec((nb, CHUNK, D_MODEL), lambda i, j: (i, j, 0)), state_spec),
        scratch_shapes=[pltpu.VMEM((nb, RWKV_PAIRS, RWKV_HEAD_DIM, LANES), F32),
                        pltpu.VMEM((nb, 1, RWKV_PROJ), F32)],
        compiler_params=pltpu.CompilerParams(dimension_semantics=("parallel", "arbitrary"),
                                             vmem_limit_bytes=VMEM_LIMIT),
        name="rwkv_scan",
    )(p_rwkv, shift_prev, s_packed, row(mu), row(w0), decay_up, row(a0), aaa_up, gate_up,
      row(k_k), row(k_a), row(r_k), row(ln_g), row(ln_b))


def _gla_kernel(p_ref, s0_ref, aup_ref, ab_ref, ng_ref, y_ref, sout_ref, s_scr, *, t_valid):
    c = pl.program_id(1)
    n_chunks = pl.num_programs(1)
    C = CHUNK

    @pl.when(c == 0)
    def _():
        s_scr[...] = s0_ref[0]

    p = p_ref[0]
    o_k = GLA_DK_TOTAL
    o_v = 2 * GLA_DK_TOTAL
    o_r = o_v + GLA_DV_TOTAL
    o_a = o_r + GLA_DV_TOTAL
    logit = _dot3(p[:, o_a:], aup_ref[...]) + ab_ref[...]
    g = (jnp.minimum(logit, 0.0) - jnp.log(1.0 + jnp.exp(-jnp.abs(logit)))) * (1.0 / GLA_GATE_NORM)
    row = lax.broadcasted_iota(jnp.int32, (C, GLA_DK_TOTAL), 0)
    if t_valid < C:
        g = jnp.where(row < t_valid, g, 0.0)
    tri = (lax.broadcasted_iota(jnp.int32, (C, C), 1)
           <= lax.broadcasted_iota(jnp.int32, (C, C), 0))
    cum = _dot_exact_lhs(tri.astype(BF16), g)
    clast = cum[C - 1:C, :]
    cmid = cum[C // 2 - 1:C // 2, :]
    valid_k = lax.broadcasted_iota(jnp.int32, (C, GLA_DK), 0) < t_valid
    valid_v = lax.broadcasted_iota(jnp.int32, (C, GLA_DV), 0) < t_valid

    for h in range(GLA_HEADS):
        ks = slice(h * GLA_DK, (h + 1) * GLA_DK)
        vs = slice(h * GLA_DV, (h + 1) * GLA_DV)
        q = p[:, ks] * (GLA_DK ** -0.5)
        k = p[:, o_k + h * GLA_DK:o_k + (h + 1) * GLA_DK]
        v = p[:, o_v + h * GLA_DV:o_v + (h + 1) * GLA_DV]
        r = p[:, o_r + h * GLA_DV:o_r + (h + 1) * GLA_DV]
        if t_valid < C:
            k = jnp.where(valid_k, k, 0.0)
            v = jnp.where(valid_v, v, 0.0)
        b = cum[:, ks]
        att = _bdot_nt(q * jnp.exp(b - cmid[:, ks]), k * jnp.exp(cmid[:, ks] - b))
        att = jnp.where(tri, att, 0.0)
        s_t = s_scr[h]
        o = _bdot(att, v) + _bdot_nt(q * jnp.exp(b), s_t)
        s_scr[h] = s_t * jnp.exp(clast[:, ks]) + _bdot_tn(v, k * jnp.exp(clast[:, ks] - b))
        o = o * lax.rsqrt(jnp.mean(o * o, axis=-1, keepdims=True) + EPS) * ng_ref[...]
        y_ref[0, :, vs] = o * _silu(r)

    @pl.when(c == n_chunks - 1)
    def _():
        sout_ref[0] = s_scr[...]


def _gla(p_gla, s_t, alpha_up_pad, alpha_b, norm_g, *, t_valid):
    b, t, _ = p_gla.shape
    full = lambda shape: pl.BlockSpec(shape, lambda i, j: (0,) * len(shape))
    return pl.pallas_call(
        functools.partial(_gla_kernel, t_valid=t_valid),
        out_shape=(jax.ShapeDtypeStruct((b, t, GLA_DV_TOTAL), F32),
                   jax.ShapeDtypeStruct((b, GLA_HEADS, GLA_DV, GLA_DK), F32)),
        grid=(b, t // CHUNK),
        in_specs=[pl.BlockSpec((1, CHUNK, GLA_PROJ_PAD), lambda i, j: (i, j, 0)),
                  pl.BlockSpec((1, GLA_HEADS, GLA_DV, GLA_DK), lambda i, j: (i, 0, 0, 0)),
                  full((LANES, GLA_DK_TOTAL)), full((1, GLA_DK_TOTAL)), full((1, GLA_DV))],
        out_specs=(pl.BlockSpec((1, CHUNK, GLA_DV_TOTAL), lambda i, j: (i, j, 0)),
                   pl.BlockSpec((1, GLA_HEADS, GLA_DV, GLA_DK), lambda i, j: (i, 0, 0, 0))),
        scratch_shapes=[pltpu.VMEM((GLA_HEADS, GLA_DV, GLA_DK), F32)],
        compiler_params=pltpu.CompilerParams(dimension_semantics=("parallel", "arbitrary"),
                                             vmem_limit_bytes=VMEM_LIMIT),
        name="gla_scan",
    )(p_gla, s_t, alpha_up_pad, alpha_b.reshape(1, -1), norm_g.reshape(1, -1))


def _merge_kernel(yr_ref, yg_ref, gate_ref, x_ref, mod_ref, wbr_ref, wbg_ref, wout_ref, g2_ref,
                  router_ref, x1_ref, h2_ref, route_ref):
    mod = mod_ref[0]
    gates = gate_ref[0]
    merged = (gates[:, :D_MODEL] * _bdot(yr_ref[0], wbr_ref[...])
              + gates[:, D_MODEL:] * _bdot(yg_ref[0], wbg_ref[...]))
    x1 = x_ref[0] + mod[2:3] * _bdot(merged, wout_ref[...])
    x1_ref[0] = x1
    h2 = _rms_scale(x1) * g2_ref[...] * (1.0 + mod[4:5]) + mod[3:4]
    h2_ref[0] = h2

    logits = _dot3(h2, router_ref[...])
    tm = logits.shape[0]
    lane = lax.broadcasted_iota(jnp.int32, (tm, LANES), 1)
    neg = -jnp.inf
    is_group = (lane >= N_EXPERTS) & (lane < N_EXPERTS + N_GROUPS)
    gl = jnp.where(is_group, logits, neg)
    gmax = jnp.max(gl, axis=1, keepdims=True)
    g_idx = jnp.min(jnp.where(gl == gmax, lane, LANES), axis=1, keepdims=True) - N_EXPERTS
    p_g = 1.0 / jnp.sum(jnp.exp(gl - gmax), axis=1, keepdims=True)
    in_group = (lane >= g_idx * EXPERTS_PER_GROUP) & (lane < (g_idx + 1) * EXPERTS_PER_GROUP)
    el = jnp.where(in_group, logits, neg)
    v1 = jnp.max(el, axis=1, keepdims=True)
    i1 = jnp.min(jnp.where(el == v1, lane, LANES), axis=1, keepdims=True)
    el2 = jnp.where(lane == i1, neg, el)
    v2 = jnp.max(el2, axis=1, keepdims=True)
    i2 = jnp.min(jnp.where(el2 == v2, lane, LANES), axis=1, keepdims=True)
    e21 = jnp.exp(v2 - v1)
    w1 = p_g / (1.0 + e21)
    route_ref[0] = (jnp.where(lane == 0, i1.astype(F32), 0.0) + jnp.where(lane == 1, i2.astype(F32), 0.0)
                    + jnp.where(lane == 2, w1, 0.0) + jnp.where(lane == 3, w1 * e21, 0.0))


def _merge(y_r, y_g, gates, x, mod, wbr, wbg, wout, norm2_g, router_cat, *, tm):
    b, t, _ = x.shape
    full = lambda shape: pl.BlockSpec(shape, lambda i, j: (0,) * len(shape))
    tile = lambda n: pl.BlockSpec((1, tm, n), lambda i, j: (i, j, 0))
    return pl.pallas_call(
        _merge_kernel,
        out_shape=(jax.ShapeDtypeStruct((b, t, D_MODEL), F32),
                   jax.ShapeDtypeStruct((b, t, D_MODEL), F32),
                   jax.ShapeDtypeStruct((b, t, LANES), F32)),
        grid=(b, t // tm),
        in_specs=[tile(D_MODEL), tile(D_MODEL), tile(GATE_PROJ), tile(D_MODEL),
                  pl.BlockSpec((1, N_MOD, D_MODEL), lambda i, j: (i, 0, 0)),
                  full((D_MODEL, D_MODEL)), full((D_MODEL, D_MODEL)), full((D_MODEL, D_MODEL)),
                  full((1, D_MODEL)), full((D_MODEL, LANES))],
        out_specs=(tile(D_MODEL), tile(D_MODEL), tile(LANES)),
        compiler_params=pltpu.CompilerParams(dimension_semantics=("parallel", "parallel"),
                                             vmem_limit_bytes=VMEM_LIMIT),
        name="merge_router",
    )(y_r, y_g, gates, x, mod, wbr, wbg, wout, norm2_g.reshape(1, -1), router_cat)


def _moe_plan(expert_ids, n_tok):
    tm = MOE_TILE
    n_assign = 2 * n_tok
    n_tiles = -(-n_assign // tm) + N_EXPERTS
    keys = expert_ids.T.reshape(-1)
    order = jnp.argsort(keys, stable=True).astype(jnp.int32)
    counts = jnp.sum((keys[:, None] == jnp.arange(N_EXPERTS, dtype=jnp.int32)[None, :]).astype(jnp.int32), axis=0)
    tiles_e = (counts + tm - 1) // tm
    tile_end = jnp.cumsum(tiles_e)
    row_start = jnp.cumsum(counts) - counts
    n_used = tile_end[-1]
    t_idx = jnp.arange(n_tiles, dtype=jnp.int32)
    t_eff = jnp.minimum(t_idx, n_used - 1)
    tile_e = jnp.minimum(jnp.searchsorted(tile_end, t_eff, side="right").astype(jnp.int32), N_EXPERTS - 1)
    local = t_idx - (tile_end[tile_e] - tiles_e[tile_e])
    n_valid = jnp.where(t_idx < n_used, jnp.clip(counts[tile_e] - local * tm, 0, tm), 0)
    j_idx = jnp.arange(tm, dtype=jnp.int32)[None, :]
    pos = jnp.clip(row_start[tile_e][:, None] + local[:, None] * tm + j_idx, 0, n_assign - 1)
    valid = j_idx < n_valid[:, None]
    a = order[pos]
    tok = jnp.where(valid, a % n_tok, 0)
    dump = n_assign + (t_idx[:, None] % 2) * tm + j_idx
    dst = jnp.where(valid, a, dump)
    return (tile_e, n_used.reshape(1).astype(jnp.int32),
            tok.reshape(n_tiles, 1, tm), dst.reshape(n_tiles, 1, tm))


def _moe_group_kernel(tile_e_ref, n_used_ref, tok_ref, tok_next_ref, dst_ref, h2_hbm, wg_ref, wu_ref,
                      wd_ref, out_hbm, xbuf, obuf, gsem, ssem):
    t = pl.program_id(0)
    n_t = pl.num_programs(0)
    tm = MOE_TILE
    slot = lax.rem(t, 2)
    n_used = n_used_ref[0]

    def gather(idx_ref, s):
        def body(j, carry):
            pltpu.make_async_copy(h2_hbm.at[pl.ds(idx_ref[0, 0, j], 1)], xbuf.at[s, pl.ds(j, 1)],
                                  gsem.at[s]).start()
            return carry
        lax.fori_loop(0, tm, body, 0, unroll=8)

    def wait_gather(s):
        pltpu.make_async_copy(h2_hbm.at[pl.ds(0, tm)], xbuf.at[s], gsem.at[s]).wait()

    def wait_scatter(s):
        pltpu.make_async_copy(obuf.at[s], out_hbm.at[pl.ds(0, tm)], ssem.at[s]).wait()

    @pl.when(t == 0)
    def _():
        n_rows = out_hbm.shape[0]
        obuf[1] = jnp.zeros((tm, D_MODEL), F32)
        for half in range(2):
            fill = pltpu.make_async_copy(obuf.at[1], out_hbm.at[pl.ds(n_rows - (2 - half) * tm, tm)],
                                         ssem.at[1])
            fill.start()
            fill.wait()

    @pl.when((t == 0) & (n_used > 0))
    def _():
        gather(tok_ref, 0)

    @pl.when(t + 1 < n_used)
    def _():
        gather(tok_next_ref, 1 - slot)

    @pl.when((t >= 2) & (t - 2 < n_used))
    def _():
        wait_scatter(slot)

    @pl.when(t < n_used)
    def _():
        wait_gather(slot)
        x = xbuf[slot].astype(BF16)
        hid = _silu(jnp.dot(x, wg_ref[0], preferred_element_type=F32)) * jnp.dot(
            x, wu_ref[0], preferred_element_type=F32)
        obuf[slot] = _bdot(hid, wd_ref[0])

        def body(j, carry):
            pltpu.make_async_copy(obuf.at[slot, pl.ds(j, 1)], out_hbm.at[pl.ds(dst_ref[0, 0, j], 1)],
                                  ssem.at[slot]).start()
            return carry
        lax.fori_loop(0, tm, body, 0, unroll=8)

    @pl.when(t == n_t - 1)
    def _():
        @pl.when(t - 1 < n_used)
        def _():
            wait_scatter(1 - slot)

        @pl.when(t < n_used)
        def _():
            wait_scatter(slot)


def _moe_grouped(h2_all, expert_ids, wg, wu, wd):
    n_tok = h2_all.shape[0]
    tm = MOE_TILE
    tile_e, n_used, tok, dst = _moe_plan(expert_ids, n_tok)
    n_tiles = tile_e.shape[0]
    smem_tile = lambda f: pl.BlockSpec((1, 1, tm), f, memory_space=pltpu.SMEM)
    w_spec = lambda shape: pl.BlockSpec(shape, lambda t, te, nu: (te[t], 0, 0))
    grid_spec = pltpu.PrefetchScalarGridSpec(
        num_scalar_prefetch=2,
        grid=(n_tiles,),
        in_specs=[smem_tile(lambda t, te, nu: (t, 0, 0)),
                  smem_tile(lambda t, te, nu: (jnp.minimum(t + 1, n_tiles - 1), 0, 0)),
                  smem_tile(lambda t, te, nu: (t, 0, 0)),
                  pl.BlockSpec(memory_space=pl.ANY),
                  w_spec((1, D_MODEL, EXPERT_FF)), w_spec((1, D_MODEL, EXPERT_FF)),
                  w_spec((1, EXPERT_FF, D_MODEL))],
        out_specs=pl.BlockSpec(memory_space=pl.ANY),
        scratch_shapes=[pltpu.VMEM((2, tm, D_MODEL), F32), pltpu.VMEM((2, tm, D_MODEL), F32),
                        pltpu.SemaphoreType.DMA((2,)), pltpu.SemaphoreType.DMA((2,))],
    )
    return pl.pallas_call(
        _moe_group_kernel,
        out_shape=jax.ShapeDtypeStruct((2 * n_tok + 2 * tm, D_MODEL), F32),
        grid_spec=grid_spec,
        compiler_params=pltpu.CompilerParams(dimension_semantics=("arbitrary",),
                                             vmem_limit_bytes=VMEM_LIMIT),
        name="moe_grouped",
    )(tile_e, n_used, tok, tok, dst, h2_all, wg, wu, wd)


def _final_kernel(x1_ref, o0_ref, o1_ref, route_ref, mod_ref, fg_ref, y_ref):
    route = route_ref[...]
    lane = lax.broadcasted_iota(jnp.int32, route.shape, 1)
    w1 = jnp.sum(jnp.where(lane == 2, route, 0.0), axis=1, keepdims=True)
    w2 = jnp.sum(jnp.where(lane == 3, route, 0.0), axis=1, keepdims=True)
    x2 = x1_ref[0] + mod_ref[0][5:6] * (w1 * o0_ref[...] + w2 * o1_ref[...])
    y_ref[0] = _rms_scale(x2) * fg_ref[...]


def _final(x1, moe_out, route_all, mod, final_g, *, row_off, tm):
    b, t, _ = x1.shape
    n_tok = route_all.shape[0]
    blk0 = row_off // tm
    blk1 = (n_tok + row_off) // tm
    per_seq = t // tm
    return pl.pallas_call(
        _final_kernel,
        out_shape=jax.ShapeDtypeStruct((b, t, D_MODEL), F32),
        grid=(b, per_seq),
        in_specs=[pl.BlockSpec((1, tm, D_MODEL), lambda i, j: (i, j, 0)),
                  pl.BlockSpec((tm, D_MODEL), lambda i, j: (blk0 + i * per_seq + j, 0)),
                  pl.BlockSpec((tm, D_MODEL), lambda i, j: (blk1 + i * per_seq + j, 0)),
                  pl.BlockSpec((tm, LANES), lambda i, j: (blk0 + i * per_seq + j, 0)),
                  pl.BlockSpec((1, N_MOD, D_MODEL), lambda i, j: (i, 0, 0)),
                  pl.BlockSpec((1, D_MODEL), lambda i, j: (0, 0))],
        out_specs=pl.BlockSpec((1, tm, D_MODEL), lambda i, j: (i, j, 0)),
        compiler_params=pltpu.CompilerParams(dimension_semantics=("parallel", "parallel"),
                                             vmem_limit_bytes=VMEM_LIMIT),
        name="moe_combine_norm",
    )(x1, moe_out, moe_out, route_all, mod, final_g.reshape(1, -1))


def _pack_rwkv_state(s):
    b = s.shape[0]
    s = s.reshape(b, RWKV_PAIRS, 2, RWKV_HEAD_DIM, RWKV_HEAD_DIM)
    return s.transpose(0, 1, 3, 2, 4).reshape(b, RWKV_PAIRS, RWKV_HEAD_DIM, LANES)


def _unpack_rwkv_state(s):
    b = s.shape[0]
    s = s.reshape(b, RWKV_PAIRS, RWKV_HEAD_DIM, 2, RWKV_HEAD_DIM)
    return s.transpose(0, 1, 3, 2, 4).reshape(b, RWKV_HEADS, RWKV_HEAD_DIM, RWKV_HEAD_DIM)


def _pad_time(a, t_pad):
    t = a.shape[1]
    return a if t == t_pad else jnp.pad(a, ((0, 0), (0, t_pad - t), (0, 0)))


def _mixer(x, mod, shift0, s_rwkv0, s_gla0, wts):
    b, t, _ = x.shape
    tm = min(t, 512)
    p_rwkv = _norm_proj(x, mod, wts["norm1_g"], wts["w_in_rwkv"], gate=False, tm=tm)
    p_gla = _norm_proj(x, mod, wts["norm1_g"], wts["w_in_gla"], gate=False, tm=tm)
    gates = _norm_proj(x, mod, wts["norm1_g"], wts["w_in_gate"], gate=True, tm=tm)

    t_pad = -(-t // CHUNK) * CHUNK
    y_r, s_rwkv = _rwkv(_pad_time(p_rwkv, t_pad), shift0, _pack_rwkv_state(s_rwkv0),
                        wts["rwkv_mu"], wts["rwkv_w0"], wts["rwkv_decay_up"], wts["rwkv_a0"],
                        wts["rwkv_aaa_up"], wts["rwkv_gate_up"], wts["rwkv_k_k"], wts["rwkv_k_a"],
                        wts["rwkv_r_k"], wts["rwkv_ln_g"], wts["rwkv_ln_b"], t_valid=min(t, CHUNK),
                        nb=RWKV_SEQS_PER_STEP)
    y_g, s_gla = _gla(_pad_time(p_gla, t_pad), s_gla0.transpose(0, 1, 3, 2), wts["gla_alpha_up"],
                      wts["gla_alpha_b"], wts["gla_norm_g"], t_valid=min(t, CHUNK))
    y_r = y_r[:, :t]
    y_g = y_g[:, :t]

    x1, h2, route = _merge(y_r, y_g, gates, x, mod, wts["w_branch_rwkv"], wts["w_branch_gla"],
                           wts["w_out"], wts["norm2_g"], wts["router_cat"], tm=tm)
    shift = p_rwkv[:, t - 1:t, :]
    states = (shift[None], _unpack_rwkv_state(s_rwkv)[None], s_gla.transpose(0, 1, 3, 2)[None])
    return x1, h2, route, states


def kernel(x_prompt, x_sample, c_prompt, c_sample, state_rwkv_shift, state_rwkv, state_gla, w_ada, b_ada, norm1_g, w_in, rwkv_mu, rwkv_w0, rwkv_decay_up, rwkv_a0, rwkv_aaa_up, rwkv_gate_up, rwkv_k_k, rwkv_k_a, rwkv_r_k, rwkv_ln_g, rwkv_ln_b, gla_alpha_up, gla_alpha_b, gla_norm_g, w_branch_rwkv, w_branch_gla, w_out, norm2_g, router_group, router_expert, expert_w_gate, expert_w_up, expert_w_down, final_norm_g):
    bp = x_prompt.shape[0]
    bs = x_sample.shape[0]
    w_in0 = w_in[0]
    g0 = RWKV_PROJ
    o1, o2, o3, o4 = (GLA_DK_TOTAL, 2 * GLA_DK_TOTAL, 2 * GLA_DK_TOTAL + GLA_DV_TOTAL,
                      2 * GLA_DK_TOTAL + GLA_DV_TOTAL + GLA_GATE_RANK)
    w_gla = w_in0[:, g0:g0 + GLA_PROJ]
    w_gla = jnp.concatenate([w_gla[:, :o3], w_gla[:, o4:], w_gla[:, o3:o4],
                             jnp.zeros((D_MODEL, LANES - GLA_GATE_RANK), F32)], axis=1)
    wts = dict(
        norm1_g=norm1_g[0].reshape(1, -1),
        w_in_rwkv=w_in0[:, :g0].astype(BF16),
        w_in_gla=w_gla.astype(BF16),
        w_in_gate=w_in0[:, g0 + GLA_PROJ:].astype(BF16),
        rwkv_mu=rwkv_mu[0], rwkv_w0=rwkv_w0[0], rwkv_decay_up=rwkv_decay_up[0], rwkv_a0=rwkv_a0[0],
        rwkv_aaa_up=rwkv_aaa_up[0], rwkv_gate_up=rwkv_gate_up[0], rwkv_k_k=rwkv_k_k[0],
        rwkv_k_a=rwkv_k_a[0], rwkv_r_k=rwkv_r_k[0], rwkv_ln_g=rwkv_ln_g[0], rwkv_ln_b=rwkv_ln_b[0],
        gla_alpha_up=jnp.pad(gla_alpha_up[0], ((0, LANES - GLA_GATE_RANK), (0, 0))),
        gla_alpha_b=gla_alpha_b[0], gla_norm_g=gla_norm_g[0],
        w_branch_rwkv=w_branch_rwkv[0].astype(BF16), w_branch_gla=w_branch_gla[0].astype(BF16),
        w_out=w_out[0].astype(BF16), norm2_g=norm2_g[0],
        router_cat=jnp.pad(jnp.concatenate([router_expert[0], router_group[0]], axis=1),
                           ((0, 0), (0, LANES - N_EXPERTS - N_GROUPS))),
        expert_w_gate=expert_w_gate[0].astype(BF16), expert_w_up=expert_w_up[0].astype(BF16),
        expert_w_down=expert_w_down[0].astype(BF16), final_norm_g=final_norm_g,
    )
    mod = _ada(jnp.concatenate([c_prompt, c_sample], axis=0), w_ada[0], b_ada[0])
    mod = mod.reshape(bp + bs, N_MOD, D_MODEL)

    dt = x_prompt.dtype
    x1_p, h2_p, route_p, states_p = _mixer(
        x_prompt, mod[:bp], jnp.zeros((bp, 1, RWKV_PROJ), dt),
        jnp.zeros((bp, RWKV_HEADS, RWKV_HEAD_DIM, RWKV_HEAD_DIM), state_rwkv.dtype),
        jnp.zeros((bp, GLA_HEADS, GLA_DK, GLA_DV), state_gla.dtype), wts)
    x1_s, h2_s, route_s, states_s = _mixer(
        x_sample, mod[bp:], state_rwkv_shift[0], state_rwkv[0], state_gla[0], wts)

    n_p = bp * x_prompt.shape[1]
    n_s = bs * x_sample.shape[1]
    n_tok = n_p + n_s
    h2_all = jnp.concatenate([h2_p.reshape(n_p, D_MODEL), h2_s.reshape(n_s, D_MODEL)], axis=0)
    route_all = jnp.concatenate([route_p.reshape(n_p, LANES), route_s.reshape(n_s, LANES)], axis=0)
    expert_ids = route_all[:, :2].astype(jnp.int32)
    moe_out = _moe_grouped(h2_all, expert_ids, wts["expert_w_gate"], wts["expert_w_up"],
                           wts["expert_w_down"])
    y_p = _final(x1_p, moe_out, route_all, mod[:bp], final_norm_g, row_off=0,
                 tm=min(x_prompt.shape[1], 512))
    y_s = _final(x1_s, moe_out, route_all, mod[bp:], final_norm_g, row_off=n_p,
                 tm=min(x_sample.shape[1], 512))
    return (y_p, y_s) + states_p + states_s
```

```python
import functools
import math

import jax
import jax.numpy as jnp
from jax import lax
from jax.experimental import pallas as pl
from jax.experimental.pallas import tpu as pltpu

F32 = jnp.float32
BF16 = jnp.bfloat16

D_MODEL = 1024
N_MOD = 6
EPS = 1e-6
RWKV_HEAD_DIM = 64
RWKV_HEADS = 16
RWKV_PAIRS = RWKV_HEADS // 2
DECAY_LORA = 64
AAA_LORA = 64
GATE_LORA = 128
RWKV_PROJ = 3 * D_MODEL + DECAY_LORA + AAA_LORA + GATE_LORA
LN_X_EPS = 64e-5
GLA_HEADS = 4
GLA_DK = 128
GLA_DV = 256
GLA_DK_TOTAL = GLA_HEADS * GLA_DK
GLA_DV_TOTAL = GLA_HEADS * GLA_DV
GLA_GATE_RANK = 16
GLA_GATE_NORM = 16.0
GLA_PROJ = 2 * GLA_DK_TOTAL + 2 * GLA_DV_TOTAL + GLA_GATE_RANK
LANES = 128
GLA_PROJ_PAD = 2 * GLA_DK_TOTAL + 2 * GLA_DV_TOTAL + LANES
GATE_PROJ = 2 * D_MODEL
N_GROUPS = 4
EXPERTS_PER_GROUP = 8
N_EXPERTS = 32
EXPERT_FF = 512

CHUNK = 64
MOE_TILE = 256
RWKV_SEQS_PER_STEP = 2
GLA_SEQS_PER_STEP = 4
VMEM_LIMIT = 56 * 1024 * 1024


def _bdot(a, b):
    return jnp.dot(a.astype(BF16), b.astype(BF16), preferred_element_type=F32)


def _bdot_nt(a, b):
    return lax.dot_general(a.astype(BF16), b.astype(BF16), (((1,), (1,)), ((), ())),
                           preferred_element_type=F32)


def _bdot_tn(a, b):
    return jnp.dot(a.T.astype(BF16), b.astype(BF16), preferred_element_type=F32)


def _split3(x):
    h1 = x.astype(BF16)
    r1 = x - h1.astype(F32)
    h2 = r1.astype(BF16)
    h3 = (r1 - h2.astype(F32)).astype(BF16)
    return h1, h2, h3


def _dot3(a, b):
    a1, a2, _ = _split3(a)
    b1, b2, _ = _split3(b)
    return (jnp.dot(a1, b1, preferred_element_type=F32)
            + jnp.dot(a2, b1, preferred_element_type=F32)
            + jnp.dot(a1, b2, preferred_element_type=F32))


def _dot_exact_lhs(a_bf16, x):
    x1, x2, _ = _split3(x)
    return (jnp.dot(a_bf16, x1, preferred_element_type=F32)
            + jnp.dot(a_bf16, x2, preferred_element_type=F32))


def _sigmoid(x):
    return 1.0 / (1.0 + jnp.exp(-x))


def _silu(x):
    return x * _sigmoid(x)


def _rms_scale(x):
    return x * lax.rsqrt(jnp.mean(x * x, axis=-1, keepdims=True) + EPS)


def _ada_kernel(c_ref, w_ref, b_ref, o_ref):
    o_ref[...] = _dot3(_silu(c_ref[...]), w_ref[...]) + b_ref[...]


def _ada(c, w_ada, b_ada):
    n_rows = c.shape[0]
    n_out = w_ada.shape[1]
    tn = 1536
    return pl.pallas_call(
        _ada_kernel,
        out_shape=jax.ShapeDtypeStruct((n_rows, n_out), F32),
        grid=(n_out // tn,),
        in_specs=[pl.BlockSpec((n_rows, D_MODEL), lambda j: (0, 0)),
                  pl.BlockSpec((D_MODEL, tn), lambda j: (0, j)),
                  pl.BlockSpec((1, tn), lambda j: (0, j))],
        out_specs=pl.BlockSpec((n_rows, tn), lambda j: (0, j)),
        compiler_params=pltpu.CompilerParams(dimension_semantics=("parallel",),
                                             vmem_limit_bytes=VMEM_LIMIT),
        name="ada",
    )(c, w_ada, b_ada.reshape(1, n_out))


def _norm_proj_kernel(x_ref, mod_ref, g_ref, w_ref, o_ref, *, gate):
    mod = mod_ref[0]
    h = _rms_scale(x_ref[0]) * g_ref[...] * (1.0 + mod[1:2]) + mod[0:1]
    out = _bdot(h, w_ref[...])
    o_ref[0] = _sigmoid(out) if gate else out


def _norm_proj(x, mod, g, w_bf16, *, gate, tm):
    b, t, _ = x.shape
    n = w_bf16.shape[1]
    return pl.pallas_call(
        functools.partial(_norm_proj_kernel, gate=gate),
        out_shape=jax.ShapeDtypeStruct((b, t, n), F32),
        grid=(b, t // tm),
        in_specs=[pl.BlockSpec((1, tm, D_MODEL), lambda i, j: (i, j, 0)),
                  pl.BlockSpec((1, N_MOD, D_MODEL), lambda i, j: (i, 0, 0)),
                  pl.BlockSpec((1, D_MODEL), lambda i, j: (0, 0)),
                  pl.BlockSpec((D_MODEL, n), lambda i, j: (0, 0))],
        out_specs=pl.BlockSpec((1, tm, n), lambda i, j: (i, j, 0)),
        compiler_params=pltpu.CompilerParams(dimension_semantics=("parallel", "parallel"),
                                             vmem_limit_bytes=VMEM_LIMIT),
        name="norm_proj_gate" if gate else "norm_proj",
    )(x, mod, g, w_bf16)


def _blockdiag(x, lo_mask):
    return jnp.concatenate([jnp.where(lo_mask, x, 0.0), jnp.where(lo_mask, 0.0, x)], axis=0)


def _rwkv_kernel(p_ref, shift_ref, s0_ref, mu_ref, w0_ref, dup_ref, a0_ref, aup_ref, gup_ref,
                 kk_ref, ka_ref, rk_ref, lng_ref, lnb_ref, y_ref, sout_ref, s_scr, carry_scr,
                 *, t_valid, nb):
    c = pl.program_id(1)
    n_chunks = pl.num_programs(1)
    C = CHUNK
    N = RWKV_HEAD_DIM

    @pl.when(c == 0)
    def _():
        for bl in range(nb):
            for pr in range(RWKV_PAIRS):
                s_scr[bl, pr] = jnp.concatenate([s0_ref[bl, 2 * pr], s0_ref[bl, 2 * pr + 1]], axis=1)
        carry_scr[...] = shift_ref[...]

    row = lax.broadcasted_iota(jnp.int32, (C, LANES), 0)
    lane = lax.broadcasted_iota(jnp.int32, (C, LANES), 1)
    lane_t = jnp.where(lane < N, lane, lane - N)
    lo = lane < N
    strict = lane_t < row
    incl = lane_t <= row
    eye = (lane_t == row).astype(F32)
    valid = row < t_valid
    tri = (lax.broadcasted_iota(jnp.int32, (C, C), 1)
           <= lax.broadcasted_iota(jnp.int32, (C, C), 0)).astype(BF16)
    sq_r = lax.broadcasted_iota(jnp.int32, (LANES, LANES), 0)
    sq_c = lax.broadcasted_iota(jnp.int32, (LANES, LANES), 1)
    same_head = (sq_r < N) == (sq_c < N)
    diag = sq_r == sq_c
    row_w = lax.broadcasted_iota(jnp.int32, (C, RWKV_PROJ), 0)
    o3 = 3 * D_MODEL

    def hsum(v):
        s_lo = jnp.sum(jnp.where(lo, v, 0.0), axis=1, keepdims=True)
        s_hi = jnp.sum(jnp.where(lo, 0.0, v), axis=1, keepdims=True)
        return jnp.where(lo, s_lo, s_hi)

    def bd(v):
        return _blockdiag(v, lo)

    def cat0(*a):
        return jnp.concatenate(a, axis=0)

    def cat1(*a):
        return jnp.concatenate(a, axis=1)

    valid_w = lax.broadcasted_iota(jnp.int32, (C, D_MODEL), 0) < t_valid
    xs, logw_all, cum_all, asig_all, g_all = [], [], [], [], []
    for bl in range(nb):
        p = p_ref[bl]
        prev = jnp.where(row_w == 0, carry_scr[bl], pltpu.roll(p, 1, axis=0))
        carry_scr[bl] = p[C - 1:C, :]
        x = p + (prev - p) * mu_ref[...]
        xw = x[:, o3:o3 + DECAY_LORA]
        xa = x[:, o3 + DECAY_LORA:o3 + DECAY_LORA + AAA_LORA]
        xg = x[:, o3 + DECAY_LORA + AAA_LORA:]
        z = w0_ref[...] + _bdot(jnp.tanh(xw), dup_ref[...])
        logw_full = -math.exp(-0.5) * _sigmoid(z)
        if t_valid < C:
            logw_full = jnp.where(valid_w, logw_full, 0.0)
        xs.append(x)
        logw_all.append(logw_full)
        cum_all.append(_dot_exact_lhs(tri, logw_full))
        asig_all.append(_sigmoid(a0_ref[...] + _bdot(xa, aup_ref[...])))
        g_all.append(_bdot(_sigmoid(xg), gup_ref[...]))

    units = [(bl, pr) for bl in range(nb) for pr in range(RWKV_PAIRS)]
    U = range(len(units))
    cols = [slice(pr * LANES, (pr + 1) * LANES) for _, pr in units]
    r = [xs[bl][:, pr * LANES:(pr + 1) * LANES] for bl, pr in units]
    k = [xs[bl][:, D_MODEL + pr * LANES:D_MODEL + (pr + 1) * LANES] for bl, pr in units]
    v = [xs[bl][:, 2 * D_MODEL + pr * LANES:2 * D_MODEL + (pr + 1) * LANES] for bl, pr in units]
    logw = [logw_all[bl][:, cols[u]] for u, (bl, _) in enumerate(units)]
    asig = [asig_all[bl][:, cols[u]] for u, (bl, _) in enumerate(units)]
    kk = [k[u] * kk_ref[:, cols[u]] for u in U]
    k2 = [k[u] * (1.0 + (asig[u] - 1.0) * ka_ref[:, cols[u]]) for u in U]
    ss = [hsum(kk[u] * kk[u]) for u in U]
    cum = [cum_all[bl][:, cols[u]] for u, (bl, _) in enumerate(units)]
    bsum = [hsum(r[u] * k2[u] * rk_ref[:, cols[u]]) for u in U]
    kk = [kk[u] * lax.rsqrt(jnp.maximum(ss[u], 1e-24)) for u in U]
    bonus = [bsum[u] * v[u] for u in U]
    if t_valid < C:
        kk = [jnp.where(valid, a, 0.0) for a in kk]
        k2 = [jnp.where(valid, a, 0.0) for a in k2]
        v = [jnp.where(valid, a, 0.0) for a in v]
        r = [jnp.where(valid, a, 0.0) for a in r]
    b_vec = [kk[u] * asig[u] for u in U]
    clast = [cum[u][C - 1:C, :] for u in U]
    e_inv = [jnp.exp(-cum[u]) for u in U]
    e_tail = [jnp.exp(clast[u] - cum[u]) for u in U]
    aq = [-kk[u] * jnp.exp(cum[u] - logw[u]) for u in U]
    rq = [r[u] * jnp.exp(cum[u]) for u in U]
    bk = [b_vec[u] * e_inv[u] for u in U]
    kx = [k2[u] * e_inv[u] for u in U]
    bb = [b_vec[u] * e_tail[u] for u in U]
    kb = [k2[u] * e_tail[u] for u in U]

    g4 = [_bdot_nt(cat0(aq[u], rq[u]), cat0(bd(bk[u]), bd(kx[u]))) for u in U]
    m_ab = [jnp.where(strict, g4[u][:C, :LANES], 0.0) for u in U]
    m_ak = [jnp.where(strict, g4[u][:C, LANES:], 0.0) for u in U]
    m_rb = [jnp.where(incl, g4[u][C:, :LANES], 0.0) for u in U]
    m_rk = [jnp.where(incl, g4[u][C:, LANES:], 0.0) for u in U]

    t_inv = [eye + m_ab[u] for u in U]
    l_pow = [_bdot(m_ab[u], bd(m_ab[u])) for u in U]
    mv = [_bdot(cat0(m_ak[u], m_rk[u]), bd(v[u])) for u in U]
    n_steps = C.bit_length() - 1
    for step in range(1, n_steps):
        if step < n_steps - 1:
            both = [_bdot(l_pow[u], cat1(bd(t_inv[u]), bd(l_pow[u]))) for u in U]
            t_inv = [t_inv[u] + both[u][:, :LANES] for u in U]
            l_pow = [both[u][:, LANES:] for u in U]
        else:
            t_inv = [t_inv[u] + _bdot(l_pow[u], bd(t_inv[u])) for u in U]

    au = [_bdot(t_inv[u], cat1(bd(aq[u]), bd(mv[u][:C]))) for u in U]
    a_bar = [au[u][:, :LANES] for u in U]
    u_bar = [au[u][:, LANES:] for u in U]
    mau = [_bdot(m_rb[u], cat1(bd(a_bar[u]), bd(u_bar[u]))) for u in U]
    p_full = [_bdot_tn(a_bar[u], bb[u]) for u in U]
    q_full = [_bdot_tn(cat0(u_bar[u], v[u]), cat0(bb[u], kb[u])) for u in U]
    r_til = [rq[u] + mau[u][:, :LANES] for u in U]
    y_loc = [mau[u][:, LANES:] + mv[u][C:] for u in U]
    p_bd = [jnp.where(same_head, p_full[u], 0.0) + jnp.where(diag, jnp.exp(clast[u]), 0.0) for u in U]
    q_pk = [jnp.where(lo[:N], q_full[u][:N], q_full[u][N:]) for u in U]

    s_old = [s_scr[bl, pr] for bl, pr in units]
    y = [_bdot_nt(r_til[u], _blockdiag(s_old[u], lo[:N])) + y_loc[u] for u in U]
    s_new = [_bdot(s_old[u], p_bd[u]) + q_pk[u] for u in U]
    for u, (bl, pr) in enumerate(units):
        s_scr[bl, pr] = s_new[u]

    mu_y = [hsum(y[u]) * (1.0 / N) for u in U]
    yc = [y[u] - mu_y[u] for u in U]
    var = [hsum(yc[u] * yc[u]) * (1.0 / N) for u in U]
    for u, (bl, pr) in enumerate(units):
        yn = yc[u] * lax.rsqrt(var[u] + LN_X_EPS) * lng_ref[:, cols[u]] + lnb_ref[:, cols[u]]
        y_ref[bl, :, cols[u]] = (yn + bonus[u]) * g_all[bl][:, cols[u]]

    @pl.when(c == n_chunks - 1)
    def _():
        for bl in range(nb):
            for pr in range(RWKV_PAIRS):
                sout_ref[bl, 2 * pr] = s_scr[bl, pr][:, :N]
                sout_ref[bl, 2 * pr + 1] = s_scr[bl, pr][:, N:]


def _rwkv(p_rwkv, shift_prev, s0, mu, w0, decay_up, a0, aaa_up, gate_up, k_k, k_a, r_k,
          ln_g, ln_b, *, t_valid, nb):
    b, t, _ = p_rwkv.shape
    row = lambda a: a.reshape(1, -1)
    full = lambda shape: pl.BlockSpec(shape, lambda i, j: (0,) * len(shape))
    state_spec = pl.BlockSpec((nb, RWKV_HEADS, RWKV_HEAD_DIM, RWKV_HEAD_DIM), lambda i, j: (i, 0, 0, 0))
    return pl.pallas_call(
        functools.partial(_rwkv_kernel, t_valid=t_valid, nb=nb),
        out_shape=(jax.ShapeDtypeStruct((b, t, D_MODEL), F32),
                   jax.ShapeDtypeStruct((b, RWKV_HEADS, RWKV_HEAD_DIM, RWKV_HEAD_DIM), F32)),
        grid=(b // nb, t // CHUNK),
        in_specs=[pl.BlockSpec((nb, CHUNK, RWKV_PROJ), lambda i, j: (i, j, 0)),
                  pl.BlockSpec((nb, 1, RWKV_PROJ), lambda i, j: (i, 0, 0)),
                  state_spec,
                  full((1, RWKV_PROJ)), full((1, D_MODEL)), full((DECAY_LORA, D_MODEL)),
                  full((1, D_MODEL)), full((AAA_LORA, D_MODEL)), full((GATE_LORA, D_MODEL)),
                  full((1, D_MODEL)), full((1, D_MODEL)), full((1, D_MODEL)),
                  full((1, D_MODEL)), full((1, D_MODEL))],
        out_specs=(pl.BlockSpec((nb, CHUNK, D_MODEL), lambda i, j: (i, j, 0)), state_spec),
        scratch_shapes=[pltpu.VMEM((nb, RWKV_PAIRS, RWKV_HEAD_DIM, LANES), F32),
                        pltpu.VMEM((nb, 1, RWKV_PROJ), F32)],
        compiler_params=pltpu.CompilerParams(dimension_semantics=("parallel", "arbitrary"),
                                             vmem_limit_bytes=VMEM_LIMIT),
        name="rwkv_scan",
    )(p_rwkv, shift_prev, s0, row(mu), row(w0), decay_up, row(a0), aaa_up, gate_up,
      row(k_k), row(k_a), row(r_k), row(ln_g), row(ln_b))


def _gla_kernel(p_ref, s0_ref, aup_ref, ab_ref, ng_ref, y_ref, sout_ref, s_scr, *, t_valid, nb):
    c = pl.program_id(1)
    n_chunks = pl.num_programs(1)
    C = CHUNK
    units = [(bl, h) for bl in range(nb) for h in range(GLA_HEADS)]
    U = range(len(units))

    @pl.when(c == 0)
    def _():
        for bl, h in units:
            s_scr[bl, h] = s0_ref[bl, h].T

    o_k = GLA_DK_TOTAL
    o_v = 2 * GLA_DK_TOTAL
    o_r = o_v + GLA_DV_TOTAL
    o_a = o_r + GLA_DV_TOTAL
    tri = (lax.broadcasted_iota(jnp.int32, (C, C), 1)
           <= lax.broadcasted_iota(jnp.int32, (C, C), 0))
    tri_bf = tri.astype(BF16)
    valid_g = lax.broadcasted_iota(jnp.int32, (C, GLA_DK_TOTAL), 0) < t_valid
    valid_k = lax.broadcasted_iota(jnp.int32, (C, GLA_DK), 0) < t_valid
    valid_v = lax.broadcasted_iota(jnp.int32, (C, GLA_DV), 0) < t_valid

    ps, cums = [], []
    for bl in range(nb):
        p = p_ref[bl]
        logit = _dot3(p[:, o_a:], aup_ref[...]) + ab_ref[...]
        g = (jnp.minimum(logit, 0.0) - jnp.log(1.0 + jnp.exp(-jnp.abs(logit)))) * (1.0 / GLA_GATE_NORM)
        if t_valid < C:
            g = jnp.where(valid_g, g, 0.0)
        ps.append(p)
        cums.append(_dot_exact_lhs(tri_bf, g))

    q = [ps[bl][:, h * GLA_DK:(h + 1) * GLA_DK] * (GLA_DK ** -0.5) for bl, h in units]
    k = [ps[bl][:, o_k + h * GLA_DK:o_k + (h + 1) * GLA_DK] for bl, h in units]
    v = [ps[bl][:, o_v + h * GLA_DV:o_v + (h + 1) * GLA_DV] for bl, h in units]
    if t_valid < C:
        k = [jnp.where(valid_k, a, 0.0) for a in k]
        v = [jnp.where(valid_v, a, 0.0) for a in v]
    b = [cums[bl][:, h * GLA_DK:(h + 1) * GLA_DK] for bl, h in units]
    clast = [b[u][C - 1:C, :] for u in U]
    cmid = [b[u][C // 2 - 1:C // 2, :] for u in U]
    att = [_bdot_nt(q[u] * jnp.exp(b[u] - cmid[u]), k[u] * jnp.exp(cmid[u] - b[u])) for u in U]
    att = [jnp.where(tri, a, 0.0) for a in att]
    s_t = [s_scr[bl, h] for bl, h in units]
    o = [_bdot(att[u], v[u]) + _bdot_nt(q[u] * jnp.exp(b[u]), s_t[u]) for u in U]
    s_new = [s_t[u] * jnp.exp(clast[u]) + _bdot_tn(v[u], k[u] * jnp.exp(clast[u] - b[u])) for u in U]
    for u, (bl, h) in enumerate(units):
        s_scr[bl, h] = s_new[u]
        r = ps[bl][:, o_r + h * GLA_DV:o_r + (h + 1) * GLA_DV]
        on = o[u] * lax.rsqrt(jnp.mean(o[u] * o[u], axis=-1, keepdims=True) + EPS) * ng_ref[...]
        y_ref[bl, :, h * GLA_DV:(h + 1) * GLA_DV] = on * _silu(r)

    @pl.when(c == n_chunks - 1)
    def _():
        for bl, h in units:
            sout_ref[bl, h] = s_scr[bl, h].T


def _gla(p_gla, s0, alpha_up_pad, alpha_b, norm_g, *, t_valid, nb):
    b, t, _ = p_gla.shape
    full = lambda shape: pl.BlockSpec(shape, lambda i, j: (0,) * len(shape))
    state_spec = pl.BlockSpec((nb, GLA_HEADS, GLA_DK, GLA_DV), lambda i, j: (i, 0, 0, 0))
    return pl.pallas_call(
        functools.partial(_gla_kernel, t_valid=t_valid, nb=nb),
        out_shape=(jax.ShapeDtypeStruct((b, t, GLA_DV_TOTAL), F32),
                   jax.ShapeDtypeStruct((b, GLA_HEADS, GLA_DK, GLA_DV), F32)),
        grid=(b // nb, t // CHUNK),
        in_specs=[pl.BlockSpec((nb, CHUNK, GLA_PROJ_PAD), lambda i, j: (i, j, 0)),
                  state_spec,
                  full((LANES, GLA_DK_TOTAL)), full((1, GLA_DK_TOTAL)), full((1, GLA_DV))],
        out_specs=(pl.BlockSpec((nb, CHUNK, GLA_DV_TOTAL), lambda i, j: (i, j, 0)), state_spec),
        scratch_shapes=[pltpu.VMEM((nb, GLA_HEADS, GLA_DV, GLA_DK), F32)],
        compiler_params=pltpu.CompilerParams(dimension_semantics=("parallel", "arbitrary"),
                                             vmem_limit_bytes=VMEM_LIMIT),
        name="gla_scan",
    )(p_gla, s0, alpha_up_pad, alpha_b.reshape(1, -1), norm_g.reshape(1, -1))


def _merge_kernel(yr_ref, yg_ref, gate_ref, x_ref, mod_ref, wbr_ref, wbg_ref, wout_ref, g2_ref,
                  router_ref, x1_ref, h2_ref, route_ref):
    mod = mod_ref[0]
    gates = gate_ref[0]
    merged = (gates[:, :D_MODEL] * _bdot(yr_ref[0], wbr_ref[...])
              + gates[:, D_MODEL:] * _bdot(yg_ref[0], wbg_ref[...]))
    x1 = x_ref[0] + mod[2:3] * _bdot(merged, wout_ref[...])
    x1_ref[0] = x1
    h2 = _rms_scale(x1) * g2_ref[...] * (1.0 + mod[4:5]) + mod[3:4]
    h2_ref[0] = h2

    logits = _dot3(h2, router_ref[...])
    tm = logits.shape[0]
    lane = lax.broadcasted_iota(jnp.int32, (tm, LANES), 1)
    neg = -jnp.inf
    is_group = (lane >= N_EXPERTS) & (lane < N_EXPERTS + N_GROUPS)
    gl = jnp.where(is_group, logits, neg)
    gmax = jnp.max(gl, axis=1, keepdims=True)
    g_idx = jnp.min(jnp.where(gl == gmax, lane, LANES), axis=1, keepdims=True) - N_EXPERTS
    p_g = 1.0 / jnp.sum(jnp.exp(gl - gmax), axis=1, keepdims=True)
    in_group = (lane >= g_idx * EXPERTS_PER_GROUP) & (lane < (g_idx + 1) * EXPERTS_PER_GROUP)
    el = jnp.where(in_group, logits, neg)
    v1 = jnp.max(el, axis=1, keepdims=True)
    i1 = jnp.min(jnp.where(el == v1, lane, LANES), axis=1, keepdims=True)
    el2 = jnp.where(lane == i1, neg, el)
    v2 = jnp.max(el2, axis=1, keepdims=True)
    i2 = jnp.min(jnp.where(el2 == v2, lane, LANES), axis=1, keepdims=True)
    e21 = jnp.exp(v2 - v1)
    w1 = p_g / (1.0 + e21)
    route_ref[0] = (jnp.where(lane == 0, i1.astype(F32), 0.0) + jnp.where(lane == 1, i2.astype(F32), 0.0)
                    + jnp.where(lane == 2, w1, 0.0) + jnp.where(lane == 3, w1 * e21, 0.0))


def _merge(y_r, y_g, gates, x, mod, wbr, wbg, wout, norm2_g, router_cat, *, tm):
    b, t, _ = x.shape
    full = lambda shape: pl.BlockSpec(shape, lambda i, j: (0,) * len(shape))
    tile = lambda n: pl.BlockSpec((1, tm, n), lambda i, j: (i, j, 0))
    return pl.pallas_call(
        _merge_kernel,
        out_shape=(jax.ShapeDtypeStruct((b, t, D_MODEL), F32),
                   jax.ShapeDtypeStruct((b, t, D_MODEL), F32),
                   jax.ShapeDtypeStruct((b, t, LANES), F32)),
        grid=(b, t // tm),
        in_specs=[tile(D_MODEL), tile(D_MODEL), tile(GATE_PROJ), tile(D_MODEL),
                  pl.BlockSpec((1, N_MOD, D_MODEL), lambda i, j: (i, 0, 0)),
                  full((D_MODEL, D_MODEL)), full((D_MODEL, D_MODEL)), full((D_MODEL, D_MODEL)),
                  full((1, D_MODEL)), full((D_MODEL, LANES))],
        out_specs=(tile(D_MODEL), tile(D_MODEL), tile(LANES)),
        compiler_params=pltpu.CompilerParams(dimension_semantics=("parallel", "parallel"),
                                             vmem_limit_bytes=VMEM_LIMIT),
        name="merge_router",
    )(y_r, y_g, gates, x, mod, wbr, wbg, wout, norm2_g.reshape(1, -1), router_cat)


def _moe_plan(expert_ids, n_tok):
    tm = MOE_TILE
    n_assign = 2 * n_tok
    n_tiles = -(-n_assign // tm) + N_EXPERTS
    keys = expert_ids.T.reshape(-1)
    a_iota = jnp.arange(n_assign, dtype=jnp.int32)
    _, order = lax.top_k(-(keys * n_assign + a_iota).astype(F32), n_assign)
    e_iota = jnp.arange(N_EXPERTS, dtype=jnp.int32)
    counts = jnp.sum((keys[None, :] == e_iota[:, None]).astype(jnp.int32), axis=1)
    tiles_e = (counts + tm - 1) // tm
    tile_end = jnp.cumsum(tiles_e)
    row_start = jnp.cumsum(counts) - counts
    n_used = tile_end[-1]
    t_idx = jnp.arange(n_tiles, dtype=jnp.int32)
    t_eff = jnp.minimum(t_idx, n_used - 1)
    tile_e = jnp.minimum(jnp.sum((t_eff[:, None] >= tile_end[None, :]).astype(jnp.int32), axis=1),
                         N_EXPERTS - 1)
    onehot = tile_e[:, None] == e_iota[None, :]
    pick = lambda vec: jnp.sum(jnp.where(onehot, vec[None, :], 0), axis=1)
    local = t_idx - pick(tile_end - tiles_e)
    n_valid = jnp.where(t_idx < n_used, jnp.clip(pick(counts) - local * tm, 0, tm), 0)
    j_idx = jnp.arange(tm, dtype=jnp.int32)[None, :]
    pos = jnp.clip(pick(row_start)[:, None] + local[:, None] * tm + j_idx, 0, n_assign - 1)
    valid = j_idx < n_valid[:, None]
    a = order[pos]
    tok = jnp.where(valid, a % n_tok, 0)
    dump = n_assign + (t_idx[:, None] % 2) * tm + j_idx
    dst = jnp.where(valid, a, dump)
    return (tile_e, n_used.reshape(1).astype(jnp.int32),
            tok.reshape(n_tiles, 1, tm), dst.reshape(n_tiles, 1, tm))


def _moe_group_kernel(tile_e_ref, n_used_ref, tok_ref, tok_next_ref, dst_ref, h2_hbm, wg_ref, wu_ref,
                      wd_ref, out_hbm, xbuf, obuf, wg_bf, wu_bf, wd_bf, gsem, ssem):
    t = pl.program_id(0)
    n_t = pl.num_programs(0)
    tm = MOE_TILE
    slot = lax.rem(t, 2)
    n_used = n_used_ref[0]

    def gather(idx_ref, s):
        for j in range(tm):
            pltpu.make_async_copy(h2_hbm.at[pl.ds(idx_ref[0, 0, j], 1)], xbuf.at[s, pl.ds(j, 1)],
                                  gsem.at[s]).start()

    def wait_gather(s):
        pltpu.make_async_copy(h2_hbm.at[pl.ds(0, tm)], xbuf.at[s], gsem.at[s]).wait()

    def wait_scatter(s):
        pltpu.make_async_copy(obuf.at[s], out_hbm.at[pl.ds(0, tm)], ssem.at[s]).wait()

    @pl.when(t == 0)
    def _():
        n_rows = out_hbm.shape[0]
        obuf[1] = jnp.zeros((tm, D_MODEL), F32)
        for half in range(2):
            fill = pltpu.make_async_copy(obuf.at[1], out_hbm.at[pl.ds(n_rows - (2 - half) * tm, tm)],
                                         ssem.at[1])
            fill.start()
            fill.wait()

    @pl.when((t == 0) & (n_used > 0))
    def _():
        gather(tok_ref, 0)

    @pl.when((t >= 2) & (t - 2 < n_used))
    def _():
        wait_scatter(slot)

    @pl.when(t < n_used)
    def _():
        @pl.when((t == 0) | (tile_e_ref[t] != tile_e_ref[jnp.maximum(t - 1, 0)]))
        def _():
            wg_bf[...] = wg_ref[0].astype(BF16)
            wu_bf[...] = wu_ref[0].astype(BF16)
            wd_bf[...] = wd_ref[0].astype(BF16)

        wait_gather(slot)
        gather(tok_next_ref, 1 - slot)
        x = xbuf[slot].astype(BF16)
        hid = _silu(jnp.dot(x, wg_bf[...], preferred_element_type=F32)) * jnp.dot(
            x, wu_bf[...], preferred_element_type=F32)
        obuf[slot] = _bdot(hid, wd_bf[...])

        for j in range(tm):
            pltpu.make_async_copy(obuf.at[slot, pl.ds(j, 1)], out_hbm.at[pl.ds(dst_ref[0, 0, j], 1)],
                                  ssem.at[slot]).start()

    @pl.when((t < n_used) & (t + 1 >= n_used))
    def _():
        wait_gather(1 - slot)

    @pl.when(t == n_t - 1)
    def _():
        @pl.when(t - 1 < n_used)
        def _():
            wait_scatter(1 - slot)

        @pl.when(t < n_used)
        def _():
            wait_scatter(slot)


def _moe_grouped(h2_all, expert_ids, wg, wu, wd):
    return _moe_call(h2_all, *_moe_plan(expert_ids, h2_all.shape[0]), wg, wu, wd)


def _moe_call(h2_all, tile_e, n_used, tok, dst, wg, wu, wd):
    n_tok = h2_all.shape[0]
    tm = MOE_TILE
    n_tiles = tile_e.shape[0]
    smem_tile = lambda f: pl.BlockSpec((1, 1, tm), f, memory_space=pltpu.SMEM)
    w_spec = lambda shape: pl.BlockSpec(shape, lambda t, te, nu: (te[t], 0, 0))
    grid_spec = pltpu.PrefetchScalarGridSpec(
        num_scalar_prefetch=2,
        grid=(n_tiles,),
        in_specs=[smem_tile(lambda t, te, nu: (t, 0, 0)),
                  smem_tile(lambda t, te, nu: (jnp.minimum(t + 1, n_tiles - 1), 0, 0)),
                  smem_tile(lambda t, te, nu: (t, 0, 0)),
                  pl.BlockSpec(memory_space=pl.ANY),
                  w_spec((1, D_MODEL, EXPERT_FF)), w_spec((1, D_MODEL, EXPERT_FF)),
                  w_spec((1, EXPERT_FF, D_MODEL))],
        out_specs=pl.BlockSpec(memory_space=pl.ANY),
        scratch_shapes=[pltpu.VMEM((2, tm, D_MODEL), F32), pltpu.VMEM((2, tm, D_MODEL), F32),
                        pltpu.VMEM((D_MODEL, EXPERT_FF), BF16), pltpu.VMEM((D_MODEL, EXPERT_FF), BF16),
                        pltpu.VMEM((EXPERT_FF, D_MODEL), BF16),
                        pltpu.SemaphoreType.DMA((2,)), pltpu.SemaphoreType.DMA((2,))],
    )
    return pl.pallas_call(
        _moe_group_kernel,
        out_shape=jax.ShapeDtypeStruct((2 * n_tok + 2 * tm, D_MODEL), F32),
        grid_spec=grid_spec,
        compiler_params=pltpu.CompilerParams(dimension_semantics=("arbitrary",),
                                             vmem_limit_bytes=VMEM_LIMIT),
        name="moe_grouped",
    )(tile_e, n_used, tok, tok, dst, h2_all, wg, wu, wd)


def _final_kernel(x1_ref, o0_ref, o1_ref, route_ref, mod_ref, fg_ref, y_ref):
    route = route_ref[...]
    lane = lax.broadcasted_iota(jnp.int32, route.shape, 1)
    w1 = jnp.sum(jnp.where(lane == 2, route, 0.0), axis=1, keepdims=True)
    w2 = jnp.sum(jnp.where(lane == 3, route, 0.0), axis=1, keepdims=True)
    x2 = x1_ref[0] + mod_ref[0][5:6] * (w1 * o0_ref[...] + w2 * o1_ref[...])
    y_ref[0] = _rms_scale(x2) * fg_ref[...]


def _final(x1, moe_out, route_all, mod, final_g, *, row_off, tm):
    b, t, _ = x1.shape
    n_tok = route_all.shape[0]
    blk0 = row_off // tm
    blk1 = (n_tok + row_off) // tm
    per_seq = t // tm
    return pl.pallas_call(
        _final_kernel,
        out_shape=jax.ShapeDtypeStruct((b, t, D_MODEL), F32),
        grid=(b, per_seq),
        in_specs=[pl.BlockSpec((1, tm, D_MODEL), lambda i, j: (i, j, 0)),
                  pl.BlockSpec((tm, D_MODEL), lambda i, j: (blk0 + i * per_seq + j, 0)),
                  pl.BlockSpec((tm, D_MODEL), lambda i, j: (blk1 + i * per_seq + j, 0)),
                  pl.BlockSpec((tm, LANES), lambda i, j: (blk0 + i * per_seq + j, 0)),
                  pl.BlockSpec((1, N_MOD, D_MODEL), lambda i, j: (i, 0, 0)),
                  pl.BlockSpec((1, D_MODEL), lambda i, j: (0, 0))],
        out_specs=pl.BlockSpec((1, tm, D_MODEL), lambda i, j: (i, j, 0)),
        compiler_params=pltpu.CompilerParams(dimension_semantics=("parallel", "parallel"),
                                             vmem_limit_bytes=VMEM_LIMIT),
        name="moe_combine_norm",
    )(x1, moe_out, moe_out, route_all, mod, final_g.reshape(1, -1))


def _pad_time(a, t_pad):
    t = a.shape[1]
    return a if t == t_pad else jnp.pad(a, ((0, 0), (0, t_pad - t), (0, 0)))


def _mixer(x, mod, shift0, s_rwkv0, s_gla0, wts):
    b, t, _ = x.shape
    tm = min(t, 512)
    p_rwkv = _norm_proj(x, mod, wts["norm1_g"], wts["w_in_rwkv"], gate=False, tm=tm)
    p_gla = _norm_proj(x, mod, wts["norm1_g"], wts["w_in_gla"], gate=False, tm=tm)
    gates = _norm_proj(x, mod, wts["norm1_g"], wts["w_in_gate"], gate=True, tm=tm)

    t_pad = -(-t // CHUNK) * CHUNK
    y_r, s_rwkv = _rwkv(_pad_time(p_rwkv, t_pad), shift0, s_rwkv0,
                        wts["rwkv_mu"], wts["rwkv_w0"], wts["rwkv_decay_up"], wts["rwkv_a0"],
                        wts["rwkv_aaa_up"], wts["rwkv_gate_up"], wts["rwkv_k_k"], wts["rwkv_k_a"],
                        wts["rwkv_r_k"], wts["rwkv_ln_g"], wts["rwkv_ln_b"], t_valid=min(t, CHUNK),
                        nb=RWKV_SEQS_PER_STEP)
    y_g, s_gla = _gla(_pad_time(p_gla, t_pad), s_gla0, wts["gla_alpha_up"],
                      wts["gla_alpha_b"], wts["gla_norm_g"], t_valid=min(t, CHUNK), nb=GLA_SEQS_PER_STEP)
    y_r = y_r[:, :t]
    y_g = y_g[:, :t]

    x1, h2, route = _merge(y_r, y_g, gates, x, mod, wts["w_branch_rwkv"], wts["w_branch_gla"],
                           wts["w_out"], wts["norm2_g"], wts["router_cat"], tm=tm)
    shift = p_rwkv[:, t - 1:t, :]
    states = (shift[None], s_rwkv[None], s_gla[None])
    return x1, h2, route, states


def kernel(x_prompt, x_sample, c_prompt, c_sample, state_rwkv_shift, state_rwkv, state_gla, w_ada, b_ada, norm1_g, w_in, rwkv_mu, rwkv_w0, rwkv_decay_up, rwkv_a0, rwkv_aaa_up, rwkv_gate_up, rwkv_k_k, rwkv_k_a, rwkv_r_k, rwkv_ln_g, rwkv_ln_b, gla_alpha_up, gla_alpha_b, gla_norm_g, w_branch_rwkv, w_branch_gla, w_out, norm2_g, router_group, router_expert, expert_w_gate, expert_w_up, expert_w_down, final_norm_g):
    bp = x_prompt.shape[0]
    bs = x_sample.shape[0]
    w_in0 = w_in[0]
    g0 = RWKV_PROJ
    o1, o2, o3, o4 = (GLA_DK_TOTAL, 2 * GLA_DK_TOTAL, 2 * GLA_DK_TOTAL + GLA_DV_TOTAL,
                      2 * GLA_DK_TOTAL + GLA_DV_TOTAL + GLA_GATE_RANK)
    w_gla = w_in0[:, g0:g0 + GLA_PROJ]
    w_gla = jnp.concatenate([w_gla[:, :o3], w_gla[:, o4:], w_gla[:, o3:o4],
                             jnp.zeros((D_MODEL, LANES - GLA_GATE_RANK), F32)], axis=1)
    wts = dict(
        norm1_g=norm1_g[0].reshape(1, -1),
        w_in_rwkv=w_in0[:, :g0].astype(BF16),
        w_in_gla=w_gla.astype(BF16),
        w_in_gate=w_in0[:, g0 + GLA_PROJ:].astype(BF16),
        rwkv_mu=rwkv_mu[0], rwkv_w0=rwkv_w0[0], rwkv_decay_up=rwkv_decay_up[0], rwkv_a0=rwkv_a0[0],
        rwkv_aaa_up=rwkv_aaa_up[0], rwkv_gate_up=rwkv_gate_up[0], rwkv_k_k=rwkv_k_k[0],
        rwkv_k_a=rwkv_k_a[0], rwkv_r_k=rwkv_r_k[0], rwkv_ln_g=rwkv_ln_g[0], rwkv_ln_b=rwkv_ln_b[0],
        gla_alpha_up=jnp.pad(gla_alpha_up[0], ((0, LANES - GLA_GATE_RANK), (0, 0))),
        gla_alpha_b=gla_alpha_b[0], gla_norm_g=gla_norm_g[0],
        w_branch_rwkv=w_branch_rwkv[0].astype(BF16), w_branch_gla=w_branch_gla[0].astype(BF16),
        w_out=w_out[0].astype(BF16), norm2_g=norm2_g[0],
        router_cat=jnp.pad(jnp.concatenate([router_expert[0], router_group[0]], axis=1),
                           ((0, 0), (0, LANES - N_EXPERTS - N_GROUPS))),
        expert_w_gate=expert_w_gate[0], expert_w_up=expert_w_up[0], expert_w_down=expert_w_down[0],
        final_norm_g=final_norm_g,
    )
    mod = _ada(jnp.concatenate([c_prompt, c_sample], axis=0), w_ada[0], b_ada[0])
    mod = mod.reshape(bp + bs, N_MOD, D_MODEL)

    dt = x_prompt.dtype
    x1_p, h2_p, route_p, states_p = _mixer(
        x_prompt, mod[:bp], jnp.zeros((bp, 1, RWKV_PROJ), dt),
        jnp.zeros((bp, RWKV_HEADS, RWKV_HEAD_DIM, RWKV_HEAD_DIM), state_rwkv.dtype),
        jnp.zeros((bp, GLA_HEADS, GLA_DK, GLA_DV), state_gla.dtype), wts)
    x1_s, h2_s, route_s, states_s = _mixer(
        x_sample, mod[bp:], state_rwkv_shift[0], state_rwkv[0], state_gla[0], wts)

    n_p = bp * x_prompt.shape[1]
    n_s = bs * x_sample.shape[1]
    n_tok = n_p + n_s
    h2_all = jnp.concatenate([h2_p.reshape(n_p, D_MODEL), h2_s.reshape(n_s, D_MODEL)], axis=0)
    route_all = jnp.concatenate([route_p.reshape(n_p, LANES), route_s.reshape(n_s, LANES)], axis=0)
    expert_ids = route_all[:, :2].astype(jnp.int32)
    moe_out = _moe_grouped(h2_all, expert_ids, wts["expert_w_gate"], wts["expert_w_up"],
                           wts["expert_w_down"])
    y_p = _final(x1_p, moe_out, route_all, mod[:bp], final_norm_g, row_off=0,
                 tm=min(x_prompt.shape[1], 512))
    y_s = _final(x1_s, moe_out, route_all, mod[bp:], final_norm_g, row_off=n_p,
                 tm=min(x_sample.shape[1], 512))
    return (y_p, y_s) + states_p + states_s
```

```python
import functools
import math

import jax
import jax.numpy as jnp
from jax import lax
from jax.experimental import pallas as pl
from jax.experimental.pallas import tpu as pltpu

F32 = jnp.float32
BF16 = jnp.bfloat16

D_MODEL = 1024
N_MOD = 6
EPS = 1e-6
RWKV_HEAD_DIM = 64
RWKV_HEADS = 16
RWKV_PAIRS = RWKV_HEADS // 2
DECAY_LORA = 64
AAA_LORA = 64
GATE_LORA = 128
RWKV_PROJ = 3 * D_MODEL + DECAY_LORA + AAA_LORA + GATE_LORA
LN_X_EPS = 64e-5
GLA_HEADS = 4
GLA_DK = 128
GLA_DV = 256
GLA_DK_TOTAL = GLA_HEADS * GLA_DK
GLA_DV_TOTAL = GLA_HEADS * GLA_DV
GLA_GATE_RANK = 16
GLA_GATE_NORM = 16.0
GLA_PROJ = 2 * GLA_DK_TOTAL + 2 * GLA_DV_TOTAL + GLA_GATE_RANK
LANES = 128
GLA_PROJ_PAD = 2 * GLA_DK_TOTAL + 2 * GLA_DV_TOTAL + LANES
GATE_PROJ = 2 * D_MODEL
N_GROUPS = 4
EXPERTS_PER_GROUP = 8
N_EXPERTS = 32
EXPERT_FF = 512

CHUNK = 64
ROW_TILE = 512
MOE_TILE = 256
RWKV_SEQS_PER_STEP = 2
GLA_SEQS_PER_STEP = 4
VMEM_LIMIT = 56 * 1024 * 1024


def _bdot(a, b):
    return jnp.dot(a.astype(BF16), b.astype(BF16), preferred_element_type=F32)


def _bdot_nt(a, b):
    return lax.dot_general(a.astype(BF16), b.astype(BF16), (((1,), (1,)), ((), ())),
                           preferred_element_type=F32)


def _bdot_tn(a, b):
    return jnp.dot(a.T.astype(BF16), b.astype(BF16), preferred_element_type=F32)


def _split3(x):
    h1 = x.astype(BF16)
    r1 = x - h1.astype(F32)
    h2 = r1.astype(BF16)
    h3 = (r1 - h2.astype(F32)).astype(BF16)
    return h1, h2, h3


def _dot3(a, b):
    a1, a2, _ = _split3(a)
    b1, b2, _ = _split3(b)
    return (jnp.dot(a1, b1, preferred_element_type=F32)
            + jnp.dot(a2, b1, preferred_element_type=F32)
            + jnp.dot(a1, b2, preferred_element_type=F32))


def _dot_exact_lhs(a_bf16, x):
    x1, x2, _ = _split3(x)
    return (jnp.dot(a_bf16, x1, preferred_element_type=F32)
            + jnp.dot(a_bf16, x2, preferred_element_type=F32))


def _split_hi_lo(w):
    hi, lo, _ = _split3(w)
    return jnp.concatenate([hi, lo], axis=1)


def _sigmoid(x):
    return 1.0 / (1.0 + jnp.exp(-x))


def _silu(x):
    return x * _sigmoid(x)


def _rms_scale(x):
    return x * lax.rsqrt(jnp.mean(x * x, axis=-1, keepdims=True) + EPS)


def _ada_kernel(c_ref, w_ref, b_ref, o_ref):
    o_ref[...] = _dot3(_silu(c_ref[...]), w_ref[...]) + b_ref[...]


def _ada(c, w_ada, b_ada):
    n_rows = c.shape[0]
    n_out = w_ada.shape[1]
    tn = 1536
    return pl.pallas_call(
        _ada_kernel,
        out_shape=jax.ShapeDtypeStruct((n_rows, n_out), F32),
        grid=(n_out // tn,),
        in_specs=[pl.BlockSpec((n_rows, D_MODEL), lambda j: (0, 0)),
                  pl.BlockSpec((D_MODEL, tn), lambda j: (0, j)),
                  pl.BlockSpec((1, tn), lambda j: (0, j))],
        out_specs=pl.BlockSpec((n_rows, tn), lambda j: (0, j)),
        compiler_params=pltpu.CompilerParams(dimension_semantics=("parallel",),
                                             vmem_limit_bytes=VMEM_LIMIT),
        name="ada",
    )(c, w_ada, b_ada.reshape(1, n_out))


def _mod_spec(mod, tm, seq_tile):
    if mod.shape[2] == 1:
        return pl.BlockSpec((1, N_MOD, 1, D_MODEL), lambda *g: (seq_tile(*g)[0], 0, 0, 0))
    return pl.BlockSpec((1, N_MOD, tm, D_MODEL), lambda *g: (seq_tile(*g)[0], 0, seq_tile(*g)[1], 0))


def _norm_proj_kernel(x_ref, mod_ref, g_ref, w_ref, o_ref, *, gate):
    mod = mod_ref[0]
    h = _rms_scale(x_ref[0]) * g_ref[...] * (1.0 + mod[1]) + mod[0]
    out = _bdot(h, w_ref[...])
    o_ref[0] = _sigmoid(out) if gate else out


def _norm_proj(x, mod, g, w_bf16, *, gate, tm):
    b, t, _ = x.shape
    n = w_bf16.shape[1]
    return pl.pallas_call(
        functools.partial(_norm_proj_kernel, gate=gate),
        out_shape=jax.ShapeDtypeStruct((b, t, n), F32),
        grid=(b, t // tm),
        in_specs=[pl.BlockSpec((1, tm, D_MODEL), lambda i, j: (i, j, 0)),
                  _mod_spec(mod, tm, lambda i, j: (i, j)),
                  pl.BlockSpec((1, D_MODEL), lambda i, j: (0, 0)),
                  pl.BlockSpec((D_MODEL, n), lambda i, j: (0, 0))],
        out_specs=pl.BlockSpec((1, tm, n), lambda i, j: (i, j, 0)),
        compiler_params=pltpu.CompilerParams(dimension_semantics=("parallel", "parallel"),
                                             vmem_limit_bytes=VMEM_LIMIT),
        name="norm_proj_gate" if gate else "norm_proj",
    )(x, mod, g, w_bf16)


def _blockdiag(x, lo_mask):
    return jnp.concatenate([jnp.where(lo_mask, x, 0.0), jnp.where(lo_mask, 0.0, x)], axis=0)


def _rwkv_kernel(p_ref, shift_ref, s0_ref, mu_ref, w0_ref, dup_ref, a0_ref, aup_ref, gup_ref,
                 kk_ref, ka_ref, rk_ref, lng_ref, lnb_ref, y_ref, sout_ref, s_scr, carry_scr,
                 *, nb):
    c = pl.program_id(1)
    n_chunks = pl.num_programs(1)
    C = CHUNK
    N = RWKV_HEAD_DIM
    t_blk = p_ref.shape[1]

    @pl.when(c == 0)
    def _():
        for bl in range(nb):
            for pr in range(RWKV_PAIRS):
                s_scr[bl, pr] = jnp.concatenate([s0_ref[bl, 2 * pr], s0_ref[bl, 2 * pr + 1]], axis=1)
        carry_scr[...] = shift_ref[...]

    row = lax.broadcasted_iota(jnp.int32, (C, LANES), 0)
    lane = lax.broadcasted_iota(jnp.int32, (C, LANES), 1)
    lane_t = jnp.where(lane < N, lane, lane - N)
    lo = lane < N
    strict = lane_t < row
    incl = lane_t <= row
    eye = (lane_t == row).astype(F32)
    valid = row < t_blk
    tri = (lax.broadcasted_iota(jnp.int32, (C, C), 1)
           <= lax.broadcasted_iota(jnp.int32, (C, C), 0)).astype(BF16)
    sq_r = lax.broadcasted_iota(jnp.int32, (LANES, LANES), 0)
    sq_c = lax.broadcasted_iota(jnp.int32, (LANES, LANES), 1)
    same_head = (sq_r < N) == (sq_c < N)
    diag = sq_r == sq_c
    row_w = lax.broadcasted_iota(jnp.int32, (C, RWKV_PROJ), 0)
    o3 = 3 * D_MODEL

    def hsum(v):
        s_lo = jnp.sum(jnp.where(lo, v, 0.0), axis=1, keepdims=True)
        s_hi = jnp.sum(jnp.where(lo, 0.0, v), axis=1, keepdims=True)
        return jnp.where(lo, s_lo, s_hi)

    def bd(v):
        return _blockdiag(v, lo)

    def cat0(*a):
        return jnp.concatenate(a, axis=0)

    def cat1(*a):
        return jnp.concatenate(a, axis=1)

    valid_w = lax.broadcasted_iota(jnp.int32, (C, D_MODEL), 0) < t_blk
    xs, logw_all, cum_all, asig_all, g_all = [], [], [], [], []
    for bl in range(nb):
        p = p_ref[bl]
        carry_new = p[t_blk - 1:t_blk, :]
        if t_blk < C:
            p = jnp.concatenate([p, jnp.zeros((C - t_blk, RWKV_PROJ), F32)], axis=0)
        prev = jnp.where(row_w == 0, carry_scr[bl], pltpu.roll(p, 1, axis=0))
        carry_scr[bl] = carry_new
        x = p + (prev - p) * mu_ref[...]
        xw = x[:, o3:o3 + DECAY_LORA]
        xa = x[:, o3 + DECAY_LORA:o3 + DECAY_LORA + AAA_LORA]
        xg = x[:, o3 + DECAY_LORA + AAA_LORA:]
        z = w0_ref[...] + _bdot(jnp.tanh(xw), dup_ref[...])
        logw_full = -math.exp(-0.5) * _sigmoid(z)
        if t_blk < C:
            logw_full = jnp.where(valid_w, logw_full, 0.0)
        xs.append(x)
        logw_all.append(logw_full)
        cum_all.append(_dot_exact_lhs(tri, logw_full))
        asig_all.append(_sigmoid(a0_ref[...] + _bdot(xa, aup_ref[...])))
        g_all.append(_bdot(_sigmoid(xg), gup_ref[...]))

    units = [(bl, pr) for bl in range(nb) for pr in range(RWKV_PAIRS)]
    U = range(len(units))
    cols = [slice(pr * LANES, (pr + 1) * LANES) for _, pr in units]
    r = [xs[bl][:, pr * LANES:(pr + 1) * LANES] for bl, pr in units]
    k = [xs[bl][:, D_MODEL + pr * LANES:D_MODEL + (pr + 1) * LANES] for bl, pr in units]
    v = [xs[bl][:, 2 * D_MODEL + pr * LANES:2 * D_MODEL + (pr + 1) * LANES] for bl, pr in units]
    logw = [logw_all[bl][:, cols[u]] for u, (bl, _) in enumerate(units)]
    asig = [asig_all[bl][:, cols[u]] for u, (bl, _) in enumerate(units)]
    kk = [k[u] * kk_ref[:, cols[u]] for u in U]
    k2 = [k[u] * (1.0 + (asig[u] - 1.0) * ka_ref[:, cols[u]]) for u in U]
    ss = [hsum(kk[u] * kk[u]) for u in U]
    cum = [cum_all[bl][:, cols[u]] for u, (bl, _) in enumerate(units)]
    bsum = [hsum(r[u] * k2[u] * rk_ref[:, cols[u]]) for u in U]
    kk = [kk[u] * lax.rsqrt(jnp.maximum(ss[u], 1e-24)) for u in U]
    bonus = [bsum[u] * v[u] for u in U]
    if t_blk < C:
        kk = [jnp.where(valid, a, 0.0) for a in kk]
        k2 = [jnp.where(valid, a, 0.0) for a in k2]
        v = [jnp.where(valid, a, 0.0) for a in v]
        r = [jnp.where(valid, a, 0.0) for a in r]
    b_vec = [kk[u] * asig[u] for u in U]
    clast = [cum[u][C - 1:C, :] for u in U]
    e_inv = [jnp.exp(-cum[u]) for u in U]
    e_tail = [jnp.exp(clast[u] - cum[u]) for u in U]
    aq = [-kk[u] * jnp.exp(cum[u] - logw[u]) for u in U]
    rq = [r[u] * jnp.exp(cum[u]) for u in U]
    bk = [b_vec[u] * e_inv[u] for u in U]
    kx = [k2[u] * e_inv[u] for u in U]
    bb = [b_vec[u] * e_tail[u] for u in U]
    kb = [k2[u] * e_tail[u] for u in U]

    g4 = [_bdot_nt(cat0(aq[u], rq[u]), cat0(bd(bk[u]), bd(kx[u]))) for u in U]
    m_ab = [jnp.where(strict, g4[u][:C, :LANES], 0.0) for u in U]
    m_ak = [jnp.where(strict, g4[u][:C, LANES:], 0.0) for u in U]
    m_rb = [jnp.where(incl, g4[u][C:, :LANES], 0.0) for u in U]
    m_rk = [jnp.where(incl, g4[u][C:, LANES:], 0.0) for u in U]

    t_inv = [eye + m_ab[u] for u in U]
    l_pow = [_bdot(m_ab[u], bd(m_ab[u])) for u in U]
    mv = [_bdot(cat0(m_ak[u], m_rk[u]), bd(v[u])) for u in U]
    n_steps = C.bit_length() - 1
    for step in range(1, n_steps):
        if step < n_steps - 1:
            both = [_bdot(l_pow[u], cat1(bd(t_inv[u]), bd(l_pow[u]))) for u in U]
            t_inv = [t_inv[u] + both[u][:, :LANES] for u in U]
            l_pow = [both[u][:, LANES:] for u in U]
        else:
            t_inv = [t_inv[u] + _bdot(l_pow[u], bd(t_inv[u])) for u in U]

    au = [_bdot(t_inv[u], cat1(bd(aq[u]), bd(mv[u][:C]))) for u in U]
    a_bar = [au[u][:, :LANES] for u in U]
    u_bar = [au[u][:, LANES:] for u in U]
    mau = [_bdot(m_rb[u], cat1(bd(a_bar[u]), bd(u_bar[u]))) for u in U]
    p_full = [_bdot_tn(a_bar[u], bb[u]) for u in U]
    q_full = [_bdot_tn(cat0(u_bar[u], v[u]), cat0(bb[u], kb[u])) for u in U]
    r_til = [rq[u] + mau[u][:, :LANES] for u in U]
    y_loc = [mau[u][:, LANES:] + mv[u][C:] for u in U]
    p_bd = [jnp.where(same_head, p_full[u], 0.0) + jnp.where(diag, jnp.exp(clast[u]), 0.0) for u in U]
    q_pk = [jnp.where(lo[:N], q_full[u][:N], q_full[u][N:]) for u in U]

    s_old = [s_scr[bl, pr] for bl, pr in units]
    y = [_bdot_nt(r_til[u], _blockdiag(s_old[u], lo[:N])) + y_loc[u] for u in U]
    s_new = [_bdot(s_old[u], p_bd[u]) + q_pk[u] for u in U]
    for u, (bl, pr) in enumerate(units):
        s_scr[bl, pr] = s_new[u]

    mu_y = [hsum(y[u]) * (1.0 / N) for u in U]
    yc = [y[u] - mu_y[u] for u in U]
    var = [hsum(yc[u] * yc[u]) * (1.0 / N) for u in U]
    for u, (bl, pr) in enumerate(units):
        yn = yc[u] * lax.rsqrt(var[u] + LN_X_EPS) * lng_ref[:, cols[u]] + lnb_ref[:, cols[u]]
        y_ref[bl, :, cols[u]] = ((yn + bonus[u]) * g_all[bl][:, cols[u]])[:t_blk]

    @pl.when(c == n_chunks - 1)
    def _():
        for bl in range(nb):
            for pr in range(RWKV_PAIRS):
                sout_ref[bl, 2 * pr] = s_scr[bl, pr][:, :N]
                sout_ref[bl, 2 * pr + 1] = s_scr[bl, pr][:, N:]


def _rwkv(p_rwkv, shift_prev, s0, mu, w0, decay_up, a0, aaa_up, gate_up, k_k, k_a, r_k,
          ln_g, ln_b, *, nb):
    b, t, _ = p_rwkv.shape
    t_blk = min(t, CHUNK)
    row = lambda a: a.reshape(1, -1)
    full = lambda shape: pl.BlockSpec(shape, lambda i, j: (0,) * len(shape))
    state_spec = pl.BlockSpec((nb, RWKV_HEADS, RWKV_HEAD_DIM, RWKV_HEAD_DIM), lambda i, j: (i, 0, 0, 0))
    return pl.pallas_call(
        functools.partial(_rwkv_kernel, nb=nb),
        out_shape=(jax.ShapeDtypeStruct((b, t, D_MODEL), F32),
                   jax.ShapeDtypeStruct((b, RWKV_HEADS, RWKV_HEAD_DIM, RWKV_HEAD_DIM), F32)),
        grid=(b // nb, t // t_blk),
        in_specs=[pl.BlockSpec((nb, t_blk, RWKV_PROJ), lambda i, j: (i, j, 0)),
                  pl.BlockSpec((nb, 1, RWKV_PROJ), lambda i, j: (i, 0, 0)),
                  state_spec,
                  full((1, RWKV_PROJ)), full((1, D_MODEL)), full((DECAY_LORA, D_MODEL)),
                  full((1, D_MODEL)), full((AAA_LORA, D_MODEL)), full((GATE_LORA, D_MODEL)),
                  full((1, D_MODEL)), full((1, D_MODEL)), full((1, D_MODEL)),
                  full((1, D_MODEL)), full((1, D_MODEL))],
        out_specs=(pl.BlockSpec((nb, t_blk, D_MODEL), lambda i, j: (i, j, 0)), state_spec),
        scratch_shapes=[pltpu.VMEM((nb, RWKV_PAIRS, RWKV_HEAD_DIM, LANES), F32),
                        pltpu.VMEM((nb, 1, RWKV_PROJ), F32)],
        compiler_params=pltpu.CompilerParams(dimension_semantics=("parallel", "arbitrary"),
                                             vmem_limit_bytes=VMEM_LIMIT),
        name="rwkv_scan",
    )(p_rwkv, shift_prev, s0, row(mu), row(w0), decay_up, row(a0), aaa_up, gate_up,
      row(k_k), row(k_a), row(r_k), row(ln_g), row(ln_b))


def _gla_kernel(p_ref, s0_ref, aup_ref, ab_ref, ng_ref, y_ref, sout_ref, s_scr, *, nb):
    c = pl.program_id(1)
    n_chunks = pl.num_programs(1)
    C = CHUNK
    t_blk = p_ref.shape[1]
    units = [(bl, h) for bl in range(nb) for h in range(GLA_HEADS)]
    U = range(len(units))

    @pl.when(c == 0)
    def _():
        for bl, h in units:
            s_scr[bl, h] = s0_ref[bl, h].T

    o_k = GLA_DK_TOTAL
    o_v = 2 * GLA_DK_TOTAL
    o_r = o_v + GLA_DV_TOTAL
    o_a = o_r + GLA_DV_TOTAL
    tri = (lax.broadcasted_iota(jnp.int32, (C, C), 1)
           <= lax.broadcasted_iota(jnp.int32, (C, C), 0))
    tri_bf = tri.astype(BF16)
    valid_g = lax.broadcasted_iota(jnp.int32, (C, GLA_DK_TOTAL), 0) < t_blk
    valid_k = lax.broadcasted_iota(jnp.int32, (C, GLA_DK), 0) < t_blk
    valid_v = lax.broadcasted_iota(jnp.int32, (C, GLA_DV), 0) < t_blk

    ps, cums = [], []
    for bl in range(nb):
        p = p_ref[bl]
        if t_blk < C:
            p = jnp.concatenate([p, jnp.zeros((C - t_blk, GLA_PROJ_PAD), F32)], axis=0)
        logit = _dot3(p[:, o_a:], aup_ref[...]) + ab_ref[...]
        g = (jnp.minimum(logit, 0.0) - jnp.log(1.0 + jnp.exp(-jnp.abs(logit)))) * (1.0 / GLA_GATE_NORM)
        if t_blk < C:
            g = jnp.where(valid_g, g, 0.0)
        ps.append(p)
        cums.append(_dot_exact_lhs(tri_bf, g))

    q = [ps[bl][:, h * GLA_DK:(h + 1) * GLA_DK] * (GLA_DK ** -0.5) for bl, h in units]
    k = [ps[bl][:, o_k + h * GLA_DK:o_k + (h + 1) * GLA_DK] for bl, h in units]
    v = [ps[bl][:, o_v + h * GLA_DV:o_v + (h + 1) * GLA_DV] for bl, h in units]
    if t_blk < C:
        k = [jnp.where(valid_k, a, 0.0) for a in k]
        v = [jnp.where(valid_v, a, 0.0) for a in v]
    b = [cums[bl][:, h * GLA_DK:(h + 1) * GLA_DK] for bl, h in units]
    clast = [b[u][C - 1:C, :] for u in U]
    cmid = [b[u][C // 2 - 1:C // 2, :] for u in U]
    att = [_bdot_nt(q[u] * jnp.exp(b[u] - cmid[u]), k[u] * jnp.exp(cmid[u] - b[u])) for u in U]
    att = [jnp.where(tri, a, 0.0) for a in att]
    s_t = [s_scr[bl, h] for bl, h in units]
    o = [_bdot(att[u], v[u]) + _bdot_nt(q[u] * jnp.exp(b[u]), s_t[u]) for u in U]
    s_new = [s_t[u] * jnp.exp(clast[u]) + _bdot_tn(v[u], k[u] * jnp.exp(clast[u] - b[u])) for u in U]
    for u, (bl, h) in enumerate(units):
        s_scr[bl, h] = s_new[u]
        r = ps[bl][:, o_r + h * GLA_DV:o_r + (h + 1) * GLA_DV]
        on = o[u] * lax.rsqrt(jnp.mean(o[u] * o[u], axis=-1, keepdims=True) + EPS) * ng_ref[...]
        y_ref[bl, :, h * GLA_DV:(h + 1) * GLA_DV] = (on * _silu(r))[:t_blk]

    @pl.when(c == n_chunks - 1)
    def _():
        for bl, h in units:
            sout_ref[bl, h] = s_scr[bl, h].T


def _gla(p_gla, s0, alpha_up_pad, alpha_b, norm_g, *, nb):
    b, t, _ = p_gla.shape
    t_blk = min(t, CHUNK)
    full = lambda shape: pl.BlockSpec(shape, lambda i, j: (0,) * len(shape))
    state_spec = pl.BlockSpec((nb, GLA_HEADS, GLA_DK, GLA_DV), lambda i, j: (i, 0, 0, 0))
    return pl.pallas_call(
        functools.partial(_gla_kernel, nb=nb),
        out_shape=(jax.ShapeDtypeStruct((b, t, GLA_DV_TOTAL), F32),
                   jax.ShapeDtypeStruct((b, GLA_HEADS, GLA_DK, GLA_DV), F32)),
        grid=(b // nb, t // t_blk),
        in_specs=[pl.BlockSpec((nb, t_blk, GLA_PROJ_PAD), lambda i, j: (i, j, 0)),
                  state_spec,
                  full((LANES, GLA_DK_TOTAL)), full((1, GLA_DK_TOTAL)), full((1, GLA_DV))],
        out_specs=(pl.BlockSpec((nb, t_blk, GLA_DV_TOTAL), lambda i, j: (i, j, 0)), state_spec),
        scratch_shapes=[pltpu.VMEM((nb, GLA_HEADS, GLA_DV, GLA_DK), F32)],
        compiler_params=pltpu.CompilerParams(dimension_semantics=("parallel", "arbitrary"),
                                             vmem_limit_bytes=VMEM_LIMIT),
        name="gla_scan",
    )(p_gla, s0, alpha_up_pad, alpha_b.reshape(1, -1), norm_g.reshape(1, -1))


def _merge_body(yr_ref, yg_ref, gate_ref, x_ref, mod_ref, wbr_ref, wbg_ref, wout_ref, g2_ref,
                router_ref, x1_ref, h2_ref, route_ref):
    mod = mod_ref[0]
    gates = gate_ref[0]
    merged = (gates[:, :D_MODEL] * _bdot(yr_ref[0], wbr_ref[...])
              + gates[:, D_MODEL:] * _bdot(yg_ref[0], wbg_ref[...]))
    x1 = x_ref[0] + mod[2] * _bdot(merged, wout_ref[...])
    x1_ref[0] = x1
    h2 = _rms_scale(x1) * g2_ref[...] * (1.0 + mod[4]) + mod[3]
    h2_ref[...] = h2

    tm = h2.shape[0]
    h_hi, h_lo, _ = _split3(h2)
    prod = jnp.dot(jnp.concatenate([h_hi, h_lo], axis=0), router_ref[...], preferred_element_type=F32)
    logits = prod[:tm, :LANES] + prod[tm:, :LANES] + prod[:tm, LANES:]
    lane = lax.broadcasted_iota(jnp.int32, (tm, LANES), 1)
    neg = -jnp.inf
    is_group = (lane >= N_EXPERTS) & (lane < N_EXPERTS + N_GROUPS)
    gl = jnp.where(is_group, logits, neg)
    gmax = jnp.max(gl, axis=1, keepdims=True)
    g_idx = jnp.min(jnp.where(gl == gmax, lane, LANES), axis=1, keepdims=True) - N_EXPERTS
    p_g = 1.0 / jnp.sum(jnp.exp(gl - gmax), axis=1, keepdims=True)
    in_group = (lane >= g_idx * EXPERTS_PER_GROUP) & (lane < (g_idx + 1) * EXPERTS_PER_GROUP)
    el = jnp.where(in_group, logits, neg)
    v1 = jnp.max(el, axis=1, keepdims=True)
    i1 = jnp.min(jnp.where(el == v1, lane, LANES), axis=1, keepdims=True)
    el2 = jnp.where(lane == i1, neg, el)
    v2 = jnp.max(el2, axis=1, keepdims=True)
    i2 = jnp.min(jnp.where(el2 == v2, lane, LANES), axis=1, keepdims=True)
    e21 = jnp.exp(v2 - v1)
    w1 = p_g / (1.0 + e21)
    route_ref[...] = (jnp.where(lane == 0, i1.astype(F32), 0.0) + jnp.where(lane == 1, i2.astype(F32), 0.0)
                      + jnp.where(lane == 2, w1, 0.0) + jnp.where(lane == 3, w1 * e21, 0.0))


def _merge_first_kernel(*refs, n_own):
    step = pl.program_id(0)

    @pl.when(step < n_own)
    def _():
        _merge_body(*refs)

    @pl.when(step >= n_own)
    def _():
        refs[-2][...] = jnp.zeros(refs[-2].shape, F32)
        refs[-1][...] = jnp.zeros(refs[-1].shape, F32)


def _merge_second_kernel(*refs):
    n_in = 10
    _merge_body(*refs[:n_in], *refs[n_in + 2:])


def _merge(y_r, y_g, gates, x, mod, wbr, wbg, wout, norm2_g, router_cat, *, tm, n_tok, row_off, shared=None):
    b, t, _ = x.shape
    per_seq = t // tm
    n_own = b * per_seq
    first = shared is None
    n_steps = n_tok // tm if first else n_own
    assert row_off % tm == 0 and n_tok % tm == 0 and (row_off == 0 or not first)
    blk0 = row_off // tm
    own = lambda s: jnp.minimum(s, n_own - 1)
    full = lambda shape: pl.BlockSpec(shape, lambda s: (0,) * len(shape))
    tile = lambda n: pl.BlockSpec((1, tm, n), lambda s: (own(s) // per_seq, own(s) % per_seq, 0))
    flat = lambda n: pl.BlockSpec((tm, n), lambda s: (blk0 + s, 0))
    in_specs = [tile(D_MODEL), tile(D_MODEL), tile(GATE_PROJ), tile(D_MODEL),
                _mod_spec(mod, tm, lambda s: (own(s) // per_seq, own(s) % per_seq)),
                full((D_MODEL, D_MODEL)), full((D_MODEL, D_MODEL)), full((D_MODEL, D_MODEL)),
                full((1, D_MODEL)), full((D_MODEL, 2 * LANES))]
    args = [y_r, y_g, gates, x, mod, wbr, wbg, wout, norm2_g.reshape(1, -1), router_cat]
    if not first:
        in_specs += [pl.BlockSpec(memory_space=pl.ANY)] * 2
        args += list(shared)
    return pl.pallas_call(
        functools.partial(_merge_first_kernel, n_own=n_own) if first else _merge_second_kernel,
        out_shape=(jax.ShapeDtypeStruct((b, t, D_MODEL), F32),
                   jax.ShapeDtypeStruct((n_tok, D_MODEL), F32),
                   jax.ShapeDtypeStruct((n_tok, LANES), F32)),
        grid=(n_steps,),
        in_specs=in_specs,
        out_specs=(tile(D_MODEL), flat(D_MODEL), flat(LANES)),
        input_output_aliases={} if first else {10: 1, 11: 2},
        compiler_params=pltpu.CompilerParams(dimension_semantics=("arbitrary",),
                                             vmem_limit_bytes=VMEM_LIMIT),
        name="merge_router",
    )(*args)


def _moe_plan(expert_ids, n_tok):
    tm = MOE_TILE
    n_assign = 2 * n_tok
    n_tiles = -(-n_assign // tm) + N_EXPERTS
    keys = expert_ids.T.reshape(-1)
    a_iota = jnp.arange(n_assign, dtype=jnp.int32)
    _, order = lax.top_k(-(keys * n_assign + a_iota).astype(F32), n_assign)
    e_iota = jnp.arange(N_EXPERTS, dtype=jnp.int32)
    counts = jnp.sum((keys[None, :] == e_iota[:, None]).astype(jnp.int32), axis=1)
    tiles_e = (counts + tm - 1) // tm
    tile_end = jnp.cumsum(tiles_e)
    row_start = jnp.cumsum(counts) - counts
    n_used = tile_end[-1]
    t_idx = jnp.arange(n_tiles, dtype=jnp.int32)
    t_eff = jnp.minimum(t_idx, n_used - 1)
    tile_e = jnp.minimum(jnp.sum((t_eff[:, None] >= tile_end[None, :]).astype(jnp.int32), axis=1),
                         N_EXPERTS - 1)
    onehot = tile_e[:, None] == e_iota[None, :]
    pick = lambda vec: jnp.sum(jnp.where(onehot, vec[None, :], 0), axis=1)
    local = t_idx - pick(tile_end - tiles_e)
    n_valid = jnp.where(t_idx < n_used, jnp.clip(pick(counts) - local * tm, 0, tm), 0)
    j_idx = jnp.arange(tm, dtype=jnp.int32)[None, :]
    pos = jnp.clip(pick(row_start)[:, None] + local[:, None] * tm + j_idx, 0, n_assign - 1)
    valid = j_idx < n_valid[:, None]
    a = order[pos]
    tok = jnp.where(valid, a % n_tok, 0)
    dump = n_assign + (t_idx[:, None] % 2) * tm + j_idx
    dst = jnp.where(valid, a, dump)
    return (tile_e, n_used.reshape(1).astype(jnp.int32),
            tok.reshape(n_tiles, 1, tm), dst.reshape(n_tiles, 1, tm))


def _moe_group_kernel(tile_e_ref, n_used_ref, tok_ref, tok_next_ref, dst_ref, h2_hbm, wg_ref, wu_ref,
                      wd_ref, out_hbm, xbuf, obuf, wg_bf, wu_bf, wd_bf, gsem, ssem):
    t = pl.program_id(0)
    n_t = pl.num_programs(0)
    tm = MOE_TILE
    slot = lax.rem(t, 2)
    n_used = n_used_ref[0]

    def gather(idx_ref, s):
        for j in range(tm):
            pltpu.make_async_copy(h2_hbm.at[pl.ds(idx_ref[0, 0, j], 1)], xbuf.at[s, pl.ds(j, 1)],
                                  gsem.at[s]).start(priority=j % 2)

    def wait_gather(s):
        pltpu.make_async_copy(h2_hbm.at[pl.ds(0, tm)], xbuf.at[s], gsem.at[s]).wait()

    def wait_scatter(s):
        pltpu.make_async_copy(obuf.at[s], out_hbm.at[pl.ds(0, tm)], ssem.at[s]).wait()

    @pl.when(t == 0)
    def _():
        n_rows = out_hbm.shape[0]
        obuf[1] = jnp.zeros((tm, D_MODEL), F32)
        for half in range(2):
            fill = pltpu.make_async_copy(obuf.at[1], out_hbm.at[pl.ds(n_rows - (2 - half) * tm, tm)],
                                         ssem.at[1])
            fill.start()
            fill.wait()

    @pl.when((t == 0) & (n_used > 0))
    def _():
        gather(tok_ref, 0)

    @pl.when((t >= 2) & (t - 2 < n_used))
    def _():
        wait_scatter(slot)

    @pl.when(t < n_used)
    def _():
        @pl.when((t == 0) | (tile_e_ref[t] != tile_e_ref[jnp.maximum(t - 1, 0)]))
        def _():
            wg_bf[...] = wg_ref[0].astype(BF16)
            wu_bf[...] = wu_ref[0].astype(BF16)
            wd_bf[...] = wd_ref[0].astype(BF16)

        wait_gather(slot)
        gather(tok_next_ref, 1 - slot)
        x = xbuf[slot].astype(BF16)
        hid = _silu(jnp.dot(x, wg_bf[...], preferred_element_type=F32)) * jnp.dot(
            x, wu_bf[...], preferred_element_type=F32)
        obuf[slot] = _bdot(hid, wd_bf[...])

        for j in range(tm):
            pltpu.make_async_copy(obuf.at[slot, pl.ds(j, 1)], out_hbm.at[pl.ds(dst_ref[0, 0, j], 1)],
                                  ssem.at[slot]).start(priority=j % 2)

    @pl.when((t < n_used) & (t + 1 >= n_used))
    def _():
        wait_gather(1 - slot)

    @pl.when(t == n_t - 1)
    def _():
        @pl.when(t - 1 < n_used)
        def _():
            wait_scatter(1 - slot)

        @pl.when(t < n_used)
        def _():
            wait_scatter(slot)


def _moe_grouped(h2_all, expert_ids, wg, wu, wd):
    return _moe_call(h2_all, *_moe_plan(expert_ids, h2_all.shape[0]), wg, wu, wd)


def _moe_call(h2_all, tile_e, n_used, tok, dst, wg, wu, wd):
    n_tok = h2_all.shape[0]
    tm = MOE_TILE
    n_tiles = tile_e.shape[0]
    smem_tile = lambda f: pl.BlockSpec((1, 1, tm), f, memory_space=pltpu.SMEM)
    w_spec = lambda shape: pl.BlockSpec(shape, lambda t, te, nu: (te[t], 0, 0))
    grid_spec = pltpu.PrefetchScalarGridSpec(
        num_scalar_prefetch=2,
        grid=(n_tiles,),
        in_specs=[smem_tile(lambda t, te, nu: (t, 0, 0)),
                  smem_tile(lambda t, te, nu: (jnp.minimum(t + 1, n_tiles - 1), 0, 0)),
                  smem_tile(lambda t, te, nu: (t, 0, 0)),
                  pl.BlockSpec(memory_space=pl.ANY),
                  w_spec((1, D_MODEL, EXPERT_FF)), w_spec((1, D_MODEL, EXPERT_FF)),
                  w_spec((1, EXPERT_FF, D_MODEL))],
        out_specs=pl.BlockSpec(memory_space=pl.ANY),
        scratch_shapes=[pltpu.VMEM((2, tm, D_MODEL), F32), pltpu.VMEM((2, tm, D_MODEL), F32),
                        pltpu.VMEM((D_MODEL, EXPERT_FF), BF16), pltpu.VMEM((D_MODEL, EXPERT_FF), BF16),
                        pltpu.VMEM((EXPERT_FF, D_MODEL), BF16),
                        pltpu.SemaphoreType.DMA((2,)), pltpu.SemaphoreType.DMA((2,))],
    )
    return pl.pallas_call(
        _moe_group_kernel,
        out_shape=jax.ShapeDtypeStruct((2 * n_tok + 2 * tm, D_MODEL), F32),
        grid_spec=grid_spec,
        compiler_params=pltpu.CompilerParams(dimension_semantics=("arbitrary",),
                                             vmem_limit_bytes=VMEM_LIMIT),
        name="moe_grouped",
    )(tile_e, n_used, tok, tok, dst, h2_all, wg, wu, wd)


def _final_kernel(x1_ref, o0_ref, o1_ref, route_ref, mod_ref, fg_ref, y_ref):
    route = route_ref[...]
    lane = lax.broadcasted_iota(jnp.int32, route.shape, 1)
    w1 = jnp.sum(jnp.where(lane == 2, route, 0.0), axis=1, keepdims=True)
    w2 = jnp.sum(jnp.where(lane == 3, route, 0.0), axis=1, keepdims=True)
    x2 = x1_ref[0] + mod_ref[0][5] * (w1 * o0_ref[...] + w2 * o1_ref[...])
    y_ref[0] = _rms_scale(x2) * fg_ref[...]


def _final(x1, moe_out, route_all, mod, final_g, *, row_off, tm):
    b, t, _ = x1.shape
    n_tok = route_all.shape[0]
    blk0 = row_off // tm
    blk1 = (n_tok + row_off) // tm
    per_seq = t // tm
    return pl.pallas_call(
        _final_kernel,
        out_shape=jax.ShapeDtypeStruct((b, t, D_MODEL), F32),
        grid=(b, per_seq),
        in_specs=[pl.BlockSpec((1, tm, D_MODEL), lambda i, j: (i, j, 0)),
                  pl.BlockSpec((tm, D_MODEL), lambda i, j: (blk0 + i * per_seq + j, 0)),
                  pl.BlockSpec((tm, D_MODEL), lambda i, j: (blk1 + i * per_seq + j, 0)),
                  pl.BlockSpec((tm, LANES), lambda i, j: (blk0 + i * per_seq + j, 0)),
                  _mod_spec(mod, tm, lambda i, j: (i, j)),
                  pl.BlockSpec((1, D_MODEL), lambda i, j: (0, 0))],
        out_specs=pl.BlockSpec((1, tm, D_MODEL), lambda i, j: (i, j, 0)),
        compiler_params=pltpu.CompilerParams(dimension_semantics=("parallel", "parallel"),
                                             vmem_limit_bytes=VMEM_LIMIT),
        name="moe_combine_norm",
    )(x1, moe_out, moe_out, route_all, mod, final_g.reshape(1, -1))


def _mixer(x, mod, shift0, s_rwkv0, s_gla0, wts, *, n_tok, row_off, shared=None):
    b, t, _ = x.shape
    per_token = mod.shape[2] > 1
    rows = x.reshape(1, b * t, D_MODEL) if per_token else x
    tm = min(rows.shape[1], ROW_TILE)
    p_rwkv = _norm_proj(rows, mod, wts["norm1_g"], wts["w_in_rwkv"], gate=False, tm=tm)
    p_gla = _norm_proj(rows, mod, wts["norm1_g"], wts["w_in_gla"], gate=False, tm=tm)
    gates = _norm_proj(rows, mod, wts["norm1_g"], wts["w_in_gate"], gate=True, tm=tm)
    p_rwkv = p_rwkv.reshape(b, t, RWKV_PROJ)

    y_r, s_rwkv = _rwkv(p_rwkv, shift0, s_rwkv0,
                        wts["rwkv_mu"], wts["rwkv_w0"], wts["rwkv_decay_up"], wts["rwkv_a0"],
                        wts["rwkv_aaa_up"], wts["rwkv_gate_up"], wts["rwkv_k_k"], wts["rwkv_k_a"],
                        wts["rwkv_r_k"], wts["rwkv_ln_g"], wts["rwkv_ln_b"], nb=RWKV_SEQS_PER_STEP)
    y_g, s_gla = _gla(p_gla.reshape(b, t, GLA_PROJ_PAD), s_gla0, wts["gla_alpha_up"],
                      wts["gla_alpha_b"], wts["gla_norm_g"], nb=GLA_SEQS_PER_STEP)

    x1, h2_all, route_all = _merge(
        y_r.reshape(rows.shape), y_g.reshape(rows.shape), gates, rows, mod, wts["w_branch_rwkv"],
        wts["w_branch_gla"], wts["w_out"], wts["norm2_g"], wts["router_cat"],
        tm=tm, n_tok=n_tok, row_off=row_off, shared=shared)
    states = (p_rwkv[:, t - 1:t, :][None], s_rwkv[None], s_gla[None])
    return x1, h2_all, route_all, states


def kernel(x_prompt, x_sample, c_prompt, c_sample, state_rwkv_shift, state_rwkv, state_gla, w_ada, b_ada, norm1_g, w_in, rwkv_mu, rwkv_w0, rwkv_decay_up, rwkv_a0, rwkv_aaa_up, rwkv_gate_up, rwkv_k_k, rwkv_k_a, rwkv_r_k, rwkv_ln_g, rwkv_ln_b, gla_alpha_up, gla_alpha_b, gla_norm_g, w_branch_rwkv, w_branch_gla, w_out, norm2_g, router_group, router_expert, expert_w_gate, expert_w_up, expert_w_down, final_norm_g):
    bp, tp = x_prompt.shape[:2]
    bs, ts = x_sample.shape[:2]
    w_in0 = w_in[0]
    g0 = RWKV_PROJ
    o3 = 2 * GLA_DK_TOTAL + GLA_DV_TOTAL
    o4 = o3 + GLA_GATE_RANK
    w_gla = w_in0[:, g0:g0 + GLA_PROJ]
    w_gla = jnp.concatenate([w_gla[:, :o3], w_gla[:, o4:], w_gla[:, o3:o4],
                             jnp.zeros((D_MODEL, GLA_PROJ_PAD - GLA_PROJ), F32)], axis=1)
    wts = dict(
        norm1_g=norm1_g[0].reshape(1, -1),
        w_in_rwkv=w_in0[:, :g0].astype(BF16),
        w_in_gla=w_gla.astype(BF16),
        w_in_gate=w_in0[:, g0 + GLA_PROJ:].astype(BF16),
        rwkv_mu=rwkv_mu[0], rwkv_w0=rwkv_w0[0], rwkv_decay_up=rwkv_decay_up[0], rwkv_a0=rwkv_a0[0],
        rwkv_aaa_up=rwkv_aaa_up[0], rwkv_gate_up=rwkv_gate_up[0], rwkv_k_k=rwkv_k_k[0],
        rwkv_k_a=rwkv_k_a[0], rwkv_r_k=rwkv_r_k[0], rwkv_ln_g=rwkv_ln_g[0], rwkv_ln_b=rwkv_ln_b[0],
        gla_alpha_up=jnp.pad(gla_alpha_up[0], ((0, LANES - GLA_GATE_RANK), (0, 0))),
        gla_alpha_b=gla_alpha_b[0], gla_norm_g=gla_norm_g[0],
        w_branch_rwkv=w_branch_rwkv[0].astype(BF16), w_branch_gla=w_branch_gla[0].astype(BF16),
        w_out=w_out[0].astype(BF16), norm2_g=norm2_g[0],
        router_cat=_split_hi_lo(jnp.pad(jnp.concatenate([router_expert[0], router_group[0]], axis=1),
                                        ((0, 0), (0, LANES - N_EXPERTS - N_GROUPS)))),
        expert_w_gate=expert_w_gate[0], expert_w_up=expert_w_up[0], expert_w_down=expert_w_down[0],
    )
    mod = _ada(jnp.concatenate([c_prompt, c_sample], axis=0), w_ada[0], b_ada[0])
    mod = mod.reshape(bp + bs, N_MOD, D_MODEL)
    n_p = bp * tp
    n_s = bs * ts
    n_tok = n_p + n_s
    mod_p = mod[:bp].reshape(bp, N_MOD, 1, D_MODEL)
    mod_s = jnp.repeat(mod[bp:].transpose(1, 0, 2), ts, axis=1)[None]

    dt = x_prompt.dtype
    x1_p, h2_all, route_all, states_p = _mixer(
        x_prompt, mod_p, jnp.zeros((bp, 1, RWKV_PROJ), dt),
        jnp.zeros((bp, RWKV_HEADS, RWKV_HEAD_DIM, RWKV_HEAD_DIM), state_rwkv.dtype),
        jnp.zeros((bp, GLA_HEADS, GLA_DK, GLA_DV), state_gla.dtype), wts, n_tok=n_tok, row_off=0)
    x1_s, h2_all, route_all, states_s = _mixer(
        x_sample, mod_s, state_rwkv_shift[0], state_rwkv[0], state_gla[0], wts,
        n_tok=n_tok, row_off=n_p, shared=(h2_all, route_all))

    expert_ids = route_all[:, :2].astype(jnp.int32)
    moe_out = _moe_grouped(h2_all, expert_ids, wts["expert_w_gate"], wts["expert_w_up"],
                           wts["expert_w_down"])
    y_p = _final(x1_p, moe_out, route_all, mod_p, final_norm_g, row_off=0, tm=min(tp, ROW_TILE))
    y_s = _final(x1_s, moe_out, route_all, mod_s, final_norm_g, row_off=n_p, tm=min(n_s, ROW_TILE))
    return (y_p, y_s.reshape(bs, ts, D_MODEL)) + states_p + states_s
```

```python
import functools
import math

import jax
import jax.numpy as jnp
from jax import lax
from jax.experimental import pallas as pl
from jax.experimental.pallas import tpu as pltpu

F32 = jnp.float32
BF16 = jnp.bfloat16

D_MODEL = 1024
N_MOD = 6
EPS = 1e-6
RWKV_HEAD_DIM = 64
RWKV_HEADS = 16
RWKV_PAIRS = RWKV_HEADS // 2
DECAY_LORA = 64
AAA_LORA = 64
GATE_LORA = 128
RWKV_PROJ = 3 * D_MODEL + DECAY_LORA + AAA_LORA + GATE_LORA
LN_X_EPS = 64e-5
GLA_HEADS = 4
GLA_DK = 128
GLA_DV = 256
GLA_DK_TOTAL = GLA_HEADS * GLA_DK
GLA_DV_TOTAL = GLA_HEADS * GLA_DV
GLA_GATE_RANK = 16
GLA_GATE_NORM = 16.0
GLA_PROJ = 2 * GLA_DK_TOTAL + 2 * GLA_DV_TOTAL + GLA_GATE_RANK
LANES = 128
GLA_PROJ_PAD = 2 * GLA_DK_TOTAL + 2 * GLA_DV_TOTAL + LANES
GATE_PROJ = 2 * D_MODEL
TOKEN_SUBLANES = D_MODEL // LANES
N_GROUPS = 4
EXPERTS_PER_GROUP = 8
N_EXPERTS = 32
EXPERT_FF = 512

CHUNK = 64
ROW_TILE = 512
MOE_TILE = 256
RWKV_SEQS_PER_STEP = 2
GLA_SEQS_PER_STEP = 4
VMEM_LIMIT = 56 * 1024 * 1024


def _bdot(a, b):
    return jnp.dot(a.astype(BF16), b.astype(BF16), preferred_element_type=F32)


def _bdot_nt(a, b):
    return lax.dot_general(a.astype(BF16), b.astype(BF16), (((1,), (1,)), ((), ())),
                           preferred_element_type=F32)


def _bdot_tn(a, b):
    return jnp.dot(a.T.astype(BF16), b.astype(BF16), preferred_element_type=F32)


def _split3(x):
    h1 = x.astype(BF16)
    r1 = x - h1.astype(F32)
    h2 = r1.astype(BF16)
    h3 = (r1 - h2.astype(F32)).astype(BF16)
    return h1, h2, h3


def _dot3(a, b):
    a1, a2, _ = _split3(a)
    b1, b2, _ = _split3(b)
    return (jnp.dot(a1, b1, preferred_element_type=F32)
            + jnp.dot(a2, b1, preferred_element_type=F32)
            + jnp.dot(a1, b2, preferred_element_type=F32))


def _dot_exact_lhs(a_bf16, x):
    x1, x2, _ = _split3(x)
    return (jnp.dot(a_bf16, x1, preferred_element_type=F32)
            + jnp.dot(a_bf16, x2, preferred_element_type=F32))


def _split_hi_lo(w):
    hi, lo, _ = _split3(w)
    return jnp.concatenate([hi, lo], axis=1)


def _store_token_tiles(ref, x):
    for c in range(TOKEN_SUBLANES):
        ref[:, c, :] = x[:, c * LANES:(c + 1) * LANES]


def _load_token_tiles(ref):
    return jnp.concatenate([ref[:, c, :] for c in range(TOKEN_SUBLANES)], axis=1)


def _sigmoid(x):
    return 1.0 / (1.0 + jnp.exp(-x))


def _silu(x):
    return x * _sigmoid(x)


def _rms_scale(x):
    return x * lax.rsqrt(jnp.mean(x * x, axis=-1, keepdims=True) + EPS)


def _ada_kernel(c_ref, w_ref, b_ref, o_ref):
    o_ref[...] = _dot3(_silu(c_ref[...]), w_ref[...]) + b_ref[...]


def _ada(c, w_ada, b_ada):
    n_rows = c.shape[0]
    n_out = w_ada.shape[1]
    tn = 1536
    return pl.pallas_call(
        _ada_kernel,
        out_shape=jax.ShapeDtypeStruct((n_rows, n_out), F32),
        grid=(n_out // tn,),
        in_specs=[pl.BlockSpec((n_rows, D_MODEL), lambda j: (0, 0)),
                  pl.BlockSpec((D_MODEL, tn), lambda j: (0, j)),
                  pl.BlockSpec((1, tn), lambda j: (0, j))],
        out_specs=pl.BlockSpec((n_rows, tn), lambda j: (0, j)),
        compiler_params=pltpu.CompilerParams(dimension_semantics=("parallel",),
                                             vmem_limit_bytes=VMEM_LIMIT),
        name="ada",
    )(c, w_ada, b_ada.reshape(1, n_out))


def _mod_spec(mod, tm, seq_tile):
    if mod.shape[2] == 1:
        return pl.BlockSpec((1, N_MOD, 1, D_MODEL), lambda *g: (seq_tile(*g)[0], 0, 0, 0))
    return pl.BlockSpec((1, N_MOD, tm, D_MODEL), lambda *g: (seq_tile(*g)[0], 0, seq_tile(*g)[1], 0))


def _norm_proj_kernel(x_ref, mod_ref, g_ref, w_ref, o_ref, *, gate):
    mod = mod_ref[0]
    h = _rms_scale(x_ref[0]) * g_ref[...] * (1.0 + mod[1]) + mod[0]
    out = _bdot(h, w_ref[...])
    o_ref[0] = _sigmoid(out) if gate else out


def _norm_proj(x, mod, g, w_bf16, *, gate, tm):
    b, t, _ = x.shape
    n = w_bf16.shape[1]
    return pl.pallas_call(
        functools.partial(_norm_proj_kernel, gate=gate),
        out_shape=jax.ShapeDtypeStruct((b, t, n), F32),
        grid=(b, t // tm),
        in_specs=[pl.BlockSpec((1, tm, D_MODEL), lambda i, j: (i, j, 0)),
                  _mod_spec(mod, tm, lambda i, j: (i, j)),
                  pl.BlockSpec((1, D_MODEL), lambda i, j: (0, 0)),
                  pl.BlockSpec((D_MODEL, n), lambda i, j: (0, 0))],
        out_specs=pl.BlockSpec((1, tm, n), lambda i, j: (i, j, 0)),
        compiler_params=pltpu.CompilerParams(dimension_semantics=("parallel", "parallel"),
                                             vmem_limit_bytes=VMEM_LIMIT),
        name="norm_proj_gate" if gate else "norm_proj",
    )(x, mod, g, w_bf16)


def _blockdiag(x, lo_mask):
    return jnp.concatenate([jnp.where(lo_mask, x, 0.0), jnp.where(lo_mask, 0.0, x)], axis=0)


def _rwkv_kernel(p_ref, shift_ref, s0_ref, mu_ref, w0_ref, dup_ref, a0_ref, aup_ref, gup_ref,
                 kk_ref, ka_ref, rk_ref, lng_ref, lnb_ref, y_ref, sout_ref, s_scr, carry_scr,
                 *, nb):
    c = pl.program_id(1)
    n_chunks = pl.num_programs(1)
    C = CHUNK
    N = RWKV_HEAD_DIM
    t_blk = p_ref.shape[1]

    @pl.when(c == 0)
    def _():
        for bl in range(nb):
            for pr in range(RWKV_PAIRS):
                s_scr[bl, pr] = jnp.concatenate([s0_ref[bl, 2 * pr], s0_ref[bl, 2 * pr + 1]], axis=1)
        carry_scr[...] = shift_ref[...]

    row = lax.broadcasted_iota(jnp.int32, (C, LANES), 0)
    lane = lax.broadcasted_iota(jnp.int32, (C, LANES), 1)
    lane_t = jnp.where(lane < N, lane, lane - N)
    lo = lane < N
    strict = lane_t < row
    incl = lane_t <= row
    eye = (lane_t == row).astype(F32)
    valid = row < t_blk
    tri = (lax.broadcasted_iota(jnp.int32, (C, C), 1)
           <= lax.broadcasted_iota(jnp.int32, (C, C), 0)).astype(BF16)
    sq_r = lax.broadcasted_iota(jnp.int32, (LANES, LANES), 0)
    sq_c = lax.broadcasted_iota(jnp.int32, (LANES, LANES), 1)
    same_head = (sq_r < N) == (sq_c < N)
    diag = sq_r == sq_c
    row_w = lax.broadcasted_iota(jnp.int32, (C, RWKV_PROJ), 0)
    o3 = 3 * D_MODEL

    def hsum(v):
        s_lo = jnp.sum(jnp.where(lo, v, 0.0), axis=1, keepdims=True)
        s_hi = jnp.sum(jnp.where(lo, 0.0, v), axis=1, keepdims=True)
        return jnp.where(lo, s_lo, s_hi)

    def bd(v):
        return _blockdiag(v, lo)

    def cat0(*a):
        return jnp.concatenate(a, axis=0)

    def cat1(*a):
        return jnp.concatenate(a, axis=1)

    valid_w = lax.broadcasted_iota(jnp.int32, (C, D_MODEL), 0) < t_blk
    xs, logw_all, cum_all, asig_all, g_all = [], [], [], [], []
    for bl in range(nb):
        p = p_ref[bl]
        carry_new = p[t_blk - 1:t_blk, :]
        if t_blk < C:
            p = jnp.concatenate([p, jnp.zeros((C - t_blk, RWKV_PROJ), F32)], axis=0)
        prev = jnp.where(row_w == 0, carry_scr[bl], pltpu.roll(p, 1, axis=0))
        carry_scr[bl] = carry_new
        x = p + (prev - p) * mu_ref[...]
        xw = x[:, o3:o3 + DECAY_LORA]
        xa = x[:, o3 + DECAY_LORA:o3 + DECAY_LORA + AAA_LORA]
        xg = x[:, o3 + DECAY_LORA + AAA_LORA:]
        z = w0_ref[...] + _bdot(jnp.tanh(xw), dup_ref[...])
        logw_full = -math.exp(-0.5) * _sigmoid(z)
        if t_blk < C:
            logw_full = jnp.where(valid_w, logw_full, 0.0)
        xs.append(x)
        logw_all.append(logw_full)
        cum_all.append(_dot_exact_lhs(tri, logw_full))
        asig_all.append(_sigmoid(a0_ref[...] + _bdot(xa, aup_ref[...])))
        g_all.append(_bdot(_sigmoid(xg), gup_ref[...]))

    units = [(bl, pr) for bl in range(nb) for pr in range(RWKV_PAIRS)]
    U = range(len(units))
    cols = [slice(pr * LANES, (pr + 1) * LANES) for _, pr in units]
    r = [xs[bl][:, pr * LANES:(pr + 1) * LANES] for bl, pr in units]
    k = [xs[bl][:, D_MODEL + pr * LANES:D_MODEL + (pr + 1) * LANES] for bl, pr in units]
    v = [xs[bl][:, 2 * D_MODEL + pr * LANES:2 * D_MODEL + (pr + 1) * LANES] for bl, pr in units]
    logw = [logw_all[bl][:, cols[u]] for u, (bl, _) in enumerate(units)]
    asig = [asig_all[bl][:, cols[u]] for u, (bl, _) in enumerate(units)]
    kk = [k[u] * kk_ref[:, cols[u]] for u in U]
    k2 = [k[u] * (1.0 + (asig[u] - 1.0) * ka_ref[:, cols[u]]) for u in U]
    ss = [hsum(kk[u] * kk[u]) for u in U]
    cum = [cum_all[bl][:, cols[u]] for u, (bl, _) in enumerate(units)]
    bsum = [hsum(r[u] * k2[u] * rk_ref[:, cols[u]]) for u in U]
    kk = [kk[u] * lax.rsqrt(jnp.maximum(ss[u], 1e-24)) for u in U]
    bonus = [bsum[u] * v[u] for u in U]
    if t_blk < C:
        kk = [jnp.where(valid, a, 0.0) for a in kk]
        k2 = [jnp.where(valid, a, 0.0) for a in k2]
        v = [jnp.where(valid, a, 0.0) for a in v]
        r = [jnp.where(valid, a, 0.0) for a in r]
    b_vec = [kk[u] * asig[u] for u in U]
    clast = [cum[u][C - 1:C, :] for u in U]
    e_inv = [jnp.exp(-cum[u]) for u in U]
    e_tail = [jnp.exp(clast[u] - cum[u]) for u in U]
    aq = [-kk[u] * jnp.exp(cum[u] - logw[u]) for u in U]
    rq = [r[u] * jnp.exp(cum[u]) for u in U]
    bk = [b_vec[u] * e_inv[u] for u in U]
    kx = [k2[u] * e_inv[u] for u in U]
    bb = [b_vec[u] * e_tail[u] for u in U]
    kb = [k2[u] * e_tail[u] for u in U]

    g4 = [_bdot_nt(cat0(aq[u], rq[u]), cat0(bd(bk[u]), bd(kx[u]))) for u in U]
    m_ab = [jnp.where(strict, g4[u][:C, :LANES], 0.0) for u in U]
    m_ak = [jnp.where(strict, g4[u][:C, LANES:], 0.0) for u in U]
    m_rb = [jnp.where(incl, g4[u][C:, :LANES], 0.0) for u in U]
    m_rk = [jnp.where(incl, g4[u][C:, LANES:], 0.0) for u in U]

    t_inv = [eye + m_ab[u] for u in U]
    l_pow = [_bdot(m_ab[u], bd(m_ab[u])) for u in U]
    mv = [_bdot(cat0(m_ak[u], m_rk[u]), bd(v[u])) for u in U]
    n_steps = C.bit_length() - 1
    for step in range(1, n_steps):
        if step < n_steps - 1:
            both = [_bdot(l_pow[u], cat1(bd(t_inv[u]), bd(l_pow[u]))) for u in U]
            t_inv = [t_inv[u] + both[u][:, :LANES] for u in U]
            l_pow = [both[u][:, LANES:] for u in U]
        else:
            t_inv = [t_inv[u] + _bdot(l_pow[u], bd(t_inv[u])) for u in U]

    au = [_bdot(t_inv[u], cat1(bd(aq[u]), bd(mv[u][:C]))) for u in U]
    a_bar = [au[u][:, :LANES] for u in U]
    u_bar = [au[u][:, LANES:] for u in U]
    mau = [_bdot(m_rb[u], cat1(bd(a_bar[u]), bd(u_bar[u]))) for u in U]
    p_full = [_bdot_tn(a_bar[u], bb[u]) for u in U]
    q_full = [_bdot_tn(cat0(u_bar[u], v[u]), cat0(bb[u], kb[u])) for u in U]
    r_til = [rq[u] + mau[u][:, :LANES] for u in U]
    y_loc = [mau[u][:, LANES:] + mv[u][C:] for u in U]
    p_bd = [jnp.where(same_head, p_full[u], 0.0) + jnp.where(diag, jnp.exp(clast[u]), 0.0) for u in U]
    q_pk = [jnp.where(lo[:N], q_full[u][:N], q_full[u][N:]) for u in U]

    s_old = [s_scr[bl, pr] for bl, pr in units]
    y = [_bdot_nt(r_til[u], _blockdiag(s_old[u], lo[:N])) + y_loc[u] for u in U]
    s_new = [_bdot(s_old[u], p_bd[u]) + q_pk[u] for u in U]
    for u, (bl, pr) in enumerate(units):
        s_scr[bl, pr] = s_new[u]

    mu_y = [hsum(y[u]) * (1.0 / N) for u in U]
    yc = [y[u] - mu_y[u] for u in U]
    var = [hsum(yc[u] * yc[u]) * (1.0 / N) for u in U]
    for u, (bl, pr) in enumerate(units):
        yn = yc[u] * lax.rsqrt(var[u] + LN_X_EPS) * lng_ref[:, cols[u]] + lnb_ref[:, cols[u]]
        y_ref[bl, :, cols[u]] = ((yn + bonus[u]) * g_all[bl][:, cols[u]])[:t_blk]

    @pl.when(c == n_chunks - 1)
    def _():
        for bl in range(nb):
            for pr in range(RWKV_PAIRS):
                sout_ref[bl, 2 * pr] = s_scr[bl, pr][:, :N]
                sout_ref[bl, 2 * pr + 1] = s_scr[bl, pr][:, N:]


def _rwkv(p_rwkv, shift_prev, s0, mu, w0, decay_up, a0, aaa_up, gate_up, k_k, k_a, r_k,
          ln_g, ln_b, *, nb):
    b, t, _ = p_rwkv.shape
    t_blk = min(t, CHUNK)
    row = lambda a: a.reshape(1, -1)
    full = lambda shape: pl.BlockSpec(shape, lambda i, j: (0,) * len(shape))
    state_spec = pl.BlockSpec((nb, RWKV_HEADS, RWKV_HEAD_DIM, RWKV_HEAD_DIM), lambda i, j: (i, 0, 0, 0))
    return pl.pallas_call(
        functools.partial(_rwkv_kernel, nb=nb),
        out_shape=(jax.ShapeDtypeStruct((b, t, D_MODEL), F32),
                   jax.ShapeDtypeStruct((b, RWKV_HEADS, RWKV_HEAD_DIM, RWKV_HEAD_DIM), F32)),
        grid=(b // nb, t // t_blk),
        in_specs=[pl.BlockSpec((nb, t_blk, RWKV_PROJ), lambda i, j: (i, j, 0)),
                  pl.BlockSpec((nb, 1, RWKV_PROJ), lambda i, j: (i, 0, 0)),
                  state_spec,
                  full((1, RWKV_PROJ)), full((1, D_MODEL)), full((DECAY_LORA, D_MODEL)),
                  full((1, D_MODEL)), full((AAA_LORA, D_MODEL)), full((GATE_LORA, D_MODEL)),
                  full((1, D_MODEL)), full((1, D_MODEL)), full((1, D_MODEL)),
                  full((1, D_MODEL)), full((1, D_MODEL))],
        out_specs=(pl.BlockSpec((nb, t_blk, D_MODEL), lambda i, j: (i, j, 0)), state_spec),
        scratch_shapes=[pltpu.VMEM((nb, RWKV_PAIRS, RWKV_HEAD_DIM, LANES), F32),
                        pltpu.VMEM((nb, 1, RWKV_PROJ), F32)],
        compiler_params=pltpu.CompilerParams(dimension_semantics=("parallel", "arbitrary"),
                                             vmem_limit_bytes=VMEM_LIMIT),
        name="rwkv_scan",
    )(p_rwkv, shift_prev, s0, row(mu), row(w0), decay_up, row(a0), aaa_up, gate_up,
      row(k_k), row(k_a), row(r_k), row(ln_g), row(ln_b))


def _gla_kernel(p_ref, s0_ref, aup_ref, ab_ref, ng_ref, y_ref, sout_ref, s_scr, *, nb):
    c = pl.program_id(1)
    n_chunks = pl.num_programs(1)
    C = CHUNK
    t_blk = p_ref.shape[1]
    units = [(bl, h) for bl in range(nb) for h in range(GLA_HEADS)]
    U = range(len(units))

    @pl.when(c == 0)
    def _():
        for bl, h in units:
            s_scr[bl, h] = s0_ref[bl, h].T

    o_k = GLA_DK_TOTAL
    o_v = 2 * GLA_DK_TOTAL
    o_r = o_v + GLA_DV_TOTAL
    o_a = o_r + GLA_DV_TOTAL
    tri = (lax.broadcasted_iota(jnp.int32, (C, C), 1)
           <= lax.broadcasted_iota(jnp.int32, (C, C), 0))
    tri_bf = tri.astype(BF16)
    valid_g = lax.broadcasted_iota(jnp.int32, (C, GLA_DK_TOTAL), 0) < t_blk
    valid_k = lax.broadcasted_iota(jnp.int32, (C, GLA_DK), 0) < t_blk
    valid_v = lax.broadcasted_iota(jnp.int32, (C, GLA_DV), 0) < t_blk

    ps, cums = [], []
    for bl in range(nb):
        p = p_ref[bl]
        if t_blk < C:
            p = jnp.concatenate([p, jnp.zeros((C - t_blk, GLA_PROJ_PAD), F32)], axis=0)
        logit = _dot3(p[:, o_a:], aup_ref[...]) + ab_ref[...]
        g = (jnp.minimum(logit, 0.0) - jnp.log(1.0 + jnp.exp(-jnp.abs(logit)))) * (1.0 / GLA_GATE_NORM)
        if t_blk < C:
            g = jnp.where(valid_g, g, 0.0)
        ps.append(p)
        cums.append(_dot_exact_lhs(tri_bf, g))

    q = [ps[bl][:, h * GLA_DK:(h + 1) * GLA_DK] * (GLA_DK ** -0.5) for bl, h in units]
    k = [ps[bl][:, o_k + h * GLA_DK:o_k + (h + 1) * GLA_DK] for bl, h in units]
    v = [ps[bl][:, o_v + h * GLA_DV:o_v + (h + 1) * GLA_DV] for bl, h in units]
    if t_blk < C:
        k = [jnp.where(valid_k, a, 0.0) for a in k]
        v = [jnp.where(valid_v, a, 0.0) for a in v]
    b = [cums[bl][:, h * GLA_DK:(h + 1) * GLA_DK] for bl, h in units]
    clast = [b[u][C - 1:C, :] for u in U]
    cmid = [b[u][C // 2 - 1:C // 2, :] for u in U]
    att = [_bdot_nt(q[u] * jnp.exp(b[u] - cmid[u]), k[u] * jnp.exp(cmid[u] - b[u])) for u in U]
    att = [jnp.where(tri, a, 0.0) for a in att]
    s_t = [s_scr[bl, h] for bl, h in units]
    o = [_bdot(att[u], v[u]) + _bdot_nt(q[u] * jnp.exp(b[u]), s_t[u]) for u in U]
    s_new = [s_t[u] * jnp.exp(clast[u]) + _bdot_tn(v[u], k[u] * jnp.exp(clast[u] - b[u])) for u in U]
    for u, (bl, h) in enumerate(units):
        s_scr[bl, h] = s_new[u]
        r = ps[bl][:, o_r + h * GLA_DV:o_r + (h + 1) * GLA_DV]
        on = o[u] * lax.rsqrt(jnp.mean(o[u] * o[u], axis=-1, keepdims=True) + EPS) * ng_ref[...]
        y_ref[bl, :, h * GLA_DV:(h + 1) * GLA_DV] = (on * _silu(r))[:t_blk]

    @pl.when(c == n_chunks - 1)
    def _():
        for bl, h in units:
            sout_ref[bl, h] = s_scr[bl, h].T


def _gla(p_gla, s0, alpha_up_pad, alpha_b, norm_g, *, nb):
    b, t, _ = p_gla.shape
    t_blk = min(t, CHUNK)
    full = lambda shape: pl.BlockSpec(shape, lambda i, j: (0,) * len(shape))
    state_spec = pl.BlockSpec((nb, GLA_HEADS, GLA_DK, GLA_DV), lambda i, j: (i, 0, 0, 0))
    return pl.pallas_call(
        functools.partial(_gla_kernel, nb=nb),
        out_shape=(jax.ShapeDtypeStruct((b, t, GLA_DV_TOTAL), F32),
                   jax.ShapeDtypeStruct((b, GLA_HEADS, GLA_DK, GLA_DV), F32)),
        grid=(b // nb, t // t_blk),
        in_specs=[pl.BlockSpec((nb, t_blk, GLA_PROJ_PAD), lambda i, j: (i, j, 0)),
                  state_spec,
                  full((LANES, GLA_DK_TOTAL)), full((1, GLA_DK_TOTAL)), full((1, GLA_DV))],
        out_specs=(pl.BlockSpec((nb, t_blk, GLA_DV_TOTAL), lambda i, j: (i, j, 0)), state_spec),
        scratch_shapes=[pltpu.VMEM((nb, GLA_HEADS, GLA_DV, GLA_DK), F32)],
        compiler_params=pltpu.CompilerParams(dimension_semantics=("parallel", "arbitrary"),
                                             vmem_limit_bytes=VMEM_LIMIT),
        name="gla_scan",
    )(p_gla, s0, alpha_up_pad, alpha_b.reshape(1, -1), norm_g.reshape(1, -1))


def _merge_body(yr_ref, yg_ref, gate_ref, x_ref, mod_ref, wbr_ref, wbg_ref, wout_ref, g2_ref,
                router_ref, x1_ref, h2_ref, route_ref):
    mod = mod_ref[0]
    gates = gate_ref[0]
    merged = (gates[:, :D_MODEL] * _bdot(yr_ref[0], wbr_ref[...])
              + gates[:, D_MODEL:] * _bdot(yg_ref[0], wbg_ref[...]))
    x1 = x_ref[0] + mod[2] * _bdot(merged, wout_ref[...])
    x1_ref[0] = x1
    h2 = _rms_scale(x1) * g2_ref[...] * (1.0 + mod[4]) + mod[3]
    _store_token_tiles(h2_ref, h2)

    tm = h2.shape[0]
    h_hi, h_lo, _ = _split3(h2)
    prod = jnp.dot(jnp.concatenate([h_hi, h_lo], axis=0), router_ref[...], preferred_element_type=F32)
    logits = prod[:tm, :LANES] + prod[tm:, :LANES] + prod[:tm, LANES:]
    lane = lax.broadcasted_iota(jnp.int32, (tm, LANES), 1)
    neg = -jnp.inf
    is_group = (lane >= N_EXPERTS) & (lane < N_EXPERTS + N_GROUPS)
    gl = jnp.where(is_group, logits, neg)
    gmax = jnp.max(gl, axis=1, keepdims=True)
    g_idx = jnp.min(jnp.where(gl == gmax, lane, LANES), axis=1, keepdims=True) - N_EXPERTS
    p_g = 1.0 / jnp.sum(jnp.exp(gl - gmax), axis=1, keepdims=True)
    in_group = (lane >= g_idx * EXPERTS_PER_GROUP) & (lane < (g_idx + 1) * EXPERTS_PER_GROUP)
    el = jnp.where(in_group, logits, neg)
    v1 = jnp.max(el, axis=1, keepdims=True)
    i1 = jnp.min(jnp.where(el == v1, lane, LANES), axis=1, keepdims=True)
    el2 = jnp.where(lane == i1, neg, el)
    v2 = jnp.max(el2, axis=1, keepdims=True)
    i2 = jnp.min(jnp.where(el2 == v2, lane, LANES), axis=1, keepdims=True)
    e21 = jnp.exp(v2 - v1)
    w1 = p_g / (1.0 + e21)
    route_ref[...] = (jnp.where(lane == 0, i1.astype(F32), 0.0) + jnp.where(lane == 1, i2.astype(F32), 0.0)
                      + jnp.where(lane == 2, w1, 0.0) + jnp.where(lane == 3, w1 * e21, 0.0))


def _merge_first_kernel(*refs, n_own):
    step = pl.program_id(0)

    @pl.when(step < n_own)
    def _():
        _merge_body(*refs)

    @pl.when(step >= n_own)
    def _():
        refs[-2][...] = jnp.zeros(refs[-2].shape, F32)
        refs[-1][...] = jnp.zeros(refs[-1].shape, F32)


def _merge_second_kernel(*refs):
    n_in = 10
    _merge_body(*refs[:n_in], *refs[n_in + 2:])


def _merge(y_r, y_g, gates, x, mod, wbr, wbg, wout, norm2_g, router_cat, *, tm, n_tok, row_off, shared=None):
    b, t, _ = x.shape
    per_seq = t // tm
    n_own = b * per_seq
    first = shared is None
    n_steps = n_tok // tm if first else n_own
    assert row_off % tm == 0 and n_tok % tm == 0 and (row_off == 0 or not first)
    blk0 = row_off // tm
    own = lambda s: jnp.minimum(s, n_own - 1)
    full = lambda shape: pl.BlockSpec(shape, lambda s: (0,) * len(shape))
    tile = lambda n: pl.BlockSpec((1, tm, n), lambda s: (own(s) // per_seq, own(s) % per_seq, 0))
    flat = lambda n: pl.BlockSpec((tm, n), lambda s: (blk0 + s, 0))
    tiles = pl.BlockSpec((tm, TOKEN_SUBLANES, LANES), lambda s: (blk0 + s, 0, 0))
    in_specs = [tile(D_MODEL), tile(D_MODEL), tile(GATE_PROJ), tile(D_MODEL),
                _mod_spec(mod, tm, lambda s: (own(s) // per_seq, own(s) % per_seq)),
                full((D_MODEL, D_MODEL)), full((D_MODEL, D_MODEL)), full((D_MODEL, D_MODEL)),
                full((1, D_MODEL)), full((D_MODEL, 2 * LANES))]
    args = [y_r, y_g, gates, x, mod, wbr, wbg, wout, norm2_g.reshape(1, -1), router_cat]
    if not first:
        in_specs += [pl.BlockSpec(memory_space=pl.ANY)] * 2
        args += list(shared)
    return pl.pallas_call(
        functools.partial(_merge_first_kernel, n_own=n_own) if first else _merge_second_kernel,
        out_shape=(jax.ShapeDtypeStruct((b, t, D_MODEL), F32),
                   jax.ShapeDtypeStruct((n_tok, TOKEN_SUBLANES, LANES), F32),
                   jax.ShapeDtypeStruct((n_tok, LANES), F32)),
        grid=(n_steps,),
        in_specs=in_specs,
        out_specs=(tile(D_MODEL), tiles, flat(LANES)),
        input_output_aliases={} if first else {10: 1, 11: 2},
        compiler_params=pltpu.CompilerParams(dimension_semantics=("arbitrary",),
                                             vmem_limit_bytes=VMEM_LIMIT),
        name="merge_router",
    )(*args)


def _moe_plan(expert_ids, n_tok):
    tm = MOE_TILE
    n_assign = 2 * n_tok
    n_tiles = -(-n_assign // tm) + N_EXPERTS
    keys = expert_ids.T.reshape(-1)
    a_iota = jnp.arange(n_assign, dtype=jnp.int32)
    _, order = lax.top_k(-(keys * n_assign + a_iota).astype(F32), n_assign)
    e_iota = jnp.arange(N_EXPERTS, dtype=jnp.int32)
    counts = jnp.sum((keys[None, :] == e_iota[:, None]).astype(jnp.int32), axis=1)
    tiles_e = (counts + tm - 1) // tm
    tile_end = jnp.cumsum(tiles_e)
    row_start = jnp.cumsum(counts) - counts
    n_used = tile_end[-1]
    t_idx = jnp.arange(n_tiles, dtype=jnp.int32)
    t_eff = jnp.minimum(t_idx, n_used - 1)
    tile_e = jnp.minimum(jnp.sum((t_eff[:, None] >= tile_end[None, :]).astype(jnp.int32), axis=1),
                         N_EXPERTS - 1)
    onehot = tile_e[:, None] == e_iota[None, :]
    pick = lambda vec: jnp.sum(jnp.where(onehot, vec[None, :], 0), axis=1)
    local = t_idx - pick(tile_end - tiles_e)
    n_valid = jnp.where(t_idx < n_used, jnp.clip(pick(counts) - local * tm, 0, tm), 0)
    j_idx = jnp.arange(tm, dtype=jnp.int32)[None, :]
    pos = jnp.clip(pick(row_start)[:, None] + local[:, None] * tm + j_idx, 0, n_assign - 1)
    valid = j_idx < n_valid[:, None]
    a = order[pos]
    tok = jnp.where(valid, a % n_tok, 0)
    dump = n_assign + (t_idx[:, None] % 2) * tm + j_idx
    dst = jnp.where(valid, a, dump)
    return (tile_e, n_used.reshape(1).astype(jnp.int32),
            tok.reshape(n_tiles, 1, tm), dst.reshape(n_tiles, 1, tm))


def _moe_group_kernel(tile_e_ref, n_used_ref, tok_ref, tok_next_ref, dst_ref, h2_hbm, wg_ref, wu_ref,
                      wd_ref, out_hbm, xbuf, obuf, wg_bf, wu_bf, wd_bf, gsem, ssem):
    t = pl.program_id(0)
    n_t = pl.num_programs(0)
    tm = MOE_TILE
    slot = lax.rem(t, 2)
    n_used = n_used_ref[0]

    def gather(idx_ref, s):
        for j in range(tm):
            pltpu.make_async_copy(h2_hbm.at[idx_ref[0, 0, j]], xbuf.at[s, j], gsem.at[s]).start()

    def wait_gather(s):
        pltpu.make_async_copy(h2_hbm.at[pl.ds(0, tm)], xbuf.at[s], gsem.at[s]).wait()

    def wait_scatter(s):
        pltpu.make_async_copy(obuf.at[s], out_hbm.at[pl.ds(0, tm)], ssem.at[s]).wait()

    @pl.when(t == 0)
    def _():
        n_rows = out_hbm.shape[0]
        obuf[1] = jnp.zeros(obuf.shape[1:], F32)
        for half in range(2):
            fill = pltpu.make_async_copy(obuf.at[1], out_hbm.at[pl.ds(n_rows - (2 - half) * tm, tm)],
                                         ssem.at[1])
            fill.start()
            fill.wait()

    @pl.when((t == 0) & (n_used > 0))
    def _():
        gather(tok_ref, 0)

    @pl.when((t >= 2) & (t - 2 < n_used))
    def _():
        wait_scatter(slot)

    @pl.when(t < n_used)
    def _():
        @pl.when((t == 0) | (tile_e_ref[t] != tile_e_ref[jnp.maximum(t - 1, 0)]))
        def _():
            wg_bf[...] = wg_ref[0].astype(BF16)
            wu_bf[...] = wu_ref[0].astype(BF16)
            wd_bf[...] = wd_ref[0].astype(BF16)

        wait_gather(slot)
        gather(tok_next_ref, 1 - slot)
        x = _load_token_tiles(xbuf.at[slot]).astype(BF16)
        hid = _silu(jnp.dot(x, wg_bf[...], preferred_element_type=F32)) * jnp.dot(
            x, wu_bf[...], preferred_element_type=F32)
        _store_token_tiles(obuf.at[slot], _bdot(hid, wd_bf[...]))

        for j in range(tm):
            pltpu.make_async_copy(obuf.at[slot, j], out_hbm.at[dst_ref[0, 0, j]], ssem.at[slot]).start()

    @pl.when((t < n_used) & (t + 1 >= n_used))
    def _():
        wait_gather(1 - slot)

    @pl.when(t == n_t - 1)
    def _():
        @pl.when(t - 1 < n_used)
        def _():
            wait_scatter(1 - slot)

        @pl.when(t < n_used)
        def _():
            wait_scatter(slot)


def _moe_grouped(h2_all, expert_ids, wg, wu, wd):
    return _moe_call(h2_all, *_moe_plan(expert_ids, h2_all.shape[0]), wg, wu, wd)


def _moe_call(h2_all, tile_e, n_used, tok, dst, wg, wu, wd):
    n_tok = h2_all.shape[0]
    tm = MOE_TILE
    n_tiles = tile_e.shape[0]
    smem_tile = lambda f: pl.BlockSpec((1, 1, tm), f, memory_space=pltpu.SMEM)
    w_spec = lambda shape: pl.BlockSpec(shape, lambda t, te, nu: (te[t], 0, 0))
    grid_spec = pltpu.PrefetchScalarGridSpec(
        num_scalar_prefetch=2,
        grid=(n_tiles,),
        in_specs=[smem_tile(lambda t, te, nu: (t, 0, 0)),
                  smem_tile(lambda t, te, nu: (jnp.minimum(t + 1, n_tiles - 1), 0, 0)),
                  smem_tile(lambda t, te, nu: (t, 0, 0)),
                  pl.BlockSpec(memory_space=pl.ANY),
                  w_spec((1, D_MODEL, EXPERT_FF)), w_spec((1, D_MODEL, EXPERT_FF)),
                  w_spec((1, EXPERT_FF, D_MODEL))],
        out_specs=pl.BlockSpec(memory_space=pl.ANY),
        scratch_shapes=[pltpu.VMEM((2, tm, TOKEN_SUBLANES, LANES), F32),
                        pltpu.VMEM((2, tm, TOKEN_SUBLANES, LANES), F32),
                        pltpu.VMEM((D_MODEL, EXPERT_FF), BF16), pltpu.VMEM((D_MODEL, EXPERT_FF), BF16),
                        pltpu.VMEM((EXPERT_FF, D_MODEL), BF16),
                        pltpu.SemaphoreType.DMA((2,)), pltpu.SemaphoreType.DMA((2,))],
    )
    return pl.pallas_call(
        _moe_group_kernel,
        out_shape=jax.ShapeDtypeStruct((2 * n_tok + 2 * tm, TOKEN_SUBLANES, LANES), F32),
        grid_spec=grid_spec,
        compiler_params=pltpu.CompilerParams(dimension_semantics=("arbitrary",),
                                             vmem_limit_bytes=VMEM_LIMIT),
        name="moe_grouped",
    )(tile_e, n_used, tok, tok, dst, h2_all, wg, wu, wd)


def _final_kernel(x1_ref, o0_ref, o1_ref, route_ref, mod_ref, fg_ref, y_ref):
    route = route_ref[...]
    lane = lax.broadcasted_iota(jnp.int32, route.shape, 1)
    w1 = jnp.sum(jnp.where(lane == 2, route, 0.0), axis=1, keepdims=True)
    w2 = jnp.sum(jnp.where(lane == 3, route, 0.0), axis=1, keepdims=True)
    x2 = x1_ref[0] + mod_ref[0][5] * (w1 * _load_token_tiles(o0_ref) + w2 * _load_token_tiles(o1_ref))
    y_ref[0] = _rms_scale(x2) * fg_ref[...]


def _final(x1, moe_out, route_all, mod, final_g, *, row_off, tm):
    b, t, _ = x1.shape
    n_tok = route_all.shape[0]
    blk0 = row_off // tm
    blk1 = (n_tok + row_off) // tm
    per_seq = t // tm
    return pl.pallas_call(
        _final_kernel,
        out_shape=jax.ShapeDtypeStruct((b, t, D_MODEL), F32),
        grid=(b, per_seq),
        in_specs=[pl.BlockSpec((1, tm, D_MODEL), lambda i, j: (i, j, 0)),
                  pl.BlockSpec((tm, TOKEN_SUBLANES, LANES), lambda i, j: (blk0 + i * per_seq + j, 0, 0)),
                  pl.BlockSpec((tm, TOKEN_SUBLANES, LANES), lambda i, j: (blk1 + i * per_seq + j, 0, 0)),
                  pl.BlockSpec((tm, LANES), lambda i, j: (blk0 + i * per_seq + j, 0)),
                  _mod_spec(mod, tm, lambda i, j: (i, j)),
                  pl.BlockSpec((1, D_MODEL), lambda i, j: (0, 0))],
        out_specs=pl.BlockSpec((1, tm, D_MODEL), lambda i, j: (i, j, 0)),
        compiler_params=pltpu.CompilerParams(dimension_semantics=("parallel", "parallel"),
                                             vmem_limit_bytes=VMEM_LIMIT),
        name="moe_combine_norm",
    )(x1, moe_out, moe_out, route_all, mod, final_g.reshape(1, -1))


def _mixer(x, mod, shift0, s_rwkv0, s_gla0, wts, *, n_tok, row_off, shared=None):
    b, t, _ = x.shape
    per_token = mod.shape[2] > 1
    rows = x.reshape(1, b * t, D_MODEL) if per_token else x
    tm = min(rows.shape[1], ROW_TILE)
    p_rwkv = _norm_proj(rows, mod, wts["norm1_g"], wts["w_in_rwkv"], gate=False, tm=tm)
    p_gla = _norm_proj(rows, mod, wts["norm1_g"], wts["w_in_gla"], gate=False, tm=tm)
    gates = _norm_proj(rows, mod, wts["norm1_g"], wts["w_in_gate"], gate=True, tm=tm)
    p_rwkv = p_rwkv.reshape(b, t, RWKV_PROJ)

    y_r, s_rwkv = _rwkv(p_rwkv, shift0, s_rwkv0,
                        wts["rwkv_mu"], wts["rwkv_w0"], wts["rwkv_decay_up"], wts["rwkv_a0"],
                        wts["rwkv_aaa_up"], wts["rwkv_gate_up"], wts["rwkv_k_k"], wts["rwkv_k_a"],
                        wts["rwkv_r_k"], wts["rwkv_ln_g"], wts["rwkv_ln_b"], nb=RWKV_SEQS_PER_STEP)
    y_g, s_gla = _gla(p_gla.reshape(b, t, GLA_PROJ_PAD), s_gla0, wts["gla_alpha_up"],
                      wts["gla_alpha_b"], wts["gla_norm_g"], nb=GLA_SEQS_PER_STEP)

    x1, h2_all, route_all = _merge(
        y_r.reshape(rows.shape), y_g.reshape(rows.shape), gates, rows, mod, wts["w_branch_rwkv"],
        wts["w_branch_gla"], wts["w_out"], wts["norm2_g"], wts["router_cat"],
        tm=tm, n_tok=n_tok, row_off=row_off, shared=shared)
    states = (p_rwkv[:, t - 1:t, :][None], s_rwkv[None], s_gla[None])
    return x1, h2_all, route_all, states


def kernel(x_prompt, x_sample, c_prompt, c_sample, state_rwkv_shift, state_rwkv, state_gla, w_ada, b_ada, norm1_g, w_in, rwkv_mu, rwkv_w0, rwkv_decay_up, rwkv_a0, rwkv_aaa_up, rwkv_gate_up, rwkv_k_k, rwkv_k_a, rwkv_r_k, rwkv_ln_g, rwkv_ln_b, gla_alpha_up, gla_alpha_b, gla_norm_g, w_branch_rwkv, w_branch_gla, w_out, norm2_g, router_group, router_expert, expert_w_gate, expert_w_up, expert_w_down, final_norm_g):
    bp, tp = x_prompt.shape[:2]
    bs, ts = x_sample.shape[:2]
    w_in0 = w_in[0]
    g0 = RWKV_PROJ
    o3 = 2 * GLA_DK_TOTAL + GLA_DV_TOTAL
    o4 = o3 + GLA_GATE_RANK
    w_gla = w_in0[:, g0:g0 + GLA_PROJ]
    w_gla = jnp.concatenate([w_gla[:, :o3], w_gla[:, o4:], w_gla[:, o3:o4],
                             jnp.zeros((D_MODEL, GLA_PROJ_PAD - GLA_PROJ), F32)], axis=1)
    wts = dict(
        norm1_g=norm1_g[0].reshape(1, -1),
        w_in_rwkv=w_in0[:, :g0].astype(BF16),
        w_in_gla=w_gla.astype(BF16),
        w_in_gate=w_in0[:, g0 + GLA_PROJ:].astype(BF16),
        rwkv_mu=rwkv_mu[0], rwkv_w0=rwkv_w0[0], rwkv_decay_up=rwkv_decay_up[0], rwkv_a0=rwkv_a0[0],
        rwkv_aaa_up=rwkv_aaa_up[0], rwkv_gate_up=rwkv_gate_up[0], rwkv_k_k=rwkv_k_k[0],
        rwkv_k_a=rwkv_k_a[0], rwkv_r_k=rwkv_r_k[0], rwkv_ln_g=rwkv_ln_g[0], rwkv_ln_b=rwkv_ln_b[0],
        gla_alpha_up=jnp.pad(gla_alpha_up[0], ((0, LANES - GLA_GATE_RANK), (0, 0))),
        gla_alpha_b=gla_alpha_b[0], gla_norm_g=gla_norm_g[0],
        w_branch_rwkv=w_branch_rwkv[0].astype(BF16), w_branch_gla=w_branch_gla[0].astype(BF16),
        w_out=w_out[0].astype(BF16), norm2_g=norm2_g[0],
        router_cat=_split_hi_lo(jnp.pad(jnp.concatenate([router_expert[0], router_group[0]], axis=1),
                                        ((0, 0), (0, LANES - N_EXPERTS - N_GROUPS)))),
        expert_w_gate=expert_w_gate[0], expert_w_up=expert_w_up[0], expert_w_down=expert_w_down[0],
    )
    mod = _ada(jnp.concatenate([c_prompt, c_sample], axis=0), w_ada[0], b_ada[0])
    mod = mod.reshape(bp + bs, N_MOD, D_MODEL)
    n_p = bp * tp
    n_s = bs * ts
    n_tok = n_p + n_s
    mod_p = mod[:bp].reshape(bp, N_MOD, 1, D_MODEL)
    mod_s = jnp.repeat(mod[bp:].transpose(1, 0, 2), ts, axis=1)[None]

    dt = x_prompt.dtype
    x1_p, h2_all, route_all, states_p = _mixer(
        x_prompt, mod_p, jnp.zeros((bp, 1, RWKV_PROJ), dt),
        jnp.zeros((bp, RWKV_HEADS, RWKV_HEAD_DIM, RWKV_HEAD_DIM), state_rwkv.dtype),
        jnp.zeros((bp, GLA_HEADS, GLA_DK, GLA_DV), state_gla.dtype), wts, n_tok=n_tok, row_off=0)
    x1_s, h2_all, route_all, states_s = _mixer(
        x_sample, mod_s, state_rwkv_shift[0], state_rwkv[0], state_gla[0], wts,
        n_tok=n_tok, row_off=n_p, shared=(h2_all, route_all))

    expert_ids = route_all[:, :2].astype(jnp.int32)
    moe_out = _moe_grouped(h2_all, expert_ids, wts["expert_w_gate"], wts["expert_w_up"],
                           wts["expert_w_down"])
    y_p = _final(x1_p, moe_out, route_all, mod_p, final_norm_g, row_off=0, tm=min(tp, ROW_TILE))
    y_s = _final(x1_s, moe_out, route_all, mod_s, final_norm_g, row_off=n_p, tm=min(n_s, ROW_TILE))
    return (y_p, y_s.reshape(bs, ts, D_MODEL)) + states_p + states_s
```

```python
import functools
import math

import jax
import jax.numpy as jnp
from jax import lax
from jax.experimental import pallas as pl
from jax.experimental.pallas import tpu as pltpu

F32 = jnp.float32
BF16 = jnp.bfloat16

D_MODEL = 1024
N_MOD = 6
EPS = 1e-6
RWKV_HEAD_DIM = 64
RWKV_HEADS = 16
RWKV_PAIRS = RWKV_HEADS // 2
DECAY_LORA = 64
AAA_LORA = 64
GATE_LORA = 128
RWKV_PROJ = 3 * D_MODEL + DECAY_LORA + AAA_LORA + GATE_LORA
LN_X_EPS = 64e-5
GLA_HEADS = 4
GLA_DK = 128
GLA_DV = 256
GLA_DK_TOTAL = GLA_HEADS * GLA_DK
GLA_DV_TOTAL = GLA_HEADS * GLA_DV
GLA_GATE_RANK = 16
GLA_GATE_NORM = 16.0
GLA_PROJ = 2 * GLA_DK_TOTAL + 2 * GLA_DV_TOTAL + GLA_GATE_RANK
LANES = 128
GLA_PROJ_PAD = 2 * GLA_DK_TOTAL + 2 * GLA_DV_TOTAL + LANES
GATE_PROJ = 2 * D_MODEL
N_GROUPS = 4
EXPERTS_PER_GROUP = 8
N_EXPERTS = 32
EXPERT_FF = 512

CHUNK = 64
ROW_TILE = 512
MOE_TILE = 256
DMA_GROUP = 8
MOE_ROW = D_MODEL + LANES
N_PAIR_CLASSES = N_EXPERTS * EXPERTS_PER_GROUP
RWKV_SEQS_PER_STEP = 2
GLA_SEQS_PER_STEP = 4
VMEM_LIMIT = 56 * 1024 * 1024


def _bdot(a, b):
    return jnp.dot(a.astype(BF16), b.astype(BF16), preferred_element_type=F32)


def _bdot_nt(a, b):
    return lax.dot_general(a.astype(BF16), b.astype(BF16), (((1,), (1,)), ((), ())),
                           preferred_element_type=F32)


def _bdot_tn(a, b):
    return jnp.dot(a.T.astype(BF16), b.astype(BF16), preferred_element_type=F32)


def _split3(x):
    h1 = x.astype(BF16)
    r1 = x - h1.astype(F32)
    h2 = r1.astype(BF16)
    h3 = (r1 - h2.astype(F32)).astype(BF16)
    return h1, h2, h3


def _dot3(a, b):
    a1, a2, _ = _split3(a)
    b1, b2, _ = _split3(b)
    return (jnp.dot(a1, b1, preferred_element_type=F32)
            + jnp.dot(a2, b1, preferred_element_type=F32)
            + jnp.dot(a1, b2, preferred_element_type=F32))


def _dot_exact_lhs(a_bf16, x):
    x1, x2, _ = _split3(x)
    return (jnp.dot(a_bf16, x1, preferred_element_type=F32)
            + jnp.dot(a_bf16, x2, preferred_element_type=F32))


def _split_hi_lo(w):
    hi, lo, _ = _split3(w)
    return jnp.concatenate([hi, lo], axis=1)


def _sigmoid(x):
    return 1.0 / (1.0 + jnp.exp(-x))


def _silu(x):
    return x * _sigmoid(x)


def _rms_scale(x):
    return x * lax.rsqrt(jnp.mean(x * x, axis=-1, keepdims=True) + EPS)


def _ada_kernel(c_ref, w_ref, b_ref, o_ref):
    o_ref[...] = _dot3(_silu(c_ref[...]), w_ref[...]) + b_ref[...]


def _ada(c, w_ada, b_ada):
    n_rows = c.shape[0]
    n_out = w_ada.shape[1]
    tn = 1536
    return pl.pallas_call(
        _ada_kernel,
        out_shape=jax.ShapeDtypeStruct((n_rows, n_out), F32),
        grid=(n_out // tn,),
        in_specs=[pl.BlockSpec((n_rows, D_MODEL), lambda j: (0, 0)),
                  pl.BlockSpec((D_MODEL, tn), lambda j: (0, j)),
                  pl.BlockSpec((1, tn), lambda j: (0, j))],
        out_specs=pl.BlockSpec((n_rows, tn), lambda j: (0, j)),
        compiler_params=pltpu.CompilerParams(dimension_semantics=("parallel",),
                                             vmem_limit_bytes=VMEM_LIMIT),
        name="ada",
    )(c, w_ada, b_ada.reshape(1, n_out))


def _mod_spec(mod, tm, seq_tile):
    if mod.shape[2] == 1:
        return pl.BlockSpec((1, N_MOD, 1, D_MODEL), lambda *g: (seq_tile(*g)[0], 0, 0, 0))
    return pl.BlockSpec((1, N_MOD, tm, D_MODEL), lambda *g: (seq_tile(*g)[0], 0, seq_tile(*g)[1], 0))


def _norm_proj_kernel(x_ref, mod_ref, g_ref, w_ref, o_ref, *, gate):
    mod = mod_ref[0]
    h = _rms_scale(x_ref[0]) * g_ref[...] * (1.0 + mod[1]) + mod[0]
    out = _bdot(h, w_ref[...])
    o_ref[0] = _sigmoid(out) if gate else out


def _norm_proj(x, mod, g, w_bf16, *, gate, tm):
    b, t, _ = x.shape
    n = w_bf16.shape[1]
    return pl.pallas_call(
        functools.partial(_norm_proj_kernel, gate=gate),
        out_shape=jax.ShapeDtypeStruct((b, t, n), F32),
        grid=(b, t // tm),
        in_specs=[pl.BlockSpec((1, tm, D_MODEL), lambda i, j: (i, j, 0)),
                  _mod_spec(mod, tm, lambda i, j: (i, j)),
                  pl.BlockSpec((1, D_MODEL), lambda i, j: (0, 0)),
                  pl.BlockSpec((D_MODEL, n), lambda i, j: (0, 0))],
        out_specs=pl.BlockSpec((1, tm, n), lambda i, j: (i, j, 0)),
        compiler_params=pltpu.CompilerParams(dimension_semantics=("parallel", "parallel"),
                                             vmem_limit_bytes=VMEM_LIMIT),
        name="norm_proj_gate" if gate else "norm_proj",
    )(x, mod, g, w_bf16)


def _blockdiag(x, lo_mask):
    return jnp.concatenate([jnp.where(lo_mask, x, 0.0), jnp.where(lo_mask, 0.0, x)], axis=0)


def _rwkv_kernel(p_ref, shift_ref, s0_ref, mu_ref, w0_ref, dup_ref, a0_ref, aup_ref, gup_ref,
                 kk_ref, ka_ref, rk_ref, lng_ref, lnb_ref, y_ref, sout_ref, s_scr, carry_scr,
                 *, nb):
    c = pl.program_id(1)
    n_chunks = pl.num_programs(1)
    C = CHUNK
    N = RWKV_HEAD_DIM
    t_blk = p_ref.shape[1]

    @pl.when(c == 0)
    def _():
        for bl in range(nb):
            for pr in range(RWKV_PAIRS):
                s_scr[bl, pr] = jnp.concatenate([s0_ref[bl, 2 * pr], s0_ref[bl, 2 * pr + 1]], axis=1)
        carry_scr[...] = shift_ref[...]

    row = lax.broadcasted_iota(jnp.int32, (C, LANES), 0)
    lane = lax.broadcasted_iota(jnp.int32, (C, LANES), 1)
    lane_t = jnp.where(lane < N, lane, lane - N)
    lo = lane < N
    strict = lane_t < row
    incl = lane_t <= row
    eye = (lane_t == row).astype(F32)
    valid = row < t_blk
    tri = (lax.broadcasted_iota(jnp.int32, (C, C), 1)
           <= lax.broadcasted_iota(jnp.int32, (C, C), 0)).astype(BF16)
    sq_r = lax.broadcasted_iota(jnp.int32, (LANES, LANES), 0)
    sq_c = lax.broadcasted_iota(jnp.int32, (LANES, LANES), 1)
    same_head = (sq_r < N) == (sq_c < N)
    diag = sq_r == sq_c
    row_w = lax.broadcasted_iota(jnp.int32, (C, RWKV_PROJ), 0)
    o3 = 3 * D_MODEL

    def hsum(v):
        s_lo = jnp.sum(jnp.where(lo, v, 0.0), axis=1, keepdims=True)
        s_hi = jnp.sum(jnp.where(lo, 0.0, v), axis=1, keepdims=True)
        return jnp.where(lo, s_lo, s_hi)

    def bd(v):
        return _blockdiag(v, lo)

    def cat0(*a):
        return jnp.concatenate(a, axis=0)

    def cat1(*a):
        return jnp.concatenate(a, axis=1)

    valid_w = lax.broadcasted_iota(jnp.int32, (C, D_MODEL), 0) < t_blk
    xs, logw_all, cum_all, asig_all, g_all = [], [], [], [], []
    for bl in range(nb):
        p = p_ref[bl]
        carry_new = p[t_blk - 1:t_blk, :]
        if t_blk < C:
            p = jnp.concatenate([p, jnp.zeros((C - t_blk, RWKV_PROJ), F32)], axis=0)
        prev = jnp.where(row_w == 0, carry_scr[bl], pltpu.roll(p, 1, axis=0))
        carry_scr[bl] = carry_new
        x = p + (prev - p) * mu_ref[...]
        xw = x[:, o3:o3 + DECAY_LORA]
        xa = x[:, o3 + DECAY_LORA:o3 + DECAY_LORA + AAA_LORA]
        xg = x[:, o3 + DECAY_LORA + AAA_LORA:]
        z = w0_ref[...] + _bdot(jnp.tanh(xw), dup_ref[...])
        logw_full = -math.exp(-0.5) * _sigmoid(z)
        if t_blk < C:
            logw_full = jnp.where(valid_w, logw_full, 0.0)
        xs.append(x)
        logw_all.append(logw_full)
        cum_all.append(_dot_exact_lhs(tri, logw_full))
        asig_all.append(_sigmoid(a0_ref[...] + _bdot(xa, aup_ref[...])))
        g_all.append(_bdot(_sigmoid(xg), gup_ref[...]))

    units = [(bl, pr) for bl in range(nb) for pr in range(RWKV_PAIRS)]
    U = range(len(units))
    cols = [slice(pr * LANES, (pr + 1) * LANES) for _, pr in units]
    r = [xs[bl][:, pr * LANES:(pr + 1) * LANES] for bl, pr in units]
    k = [xs[bl][:, D_MODEL + pr * LANES:D_MODEL + (pr + 1) * LANES] for bl, pr in units]
    v = [xs[bl][:, 2 * D_MODEL + pr * LANES:2 * D_MODEL + (pr + 1) * LANES] for bl, pr in units]
    logw = [logw_all[bl][:, cols[u]] for u, (bl, _) in enumerate(units)]
    asig = [asig_all[bl][:, cols[u]] for u, (bl, _) in enumerate(units)]
    kk = [k[u] * kk_ref[:, cols[u]] for u in U]
    k2 = [k[u] * (1.0 + (asig[u] - 1.0) * ka_ref[:, cols[u]]) for u in U]
    ss = [hsum(kk[u] * kk[u]) for u in U]
    cum = [cum_all[bl][:, cols[u]] for u, (bl, _) in enumerate(units)]
    bsum = [hsum(r[u] * k2[u] * rk_ref[:, cols[u]]) for u in U]
    kk = [kk[u] * lax.rsqrt(jnp.maximum(ss[u], 1e-24)) for u in U]
    bonus = [bsum[u] * v[u] for u in U]
    if t_blk < C:
        kk = [jnp.where(valid, a, 0.0) for a in kk]
        k2 = [jnp.where(valid, a, 0.0) for a in k2]
        v = [jnp.where(valid, a, 0.0) for a in v]
        r = [jnp.where(valid, a, 0.0) for a in r]
    b_vec = [kk[u] * asig[u] for u in U]
    clast = [cum[u][C - 1:C, :] for u in U]
    e_inv = [jnp.exp(-cum[u]) for u in U]
    e_tail = [jnp.exp(clast[u] - cum[u]) for u in U]
    aq = [-kk[u] * jnp.exp(cum[u] - logw[u]) for u in U]
    rq = [r[u] * jnp.exp(cum[u]) for u in U]
    bk = [b_vec[u] * e_inv[u] for u in U]
    kx = [k2[u] * e_inv[u] for u in U]
    bb = [b_vec[u] * e_tail[u] for u in U]
    kb = [k2[u] * e_tail[u] for u in U]

    g4 = [_bdot_nt(cat0(aq[u], rq[u]), cat0(bd(bk[u]), bd(kx[u]))) for u in U]
    m_ab = [jnp.where(strict, g4[u][:C, :LANES], 0.0) for u in U]
    m_ak = [jnp.where(strict, g4[u][:C, LANES:], 0.0) for u in U]
    m_rb = [jnp.where(incl, g4[u][C:, :LANES], 0.0) for u in U]
    m_rk = [jnp.where(incl, g4[u][C:, LANES:], 0.0) for u in U]

    t_inv = [eye + m_ab[u] for u in U]
    l_pow = [_bdot(m_ab[u], bd(m_ab[u])) for u in U]
    mv = [_bdot(cat0(m_ak[u], m_rk[u]), bd(v[u])) for u in U]
    n_steps = C.bit_length() - 1
    for step in range(1, n_steps):
        if step < n_steps - 1:
            both = [_bdot(l_pow[u], cat1(bd(t_inv[u]), bd(l_pow[u]))) for u in U]
            t_inv = [t_inv[u] + both[u][:, :LANES] for u in U]
            l_pow = [both[u][:, LANES:] for u in U]
        else:
            t_inv = [t_inv[u] + _bdot(l_pow[u], bd(t_inv[u])) for u in U]

    au = [_bdot(t_inv[u], cat1(bd(aq[u]), bd(mv[u][:C]))) for u in U]
    a_bar = [au[u][:, :LANES] for u in U]
    u_bar = [au[u][:, LANES:] for u in U]
    mau = [_bdot(m_rb[u], cat1(bd(a_bar[u]), bd(u_bar[u]))) for u in U]
    p_full = [_bdot_tn(a_bar[u], bb[u]) for u in U]
    q_full = [_bdot_tn(cat0(u_bar[u], v[u]), cat0(bb[u], kb[u])) for u in U]
    r_til = [rq[u] + mau[u][:, :LANES] for u in U]
    y_loc = [mau[u][:, LANES:] + mv[u][C:] for u in U]
    p_bd = [jnp.where(same_head, p_full[u], 0.0) + jnp.where(diag, jnp.exp(clast[u]), 0.0) for u in U]
    q_pk = [jnp.where(lo[:N], q_full[u][:N], q_full[u][N:]) for u in U]

    s_old = [s_scr[bl, pr] for bl, pr in units]
    y = [_bdot_nt(r_til[u], _blockdiag(s_old[u], lo[:N])) + y_loc[u] for u in U]
    s_new = [_bdot(s_old[u], p_bd[u]) + q_pk[u] for u in U]
    for u, (bl, pr) in enumerate(units):
        s_scr[bl, pr] = s_new[u]

    mu_y = [hsum(y[u]) * (1.0 / N) for u in U]
    yc = [y[u] - mu_y[u] for u in U]
    var = [hsum(yc[u] * yc[u]) * (1.0 / N) for u in U]
    for u, (bl, pr) in enumerate(units):
        yn = yc[u] * lax.rsqrt(var[u] + LN_X_EPS) * lng_ref[:, cols[u]] + lnb_ref[:, cols[u]]
        y_ref[bl, :, cols[u]] = ((yn + bonus[u]) * g_all[bl][:, cols[u]])[:t_blk]

    @pl.when(c == n_chunks - 1)
    def _():
        for bl in range(nb):
            for pr in range(RWKV_PAIRS):
                sout_ref[bl, 2 * pr] = s_scr[bl, pr][:, :N]
                sout_ref[bl, 2 * pr + 1] = s_scr[bl, pr][:, N:]


def _rwkv(p_rwkv, shift_prev, s0, mu, w0, decay_up, a0, aaa_up, gate_up, k_k, k_a, r_k,
          ln_g, ln_b, *, nb):
    b, t, _ = p_rwkv.shape
    t_blk = min(t, CHUNK)
    row = lambda a: a.reshape(1, -1)
    full = lambda shape: pl.BlockSpec(shape, lambda i, j: (0,) * len(shape))
    state_spec = pl.BlockSpec((nb, RWKV_HEADS, RWKV_HEAD_DIM, RWKV_HEAD_DIM), lambda i, j: (i, 0, 0, 0))
    return pl.pallas_call(
        functools.partial(_rwkv_kernel, nb=nb),
        out_shape=(jax.ShapeDtypeStruct((b, t, D_MODEL), F32),
                   jax.ShapeDtypeStruct((b, RWKV_HEADS, RWKV_HEAD_DIM, RWKV_HEAD_DIM), F32)),
        grid=(b // nb, t // t_blk),
        in_specs=[pl.BlockSpec((nb, t_blk, RWKV_PROJ), lambda i, j: (i, j, 0)),
                  pl.BlockSpec((nb, 1, RWKV_PROJ), lambda i, j: (i, 0, 0)),
                  state_spec,
                  full((1, RWKV_PROJ)), full((1, D_MODEL)), full((DECAY_LORA, D_MODEL)),
                  full((1, D_MODEL)), full((AAA_LORA, D_MODEL)), full((GATE_LORA, D_MODEL)),
                  full((1, D_MODEL)), full((1, D_MODEL)), full((1, D_MODEL)),
                  full((1, D_MODEL)), full((1, D_MODEL))],
        out_specs=(pl.BlockSpec((nb, t_blk, D_MODEL), lambda i, j: (i, j, 0)), state_spec),
        scratch_shapes=[pltpu.VMEM((nb, RWKV_PAIRS, RWKV_HEAD_DIM, LANES), F32),
                        pltpu.VMEM((nb, 1, RWKV_PROJ), F32)],
        compiler_params=pltpu.CompilerParams(dimension_semantics=("parallel", "arbitrary"),
                                             vmem_limit_bytes=VMEM_LIMIT),
        name="rwkv_scan",
    )(p_rwkv, shift_prev, s0, row(mu), row(w0), decay_up, row(a0), aaa_up, gate_up,
      row(k_k), row(k_a), row(r_k), row(ln_g), row(ln_b))


def _gla_kernel(p_ref, s0_ref, aup_ref, ab_ref, ng_ref, y_ref, sout_ref, s_scr, *, nb):
    c = pl.program_id(1)
    n_chunks = pl.num_programs(1)
    C = CHUNK
    t_blk = p_ref.shape[1]
    units = [(bl, h) for bl in range(nb) for h in range(GLA_HEADS)]
    U = range(len(units))

    @pl.when(c == 0)
    def _():
        for bl, h in units:
            s_scr[bl, h] = s0_ref[bl, h].T

    o_k = GLA_DK_TOTAL
    o_v = 2 * GLA_DK_TOTAL
    o_r = o_v + GLA_DV_TOTAL
    o_a = o_r + GLA_DV_TOTAL
    tri = (lax.broadcasted_iota(jnp.int32, (C, C), 1)
           <= lax.broadcasted_iota(jnp.int32, (C, C), 0))
    tri_bf = tri.astype(BF16)
    valid_g = lax.broadcasted_iota(jnp.int32, (C, GLA_DK_TOTAL), 0) < t_blk
    valid_k = lax.broadcasted_iota(jnp.int32, (C, GLA_DK), 0) < t_blk
    valid_v = lax.broadcasted_iota(jnp.int32, (C, GLA_DV), 0) < t_blk

    ps, cums = [], []
    for bl in range(nb):
        p = p_ref[bl]
        if t_blk < C:
            p = jnp.concatenate([p, jnp.zeros((C - t_blk, GLA_PROJ_PAD), F32)], axis=0)
        logit = _dot3(p[:, o_a:], aup_ref[...]) + ab_ref[...]
        g = (jnp.minimum(logit, 0.0) - jnp.log(1.0 + jnp.exp(-jnp.abs(logit)))) * (1.0 / GLA_GATE_NORM)
        if t_blk < C:
            g = jnp.where(valid_g, g, 0.0)
        ps.append(p)
        cums.append(_dot_exact_lhs(tri_bf, g))

    q = [ps[bl][:, h * GLA_DK:(h + 1) * GLA_DK] * (GLA_DK ** -0.5) for bl, h in units]
    k = [ps[bl][:, o_k + h * GLA_DK:o_k + (h + 1) * GLA_DK] for bl, h in units]
    v = [ps[bl][:, o_v + h * GLA_DV:o_v + (h + 1) * GLA_DV] for bl, h in units]
    if t_blk < C:
        k = [jnp.where(valid_k, a, 0.0) for a in k]
        v = [jnp.where(valid_v, a, 0.0) for a in v]
    b = [cums[bl][:, h * GLA_DK:(h + 1) * GLA_DK] for bl, h in units]
    clast = [b[u][C - 1:C, :] for u in U]
    cmid = [b[u][C // 2 - 1:C // 2, :] for u in U]
    att = [_bdot_nt(q[u] * jnp.exp(b[u] - cmid[u]), k[u] * jnp.exp(cmid[u] - b[u])) for u in U]
    att = [jnp.where(tri, a, 0.0) for a in att]
    s_t = [s_scr[bl, h] for bl, h in units]
    o = [_bdot(att[u], v[u]) + _bdot_nt(q[u] * jnp.exp(b[u]), s_t[u]) for u in U]
    s_new = [s_t[u] * jnp.exp(clast[u]) + _bdot_tn(v[u], k[u] * jnp.exp(clast[u] - b[u])) for u in U]
    for u, (bl, h) in enumerate(units):
        s_scr[bl, h] = s_new[u]
        r = ps[bl][:, o_r + h * GLA_DV:o_r + (h + 1) * GLA_DV]
        on = o[u] * lax.rsqrt(jnp.mean(o[u] * o[u], axis=-1, keepdims=True) + EPS) * ng_ref[...]
        y_ref[bl, :, h * GLA_DV:(h + 1) * GLA_DV] = (on * _silu(r))[:t_blk]

    @pl.when(c == n_chunks - 1)
    def _():
        for bl, h in units:
            sout_ref[bl, h] = s_scr[bl, h].T


def _gla(p_gla, s0, alpha_up_pad, alpha_b, norm_g, *, nb):
    b, t, _ = p_gla.shape
    t_blk = min(t, CHUNK)
    full = lambda shape: pl.BlockSpec(shape, lambda i, j: (0,) * len(shape))
    state_spec = pl.BlockSpec((nb, GLA_HEADS, GLA_DK, GLA_DV), lambda i, j: (i, 0, 0, 0))
    return pl.pallas_call(
        functools.partial(_gla_kernel, nb=nb),
        out_shape=(jax.ShapeDtypeStruct((b, t, GLA_DV_TOTAL), F32),
                   jax.ShapeDtypeStruct((b, GLA_HEADS, GLA_DK, GLA_DV), F32)),
        grid=(b // nb, t // t_blk),
        in_specs=[pl.BlockSpec((nb, t_blk, GLA_PROJ_PAD), lambda i, j: (i, j, 0)),
                  state_spec,
                  full((LANES, GLA_DK_TOTAL)), full((1, GLA_DK_TOTAL)), full((1, GLA_DV))],
        out_specs=(pl.BlockSpec((nb, t_blk, GLA_DV_TOTAL), lambda i, j: (i, j, 0)), state_spec),
        scratch_shapes=[pltpu.VMEM((nb, GLA_HEADS, GLA_DV, GLA_DK), F32)],
        compiler_params=pltpu.CompilerParams(dimension_semantics=("parallel", "arbitrary"),
                                             vmem_limit_bytes=VMEM_LIMIT),
        name="gla_scan",
    )(p_gla, s0, alpha_up_pad, alpha_b.reshape(1, -1), norm_g.reshape(1, -1))


def _merge_body(yr_ref, yg_ref, gate_ref, x_ref, mod_ref, wbr_ref, wbg_ref, wout_ref, g2_ref,
                router_ref, x1_ref, h2_ref, route_ref):
    mod = mod_ref[0]
    gates = gate_ref[0]
    merged = (gates[:, :D_MODEL] * _bdot(yr_ref[0], wbr_ref[...])
              + gates[:, D_MODEL:] * _bdot(yg_ref[0], wbg_ref[...]))
    x1 = x_ref[0] + mod[2] * _bdot(merged, wout_ref[...])
    x1_ref[0] = x1
    h2 = _rms_scale(x1) * g2_ref[...] * (1.0 + mod[4]) + mod[3]
    h2_ref[:, :D_MODEL] = h2

    tm = h2.shape[0]
    h_hi, h_lo, _ = _split3(h2)
    prod = jnp.dot(jnp.concatenate([h_hi, h_lo], axis=0), router_ref[...], preferred_element_type=F32)
    logits = prod[:tm, :LANES] + prod[tm:, :LANES] + prod[:tm, LANES:]
    lane = lax.broadcasted_iota(jnp.int32, (tm, LANES), 1)
    neg = -jnp.inf
    is_group = (lane >= N_EXPERTS) & (lane < N_EXPERTS + N_GROUPS)
    gl = jnp.where(is_group, logits, neg)
    gmax = jnp.max(gl, axis=1, keepdims=True)
    g_idx = jnp.min(jnp.where(gl == gmax, lane, LANES), axis=1, keepdims=True) - N_EXPERTS
    p_g = 1.0 / jnp.sum(jnp.exp(gl - gmax), axis=1, keepdims=True)
    in_group = (lane >= g_idx * EXPERTS_PER_GROUP) & (lane < (g_idx + 1) * EXPERTS_PER_GROUP)
    el = jnp.where(in_group, logits, neg)
    v1 = jnp.max(el, axis=1, keepdims=True)
    i1 = jnp.min(jnp.where(el == v1, lane, LANES), axis=1, keepdims=True)
    el2 = jnp.where(lane == i1, neg, el)
    v2 = jnp.max(el2, axis=1, keepdims=True)
    i2 = jnp.min(jnp.where(el2 == v2, lane, LANES), axis=1, keepdims=True)
    e21 = jnp.exp(v2 - v1)
    w1 = p_g / (1.0 + e21)
    route = (jnp.where(lane == 0, i1.astype(F32), 0.0) + jnp.where(lane == 1, i2.astype(F32), 0.0)
             + jnp.where(lane == 2, w1, 0.0) + jnp.where(lane == 3, w1 * e21, 0.0))
    route_ref[...] = route
    h2_ref[:, D_MODEL:] = route


def _merge_first_kernel(*refs, n_own):
    step = pl.program_id(0)

    @pl.when(step < n_own)
    def _():
        _merge_body(*refs)

    @pl.when(step >= n_own)
    def _():
        refs[-2][...] = jnp.zeros(refs[-2].shape, F32)
        refs[-1][...] = jnp.zeros(refs[-1].shape, F32)


def _merge_second_kernel(*refs):
    n_in = 10
    _merge_body(*refs[:n_in], *refs[n_in + 2:])


def _merge(y_r, y_g, gates, x, mod, wbr, wbg, wout, norm2_g, router_cat, *, tm, n_tok, row_off, shared=None):
    b, t, _ = x.shape
    per_seq = t // tm
    n_own = b * per_seq
    first = shared is None
    n_steps = n_tok // tm if first else n_own
    assert row_off % tm == 0 and n_tok % tm == 0 and (row_off == 0 or not first)
    blk0 = row_off // tm
    own = lambda s: jnp.minimum(s, n_own - 1)
    full = lambda shape: pl.BlockSpec(shape, lambda s: (0,) * len(shape))
    tile = lambda n: pl.BlockSpec((1, tm, n), lambda s: (own(s) // per_seq, own(s) % per_seq, 0))
    flat = lambda n: pl.BlockSpec((tm, n), lambda s: (blk0 + s, 0))
    in_specs = [tile(D_MODEL), tile(D_MODEL), tile(GATE_PROJ), tile(D_MODEL),
                _mod_spec(mod, tm, lambda s: (own(s) // per_seq, own(s) % per_seq)),
                full((D_MODEL, D_MODEL)), full((D_MODEL, D_MODEL)), full((D_MODEL, D_MODEL)),
                full((1, D_MODEL)), full((D_MODEL, 2 * LANES))]
    args = [y_r, y_g, gates, x, mod, wbr, wbg, wout, norm2_g.reshape(1, -1), router_cat]
    if not first:
        in_specs += [pl.BlockSpec(memory_space=pl.ANY)] * 2
        args += list(shared)
    return pl.pallas_call(
        functools.partial(_merge_first_kernel, n_own=n_own) if first else _merge_second_kernel,
        out_shape=(jax.ShapeDtypeStruct((b, t, D_MODEL), F32),
                   jax.ShapeDtypeStruct((n_tok, MOE_ROW), F32),
                   jax.ShapeDtypeStruct((n_tok, LANES), F32)),
        grid=(n_steps,),
        in_specs=in_specs,
        out_specs=(tile(D_MODEL), flat(MOE_ROW), flat(LANES)),
        input_output_aliases={} if first else {10: 1, 11: 2},
        compiler_params=pltpu.CompilerParams(dimension_semantics=("arbitrary",),
                                             vmem_limit_bytes=VMEM_LIMIT),
        name="merge_router",
    )(*args)


def _moe_plan(expert_ids, n_tok):
    tm = MOE_TILE
    n_tiles = -(-n_tok // tm) + N_GROUPS * (EXPERTS_PER_GROUP * (EXPERTS_PER_GROUP - 1) // 2)
    e_lo = jnp.minimum(expert_ids[:, 0], expert_ids[:, 1])
    e_hi = jnp.maximum(expert_ids[:, 0], expert_ids[:, 1])
    keys = e_lo * EXPERTS_PER_GROUP + e_hi % EXPERTS_PER_GROUP
    t_iota = jnp.arange(n_tok, dtype=jnp.int32)
    _, order = lax.top_k(-(keys * n_tok + t_iota).astype(F32), n_tok)
    c_iota = jnp.arange(N_PAIR_CLASSES, dtype=jnp.int32)
    counts = jnp.sum((keys[None, :] == c_iota[:, None]).astype(jnp.int32), axis=1)
    tiles_c = (counts + tm - 1) // tm
    tile_end = jnp.cumsum(tiles_c)
    row_start = jnp.cumsum(counts) - counts
    n_used = tile_end[-1]
    t_idx = jnp.arange(n_tiles, dtype=jnp.int32)
    t_eff = jnp.minimum(t_idx, n_used - 1)
    tile_c = jnp.minimum(jnp.sum((t_eff[:, None] >= tile_end[None, :]).astype(jnp.int32), axis=1),
                         N_PAIR_CLASSES - 1)
    onehot = tile_c[:, None] == c_iota[None, :]
    pick = lambda vec: jnp.sum(jnp.where(onehot, vec[None, :], 0), axis=1)
    local = t_idx - pick(tile_end - tiles_c)
    n_valid = jnp.where(t_idx < n_used, jnp.clip(pick(counts) - local * tm, 0, tm), 0)
    j_idx = jnp.arange(tm, dtype=jnp.int32)[None, :]
    pos = jnp.clip(pick(row_start)[:, None] + local[:, None] * tm + j_idx, 0, n_tok - 1)
    valid = j_idx < n_valid[:, None]
    tok = jnp.where(valid, order[pos], 0)
    dst = jnp.where(valid, tok, n_tok + (t_idx[:, None] % 2) * DMA_GROUP + j_idx % DMA_GROUP)
    n_groups = (n_valid + DMA_GROUP - 1) // DMA_GROUP
    tile_lo = tile_c // EXPERTS_PER_GROUP
    tile_hi = (tile_c // (EXPERTS_PER_GROUP * EXPERTS_PER_GROUP)) * EXPERTS_PER_GROUP + tile_c % EXPERTS_PER_GROUP
    return (tile_lo, tile_hi, n_groups.astype(jnp.int32), n_used.reshape(1).astype(jnp.int32),
            tok.reshape(n_tiles, 1, tm), dst.reshape(n_tiles, 1, tm))


def _moe_pair_kernel(lo_ref, hi_ref, ng_ref, n_used_ref, tok_ref, tok_next_ref, dst_ref, h2_hbm,
                     wg_lo_ref, wu_lo_ref, wd_lo_ref, wg_hi_ref, wu_hi_ref, wd_hi_ref, out_hbm,
                     xbuf, obuf, w_bf, gsem, ssem):
    t = pl.program_id(0)
    n_t = pl.num_programs(0)
    tm = MOE_TILE
    slot = lax.rem(t, 2)
    n_used = n_used_ref[0]

    def gather(idx_ref, s, n_groups):
        def body(g, carry):
            for u in range(DMA_GROUP):
                pltpu.make_async_copy(h2_hbm.at[pl.ds(idx_ref[0, 0, g * DMA_GROUP + u], 1)],
                                      xbuf.at[s, g, pl.ds(u, 1)], gsem.at[s]).start()
            return carry
        lax.fori_loop(0, n_groups, body, 0)

    def wait_gather(s, n_groups):
        def body(g, carry):
            pltpu.make_async_copy(h2_hbm.at[pl.ds(0, DMA_GROUP)], xbuf.at[s, 0], gsem.at[s]).wait()
            return carry
        lax.fori_loop(0, n_groups, body, 0)

    def wait_scatter(s, n_groups):
        def body(g, carry):
            pltpu.make_async_copy(obuf.at[s, 0], out_hbm.at[pl.ds(0, DMA_GROUP)], ssem.at[s]).wait()
            return carry
        lax.fori_loop(0, n_groups, body, 0)

    @pl.when(t == 0)
    def _():
        xbuf[...] = jnp.zeros(xbuf.shape, F32)
        obuf[0, 0] = jnp.zeros(obuf.shape[2:], F32)
        n_rows = out_hbm.shape[0]
        for half in range(2):
            fill = pltpu.make_async_copy(obuf.at[0, 0], out_hbm.at[pl.ds(n_rows - (2 - half) * DMA_GROUP, DMA_GROUP)],
                                         ssem.at[0])
            fill.start()
            fill.wait()
        gather(tok_ref, 0, ng_ref[0])

    @pl.when((t >= 2) & (t - 2 < n_used))
    def _():
        wait_scatter(slot, ng_ref[jnp.maximum(t - 2, 0)])

    @pl.when(t < n_used)
    def _():
        n_here = ng_ref[t]
        wait_gather(slot, n_here)

        @pl.when(t + 1 < n_used)
        def _():
            gather(tok_next_ref, 1 - slot, ng_ref[jnp.minimum(t + 1, n_t - 1)])

        t_prev = jnp.maximum(t - 1, 0)
        for which, (e_ref, refs) in enumerate(((lo_ref, (wg_lo_ref, wu_lo_ref, wd_lo_ref)),
                                               (hi_ref, (wg_hi_ref, wu_hi_ref, wd_hi_ref)))):
            @pl.when((t == 0) | (e_ref[t] != e_ref[t_prev]))
            def _():
                for m, ref in enumerate(refs):
                    w_bf[which][m][...] = ref[0].astype(BF16)

        xe = xbuf[slot].reshape(tm, MOE_ROW)
        x = xe[:, :D_MODEL].astype(BF16)
        route = xe[:, D_MODEL:]
        lane = lax.broadcasted_iota(jnp.int32, route.shape, 1)
        pick = lambda k: jnp.sum(jnp.where(lane == k, route, 0.0), axis=1, keepdims=True)
        i1, i2, w1, w2 = pick(0), pick(1), pick(2), pick(3)
        w_lo = jnp.where(i1 < i2, w1, w2)
        w_hi = jnp.where(i1 < i2, w2, w1)
        acc = None
        for which, wt in enumerate((w_lo, w_hi)):
            wg, wu, wd = w_bf[which]
            hid = _silu(jnp.dot(x, wg[...], preferred_element_type=F32)) * jnp.dot(
                x, wu[...], preferred_element_type=F32)
            part = wt * _bdot(hid, wd[...])
            acc = part if acc is None else acc + part
        obuf[slot] = acc.reshape(obuf.shape[1:])

        def body(g, carry):
            for u in range(DMA_GROUP):
                pltpu.make_async_copy(obuf.at[slot, g, pl.ds(u, 1)],
                                      out_hbm.at[pl.ds(dst_ref[0, 0, g * DMA_GROUP + u], 1)], ssem.at[slot]).start()
            return carry
        lax.fori_loop(0, n_here, body, 0)

    @pl.when(t == n_t - 1)
    def _():
        @pl.when((t >= 1) & (t - 1 < n_used))
        def _():
            wait_scatter(1 - slot, ng_ref[jnp.maximum(t - 1, 0)])

        @pl.when(t < n_used)
        def _():
            wait_scatter(slot, ng_ref[t])


def _moe_grouped(h2_ext, expert_ids, wg, wu, wd):
    return _moe_call(h2_ext, *_moe_plan(expert_ids, h2_ext.shape[0]), wg, wu, wd)


def _moe_call(h2_ext, tile_lo, tile_hi, n_groups, n_used, tok, dst, wg, wu, wd):
    n_tok = h2_ext.shape[0]
    tm = MOE_TILE
    n_tiles = tile_lo.shape[0]
    smem_tile = lambda f: pl.BlockSpec((1, 1, tm), f, memory_space=pltpu.SMEM)
    w_lo = lambda shape: pl.BlockSpec(shape, lambda t, lo, hi, nv, nu: (lo[t], 0, 0))
    w_hi = lambda shape: pl.BlockSpec(shape, lambda t, lo, hi, nv, nu: (hi[t], 0, 0))
    up, down = (1, D_MODEL, EXPERT_FF), (1, EXPERT_FF, D_MODEL)
    bf_set = [pltpu.VMEM(up[1:], BF16), pltpu.VMEM(up[1:], BF16), pltpu.VMEM(down[1:], BF16)]
    grid_spec = pltpu.PrefetchScalarGridSpec(
        num_scalar_prefetch=4,
        grid=(n_tiles,),
        in_specs=[smem_tile(lambda t, *_: (t, 0, 0)),
                  smem_tile(lambda t, *_: (jnp.minimum(t + 1, n_tiles - 1), 0, 0)),
                  smem_tile(lambda t, *_: (t, 0, 0)),
                  pl.BlockSpec(memory_space=pl.ANY),
                  w_lo(up), w_lo(up), w_lo(down), w_hi(up), w_hi(up), w_hi(down)],
        out_specs=pl.BlockSpec(memory_space=pl.ANY),
        scratch_shapes=[pltpu.VMEM((2, tm // DMA_GROUP, DMA_GROUP, MOE_ROW), F32),
                        pltpu.VMEM((2, tm // DMA_GROUP, DMA_GROUP, D_MODEL), F32),
                        [bf_set, bf_set],
                        pltpu.SemaphoreType.DMA((2,)), pltpu.SemaphoreType.DMA((2,))],
    )
    return pl.pallas_call(
        _moe_pair_kernel,
        out_shape=jax.ShapeDtypeStruct((n_tok + 2 * DMA_GROUP, D_MODEL), F32),
        grid_spec=grid_spec,
        compiler_params=pltpu.CompilerParams(dimension_semantics=("arbitrary",),
                                             vmem_limit_bytes=VMEM_LIMIT),
        name="moe_pairs",
    )(tile_lo, tile_hi, n_groups, n_used, tok, tok, dst, h2_ext, wg, wu, wd, wg, wu, wd)


def _final_kernel(x1_ref, moe_ref, mod_ref, fg_ref, y_ref):
    x2 = x1_ref[0] + mod_ref[0][5] * moe_ref[...]
    y_ref[0] = _rms_scale(x2) * fg_ref[...]


def _final(x1, moe_out, mod, final_g, *, row_off, tm):
    b, t, _ = x1.shape
    blk0 = row_off // tm
    per_seq = t // tm
    return pl.pallas_call(
        _final_kernel,
        out_shape=jax.ShapeDtypeStruct((b, t, D_MODEL), F32),
        grid=(b, per_seq),
        in_specs=[pl.BlockSpec((1, tm, D_MODEL), lambda i, j: (i, j, 0)),
                  pl.BlockSpec((tm, D_MODEL), lambda i, j: (blk0 + i * per_seq + j, 0)),
                  _mod_spec(mod, tm, lambda i, j: (i, j)),
                  pl.BlockSpec((1, D_MODEL), lambda i, j: (0, 0))],
        out_specs=pl.BlockSpec((1, tm, D_MODEL), lambda i, j: (i, j, 0)),
        compiler_params=pltpu.CompilerParams(dimension_semantics=("parallel", "parallel"),
                                             vmem_limit_bytes=VMEM_LIMIT),
        name="moe_combine_norm",
    )(x1, moe_out, mod, final_g.reshape(1, -1))


def _mixer(x, mod, shift0, s_rwkv0, s_gla0, wts, *, n_tok, row_off, shared=None):
    b, t, _ = x.shape
    per_token = mod.shape[2] > 1
    rows = x.reshape(1, b * t, D_MODEL) if per_token else x
    tm = min(rows.shape[1], ROW_TILE)
    p_rwkv = _norm_proj(rows, mod, wts["norm1_g"], wts["w_in_rwkv"], gate=False, tm=tm)
    p_gla = _norm_proj(rows, mod, wts["norm1_g"], wts["w_in_gla"], gate=False, tm=tm)
    gates = _norm_proj(rows, mod, wts["norm1_g"], wts["w_in_gate"], gate=True, tm=tm)
    p_rwkv = p_rwkv.reshape(b, t, RWKV_PROJ)

    y_r, s_rwkv = _rwkv(p_rwkv, shift0, s_rwkv0,
                        wts["rwkv_mu"], wts["rwkv_w0"], wts["rwkv_decay_up"], wts["rwkv_a0"],
                        wts["rwkv_aaa_up"], wts["rwkv_gate_up"], wts["rwkv_k_k"], wts["rwkv_k_a"],
                        wts["rwkv_r_k"], wts["rwkv_ln_g"], wts["rwkv_ln_b"], nb=RWKV_SEQS_PER_STEP)
    y_g, s_gla = _gla(p_gla.reshape(b, t, GLA_PROJ_PAD), s_gla0, wts["gla_alpha_up"],
                      wts["gla_alpha_b"], wts["gla_norm_g"], nb=GLA_SEQS_PER_STEP)

    x1, h2_all, route_all = _merge(
        y_r.reshape(rows.shape), y_g.reshape(rows.shape), gates, rows, mod, wts["w_branch_rwkv"],
        wts["w_branch_gla"], wts["w_out"], wts["norm2_g"], wts["router_cat"],
        tm=tm, n_tok=n_tok, row_off=row_off, shared=shared)
    states = (p_rwkv[:, t - 1:t, :][None], s_rwkv[None], s_gla[None])
    return x1, h2_all, route_all, states


def kernel(x_prompt, x_sample, c_prompt, c_sample, state_rwkv_shift, state_rwkv, state_gla, w_ada, b_ada, norm1_g, w_in, rwkv_mu, rwkv_w0, rwkv_decay_up, rwkv_a0, rwkv_aaa_up, rwkv_gate_up, rwkv_k_k, rwkv_k_a, rwkv_r_k, rwkv_ln_g, rwkv_ln_b, gla_alpha_up, gla_alpha_b, gla_norm_g, w_branch_rwkv, w_branch_gla, w_out, norm2_g, router_group, router_expert, expert_w_gate, expert_w_up, expert_w_down, final_norm_g):
    bp, tp = x_prompt.shape[:2]
    bs, ts = x_sample.shape[:2]
    w_in0 = w_in[0]
    g0 = RWKV_PROJ
    o3 = 2 * GLA_DK_TOTAL + GLA_DV_TOTAL
    o4 = o3 + GLA_GATE_RANK
    w_gla = w_in0[:, g0:g0 + GLA_PROJ]
    w_gla = jnp.concatenate([w_gla[:, :o3], w_gla[:, o4:], w_gla[:, o3:o4],
                             jnp.zeros((D_MODEL, GLA_PROJ_PAD - GLA_PROJ), F32)], axis=1)
    wts = dict(
        norm1_g=norm1_g[0].reshape(1, -1),
        w_in_rwkv=w_in0[:, :g0].astype(BF16),
        w_in_gla=w_gla.astype(BF16),
        w_in_gate=w_in0[:, g0 + GLA_PROJ:].astype(BF16),
        rwkv_mu=rwkv_mu[0], rwkv_w0=rwkv_w0[0], rwkv_decay_up=rwkv_decay_up[0], rwkv_a0=rwkv_a0[0],
        rwkv_aaa_up=rwkv_aaa_up[0], rwkv_gate_up=rwkv_gate_up[0], rwkv_k_k=rwkv_k_k[0],
        rwkv_k_a=rwkv_k_a[0], rwkv_r_k=rwkv_r_k[0], rwkv_ln_g=rwkv_ln_g[0], rwkv_ln_b=rwkv_ln_b[0],
        gla_alpha_up=jnp.pad(gla_alpha_up[0], ((0, LANES - GLA_GATE_RANK), (0, 0))),
        gla_alpha_b=gla_alpha_b[0], gla_norm_g=gla_norm_g[0],
        w_branch_rwkv=w_branch_rwkv[0].astype(BF16), w_branch_gla=w_branch_gla[0].astype(BF16),
        w_out=w_out[0].astype(BF16), norm2_g=norm2_g[0],
        router_cat=_split_hi_lo(jnp.pad(jnp.concatenate([router_expert[0], router_group[0]], axis=1),
                                        ((0, 0), (0, LANES - N_EXPERTS - N_GROUPS)))),
        expert_w_gate=expert_w_gate[0], expert_w_up=expert_w_up[0], expert_w_down=expert_w_down[0],
    )
    mod = _ada(jnp.concatenate([c_prompt, c_sample], axis=0), w_ada[0], b_ada[0])
    mod = mod.reshape(bp + bs, N_MOD, D_MODEL)
    n_p = bp * tp
    n_s = bs * ts
    n_tok = n_p + n_s
    mod_p = mod[:bp].reshape(bp, N_MOD, 1, D_MODEL)
    mod_s = jnp.repeat(mod[bp:].transpose(1, 0, 2), ts, axis=1)[None]

    dt = x_prompt.dtype
    x1_p, h2_all, route_all, states_p = _mixer(
        x_prompt, mod_p, jnp.zeros((bp, 1, RWKV_PROJ), dt),
        jnp.zeros((bp, RWKV_HEADS, RWKV_HEAD_DIM, RWKV_HEAD_DIM), state_rwkv.dtype),
        jnp.zeros((bp, GLA_HEADS, GLA_DK, GLA_DV), state_gla.dtype), wts, n_tok=n_tok, row_off=0)
    x1_s, h2_all, route_all, states_s = _mixer(
        x_sample, mod_s, state_rwkv_shift[0], state_rwkv[0], state_gla[0], wts,
        n_tok=n_tok, row_off=n_p, shared=(h2_all, route_all))

    expert_ids = route_all[:, :2].astype(jnp.int32)
    moe_out = _moe_grouped(h2_all, expert_ids, wts["expert_w_gate"], wts["expert_w_up"],
                           wts["expert_w_down"])
    y_p = _final(x1_p, moe_out, mod_p, final_norm_g, row_off=0, tm=min(tp, ROW_TILE))
    y_s = _final(x1_s, moe_out, mod_s, final_norm_g, row_off=n_p, tm=min(n_s, ROW_TILE))
    return (y_p, y_s.reshape(bs, ts, D_MODEL)) + states_p + states_s
```

```python
import functools
import math

import jax
import jax.numpy as jnp
from jax import lax
from jax.experimental import pallas as pl
from jax.experimental.pallas import tpu as pltpu

F32 = jnp.float32
BF16 = jnp.bfloat16

D_MODEL = 1024
N_MOD = 6
EPS = 1e-6
RWKV_HEAD_DIM = 64
RWKV_HEADS = 16
RWKV_PAIRS = RWKV_HEADS // 2
DECAY_LORA = 64
AAA_LORA = 64
GATE_LORA = 128
RWKV_PROJ = 3 * D_MODEL + DECAY_LORA + AAA_LORA + GATE_LORA
LN_X_EPS = 64e-5
GLA_HEADS = 4
GLA_DK = 128
GLA_DV = 256
GLA_DK_TOTAL = GLA_HEADS * GLA_DK
GLA_DV_TOTAL = GLA_HEADS * GLA_DV
GLA_GATE_RANK = 16
GLA_GATE_NORM = 16.0
GLA_PROJ = 2 * GLA_DK_TOTAL + 2 * GLA_DV_TOTAL + GLA_GATE_RANK
LANES = 128
GLA_PROJ_PAD = 2 * GLA_DK_TOTAL + 2 * GLA_DV_TOTAL + LANES
GATE_PROJ = 2 * D_MODEL
N_GROUPS = 4
EXPERTS_PER_GROUP = 8
N_EXPERTS = 32
EXPERT_FF = 512

CHUNK = 64
ROW_TILE = 512
MOE_TILE = 256
DMA_GROUP = 8
MOE_ROW = D_MODEL + LANES
N_PAIR_CLASSES = N_EXPERTS * EXPERTS_PER_GROUP
RWKV_SEQS_PER_STEP = 2
GLA_SEQS_PER_STEP = 4
VMEM_LIMIT = 56 * 1024 * 1024


def _bdot(a, b):
    return jnp.dot(a.astype(BF16), b.astype(BF16), preferred_element_type=F32)


def _bdot_nt(a, b):
    return lax.dot_general(a.astype(BF16), b.astype(BF16), (((1,), (1,)), ((), ())),
                           preferred_element_type=F32)


def _bdot_tn(a, b):
    return jnp.dot(a.T.astype(BF16), b.astype(BF16), preferred_element_type=F32)


def _split3(x):
    h1 = x.astype(BF16)
    r1 = x - h1.astype(F32)
    h2 = r1.astype(BF16)
    h3 = (r1 - h2.astype(F32)).astype(BF16)
    return h1, h2, h3


def _dot3(a, b):
    a1, a2, _ = _split3(a)
    b1, b2, _ = _split3(b)
    return (jnp.dot(a1, b1, preferred_element_type=F32)
            + jnp.dot(a2, b1, preferred_element_type=F32)
            + jnp.dot(a1, b2, preferred_element_type=F32))


def _dot_exact_lhs(a_bf16, x):
    x1, x2, _ = _split3(x)
    return (jnp.dot(a_bf16, x1, preferred_element_type=F32)
            + jnp.dot(a_bf16, x2, preferred_element_type=F32))


def _split_hi_lo(w):
    hi, lo, _ = _split3(w)
    return jnp.concatenate([hi, lo], axis=1)


def _sigmoid(x):
    return 1.0 / (1.0 + jnp.exp(-x))


def _silu(x):
    return x * _sigmoid(x)


def _rms_scale(x):
    return x * lax.rsqrt(jnp.mean(x * x, axis=-1, keepdims=True) + EPS)


def _ada_kernel(c_ref, w_ref, b_ref, o_ref):
    o_ref[...] = _dot3(_silu(c_ref[...]), w_ref[...]) + b_ref[...]


def _ada(c, w_ada, b_ada):
    n_rows = c.shape[0]
    n_out = w_ada.shape[1]
    tn = 1536
    return pl.pallas_call(
        _ada_kernel,
        out_shape=jax.ShapeDtypeStruct((n_rows, n_out), F32),
        grid=(n_out // tn,),
        in_specs=[pl.BlockSpec((n_rows, D_MODEL), lambda j: (0, 0)),
                  pl.BlockSpec((D_MODEL, tn), lambda j: (0, j)),
                  pl.BlockSpec((1, tn), lambda j: (0, j))],
        out_specs=pl.BlockSpec((n_rows, tn), lambda j: (0, j)),
        compiler_params=pltpu.CompilerParams(dimension_semantics=("parallel",),
                                             vmem_limit_bytes=VMEM_LIMIT),
        name="ada",
    )(c, w_ada, b_ada.reshape(1, n_out))


def _mod_spec(mod, tm, seq_tile):
    if mod.shape[2] == 1:
        return pl.BlockSpec((1, N_MOD, 1, D_MODEL), lambda *g: (seq_tile(*g)[0], 0, 0, 0))
    return pl.BlockSpec((1, N_MOD, tm, D_MODEL), lambda *g: (seq_tile(*g)[0], 0, seq_tile(*g)[1], 0))


def _norm_proj_kernel(x_ref, mod_ref, g_ref, w_ref, o_ref, *, gate):
    mod = mod_ref[0]
    h = _rms_scale(x_ref[0]) * g_ref[...] * (1.0 + mod[1]) + mod[0]
    out = _bdot(h, w_ref[...])
    o_ref[0] = _sigmoid(out) if gate else out


def _norm_proj(x, mod, g, w_bf16, *, gate, tm):
    b, t, _ = x.shape
    n = w_bf16.shape[1]
    return pl.pallas_call(
        functools.partial(_norm_proj_kernel, gate=gate),
        out_shape=jax.ShapeDtypeStruct((b, t, n), F32),
        grid=(b, t // tm),
        in_specs=[pl.BlockSpec((1, tm, D_MODEL), lambda i, j: (i, j, 0)),
                  _mod_spec(mod, tm, lambda i, j: (i, j)),
                  pl.BlockSpec((1, D_MODEL), lambda i, j: (0, 0)),
                  pl.BlockSpec((D_MODEL, n), lambda i, j: (0, 0))],
        out_specs=pl.BlockSpec((1, tm, n), lambda i, j: (i, j, 0)),
        compiler_params=pltpu.CompilerParams(dimension_semantics=("parallel", "parallel"),
                                             vmem_limit_bytes=VMEM_LIMIT),
        name="norm_proj_gate" if gate else "norm_proj",
    )(x, mod, g, w_bf16)


def _blockdiag(x, lo_mask):
    return jnp.concatenate([jnp.where(lo_mask, x, 0.0), jnp.where(lo_mask, 0.0, x)], axis=0)


def _rwkv_kernel(p_ref, shift_ref, s0_ref, mu_ref, w0_ref, dup_ref, a0_ref, aup_ref, gup_ref,
                 kk_ref, ka_ref, rk_ref, lng_ref, lnb_ref, y_ref, sout_ref, s_scr, carry_scr,
                 *, nb):
    c = pl.program_id(1)
    n_chunks = pl.num_programs(1)
    C = CHUNK
    N = RWKV_HEAD_DIM
    t_blk = p_ref.shape[1]

    @pl.when(c == 0)
    def _():
        for bl in range(nb):
            for pr in range(RWKV_PAIRS):
                s_scr[bl, pr] = jnp.concatenate([s0_ref[bl, 2 * pr], s0_ref[bl, 2 * pr + 1]], axis=1)
        carry_scr[...] = shift_ref[...]

    row = lax.broadcasted_iota(jnp.int32, (C, LANES), 0)
    lane = lax.broadcasted_iota(jnp.int32, (C, LANES), 1)
    lane_t = jnp.where(lane < N, lane, lane - N)
    lo = lane < N
    strict = lane_t < row
    incl = lane_t <= row
    eye = (lane_t == row).astype(F32)
    valid = row < t_blk
    tri = (lax.broadcasted_iota(jnp.int32, (C, C), 1)
           <= lax.broadcasted_iota(jnp.int32, (C, C), 0)).astype(BF16)
    sq_r = lax.broadcasted_iota(jnp.int32, (LANES, LANES), 0)
    sq_c = lax.broadcasted_iota(jnp.int32, (LANES, LANES), 1)
    same_head = (sq_r < N) == (sq_c < N)
    diag = sq_r == sq_c
    row_w = lax.broadcasted_iota(jnp.int32, (C, RWKV_PROJ), 0)
    o3 = 3 * D_MODEL

    def hsum(v):
        s_lo = jnp.sum(jnp.where(lo, v, 0.0), axis=1, keepdims=True)
        s_hi = jnp.sum(jnp.where(lo, 0.0, v), axis=1, keepdims=True)
        return jnp.where(lo, s_lo, s_hi)

    def bd(v):
        return _blockdiag(v, lo)

    def cat0(*a):
        return jnp.concatenate(a, axis=0)

    def cat1(*a):
        return jnp.concatenate(a, axis=1)

    valid_w = lax.broadcasted_iota(jnp.int32, (C, D_MODEL), 0) < t_blk
    xs, logw_all, cum_all, asig_all, g_all = [], [], [], [], []
    for bl in range(nb):
        p = p_ref[bl]
        carry_new = p[t_blk - 1:t_blk, :]
        if t_blk < C:
            p = jnp.concatenate([p, jnp.zeros((C - t_blk, RWKV_PROJ), F32)], axis=0)
        prev = jnp.where(row_w == 0, carry_scr[bl], pltpu.roll(p, 1, axis=0))
        carry_scr[bl] = carry_new
        x = p + (prev - p) * mu_ref[...]
        xw = x[:, o3:o3 + DECAY_LORA]
        xa = x[:, o3 + DECAY_LORA:o3 + DECAY_LORA + AAA_LORA]
        xg = x[:, o3 + DECAY_LORA + AAA_LORA:]
        z = w0_ref[...] + _bdot(jnp.tanh(xw), dup_ref[...])
        logw_full = -math.exp(-0.5) * _sigmoid(z)
        if t_blk < C:
            logw_full = jnp.where(valid_w, logw_full, 0.0)
        xs.append(x)
        logw_all.append(logw_full)
        cum_all.append(_dot_exact_lhs(tri, logw_full))
        asig_all.append(_sigmoid(a0_ref[...] + _bdot(xa, aup_ref[...])))
        g_all.append(_bdot(_sigmoid(xg), gup_ref[...]))

    units = [(bl, pr) for bl in range(nb) for pr in range(RWKV_PAIRS)]
    U = range(len(units))
    cols = [slice(pr * LANES, (pr + 1) * LANES) for _, pr in units]
    r = [xs[bl][:, pr * LANES:(pr + 1) * LANES] for bl, pr in units]
    k = [xs[bl][:, D_MODEL + pr * LANES:D_MODEL + (pr + 1) * LANES] for bl, pr in units]
    v = [xs[bl][:, 2 * D_MODEL + pr * LANES:2 * D_MODEL + (pr + 1) * LANES] for bl, pr in units]
    logw = [logw_all[bl][:, cols[u]] for u, (bl, _) in enumerate(units)]
    asig = [asig_all[bl][:, cols[u]] for u, (bl, _) in enumerate(units)]
    kk = [k[u] * kk_ref[:, cols[u]] for u in U]
    k2 = [k[u] * (1.0 + (asig[u] - 1.0) * ka_ref[:, cols[u]]) for u in U]
    ss = [hsum(kk[u] * kk[u]) for u in U]
    cum = [cum_all[bl][:, cols[u]] for u, (bl, _) in enumerate(units)]
    bsum = [hsum(r[u] * k2[u] * rk_ref[:, cols[u]]) for u in U]
    kk = [kk[u] * lax.rsqrt(jnp.maximum(ss[u], 1e-24)) for u in U]
    bonus = [bsum[u] * v[u] for u in U]
    if t_blk < C:
        kk = [jnp.where(valid, a, 0.0) for a in kk]
        k2 = [jnp.where(valid, a, 0.0) for a in k2]
        v = [jnp.where(valid, a, 0.0) for a in v]
        r = [jnp.where(valid, a, 0.0) for a in r]
    b_vec = [kk[u] * asig[u] for u in U]
    clast = [cum[u][C - 1:C, :] for u in U]
    e_inv = [jnp.exp(-cum[u]) for u in U]
    e_tail = [jnp.exp(clast[u] - cum[u]) for u in U]
    aq = [-kk[u] * jnp.exp(cum[u] - logw[u]) for u in U]
    rq = [r[u] * jnp.exp(cum[u]) for u in U]
    bk = [b_vec[u] * e_inv[u] for u in U]
    kx = [k2[u] * e_inv[u] for u in U]
    bb = [b_vec[u] * e_tail[u] for u in U]
    kb = [k2[u] * e_tail[u] for u in U]

    g4 = [_bdot_nt(cat0(aq[u], rq[u]), cat0(bd(bk[u]), bd(kx[u]))) for u in U]
    m_ab = [jnp.where(strict, g4[u][:C, :LANES], 0.0) for u in U]
    m_ak = [jnp.where(strict, g4[u][:C, LANES:], 0.0) for u in U]
    m_rb = [jnp.where(incl, g4[u][C:, :LANES], 0.0) for u in U]
    m_rk = [jnp.where(incl, g4[u][C:, LANES:], 0.0) for u in U]

    t_inv = [eye + m_ab[u] for u in U]
    l_pow = [_bdot(m_ab[u], bd(m_ab[u])) for u in U]
    mv = [_bdot(cat0(m_ak[u], m_rk[u]), bd(v[u])) for u in U]
    n_steps = C.bit_length() - 1
    for step in range(1, n_steps):
        if step < n_steps - 1:
            both = [_bdot(l_pow[u], cat1(bd(t_inv[u]), bd(l_pow[u]))) for u in U]
            t_inv = [t_inv[u] + both[u][:, :LANES] for u in U]
            l_pow = [both[u][:, LANES:] for u in U]
        else:
            t_inv = [t_inv[u] + _bdot(l_pow[u], bd(t_inv[u])) for u in U]

    au = [_bdot(t_inv[u], cat1(bd(aq[u]), bd(mv[u][:C]))) for u in U]
    a_bar = [au[u][:, :LANES] for u in U]
    u_bar = [au[u][:, LANES:] for u in U]
    mau = [_bdot(m_rb[u], cat1(bd(a_bar[u]), bd(u_bar[u]))) for u in U]
    p_full = [_bdot_tn(a_bar[u], bb[u]) for u in U]
    q_full = [_bdot_tn(cat0(u_bar[u], v[u]), cat0(bb[u], kb[u])) for u in U]
    r_til = [rq[u] + mau[u][:, :LANES] for u in U]
    y_loc = [mau[u][:, LANES:] + mv[u][C:] for u in U]
    p_bd = [jnp.where(same_head, p_full[u], 0.0) + jnp.where(diag, jnp.exp(clast[u]), 0.0) for u in U]
    q_pk = [jnp.where(lo[:N], q_full[u][:N], q_full[u][N:]) for u in U]

    s_old = [s_scr[bl, pr] for bl, pr in units]
    y = [_bdot_nt(r_til[u], _blockdiag(s_old[u], lo[:N])) + y_loc[u] for u in U]
    s_new = [_bdot(s_old[u], p_bd[u]) + q_pk[u] for u in U]
    for u, (bl, pr) in enumerate(units):
        s_scr[bl, pr] = s_new[u]

    mu_y = [hsum(y[u]) * (1.0 / N) for u in U]
    yc = [y[u] - mu_y[u] for u in U]
    var = [hsum(yc[u] * yc[u]) * (1.0 / N) for u in U]
    for u, (bl, pr) in enumerate(units):
        yn = yc[u] * lax.rsqrt(var[u] + LN_X_EPS) * lng_ref[:, cols[u]] + lnb_ref[:, cols[u]]
        y_ref[bl, :, cols[u]] = ((yn + bonus[u]) * g_all[bl][:, cols[u]])[:t_blk]

    @pl.when(c == n_chunks - 1)
    def _():
        for bl in range(nb):
            for pr in range(RWKV_PAIRS):
                sout_ref[bl, 2 * pr] = s_scr[bl, pr][:, :N]
                sout_ref[bl, 2 * pr + 1] = s_scr[bl, pr][:, N:]


def _rwkv(p_rwkv, shift_prev, s0, mu, w0, decay_up, a0, aaa_up, gate_up, k_k, k_a, r_k,
          ln_g, ln_b, *, nb):
    b, t, _ = p_rwkv.shape
    t_blk = min(t, CHUNK)
    row = lambda a: a.reshape(1, -1)
    full = lambda shape: pl.BlockSpec(shape, lambda i, j: (0,) * len(shape))
    state_spec = pl.BlockSpec((nb, RWKV_HEADS, RWKV_HEAD_DIM, RWKV_HEAD_DIM), lambda i, j: (i, 0, 0, 0))
    return pl.pallas_call(
        functools.partial(_rwkv_kernel, nb=nb),
        out_shape=(jax.ShapeDtypeStruct((b, t, D_MODEL), F32),
                   jax.ShapeDtypeStruct((b, RWKV_HEADS, RWKV_HEAD_DIM, RWKV_HEAD_DIM), F32)),
        grid=(b // nb, t // t_blk),
        in_specs=[pl.BlockSpec((nb, t_blk, RWKV_PROJ), lambda i, j: (i, j, 0)),
                  pl.BlockSpec((nb, 1, RWKV_PROJ), lambda i, j: (i, 0, 0)),
                  state_spec,
                  full((1, RWKV_PROJ)), full((1, D_MODEL)), full((DECAY_LORA, D_MODEL)),
                  full((1, D_MODEL)), full((AAA_LORA, D_MODEL)), full((GATE_LORA, D_MODEL)),
                  full((1, D_MODEL)), full((1, D_MODEL)), full((1, D_MODEL)),
                  full((1, D_MODEL)), full((1, D_MODEL))],
        out_specs=(pl.BlockSpec((nb, t_blk, D_MODEL), lambda i, j: (i, j, 0)), state_spec),
        scratch_shapes=[pltpu.VMEM((nb, RWKV_PAIRS, RWKV_HEAD_DIM, LANES), F32),
                        pltpu.VMEM((nb, 1, RWKV_PROJ), F32)],
        compiler_params=pltpu.CompilerParams(dimension_semantics=("parallel", "arbitrary"),
                                             vmem_limit_bytes=VMEM_LIMIT),
        name="rwkv_scan",
    )(p_rwkv, shift_prev, s0, row(mu), row(w0), decay_up, row(a0), aaa_up, gate_up,
      row(k_k), row(k_a), row(r_k), row(ln_g), row(ln_b))


def _gla_kernel(p_ref, s0_ref, aup_ref, ab_ref, ng_ref, y_ref, sout_ref, s_scr, *, nb):
    c = pl.program_id(1)
    n_chunks = pl.num_programs(1)
    C = CHUNK
    t_blk = p_ref.shape[1]
    units = [(bl, h) for bl in range(nb) for h in range(GLA_HEADS)]
    U = range(len(units))

    @pl.when(c == 0)
    def _():
        for bl, h in units:
            s_scr[bl, h] = s0_ref[bl, h].T

    o_k = GLA_DK_TOTAL
    o_v = 2 * GLA_DK_TOTAL
    o_r = o_v + GLA_DV_TOTAL
    o_a = o_r + GLA_DV_TOTAL
    tri = (lax.broadcasted_iota(jnp.int32, (C, C), 1)
           <= lax.broadcasted_iota(jnp.int32, (C, C), 0))
    tri_bf = tri.astype(BF16)
    valid_g = lax.broadcasted_iota(jnp.int32, (C, GLA_DK_TOTAL), 0) < t_blk
    valid_k = lax.broadcasted_iota(jnp.int32, (C, GLA_DK), 0) < t_blk
    valid_v = lax.broadcasted_iota(jnp.int32, (C, GLA_DV), 0) < t_blk

    ps, cums = [], []
    for bl in range(nb):
        p = p_ref[bl]
        if t_blk < C:
            p = jnp.concatenate([p, jnp.zeros((C - t_blk, GLA_PROJ_PAD), F32)], axis=0)
        logit = _dot3(p[:, o_a:], aup_ref[...]) + ab_ref[...]
        g = (jnp.minimum(logit, 0.0) - jnp.log(1.0 + jnp.exp(-jnp.abs(logit)))) * (1.0 / GLA_GATE_NORM)
        if t_blk < C:
            g = jnp.where(valid_g, g, 0.0)
        ps.append(p)
        cums.append(_dot_exact_lhs(tri_bf, g))

    q = [ps[bl][:, h * GLA_DK:(h + 1) * GLA_DK] * (GLA_DK ** -0.5) for bl, h in units]
    k = [ps[bl][:, o_k + h * GLA_DK:o_k + (h + 1) * GLA_DK] for bl, h in units]
    v = [ps[bl][:, o_v + h * GLA_DV:o_v + (h + 1) * GLA_DV] for bl, h in units]
    if t_blk < C:
        k = [jnp.where(valid_k, a, 0.0) for a in k]
        v = [jnp.where(valid_v, a, 0.0) for a in v]
    b = [cums[bl][:, h * GLA_DK:(h + 1) * GLA_DK] for bl, h in units]
    clast = [b[u][C - 1:C, :] for u in U]
    cmid = [b[u][C // 2 - 1:C // 2, :] for u in U]
    att = [_bdot_nt(q[u] * jnp.exp(b[u] - cmid[u]), k[u] * jnp.exp(cmid[u] - b[u])) for u in U]
    att = [jnp.where(tri, a, 0.0) for a in att]
    s_t = [s_scr[bl, h] for bl, h in units]
    o = [_bdot(att[u], v[u]) + _bdot_nt(q[u] * jnp.exp(b[u]), s_t[u]) for u in U]
    s_new = [s_t[u] * jnp.exp(clast[u]) + _bdot_tn(v[u], k[u] * jnp.exp(clast[u] - b[u])) for u in U]
    for u, (bl, h) in enumerate(units):
        s_scr[bl, h] = s_new[u]
        r = ps[bl][:, o_r + h * GLA_DV:o_r + (h + 1) * GLA_DV]
        on = o[u] * lax.rsqrt(jnp.mean(o[u] * o[u], axis=-1, keepdims=True) + EPS) * ng_ref[...]
        y_ref[bl, :, h * GLA_DV:(h + 1) * GLA_DV] = (on * _silu(r))[:t_blk]

    @pl.when(c == n_chunks - 1)
    def _():
        for bl, h in units:
            sout_ref[bl, h] = s_scr[bl, h].T


def _gla(p_gla, s0, alpha_up_pad, alpha_b, norm_g, *, nb):
    b, t, _ = p_gla.shape
    t_blk = min(t, CHUNK)
    full = lambda shape: pl.BlockSpec(shape, lambda i, j: (0,) * len(shape))
    state_spec = pl.BlockSpec((nb, GLA_HEADS, GLA_DK, GLA_DV), lambda i, j: (i, 0, 0, 0))
    return pl.pallas_call(
        functools.partial(_gla_kernel, nb=nb),
        out_shape=(jax.ShapeDtypeStruct((b, t, GLA_DV_TOTAL), F32),
                   jax.ShapeDtypeStruct((b, GLA_HEADS, GLA_DK, GLA_DV), F32)),
        grid=(b // nb, t // t_blk),
        in_specs=[pl.BlockSpec((nb, t_blk, GLA_PROJ_PAD), lambda i, j: (i, j, 0)),
                  state_spec,
                  full((LANES, GLA_DK_TOTAL)), full((1, GLA_DK_TOTAL)), full((1, GLA_DV))],
        out_specs=(pl.BlockSpec((nb, t_blk, GLA_DV_TOTAL), lambda i, j: (i, j, 0)), state_spec),
        scratch_shapes=[pltpu.VMEM((nb, GLA_HEADS, GLA_DV, GLA_DK), F32)],
        compiler_params=pltpu.CompilerParams(dimension_semantics=("parallel", "arbitrary"),
                                             vmem_limit_bytes=VMEM_LIMIT),
        name="gla_scan",
    )(p_gla, s0, alpha_up_pad, alpha_b.reshape(1, -1), norm_g.reshape(1, -1))


def _merge_body(yr_ref, yg_ref, gate_ref, x_ref, mod_ref, wbr_ref, wbg_ref, wout_ref, g2_ref,
                router_ref, x1_ref, h2_ref, route_ref):
    mod = mod_ref[0]
    gates = gate_ref[0]
    merged = (gates[:, :D_MODEL] * _bdot(yr_ref[0], wbr_ref[...])
              + gates[:, D_MODEL:] * _bdot(yg_ref[0], wbg_ref[...]))
    x1 = x_ref[0] + mod[2] * _bdot(merged, wout_ref[...])
    x1_ref[0] = x1
    h2 = _rms_scale(x1) * g2_ref[...] * (1.0 + mod[4]) + mod[3]
    h2_ref[:, :D_MODEL] = h2

    tm = h2.shape[0]
    h_hi, h_lo, _ = _split3(h2)
    prod = jnp.dot(jnp.concatenate([h_hi, h_lo], axis=0), router_ref[...], preferred_element_type=F32)
    logits = prod[:tm, :LANES] + prod[tm:, :LANES] + prod[:tm, LANES:]
    lane = lax.broadcasted_iota(jnp.int32, (tm, LANES), 1)
    neg = -jnp.inf
    is_group = (lane >= N_EXPERTS) & (lane < N_EXPERTS + N_GROUPS)
    gl = jnp.where(is_group, logits, neg)
    gmax = jnp.max(gl, axis=1, keepdims=True)
    g_idx = jnp.min(jnp.where(gl == gmax, lane, LANES), axis=1, keepdims=True) - N_EXPERTS
    p_g = 1.0 / jnp.sum(jnp.exp(gl - gmax), axis=1, keepdims=True)
    in_group = (lane >= g_idx * EXPERTS_PER_GROUP) & (lane < (g_idx + 1) * EXPERTS_PER_GROUP)
    el = jnp.where(in_group, logits, neg)
    v1 = jnp.max(el, axis=1, keepdims=True)
    i1 = jnp.min(jnp.where(el == v1, lane, LANES), axis=1, keepdims=True)
    el2 = jnp.where(lane == i1, neg, el)
    v2 = jnp.max(el2, axis=1, keepdims=True)
    i2 = jnp.min(jnp.where(el2 == v2, lane, LANES), axis=1, keepdims=True)
    e21 = jnp.exp(v2 - v1)
    w1 = p_g / (1.0 + e21)
    route = (jnp.where(lane == 0, i1.astype(F32), 0.0) + jnp.where(lane == 1, i2.astype(F32), 0.0)
             + jnp.where(lane == 2, w1, 0.0) + jnp.where(lane == 3, w1 * e21, 0.0))
    route_ref[...] = route
    h2_ref[:, D_MODEL:] = route


def _merge_first_kernel(*refs, n_own):
    step = pl.program_id(0)

    @pl.when(step < n_own)
    def _():
        _merge_body(*refs)

    @pl.when(step >= n_own)
    def _():
        refs[-2][...] = jnp.zeros(refs[-2].shape, F32)
        refs[-1][...] = jnp.zeros(refs[-1].shape, F32)


def _merge_second_kernel(*refs):
    n_in = 10
    _merge_body(*refs[:n_in], *refs[n_in + 2:])


def _merge(y_r, y_g, gates, x, mod, wbr, wbg, wout, norm2_g, router_cat, *, tm, n_tok, row_off, shared=None):
    b, t, _ = x.shape
    per_seq = t // tm
    n_own = b * per_seq
    first = shared is None
    n_steps = n_tok // tm if first else n_own
    assert row_off % tm == 0 and n_tok % tm == 0 and (row_off == 0 or not first)
    blk0 = row_off // tm
    own = lambda s: jnp.minimum(s, n_own - 1)
    full = lambda shape: pl.BlockSpec(shape, lambda s: (0,) * len(shape))
    tile = lambda n: pl.BlockSpec((1, tm, n), lambda s: (own(s) // per_seq, own(s) % per_seq, 0))
    flat = lambda n: pl.BlockSpec((tm, n), lambda s: (blk0 + s, 0))
    in_specs = [tile(D_MODEL), tile(D_MODEL), tile(GATE_PROJ), tile(D_MODEL),
                _mod_spec(mod, tm, lambda s: (own(s) // per_seq, own(s) % per_seq)),
                full((D_MODEL, D_MODEL)), full((D_MODEL, D_MODEL)), full((D_MODEL, D_MODEL)),
                full((1, D_MODEL)), full((D_MODEL, 2 * LANES))]
    args = [y_r, y_g, gates, x, mod, wbr, wbg, wout, norm2_g.reshape(1, -1), router_cat]
    if not first:
        in_specs += [pl.BlockSpec(memory_space=pl.ANY)] * 2
        args += list(shared)
    return pl.pallas_call(
        functools.partial(_merge_first_kernel, n_own=n_own) if first else _merge_second_kernel,
        out_shape=(jax.ShapeDtypeStruct((b, t, D_MODEL), F32),
                   jax.ShapeDtypeStruct((n_tok, MOE_ROW), F32),
                   jax.ShapeDtypeStruct((n_tok, LANES), F32)),
        grid=(n_steps,),
        in_specs=in_specs,
        out_specs=(tile(D_MODEL), flat(MOE_ROW), flat(LANES)),
        input_output_aliases={} if first else {10: 1, 11: 2},
        compiler_params=pltpu.CompilerParams(dimension_semantics=("arbitrary",),
                                             vmem_limit_bytes=VMEM_LIMIT),
        name="merge_router",
    )(*args)


def _moe_plan(expert_ids, n_tok):
    tm = MOE_TILE
    n_tiles = -(-n_tok // tm) + N_GROUPS * (EXPERTS_PER_GROUP * (EXPERTS_PER_GROUP - 1) // 2)
    e_lo = jnp.minimum(expert_ids[:, 0], expert_ids[:, 1])
    e_hi = jnp.maximum(expert_ids[:, 0], expert_ids[:, 1])
    keys = e_lo * EXPERTS_PER_GROUP + e_hi % EXPERTS_PER_GROUP
    t_iota = jnp.arange(n_tok, dtype=jnp.int32)
    _, order = lax.top_k(-(keys * n_tok + t_iota).astype(F32), n_tok)
    c_iota = jnp.arange(N_PAIR_CLASSES, dtype=jnp.int32)
    counts = jnp.sum((keys[None, :] == c_iota[:, None]).astype(jnp.int32), axis=1)
    tiles_c = (counts + tm - 1) // tm
    tile_end = jnp.cumsum(tiles_c)
    row_start = jnp.cumsum(counts) - counts
    n_used = tile_end[-1]
    t_idx = jnp.arange(n_tiles, dtype=jnp.int32)
    t_eff = jnp.minimum(t_idx, n_used - 1)
    tile_c = jnp.minimum(jnp.sum((t_eff[:, None] >= tile_end[None, :]).astype(jnp.int32), axis=1),
                         N_PAIR_CLASSES - 1)
    onehot = tile_c[:, None] == c_iota[None, :]
    pick = lambda vec: jnp.sum(jnp.where(onehot, vec[None, :], 0), axis=1)
    local = t_idx - pick(tile_end - tiles_c)
    n_valid = jnp.where(t_idx < n_used, jnp.clip(pick(counts) - local * tm, 0, tm), 0)
    j_idx = jnp.arange(tm, dtype=jnp.int32)[None, :]
    pos0 = jnp.clip(pick(row_start) + local * tm, 0, n_tok - 1)
    order_pad = jnp.concatenate([order, jnp.zeros((tm,), order.dtype)])
    runs = jax.vmap(lambda p: lax.dynamic_slice(order_pad, (p,), (tm,)))(pos0)
    valid = j_idx < n_valid[:, None]
    tok = jnp.where(valid, runs, 0)
    dst = jnp.where(valid, tok, n_tok + (t_idx[:, None] % 2) * DMA_GROUP + j_idx % DMA_GROUP)
    n_groups = (n_valid + DMA_GROUP - 1) // DMA_GROUP
    tile_lo = tile_c // EXPERTS_PER_GROUP
    tile_hi = (tile_c // (EXPERTS_PER_GROUP * EXPERTS_PER_GROUP)) * EXPERTS_PER_GROUP + tile_c % EXPERTS_PER_GROUP
    return (tile_lo, tile_hi, n_groups.astype(jnp.int32), n_used.reshape(1).astype(jnp.int32),
            tok.reshape(n_tiles, 1, tm), dst.reshape(n_tiles, 1, tm))


def _moe_pair_kernel(lo_ref, hi_ref, ng_ref, n_used_ref, tok_ref, tok_next_ref, dst_ref, h2_hbm,
                     wg_lo_ref, wu_lo_ref, wd_lo_ref, wg_hi_ref, wu_hi_ref, wd_hi_ref, out_hbm,
                     xbuf, obuf, w_bf, gsem, ssem):
    t = pl.program_id(0)
    n_t = pl.num_programs(0)
    tm = MOE_TILE
    slot = lax.rem(t, 2)
    n_used = n_used_ref[0]

    def gather(idx_ref, s, n_groups):
        def body(g, carry):
            for u in range(DMA_GROUP):
                pltpu.make_async_copy(h2_hbm.at[pl.ds(idx_ref[0, 0, g * DMA_GROUP + u], 1)],
                                      xbuf.at[s, g, pl.ds(u, 1)], gsem.at[s]).start()
            return carry
        lax.fori_loop(0, n_groups, body, 0)

    def wait_gather(s, n_groups):
        def body(g, carry):
            pltpu.make_async_copy(h2_hbm.at[pl.ds(0, DMA_GROUP)], xbuf.at[s, 0], gsem.at[s]).wait()
            return carry
        lax.fori_loop(0, n_groups, body, 0)

    def wait_scatter(s, n_groups):
        def body(g, carry):
            pltpu.make_async_copy(obuf.at[s, 0], out_hbm.at[pl.ds(0, DMA_GROUP)], ssem.at[s]).wait()
            return carry
        lax.fori_loop(0, n_groups, body, 0)

    @pl.when(t == 0)
    def _():
        xbuf[...] = jnp.zeros(xbuf.shape, F32)
        obuf[0, 0] = jnp.zeros(obuf.shape[2:], F32)
        n_rows = out_hbm.shape[0]
        for half in range(2):
            fill = pltpu.make_async_copy(obuf.at[0, 0], out_hbm.at[pl.ds(n_rows - (2 - half) * DMA_GROUP, DMA_GROUP)],
                                         ssem.at[0])
            fill.start()
            fill.wait()
        gather(tok_ref, 0, ng_ref[0])

    @pl.when((t >= 2) & (t - 2 < n_used))
    def _():
        wait_scatter(slot, ng_ref[jnp.maximum(t - 2, 0)])

    @pl.when(t < n_used)
    def _():
        n_here = ng_ref[t]
        wait_gather(slot, n_here)

        @pl.when(t + 1 < n_used)
        def _():
            gather(tok_next_ref, 1 - slot, ng_ref[jnp.minimum(t + 1, n_t - 1)])

        t_prev = jnp.maximum(t - 1, 0)
        for which, (e_ref, refs) in enumerate(((lo_ref, (wg_lo_ref, wu_lo_ref, wd_lo_ref)),
                                               (hi_ref, (wg_hi_ref, wu_hi_ref, wd_hi_ref)))):
            @pl.when((t == 0) | (e_ref[t] != e_ref[t_prev]))
            def _():
                for m, ref in enumerate(refs):
                    w_bf[which][m][...] = ref[0].astype(BF16)

        xe = xbuf[slot].reshape(tm, MOE_ROW)
        x = xe[:, :D_MODEL].astype(BF16)
        route = xe[:, D_MODEL:]
        lane = lax.broadcasted_iota(jnp.int32, route.shape, 1)
        pick = lambda k: jnp.sum(jnp.where(lane == k, route, 0.0), axis=1, keepdims=True)
        i1, i2, w1, w2 = pick(0), pick(1), pick(2), pick(3)
        w_lo = jnp.where(i1 < i2, w1, w2)
        w_hi = jnp.where(i1 < i2, w2, w1)
        acc = None
        for which, wt in enumerate((w_lo, w_hi)):
            wg, wu, wd = w_bf[which]
            hid = _silu(jnp.dot(x, wg[...], preferred_element_type=F32)) * jnp.dot(
                x, wu[...], preferred_element_type=F32)
            part = wt * _bdot(hid, wd[...])
            acc = part if acc is None else acc + part
        obuf[slot] = acc.reshape(obuf.shape[1:])

        def body(g, carry):
            for u in range(DMA_GROUP):
                pltpu.make_async_copy(obuf.at[slot, g, pl.ds(u, 1)],
                                      out_hbm.at[pl.ds(dst_ref[0, 0, g * DMA_GROUP + u], 1)], ssem.at[slot]).start()
            return carry
        lax.fori_loop(0, n_here, body, 0)

    @pl.when(t == n_t - 1)
    def _():
        @pl.when((t >= 1) & (t - 1 < n_used))
        def _():
            wait_scatter(1 - slot, ng_ref[jnp.maximum(t - 1, 0)])

        @pl.when(t < n_used)
        def _():
            wait_scatter(slot, ng_ref[t])


def _moe_grouped(h2_ext, expert_ids, wg, wu, wd):
    return _moe_call(h2_ext, *_moe_plan(expert_ids, h2_ext.shape[0]), wg, wu, wd)


def _moe_call(h2_ext, tile_lo, tile_hi, n_groups, n_used, tok, dst, wg, wu, wd):
    n_tok = h2_ext.shape[0]
    tm = MOE_TILE
    n_tiles = tile_lo.shape[0]
    smem_tile = lambda f: pl.BlockSpec((1, 1, tm), f, memory_space=pltpu.SMEM)
    w_lo = lambda shape: pl.BlockSpec(shape, lambda t, lo, hi, nv, nu: (lo[t], 0, 0))
    w_hi = lambda shape: pl.BlockSpec(shape, lambda t, lo, hi, nv, nu: (hi[t], 0, 0))
    up, down = (1, D_MODEL, EXPERT_FF), (1, EXPERT_FF, D_MODEL)
    bf_set = [pltpu.VMEM(up[1:], BF16), pltpu.VMEM(up[1:], BF16), pltpu.VMEM(down[1:], BF16)]
    grid_spec = pltpu.PrefetchScalarGridSpec(
        num_scalar_prefetch=4,
        grid=(n_tiles,),
        in_specs=[smem_tile(lambda t, *_: (t, 0, 0)),
                  smem_tile(lambda t, *_: (jnp.minimum(t + 1, n_tiles - 1), 0, 0)),
                  smem_tile(lambda t, *_: (t, 0, 0)),
                  pl.BlockSpec(memory_space=pl.ANY),
                  w_lo(up), w_lo(up), w_lo(down), w_hi(up), w_hi(up), w_hi(down)],
        out_specs=pl.BlockSpec(memory_space=pl.ANY),
        scratch_shapes=[pltpu.VMEM((2, tm // DMA_GROUP, DMA_GROUP, MOE_ROW), F32),
                        pltpu.VMEM((2, tm // DMA_GROUP, DMA_GROUP, D_MODEL), F32),
                        [bf_set, bf_set],
                        pltpu.SemaphoreType.DMA((2,)), pltpu.SemaphoreType.DMA((2,))],
    )
    return pl.pallas_call(
        _moe_pair_kernel,
        out_shape=jax.ShapeDtypeStruct((n_tok + 2 * DMA_GROUP, D_MODEL), F32),
        grid_spec=grid_spec,
        compiler_params=pltpu.CompilerParams(dimension_semantics=("arbitrary",),
                                             vmem_limit_bytes=VMEM_LIMIT),
        name="moe_pairs",
    )(tile_lo, tile_hi, n_groups, n_used, tok, tok, dst, h2_ext, wg, wu, wd, wg, wu, wd)


def _final_kernel(x1_ref, moe_ref, mod_ref, fg_ref, y_ref):
    x2 = x1_ref[0] + mod_ref[0][5] * moe_ref[...]
    y_ref[0] = _rms_scale(x2) * fg_ref[...]


def _final(x1, moe_out, mod, final_g, *, row_off, tm):
    b, t, _ = x1.shape
    blk0 = row_off // tm
    per_seq = t // tm
    return pl.pallas_call(
        _final_kernel,
        out_shape=jax.ShapeDtypeStruct((b, t, D_MODEL), F32),
        grid=(b, per_seq),
        in_specs=[pl.BlockSpec((1, tm, D_MODEL), lambda i, j: (i, j, 0)),
                  pl.BlockSpec((tm, D_MODEL), lambda i, j: (blk0 + i * per_seq + j, 0)),
                  _mod_spec(mod, tm, lambda i, j: (i, j)),
                  pl.BlockSpec((1, D_MODEL), lambda i, j: (0, 0))],
        out_specs=pl.BlockSpec((1, tm, D_MODEL), lambda i, j: (i, j, 0)),
        compiler_params=pltpu.CompilerParams(dimension_semantics=("parallel", "parallel"),
                                             vmem_limit_bytes=VMEM_LIMIT),
        name="moe_combine_norm",
    )(x1, moe_out, mod, final_g.reshape(1, -1))


def _mixer(x, mod, shift0, s_rwkv0, s_gla0, wts, *, n_tok, row_off, shared=None):
    b, t, _ = x.shape
    per_token = mod.shape[2] > 1
    rows = x.reshape(1, b * t, D_MODEL) if per_token else x
    tm = min(rows.shape[1], ROW_TILE)
    p_rwkv = _norm_proj(rows, mod, wts["norm1_g"], wts["w_in_rwkv"], gate=False, tm=tm)
    p_gla = _norm_proj(rows, mod, wts["norm1_g"], wts["w_in_gla"], gate=False, tm=tm)
    gates = _norm_proj(rows, mod, wts["norm1_g"], wts["w_in_gate"], gate=True, tm=tm)
    p_rwkv = p_rwkv.reshape(b, t, RWKV_PROJ)

    y_r, s_rwkv = _rwkv(p_rwkv, shift0, s_rwkv0,
                        wts["rwkv_mu"], wts["rwkv_w0"], wts["rwkv_decay_up"], wts["rwkv_a0"],
                        wts["rwkv_aaa_up"], wts["rwkv_gate_up"], wts["rwkv_k_k"], wts["rwkv_k_a"],
                        wts["rwkv_r_k"], wts["rwkv_ln_g"], wts["rwkv_ln_b"], nb=RWKV_SEQS_PER_STEP)
    y_g, s_gla = _gla(p_gla.reshape(b, t, GLA_PROJ_PAD), s_gla0, wts["gla_alpha_up"],
                      wts["gla_alpha_b"], wts["gla_norm_g"], nb=GLA_SEQS_PER_STEP)

    x1, h2_all, route_all = _merge(
        y_r.reshape(rows.shape), y_g.reshape(rows.shape), gates, rows, mod, wts["w_branch_rwkv"],
        wts["w_branch_gla"], wts["w_out"], wts["norm2_g"], wts["router_cat"],
        tm=tm, n_tok=n_tok, row_off=row_off, shared=shared)
    states = (p_rwkv[:, t - 1:t, :][None], s_rwkv[None], s_gla[None])
    return x1, h2_all, route_all, states


def kernel(x_prompt, x_sample, c_prompt, c_sample, state_rwkv_shift, state_rwkv, state_gla, w_ada, b_ada, norm1_g, w_in, rwkv_mu, rwkv_w0, rwkv_decay_up, rwkv_a0, rwkv_aaa_up, rwkv_gate_up, rwkv_k_k, rwkv_k_a, rwkv_r_k, rwkv_ln_g, rwkv_ln_b, gla_alpha_up, gla_alpha_b, gla_norm_g, w_branch_rwkv, w_branch_gla, w_out, norm2_g, router_group, router_expert, expert_w_gate, expert_w_up, expert_w_down, final_norm_g):
    bp, tp = x_prompt.shape[:2]
    bs, ts = x_sample.shape[:2]
    w_in0 = w_in[0]
    g0 = RWKV_PROJ
    o3 = 2 * GLA_DK_TOTAL + GLA_DV_TOTAL
    o4 = o3 + GLA_GATE_RANK
    w_gla = w_in0[:, g0:g0 + GLA_PROJ]
    w_gla = jnp.concatenate([w_gla[:, :o3], w_gla[:, o4:], w_gla[:, o3:o4],
                             jnp.zeros((D_MODEL, GLA_PROJ_PAD - GLA_PROJ), F32)], axis=1)
    wts = dict(
        norm1_g=norm1_g[0].reshape(1, -1),
        w_in_rwkv=w_in0[:, :g0].astype(BF16),
        w_in_gla=w_gla.astype(BF16),
        w_in_gate=w_in0[:, g0 + GLA_PROJ:].astype(BF16),
        rwkv_mu=rwkv_mu[0], rwkv_w0=rwkv_w0[0], rwkv_decay_up=rwkv_decay_up[0], rwkv_a0=rwkv_a0[0],
        rwkv_aaa_up=rwkv_aaa_up[0], rwkv_gate_up=rwkv_gate_up[0], rwkv_k_k=rwkv_k_k[0],
        rwkv_k_a=rwkv_k_a[0], rwkv_r_k=rwkv_r_k[0], rwkv_ln_g=rwkv_ln_g[0], rwkv_ln_b=rwkv_ln_b[0],
        gla_alpha_up=jnp.pad(gla_alpha_up[0], ((0, LANES - GLA_GATE_RANK), (0, 0))),
        gla_alpha_b=gla_alpha_b[0], gla_norm_g=gla_norm_g[0],
        w_branch_rwkv=w_branch_rwkv[0].astype(BF16), w_branch_gla=w_branch_gla[0].astype(BF16),
        w_out=w_out[0].astype(BF16), norm2_g=norm2_g[0],
        router_cat=_split_hi_lo(jnp.pad(jnp.concatenate([router_expert[0], router_group[0]], axis=1),
                                        ((0, 0), (0, LANES - N_EXPERTS - N_GROUPS)))),
        expert_w_gate=expert_w_gate[0], expert_w_up=expert_w_up[0], expert_w_down=expert_w_down[0],
    )
    mod = _ada(jnp.concatenate([c_prompt, c_sample], axis=0), w_ada[0], b_ada[0])
    mod = mod.reshape(bp + bs, N_MOD, D_MODEL)
    n_p = bp * tp
    n_s = bs * ts
    n_tok = n_p + n_s
    mod_p = mod[:bp].reshape(bp, N_MOD, 1, D_MODEL)
    mod_s = jnp.repeat(mod[bp:].transpose(1, 0, 2), ts, axis=1)[None]

    dt = x_prompt.dtype
    x1_p, h2_all, route_all, states_p = _mixer(
        x_prompt, mod_p, jnp.zeros((bp, 1, RWKV_PROJ), dt),
        jnp.zeros((bp, RWKV_HEADS, RWKV_HEAD_DIM, RWKV_HEAD_DIM), state_rwkv.dtype),
        jnp.zeros((bp, GLA_HEADS, GLA_DK, GLA_DV), state_gla.dtype), wts, n_tok=n_tok, row_off=0)
    x1_s, h2_all, route_all, states_s = _mixer(
        x_sample, mod_s, state_rwkv_shift[0], state_rwkv[0], state_gla[0], wts,
        n_tok=n_tok, row_off=n_p, shared=(h2_all, route_all))

    expert_ids = route_all[:, :2].astype(jnp.int32)
    moe_out = _moe_grouped(h2_all, expert_ids, wts["expert_w_gate"], wts["expert_w_up"],
                           wts["expert_w_down"])
    y_p = _final(x1_p, moe_out, mod_p, final_norm_g, row_off=0, tm=min(tp, ROW_TILE))
    y_s = _final(x1_s, moe_out, mod_s, final_norm_g, row_off=n_p, tm=min(n_s, ROW_TILE))
    return (y_p, y_s.reshape(bs, ts, D_MODEL)) + states_p + states_s
```

```python
import functools
import math

import jax
import jax.numpy as jnp
from jax import lax
from jax.experimental import pallas as pl
from jax.experimental.pallas import tpu as pltpu

F32 = jnp.float32
BF16 = jnp.bfloat16

D_MODEL = 1024
N_MOD = 6
EPS = 1e-6
RWKV_HEAD_DIM = 64
RWKV_HEADS = 16
RWKV_PAIRS = RWKV_HEADS // 2
DECAY_LORA = 64
AAA_LORA = 64
GATE_LORA = 128
RWKV_PROJ = 3 * D_MODEL + DECAY_LORA + AAA_LORA + GATE_LORA
LN_X_EPS = 64e-5
GLA_HEADS = 4
GLA_DK = 128
GLA_DV = 256
GLA_DK_TOTAL = GLA_HEADS * GLA_DK
GLA_DV_TOTAL = GLA_HEADS * GLA_DV
GLA_GATE_RANK = 16
GLA_GATE_NORM = 16.0
GLA_PROJ = 2 * GLA_DK_TOTAL + 2 * GLA_DV_TOTAL + GLA_GATE_RANK
LANES = 128
GLA_PROJ_PAD = 2 * GLA_DK_TOTAL + 2 * GLA_DV_TOTAL + LANES
GATE_PROJ = 2 * D_MODEL
N_GROUPS = 4
EXPERTS_PER_GROUP = 8
N_EXPERTS = 32
EXPERT_FF = 512

CHUNK = 64
ROW_TILE = 512
MOE_TILE = 256
DMA_GROUP = 8
MOE_ROW = D_MODEL + LANES
N_PAIR_CLASSES = N_EXPERTS * EXPERTS_PER_GROUP
RWKV_SEQS_PER_STEP = 2
GLA_SEQS_PER_STEP = 4
VMEM_LIMIT = 56 * 1024 * 1024


def _bdot(a, b):
    return jnp.dot(a.astype(BF16), b.astype(BF16), preferred_element_type=F32)


def _bdot_nt(a, b):
    return lax.dot_general(a.astype(BF16), b.astype(BF16), (((1,), (1,)), ((), ())),
                           preferred_element_type=F32)


def _bdot_tn(a, b):
    return jnp.dot(a.T.astype(BF16), b.astype(BF16), preferred_element_type=F32)


def _split3(x):
    h1 = x.astype(BF16)
    r1 = x - h1.astype(F32)
    h2 = r1.astype(BF16)
    h3 = (r1 - h2.astype(F32)).astype(BF16)
    return h1, h2, h3


def _dot3(a, b):
    a1, a2, _ = _split3(a)
    b1, b2, _ = _split3(b)
    return (jnp.dot(a1, b1, preferred_element_type=F32)
            + jnp.dot(a2, b1, preferred_element_type=F32)
            + jnp.dot(a1, b2, preferred_element_type=F32))


def _dot_exact_lhs(a_bf16, x):
    x1, x2, _ = _split3(x)
    return (jnp.dot(a_bf16, x1, preferred_element_type=F32)
            + jnp.dot(a_bf16, x2, preferred_element_type=F32))


def _split_hi_lo(w):
    hi, lo, _ = _split3(w)
    return jnp.concatenate([hi, lo], axis=1)


def _sigmoid(x):
    return 1.0 / (1.0 + jnp.exp(-x))


def _silu(x):
    return x * _sigmoid(x)


def _rms_scale(x):
    return x * lax.rsqrt(jnp.mean(x * x, axis=-1, keepdims=True) + EPS)


def _ada_kernel(c_ref, w_ref, b_ref, o_ref):
    o_ref[...] = _dot3(_silu(c_ref[...]), w_ref[...]) + b_ref[...]


def _ada(c, w_ada, b_ada):
    n_rows = c.shape[0]
    n_out = w_ada.shape[1]
    tn = 1536
    return pl.pallas_call(
        _ada_kernel,
        out_shape=jax.ShapeDtypeStruct((n_rows, n_out), F32),
        grid=(n_out // tn,),
        in_specs=[pl.BlockSpec((n_rows, D_MODEL), lambda j: (0, 0)),
                  pl.BlockSpec((D_MODEL, tn), lambda j: (0, j)),
                  pl.BlockSpec((1, tn), lambda j: (0, j))],
        out_specs=pl.BlockSpec((n_rows, tn), lambda j: (0, j)),
        compiler_params=pltpu.CompilerParams(dimension_semantics=("parallel",),
                                             vmem_limit_bytes=VMEM_LIMIT),
        name="ada",
    )(c, w_ada, b_ada.reshape(1, n_out))


def _mod_spec(mod, tm, seq_tile):
    if mod.shape[2] == 1:
        return pl.BlockSpec((1, N_MOD, 1, D_MODEL), lambda *g: (seq_tile(*g)[0], 0, 0, 0))
    return pl.BlockSpec((1, N_MOD, tm, D_MODEL), lambda *g: (seq_tile(*g)[0], 0, seq_tile(*g)[1], 0))


def _norm_proj_kernel(x_ref, mod_ref, g_ref, w_ref, o_ref, *, gate):
    mod = mod_ref[0]
    h = _rms_scale(x_ref[0]) * g_ref[...] * (1.0 + mod[1]) + mod[0]
    out = _bdot(h, w_ref[...])
    o_ref[0] = _sigmoid(out) if gate else out


def _norm_proj(x, mod, g, w_bf16, *, gate, tm):
    b, t, _ = x.shape
    n = w_bf16.shape[1]
    return pl.pallas_call(
        functools.partial(_norm_proj_kernel, gate=gate),
        out_shape=jax.ShapeDtypeStruct((b, t, n), F32),
        grid=(b, t // tm),
        in_specs=[pl.BlockSpec((1, tm, D_MODEL), lambda i, j: (i, j, 0)),
                  _mod_spec(mod, tm, lambda i, j: (i, j)),
                  pl.BlockSpec((1, D_MODEL), lambda i, j: (0, 0)),
                  pl.BlockSpec((D_MODEL, n), lambda i, j: (0, 0))],
        out_specs=pl.BlockSpec((1, tm, n), lambda i, j: (i, j, 0)),
        compiler_params=pltpu.CompilerParams(dimension_semantics=("parallel", "parallel"),
                                             vmem_limit_bytes=VMEM_LIMIT),
        name="norm_proj_gate" if gate else "norm_proj",
    )(x, mod, g, w_bf16)


def _blockdiag(x, lo_mask):
    return jnp.concatenate([jnp.where(lo_mask, x, 0.0), jnp.where(lo_mask, 0.0, x)], axis=0)


def _rwkv_kernel(p_ref, shift_ref, s0_ref, mu_ref, w0_ref, dup_ref, a0_ref, aup_ref, gup_ref,
                 kk_ref, ka_ref, rk_ref, lng_ref, lnb_ref, y_ref, sout_ref, s_scr, carry_scr,
                 *, nb):
    c = pl.program_id(1)
    n_chunks = pl.num_programs(1)
    C = CHUNK
    N = RWKV_HEAD_DIM
    t_blk = p_ref.shape[1]

    @pl.when(c == 0)
    def _():
        for bl in range(nb):
            for pr in range(RWKV_PAIRS):
                s_scr[bl, pr] = jnp.concatenate([s0_ref[bl, 2 * pr], s0_ref[bl, 2 * pr + 1]], axis=1)
        carry_scr[...] = shift_ref[...]

    row = lax.broadcasted_iota(jnp.int32, (C, LANES), 0)
    lane = lax.broadcasted_iota(jnp.int32, (C, LANES), 1)
    lane_t = jnp.where(lane < N, lane, lane - N)
    lo = lane < N
    strict = lane_t < row
    incl = lane_t <= row
    eye = (lane_t == row).astype(F32)
    valid = row < t_blk
    tri = (lax.broadcasted_iota(jnp.int32, (C, C), 1)
           <= lax.broadcasted_iota(jnp.int32, (C, C), 0)).astype(BF16)
    sq_r = lax.broadcasted_iota(jnp.int32, (LANES, LANES), 0)
    sq_c = lax.broadcasted_iota(jnp.int32, (LANES, LANES), 1)
    same_head = (sq_r < N) == (sq_c < N)
    diag = sq_r == sq_c
    row_w = lax.broadcasted_iota(jnp.int32, (C, RWKV_PROJ), 0)
    o3 = 3 * D_MODEL

    def hsum(v):
        s_lo = jnp.sum(jnp.where(lo, v, 0.0), axis=1, keepdims=True)
        s_hi = jnp.sum(jnp.where(lo, 0.0, v), axis=1, keepdims=True)
        return jnp.where(lo, s_lo, s_hi)

    def bd(v):
        return _blockdiag(v, lo)

    def cat0(*a):
        return jnp.concatenate(a, axis=0)

    def cat1(*a):
        return jnp.concatenate(a, axis=1)

    valid_w = lax.broadcasted_iota(jnp.int32, (C, D_MODEL), 0) < t_blk
    xs, logw_all, cum_all, asig_all, g_all = [], [], [], [], []
    for bl in range(nb):
        p = p_ref[bl]
        carry_new = p[t_blk - 1:t_blk, :]
        if t_blk < C:
            p = jnp.concatenate([p, jnp.zeros((C - t_blk, RWKV_PROJ), F32)], axis=0)
        prev = jnp.where(row_w == 0, carry_scr[bl], pltpu.roll(p, 1, axis=0))
        carry_scr[bl] = carry_new
        x = p + (prev - p) * mu_ref[...]
        xw = x[:, o3:o3 + DECAY_LORA]
        xa = x[:, o3 + DECAY_LORA:o3 + DECAY_LORA + AAA_LORA]
        xg = x[:, o3 + DECAY_LORA + AAA_LORA:]
        z = w0_ref[...] + _bdot(jnp.tanh(xw), dup_ref[...])
        logw_full = -math.exp(-0.5) * _sigmoid(z)
        if t_blk < C:
            logw_full = jnp.where(valid_w, logw_full, 0.0)
        xs.append(x)
        logw_all.append(logw_full)
        cum_all.append(_dot_exact_lhs(tri, logw_full))
        asig_all.append(_sigmoid(a0_ref[...] + _bdot(xa, aup_ref[...])))
        g_all.append(_bdot(_sigmoid(xg), gup_ref[...]))

    units = [(bl, pr) for bl in range(nb) for pr in range(RWKV_PAIRS)]
    U = range(len(units))
    cols = [slice(pr * LANES, (pr + 1) * LANES) for _, pr in units]
    r = [xs[bl][:, pr * LANES:(pr + 1) * LANES] for bl, pr in units]
    k = [xs[bl][:, D_MODEL + pr * LANES:D_MODEL + (pr + 1) * LANES] for bl, pr in units]
    v = [xs[bl][:, 2 * D_MODEL + pr * LANES:2 * D_MODEL + (pr + 1) * LANES] for bl, pr in units]
    logw = [logw_all[bl][:, cols[u]] for u, (bl, _) in enumerate(units)]
    asig = [asig_all[bl][:, cols[u]] for u, (bl, _) in enumerate(units)]
    kk = [k[u] * kk_ref[:, cols[u]] for u in U]
    k2 = [k[u] * (1.0 + (asig[u] - 1.0) * ka_ref[:, cols[u]]) for u in U]
    ss = [hsum(kk[u] * kk[u]) for u in U]
    cum = [cum_all[bl][:, cols[u]] for u, (bl, _) in enumerate(units)]
    bsum = [hsum(r[u] * k2[u] * rk_ref[:, cols[u]]) for u in U]
    kk = [kk[u] * lax.rsqrt(jnp.maximum(ss[u], 1e-24)) for u in U]
    bonus = [bsum[u] * v[u] for u in U]
    if t_blk < C:
        kk = [jnp.where(valid, a, 0.0) for a in kk]
        k2 = [jnp.where(valid, a, 0.0) for a in k2]
        v = [jnp.where(valid, a, 0.0) for a in v]
        r = [jnp.where(valid, a, 0.0) for a in r]
    b_vec = [kk[u] * asig[u] for u in U]
    clast = [cum[u][C - 1:C, :] for u in U]
    e_inv = [jnp.exp(-cum[u]) for u in U]
    e_tail = [jnp.exp(clast[u] - cum[u]) for u in U]
    aq = [-kk[u] * jnp.exp(cum[u] - logw[u]) for u in U]
    rq = [r[u] * jnp.exp(cum[u]) for u in U]
    bk = [b_vec[u] * e_inv[u] for u in U]
    kx = [k2[u] * e_inv[u] for u in U]
    bb = [b_vec[u] * e_tail[u] for u in U]
    kb = [k2[u] * e_tail[u] for u in U]

    g4 = [_bdot_nt(cat0(aq[u], rq[u]), cat0(bd(bk[u]), bd(kx[u]))) for u in U]
    m_ab = [jnp.where(strict, g4[u][:C, :LANES], 0.0) for u in U]
    m_ak = [jnp.where(strict, g4[u][:C, LANES:], 0.0) for u in U]
    m_rb = [jnp.where(incl, g4[u][C:, :LANES], 0.0) for u in U]
    m_rk = [jnp.where(incl, g4[u][C:, LANES:], 0.0) for u in U]

    t_inv = [eye + m_ab[u] for u in U]
    l_pow = [_bdot(m_ab[u], bd(m_ab[u])) for u in U]
    mv = [_bdot(cat0(m_ak[u], m_rk[u]), bd(v[u])) for u in U]
    n_steps = C.bit_length() - 1
    for step in range(1, n_steps):
        if step < n_steps - 1:
            both = [_bdot(l_pow[u], cat1(bd(t_inv[u]), bd(l_pow[u]))) for u in U]
            t_inv = [t_inv[u] + both[u][:, :LANES] for u in U]
            l_pow = [both[u][:, LANES:] for u in U]
        else:
            t_inv = [t_inv[u] + _bdot(l_pow[u], bd(t_inv[u])) for u in U]

    au = [_bdot(t_inv[u], cat1(bd(aq[u]), bd(mv[u][:C]))) for u in U]
    a_bar = [au[u][:, :LANES] for u in U]
    u_bar = [au[u][:, LANES:] for u in U]
    mau = [_bdot(m_rb[u], cat1(bd(a_bar[u]), bd(u_bar[u]))) for u in U]
    p_full = [_bdot_tn(a_bar[u], bb[u]) for u in U]
    q_full = [_bdot_tn(cat0(u_bar[u], v[u]), cat0(bb[u], kb[u])) for u in U]
    r_til = [rq[u] + mau[u][:, :LANES] for u in U]
    y_loc = [mau[u][:, LANES:] + mv[u][C:] for u in U]
    p_bd = [jnp.where(same_head, p_full[u], 0.0) + jnp.where(diag, jnp.exp(clast[u]), 0.0) for u in U]
    q_pk = [jnp.where(lo[:N], q_full[u][:N], q_full[u][N:]) for u in U]

    s_old = [s_scr[bl, pr] for bl, pr in units]
    y = [_bdot_nt(r_til[u], _blockdiag(s_old[u], lo[:N])) + y_loc[u] for u in U]
    s_new = [_bdot(s_old[u], p_bd[u]) + q_pk[u] for u in U]
    for u, (bl, pr) in enumerate(units):
        s_scr[bl, pr] = s_new[u]

    mu_y = [hsum(y[u]) * (1.0 / N) for u in U]
    yc = [y[u] - mu_y[u] for u in U]
    var = [hsum(yc[u] * yc[u]) * (1.0 / N) for u in U]
    for u, (bl, pr) in enumerate(units):
        yn = yc[u] * lax.rsqrt(var[u] + LN_X_EPS) * lng_ref[:, cols[u]] + lnb_ref[:, cols[u]]
        y_ref[bl, :, cols[u]] = ((yn + bonus[u]) * g_all[bl][:, cols[u]])[:t_blk]

    @pl.when(c == n_chunks - 1)
    def _():
        for bl in range(nb):
            for pr in range(RWKV_PAIRS):
                sout_ref[bl, 2 * pr] = s_scr[bl, pr][:, :N]
                sout_ref[bl, 2 * pr + 1] = s_scr[bl, pr][:, N:]


def _rwkv(p_rwkv, shift_prev, s0, mu, w0, decay_up, a0, aaa_up, gate_up, k_k, k_a, r_k,
          ln_g, ln_b, *, nb):
    b, t, _ = p_rwkv.shape
    t_blk = min(t, CHUNK)
    row = lambda a: a.reshape(1, -1)
    full = lambda shape: pl.BlockSpec(shape, lambda i, j: (0,) * len(shape))
    state_spec = pl.BlockSpec((nb, RWKV_HEADS, RWKV_HEAD_DIM, RWKV_HEAD_DIM), lambda i, j: (i, 0, 0, 0))
    return pl.pallas_call(
        functools.partial(_rwkv_kernel, nb=nb),
        out_shape=(jax.ShapeDtypeStruct((b, t, D_MODEL), F32),
                   jax.ShapeDtypeStruct((b, RWKV_HEADS, RWKV_HEAD_DIM, RWKV_HEAD_DIM), F32)),
        grid=(b // nb, t // t_blk),
        in_specs=[pl.BlockSpec((nb, t_blk, RWKV_PROJ), lambda i, j: (i, j, 0)),
                  pl.BlockSpec((nb, 1, RWKV_PROJ), lambda i, j: (i, 0, 0)),
                  state_spec,
                  full((1, RWKV_PROJ)), full((1, D_MODEL)), full((DECAY_LORA, D_MODEL)),
                  full((1, D_MODEL)), full((AAA_LORA, D_MODEL)), full((GATE_LORA, D_MODEL)),
                  full((1, D_MODEL)), full((1, D_MODEL)), full((1, D_MODEL)),
                  full((1, D_MODEL)), full((1, D_MODEL))],
        out_specs=(pl.BlockSpec((nb, t_blk, D_MODEL), lambda i, j: (i, j, 0)), state_spec),
        scratch_shapes=[pltpu.VMEM((nb, RWKV_PAIRS, RWKV_HEAD_DIM, LANES), F32),
                        pltpu.VMEM((nb, 1, RWKV_PROJ), F32)],
        compiler_params=pltpu.CompilerParams(dimension_semantics=("parallel", "arbitrary"),
                                             vmem_limit_bytes=VMEM_LIMIT),
        name="rwkv_scan",
    )(p_rwkv, shift_prev, s0, row(mu), row(w0), decay_up, row(a0), aaa_up, gate_up,
      row(k_k), row(k_a), row(r_k), row(ln_g), row(ln_b))


def _gla_kernel(p_ref, s0_ref, aup_ref, ab_ref, ng_ref, y_ref, sout_ref, s_scr, *, nb):
    c = pl.program_id(1)
    n_chunks = pl.num_programs(1)
    C = CHUNK
    t_blk = p_ref.shape[1]
    units = [(bl, h) for bl in range(nb) for h in range(GLA_HEADS)]
    U = range(len(units))

    @pl.when(c == 0)
    def _():
        for bl, h in units:
            s_scr[bl, h] = s0_ref[bl, h].T

    o_k = GLA_DK_TOTAL
    o_v = 2 * GLA_DK_TOTAL
    o_r = o_v + GLA_DV_TOTAL
    o_a = o_r + GLA_DV_TOTAL
    tri = (lax.broadcasted_iota(jnp.int32, (C, C), 1)
           <= lax.broadcasted_iota(jnp.int32, (C, C), 0))
    tri_bf = tri.astype(BF16)
    valid_g = lax.broadcasted_iota(jnp.int32, (C, GLA_DK_TOTAL), 0) < t_blk
    valid_k = lax.broadcasted_iota(jnp.int32, (C, GLA_DK), 0) < t_blk
    valid_v = lax.broadcasted_iota(jnp.int32, (C, GLA_DV), 0) < t_blk

    ps, cums = [], []
    for bl in range(nb):
        p = p_ref[bl]
        if t_blk < C:
            p = jnp.concatenate([p, jnp.zeros((C - t_blk, GLA_PROJ_PAD), F32)], axis=0)
        logit = _dot3(p[:, o_a:], aup_ref[...]) + ab_ref[...]
        g = (jnp.minimum(logit, 0.0) - jnp.log(1.0 + jnp.exp(-jnp.abs(logit)))) * (1.0 / GLA_GATE_NORM)
        if t_blk < C:
            g = jnp.where(valid_g, g, 0.0)
        ps.append(p)
        cums.append(_dot_exact_lhs(tri_bf, g))

    q = [ps[bl][:, h * GLA_DK:(h + 1) * GLA_DK] * (GLA_DK ** -0.5) for bl, h in units]
    k = [ps[bl][:, o_k + h * GLA_DK:o_k + (h + 1) * GLA_DK] for bl, h in units]
    v = [ps[bl][:, o_v + h * GLA_DV:o_v + (h + 1) * GLA_DV] for bl, h in units]
    if t_blk < C:
        k = [jnp.where(valid_k, a, 0.0) for a in k]
        v = [jnp.where(valid_v, a, 0.0) for a in v]
    b = [cums[bl][:, h * GLA_DK:(h + 1) * GLA_DK] for bl, h in units]
    clast = [b[u][C - 1:C, :] for u in U]
    cmid = [b[u][C // 2 - 1:C // 2, :] for u in U]
    att = [_bdot_nt(q[u] * jnp.exp(b[u] - cmid[u]), k[u] * jnp.exp(cmid[u] - b[u])) for u in U]
    att = [jnp.where(tri, a, 0.0) for a in att]
    s_t = [s_scr[bl, h] for bl, h in units]
    o = [_bdot(att[u], v[u]) + _bdot_nt(q[u] * jnp.exp(b[u]), s_t[u]) for u in U]
    s_new = [s_t[u] * jnp.exp(clast[u]) + _bdot_tn(v[u], k[u] * jnp.exp(clast[u] - b[u])) for u in U]
    for u, (bl, h) in enumerate(units):
        s_scr[bl, h] = s_new[u]
        r = ps[bl][:, o_r + h * GLA_DV:o_r + (h + 1) * GLA_DV]
        on = o[u] * lax.rsqrt(jnp.mean(o[u] * o[u], axis=-1, keepdims=True) + EPS) * ng_ref[...]
        y_ref[bl, :, h * GLA_DV:(h + 1) * GLA_DV] = (on * _silu(r))[:t_blk]

    @pl.when(c == n_chunks - 1)
    def _():
        for bl, h in units:
            sout_ref[bl, h] = s_scr[bl, h].T


def _gla(p_gla, s0, alpha_up_pad, alpha_b, norm_g, *, nb):
    b, t, _ = p_gla.shape
    t_blk = min(t, CHUNK)
    full = lambda shape: pl.BlockSpec(shape, lambda i, j: (0,) * len(shape))
    state_spec = pl.BlockSpec((nb, GLA_HEADS, GLA_DK, GLA_DV), lambda i, j: (i, 0, 0, 0))
    return pl.pallas_call(
        functools.partial(_gla_kernel, nb=nb),
        out_shape=(jax.ShapeDtypeStruct((b, t, GLA_DV_TOTAL), F32),
                   jax.ShapeDtypeStruct((b, GLA_HEADS, GLA_DK, GLA_DV), F32)),
        grid=(b // nb, t // t_blk),
        in_specs=[pl.BlockSpec((nb, t_blk, GLA_PROJ_PAD), lambda i, j: (i, j, 0)),
                  state_spec,
                  full((LANES, GLA_DK_TOTAL)), full((1, GLA_DK_TOTAL)), full((1, GLA_DV))],
        out_specs=(pl.BlockSpec((nb, t_blk, GLA_DV_TOTAL), lambda i, j: (i, j, 0)), state_spec),
        scratch_shapes=[pltpu.VMEM((nb, GLA_HEADS, GLA_DV, GLA_DK), F32)],
        compiler_params=pltpu.CompilerParams(dimension_semantics=("parallel", "arbitrary"),
                                             vmem_limit_bytes=VMEM_LIMIT),
        name="gla_scan",
    )(p_gla, s0, alpha_up_pad, alpha_b.reshape(1, -1), norm_g.reshape(1, -1))


def _merge_body(yr_ref, yg_ref, gate_ref, x_ref, mod_ref, wbr_ref, wbg_ref, wout_ref, g2_ref,
                router_ref, x1_ref, h2_ref, route_ref):
    mod = mod_ref[0]
    gates = gate_ref[0]
    merged = (gates[:, :D_MODEL] * _bdot(yr_ref[0], wbr_ref[...])
              + gates[:, D_MODEL:] * _bdot(yg_ref[0], wbg_ref[...]))
    x1 = x_ref[0] + mod[2] * _bdot(merged, wout_ref[...])
    x1_ref[0] = x1
    h2 = _rms_scale(x1) * g2_ref[...] * (1.0 + mod[4]) + mod[3]
    h2_ref[:, :D_MODEL] = h2

    tm = h2.shape[0]
    h_hi, h_lo, _ = _split3(h2)
    prod = jnp.dot(jnp.concatenate([h_hi, h_lo], axis=0), router_ref[...], preferred_element_type=F32)
    logits = prod[:tm, :LANES] + prod[tm:, :LANES] + prod[:tm, LANES:]
    lane = lax.broadcasted_iota(jnp.int32, (tm, LANES), 1)
    neg = -jnp.inf
    is_group = (lane >= N_EXPERTS) & (lane < N_EXPERTS + N_GROUPS)
    gl = jnp.where(is_group, logits, neg)
    gmax = jnp.max(gl, axis=1, keepdims=True)
    g_idx = jnp.min(jnp.where(gl == gmax, lane, LANES), axis=1, keepdims=True) - N_EXPERTS
    p_g = 1.0 / jnp.sum(jnp.exp(gl - gmax), axis=1, keepdims=True)
    in_group = (lane >= g_idx * EXPERTS_PER_GROUP) & (lane < (g_idx + 1) * EXPERTS_PER_GROUP)
    el = jnp.where(in_group, logits, neg)
    v1 = jnp.max(el, axis=1, keepdims=True)
    i1 = jnp.min(jnp.where(el == v1, lane, LANES), axis=1, keepdims=True)
    el2 = jnp.where(lane == i1, neg, el)
    v2 = jnp.max(el2, axis=1, keepdims=True)
    i2 = jnp.min(jnp.where(el2 == v2, lane, LANES), axis=1, keepdims=True)
    e21 = jnp.exp(v2 - v1)
    w1 = p_g / (1.0 + e21)
    route = (jnp.where(lane == 0, i1.astype(F32), 0.0) + jnp.where(lane == 1, i2.astype(F32), 0.0)
             + jnp.where(lane == 2, w1, 0.0) + jnp.where(lane == 3, w1 * e21, 0.0))
    route_ref[...] = route
    h2_ref[:, D_MODEL:] = route


def _merge_first_kernel(*refs, n_own):
    step = pl.program_id(0)

    @pl.when(step < n_own)
    def _():
        _merge_body(*refs)

    @pl.when(step >= n_own)
    def _():
        refs[-2][...] = jnp.zeros(refs[-2].shape, F32)
        refs[-1][...] = jnp.zeros(refs[-1].shape, F32)


def _merge_second_kernel(*refs):
    n_in = 10
    _merge_body(*refs[:n_in], *refs[n_in + 2:])


def _merge(y_r, y_g, gates, x, mod, wbr, wbg, wout, norm2_g, router_cat, *, tm, n_tok, row_off, shared=None):
    b, t, _ = x.shape
    per_seq = t // tm
    n_own = b * per_seq
    first = shared is None
    n_steps = n_tok // tm if first else n_own
    assert row_off % tm == 0 and n_tok % tm == 0 and (row_off == 0 or not first)
    blk0 = row_off // tm
    own = lambda s: jnp.minimum(s, n_own - 1)
    full = lambda shape: pl.BlockSpec(shape, lambda s: (0,) * len(shape))
    tile = lambda n: pl.BlockSpec((1, tm, n), lambda s: (own(s) // per_seq, own(s) % per_seq, 0))
    flat = lambda n: pl.BlockSpec((tm, n), lambda s: (blk0 + s, 0))
    in_specs = [tile(D_MODEL), tile(D_MODEL), tile(GATE_PROJ), tile(D_MODEL),
                _mod_spec(mod, tm, lambda s: (own(s) // per_seq, own(s) % per_seq)),
                full((D_MODEL, D_MODEL)), full((D_MODEL, D_MODEL)), full((D_MODEL, D_MODEL)),
                full((1, D_MODEL)), full((D_MODEL, 2 * LANES))]
    args = [y_r, y_g, gates, x, mod, wbr, wbg, wout, norm2_g.reshape(1, -1), router_cat]
    if not first:
        in_specs += [pl.BlockSpec(memory_space=pl.ANY)] * 2
        args += list(shared)
    return pl.pallas_call(
        functools.partial(_merge_first_kernel, n_own=n_own) if first else _merge_second_kernel,
        out_shape=(jax.ShapeDtypeStruct((b, t, D_MODEL), F32),
                   jax.ShapeDtypeStruct((n_tok, MOE_ROW), F32),
                   jax.ShapeDtypeStruct((n_tok, LANES), F32)),
        grid=(n_steps,),
        in_specs=in_specs,
        out_specs=(tile(D_MODEL), flat(MOE_ROW), flat(LANES)),
        input_output_aliases={} if first else {10: 1, 11: 2},
        compiler_params=pltpu.CompilerParams(dimension_semantics=("arbitrary",),
                                             vmem_limit_bytes=VMEM_LIMIT),
        name="merge_router",
    )(*args)


def _moe_plan(expert_ids, n_tok):
    tm = MOE_TILE
    n_tiles = -(-n_tok // tm) + N_GROUPS * (EXPERTS_PER_GROUP * (EXPERTS_PER_GROUP - 1) // 2)
    n_slots = n_tiles * tm
    n_fill = n_slots - n_tok
    row_bits = 15
    assert max(n_tok, n_fill) <= 1 << row_bits and N_PAIR_CLASSES << (row_bits + 1) <= 1 << 24
    e_lo = jnp.minimum(expert_ids[:, 0], expert_ids[:, 1])
    e_hi = jnp.maximum(expert_ids[:, 0], expert_ids[:, 1])
    cls = e_lo * EXPERTS_PER_GROUP + e_hi % EXPERTS_PER_GROUP
    c_iota = jnp.arange(N_PAIR_CLASSES, dtype=jnp.int32)
    counts = jnp.sum((cls[None, :] == c_iota[:, None]).astype(jnp.int32), axis=1)
    tiles_c = (counts + tm - 1) // tm
    n_used = jnp.sum(tiles_c)
    pad_end = jnp.cumsum(tiles_c * tm - counts)
    f_iota = jnp.arange(n_fill, dtype=jnp.int32)
    f_cls = jnp.minimum(jnp.sum((f_iota[:, None] >= pad_end[None, :]).astype(jnp.int32), axis=1), N_PAIR_CLASSES - 1)
    keys = jnp.concatenate([(cls << (row_bits + 1)) + jnp.arange(n_tok, dtype=jnp.int32),
                            (f_cls << (row_bits + 1)) + (1 << row_bits) + f_iota])
    neg_sorted, _ = lax.top_k(-keys.astype(F32), n_slots)
    slots = (-neg_sorted).astype(jnp.int32).reshape(n_tiles, tm)
    valid = (slots & (1 << row_bits)) == 0
    tok = jnp.where(valid, slots & ((1 << row_bits) - 1), 0)
    t_idx = jnp.arange(n_tiles, dtype=jnp.int32)
    j_idx = jnp.arange(tm, dtype=jnp.int32)[None, :]
    dst = jnp.where(valid, tok, n_tok + (t_idx[:, None] % 2) * DMA_GROUP + j_idx % DMA_GROUP)
    n_valid = jnp.sum(valid.astype(jnp.int32), axis=1)
    n_groups = jnp.where(t_idx < n_used, (n_valid + DMA_GROUP - 1) // DMA_GROUP, 0)
    tile_c = slots[:, 0] >> (row_bits + 1)
    last_c = lax.dynamic_slice(tile_c, (jnp.maximum(n_used - 1, 0),), (1,))
    tile_c = jnp.where(t_idx < n_used, tile_c, last_c)
    tile_lo = tile_c // EXPERTS_PER_GROUP
    tile_hi = (tile_c // (EXPERTS_PER_GROUP * EXPERTS_PER_GROUP)) * EXPERTS_PER_GROUP + tile_c % EXPERTS_PER_GROUP
    return (tile_lo, tile_hi, n_groups.astype(jnp.int32), n_used.reshape(1).astype(jnp.int32),
            tok.reshape(n_tiles, 1, tm), dst.reshape(n_tiles, 1, tm))


def _moe_pair_kernel(lo_ref, hi_ref, ng_ref, n_used_ref, tok_ref, tok_next_ref, dst_ref, h2_hbm,
                     wg_lo_ref, wu_lo_ref, wd_lo_ref, wg_hi_ref, wu_hi_ref, wd_hi_ref, out_hbm,
                     xbuf, obuf, w_bf, gsem, ssem):
    t = pl.program_id(0)
    n_t = pl.num_programs(0)
    tm = MOE_TILE
    slot = lax.rem(t, 2)
    n_used = n_used_ref[0]

    def gather(idx_ref, s, n_groups):
        def body(g, carry):
            for u in range(DMA_GROUP):
                pltpu.make_async_copy(h2_hbm.at[pl.ds(idx_ref[0, 0, g * DMA_GROUP + u], 1)],
                                      xbuf.at[s, g, pl.ds(u, 1)], gsem.at[s]).start()
            return carry
        lax.fori_loop(0, n_groups, body, 0)

    def wait_gather(s, n_groups):
        def body(g, carry):
            pltpu.make_async_copy(h2_hbm.at[pl.ds(0, DMA_GROUP)], xbuf.at[s, 0], gsem.at[s]).wait()
            return carry
        lax.fori_loop(0, n_groups, body, 0)

    def wait_scatter(s, n_groups):
        def body(g, carry):
            pltpu.make_async_copy(obuf.at[s, 0], out_hbm.at[pl.ds(0, DMA_GROUP)], ssem.at[s]).wait()
            return carry
        lax.fori_loop(0, n_groups, body, 0)

    @pl.when(t == 0)
    def _():
        xbuf[...] = jnp.zeros(xbuf.shape, F32)
        obuf[0, 0] = jnp.zeros(obuf.shape[2:], F32)
        n_rows = out_hbm.shape[0]
        for half in range(2):
            fill = pltpu.make_async_copy(obuf.at[0, 0], out_hbm.at[pl.ds(n_rows - (2 - half) * DMA_GROUP, DMA_GROUP)],
                                         ssem.at[0])
            fill.start()
            fill.wait()
        gather(tok_ref, 0, ng_ref[0])

    @pl.when((t >= 2) & (t - 2 < n_used))
    def _():
        wait_scatter(slot, ng_ref[jnp.maximum(t - 2, 0)])

    @pl.when(t < n_used)
    def _():
        n_here = ng_ref[t]
        wait_gather(slot, n_here)

        @pl.when(t + 1 < n_used)
        def _():
            gather(tok_next_ref, 1 - slot, ng_ref[jnp.minimum(t + 1, n_t - 1)])

        t_prev = jnp.maximum(t - 1, 0)
        for which, (e_ref, refs) in enumerate(((lo_ref, (wg_lo_ref, wu_lo_ref, wd_lo_ref)),
                                               (hi_ref, (wg_hi_ref, wu_hi_ref, wd_hi_ref)))):
            @pl.when((t == 0) | (e_ref[t] != e_ref[t_prev]))
            def _():
                for m, ref in enumerate(refs):
                    w_bf[which][m][...] = ref[0].astype(BF16)

        xe = xbuf[slot].reshape(tm, MOE_ROW)
        x = xe[:, :D_MODEL].astype(BF16)
        route = xe[:, D_MODEL:]
        lane = lax.broadcasted_iota(jnp.int32, route.shape, 1)
        pick = lambda k: jnp.sum(jnp.where(lane == k, route, 0.0), axis=1, keepdims=True)
        i1, i2, w1, w2 = pick(0), pick(1), pick(2), pick(3)
        w_lo = jnp.where(i1 < i2, w1, w2)
        w_hi = jnp.where(i1 < i2, w2, w1)
        acc = None
        for which, wt in enumerate((w_lo, w_hi)):
            wg, wu, wd = w_bf[which]
            hid = _silu(jnp.dot(x, wg[...], preferred_element_type=F32)) * jnp.dot(
                x, wu[...], preferred_element_type=F32)
            part = wt * _bdot(hid, wd[...])
            acc = part if acc is None else acc + part
        obuf[slot] = acc.reshape(obuf.shape[1:])

        def body(g, carry):
            for u in range(DMA_GROUP):
                pltpu.make_async_copy(obuf.at[slot, g, pl.ds(u, 1)],
                                      out_hbm.at[pl.ds(dst_ref[0, 0, g * DMA_GROUP + u], 1)], ssem.at[slot]).start()
            return carry
        lax.fori_loop(0, n_here, body, 0)

    @pl.when(t == n_t - 1)
    def _():
        @pl.when((t >= 1) & (t - 1 < n_used))
        def _():
            wait_scatter(1 - slot, ng_ref[jnp.maximum(t - 1, 0)])

        @pl.when(t < n_used)
        def _():
            wait_scatter(slot, ng_ref[t])


def _moe_grouped(h2_ext, expert_ids, wg, wu, wd):
    return _moe_call(h2_ext, *_moe_plan(expert_ids, h2_ext.shape[0]), wg, wu, wd)


def _moe_call(h2_ext, tile_lo, tile_hi, n_groups, n_used, tok, dst, wg, wu, wd):
    n_tok = h2_ext.shape[0]
    tm = MOE_TILE
    n_tiles = tile_lo.shape[0]
    smem_tile = lambda f: pl.BlockSpec((1, 1, tm), f, memory_space=pltpu.SMEM)
    w_lo = lambda shape: pl.BlockSpec(shape, lambda t, lo, hi, nv, nu: (lo[t], 0, 0))
    w_hi = lambda shape: pl.BlockSpec(shape, lambda t, lo, hi, nv, nu: (hi[t], 0, 0))
    up, down = (1, D_MODEL, EXPERT_FF), (1, EXPERT_FF, D_MODEL)
    bf_set = [pltpu.VMEM(up[1:], BF16), pltpu.VMEM(up[1:], BF16), pltpu.VMEM(down[1:], BF16)]
    grid_spec = pltpu.PrefetchScalarGridSpec(
        num_scalar_prefetch=4,
        grid=(n_tiles,),
        in_specs=[smem_tile(lambda t, *_: (t, 0, 0)),
                  smem_tile(lambda t, *_: (jnp.minimum(t + 1, n_tiles - 1), 0, 0)),
                  smem_tile(lambda t, *_: (t, 0, 0)),
                  pl.BlockSpec(memory_space=pl.ANY),
                  w_lo(up), w_lo(up), w_lo(down), w_hi(up), w_hi(up), w_hi(down)],
        out_specs=pl.BlockSpec(memory_space=pl.ANY),
        scratch_shapes=[pltpu.VMEM((2, tm // DMA_GROUP, DMA_GROUP, MOE_ROW), F32),
                        pltpu.VMEM((2, tm // DMA_GROUP, DMA_GROUP, D_MODEL), F32),
                        [bf_set, bf_set],
                        pltpu.SemaphoreType.DMA((2,)), pltpu.SemaphoreType.DMA((2,))],
    )
    return pl.pallas_call(
        _moe_pair_kernel,
        out_shape=jax.ShapeDtypeStruct((n_tok + 2 * DMA_GROUP, D_MODEL), F32),
        grid_spec=grid_spec,
        compiler_params=pltpu.CompilerParams(dimension_semantics=("arbitrary",),
                                             vmem_limit_bytes=VMEM_LIMIT),
        name="moe_pairs",
    )(tile_lo, tile_hi, n_groups, n_used, tok, tok, dst, h2_ext, wg, wu, wd, wg, wu, wd)


def _final_kernel(x1_ref, moe_ref, mod_ref, fg_ref, y_ref):
    x2 = x1_ref[0] + mod_ref[0][5] * moe_ref[...]
    y_ref[0] = _rms_scale(x2) * fg_ref[...]


def _final(x1, moe_out, mod, final_g, *, row_off, tm):
    b, t, _ = x1.shape
    blk0 = row_off // tm
    per_seq = t // tm
    return pl.pallas_call(
        _final_kernel,
        out_shape=jax.ShapeDtypeStruct((b, t, D_MODEL), F32),
        grid=(b, per_seq),
        in_specs=[pl.BlockSpec((1, tm, D_MODEL), lambda i, j: (i, j, 0)),
                  pl.BlockSpec((tm, D_MODEL), lambda i, j: (blk0 + i * per_seq + j, 0)),
                  _mod_spec(mod, tm, lambda i, j: (i, j)),
                  pl.BlockSpec((1, D_MODEL), lambda i, j: (0, 0))],
        out_specs=pl.BlockSpec((1, tm, D_MODEL), lambda i, j: (i, j, 0)),
        compiler_params=pltpu.CompilerParams(dimension_semantics=("parallel", "parallel"),
                                             vmem_limit_bytes=VMEM_LIMIT),
        name="moe_combine_norm",
    )(x1, moe_out, mod, final_g.reshape(1, -1))


def _mixer(x, mod, shift0, s_rwkv0, s_gla0, wts, *, n_tok, row_off, shared=None):
    b, t, _ = x.shape
    per_token = mod.shape[2] > 1
    rows = x.reshape(1, b * t, D_MODEL) if per_token else x
    tm = min(rows.shape[1], ROW_TILE)
    p_rwkv = _norm_proj(rows, mod, wts["norm1_g"], wts["w_in_rwkv"], gate=False, tm=tm)
    p_gla = _norm_proj(rows, mod, wts["norm1_g"], wts["w_in_gla"], gate=False, tm=tm)
    gates = _norm_proj(rows, mod, wts["norm1_g"], wts["w_in_gate"], gate=True, tm=tm)
    p_rwkv = p_rwkv.reshape(b, t, RWKV_PROJ)

    y_r, s_rwkv = _rwkv(p_rwkv, shift0, s_rwkv0,
                        wts["rwkv_mu"], wts["rwkv_w0"], wts["rwkv_decay_up"], wts["rwkv_a0"],
                        wts["rwkv_aaa_up"], wts["rwkv_gate_up"], wts["rwkv_k_k"], wts["rwkv_k_a"],
                        wts["rwkv_r_k"], wts["rwkv_ln_g"], wts["rwkv_ln_b"], nb=RWKV_SEQS_PER_STEP)
    y_g, s_gla = _gla(p_gla.reshape(b, t, GLA_PROJ_PAD), s_gla0, wts["gla_alpha_up"],
                      wts["gla_alpha_b"], wts["gla_norm_g"], nb=GLA_SEQS_PER_STEP)

    x1, h2_all, route_all = _merge(
        y_r.reshape(rows.shape), y_g.reshape(rows.shape), gates, rows, mod, wts["w_branch_rwkv"],
        wts["w_branch_gla"], wts["w_out"], wts["norm2_g"], wts["router_cat"],
        tm=tm, n_tok=n_tok, row_off=row_off, shared=shared)
    states = (p_rwkv[:, t - 1:t, :][None], s_rwkv[None], s_gla[None])
    return x1, h2_all, route_all, states


def kernel(x_prompt, x_sample, c_prompt, c_sample, state_rwkv_shift, state_rwkv, state_gla, w_ada, b_ada, norm1_g, w_in, rwkv_mu, rwkv_w0, rwkv_decay_up, rwkv_a0, rwkv_aaa_up, rwkv_gate_up, rwkv_k_k, rwkv_k_a, rwkv_r_k, rwkv_ln_g, rwkv_ln_b, gla_alpha_up, gla_alpha_b, gla_norm_g, w_branch_rwkv, w_branch_gla, w_out, norm2_g, router_group, router_expert, expert_w_gate, expert_w_up, expert_w_down, final_norm_g):
    bp, tp = x_prompt.shape[:2]
    bs, ts = x_sample.shape[:2]
    w_in0 = w_in[0]
    g0 = RWKV_PROJ
    o3 = 2 * GLA_DK_TOTAL + GLA_DV_TOTAL
    o4 = o3 + GLA_GATE_RANK
    w_gla = w_in0[:, g0:g0 + GLA_PROJ]
    w_gla = jnp.concatenate([w_gla[:, :o3], w_gla[:, o4:], w_gla[:, o3:o4],
                             jnp.zeros((D_MODEL, GLA_PROJ_PAD - GLA_PROJ), F32)], axis=1)
    wts = dict(
        norm1_g=norm1_g[0].reshape(1, -1),
        w_in_rwkv=w_in0[:, :g0].astype(BF16),
        w_in_gla=w_gla.astype(BF16),
        w_in_gate=w_in0[:, g0 + GLA_PROJ:].astype(BF16),
        rwkv_mu=rwkv_mu[0], rwkv_w0=rwkv_w0[0], rwkv_decay_up=rwkv_decay_up[0], rwkv_a0=rwkv_a0[0],
        rwkv_aaa_up=rwkv_aaa_up[0], rwkv_gate_up=rwkv_gate_up[0], rwkv_k_k=rwkv_k_k[0],
        rwkv_k_a=rwkv_k_a[0], rwkv_r_k=rwkv_r_k[0], rwkv_ln_g=rwkv_ln_g[0], rwkv_ln_b=rwkv_ln_b[0],
        gla_alpha_up=jnp.pad(gla_alpha_up[0], ((0, LANES - GLA_GATE_RANK), (0, 0))),
        gla_alpha_b=gla_alpha_b[0], gla_norm_g=gla_norm_g[0],
        w_branch_rwkv=w_branch_rwkv[0].astype(BF16), w_branch_gla=w_branch_gla[0].astype(BF16),
        w_out=w_out[0].astype(BF16), norm2_g=norm2_g[0],
        router_cat=_split_hi_lo(jnp.pad(jnp.concatenate([router_expert[0], router_group[0]], axis=1),
                                        ((0, 0), (0, LANES - N_EXPERTS - N_GROUPS)))),
        expert_w_gate=expert_w_gate[0], expert_w_up=expert_w_up[0], expert_w_down=expert_w_down[0],
    )
    mod = _ada(jnp.concatenate([c_prompt, c_sample], axis=0), w_ada[0], b_ada[0])
    mod = mod.reshape(bp + bs, N_MOD, D_MODEL)
    n_p = bp * tp
    n_s = bs * ts
    n_tok = n_p + n_s
    mod_p = mod[:bp].reshape(bp, N_MOD, 1, D_MODEL)
    mod_s = jnp.repeat(mod[bp:].transpose(1, 0, 2), ts, axis=1)[None]

    dt = x_prompt.dtype
    x1_p, h2_all, route_all, states_p = _mixer(
        x_prompt, mod_p, jnp.zeros((bp, 1, RWKV_PROJ), dt),
        jnp.zeros((bp, RWKV_HEADS, RWKV_HEAD_DIM, RWKV_HEAD_DIM), state_rwkv.dtype),
        jnp.zeros((bp, GLA_HEADS, GLA_DK, GLA_DV), state_gla.dtype), wts, n_tok=n_tok, row_off=0)
    x1_s, h2_all, route_all, states_s = _mixer(
        x_sample, mod_s, state_rwkv_shift[0], state_rwkv[0], state_gla[0], wts,
        n_tok=n_tok, row_off=n_p, shared=(h2_all, route_all))

    expert_ids = route_all[:, :2].astype(jnp.int32)
    moe_out = _moe_grouped(h2_all, expert_ids, wts["expert_w_gate"], wts["expert_w_up"],
                           wts["expert_w_down"])
    y_p = _final(x1_p, moe_out, mod_p, final_norm_g, row_off=0, tm=min(tp, ROW_TILE))
    y_s = _final(x1_s, moe_out, mod_s, final_norm_g, row_off=n_p, tm=min(n_s, ROW_TILE))
    return (y_p, y_s.reshape(bs, ts, D_MODEL)) + states_p + states_s
```

```python
import functools
import math

import jax
import jax.numpy as jnp
from jax import lax
from jax.experimental import pallas as pl
from jax.experimental.pallas import tpu as pltpu

F32 = jnp.float32
BF16 = jnp.bfloat16

D_MODEL = 1024
N_MOD = 6
EPS = 1e-6
RWKV_HEAD_DIM = 64
RWKV_HEADS = 16
RWKV_PAIRS = RWKV_HEADS // 2
DECAY_LORA = 64
AAA_LORA = 64
GATE_LORA = 128
RWKV_PROJ = 3 * D_MODEL + DECAY_LORA + AAA_LORA + GATE_LORA
LN_X_EPS = 64e-5
GLA_HEADS = 4
GLA_DK = 128
GLA_DV = 256
GLA_DK_TOTAL = GLA_HEADS * GLA_DK
GLA_DV_TOTAL = GLA_HEADS * GLA_DV
GLA_GATE_RANK = 16
GLA_GATE_NORM = 16.0
GLA_PROJ = 2 * GLA_DK_TOTAL + 2 * GLA_DV_TOTAL + GLA_GATE_RANK
LANES = 128
GLA_PROJ_PAD = 2 * GLA_DK_TOTAL + 2 * GLA_DV_TOTAL + LANES
GATE_PROJ = 2 * D_MODEL
N_GROUPS = 4
EXPERTS_PER_GROUP = 8
N_EXPERTS = 32
EXPERT_FF = 512

CHUNK = 64
ROW_TILE = 512
MOE_TILE = 192
MOE_ROW = D_MODEL + LANES
N_PAIR_CLASSES = N_EXPERTS * EXPERTS_PER_GROUP
RWKV_SEQS_PER_STEP = 4
GLA_SEQS_PER_STEP = 8
VMEM_LIMIT = 56 * 1024 * 1024


def _bdot(a, b):
    return jnp.dot(a.astype(BF16), b.astype(BF16), preferred_element_type=F32)


def _bdot_nt(a, b):
    return lax.dot_general(a.astype(BF16), b.astype(BF16), (((1,), (1,)), ((), ())),
                           preferred_element_type=F32)


def _bdot_tn(a, b):
    return jnp.dot(a.T.astype(BF16), b.astype(BF16), preferred_element_type=F32)


def _split3(x):
    h1 = x.astype(BF16)
    r1 = x - h1.astype(F32)
    h2 = r1.astype(BF16)
    h3 = (r1 - h2.astype(F32)).astype(BF16)
    return h1, h2, h3


def _dot3(a, b):
    a1, a2, _ = _split3(a)
    b1, b2, _ = _split3(b)
    return (jnp.dot(a1, b1, preferred_element_type=F32)
            + jnp.dot(a2, b1, preferred_element_type=F32)
            + jnp.dot(a1, b2, preferred_element_type=F32))


def _dot_exact_lhs(a_bf16, x):
    x1, x2, _ = _split3(x)
    return (jnp.dot(a_bf16, x1, preferred_element_type=F32)
            + jnp.dot(a_bf16, x2, preferred_element_type=F32))


def _split_hi_lo(w):
    hi, lo, _ = _split3(w)
    return jnp.concatenate([hi, lo], axis=1)


def _sigmoid(x):
    return 1.0 / (1.0 + jnp.exp(-x))


def _silu(x):
    return x * _sigmoid(x)


def _rms_scale(x):
    return x * lax.rsqrt(jnp.mean(x * x, axis=-1, keepdims=True) + EPS)


def _ada_kernel(c_ref, w_ref, b_ref, o_ref):
    o_ref[...] = _dot3(_silu(c_ref[...]), w_ref[...]) + b_ref[...]


def _ada(c, w_ada, b_ada):
    n_rows = c.shape[0]
    n_out = w_ada.shape[1]
    tn = 1536
    return pl.pallas_call(
        _ada_kernel,
        out_shape=jax.ShapeDtypeStruct((n_rows, n_out), F32),
        grid=(n_out // tn,),
        in_specs=[pl.BlockSpec((n_rows, D_MODEL), lambda j: (0, 0)),
                  pl.BlockSpec((D_MODEL, tn), lambda j: (0, j)),
                  pl.BlockSpec((1, tn), lambda j: (0, j))],
        out_specs=pl.BlockSpec((n_rows, tn), lambda j: (0, j)),
        compiler_params=pltpu.CompilerParams(dimension_semantics=("parallel",),
                                             vmem_limit_bytes=VMEM_LIMIT),
        name="ada",
    )(c, w_ada, b_ada.reshape(1, n_out))


def _mod_spec(mod, tm, seq_tile):
    if mod.shape[2] == 1:
        return pl.BlockSpec((1, N_MOD, 1, D_MODEL), lambda *g: (seq_tile(*g)[0], 0, 0, 0))
    return pl.BlockSpec((1, N_MOD, tm, D_MODEL), lambda *g: (seq_tile(*g)[0], 0, seq_tile(*g)[1], 0))


def _norm_proj_kernel(x_ref, mod_ref, g_ref, w_ref, o_ref, *, gate):
    mod = mod_ref[0]
    h = _rms_scale(x_ref[0]) * g_ref[...] * (1.0 + mod[1]) + mod[0]
    out = _bdot(h, w_ref[...])
    o_ref[0] = _sigmoid(out) if gate else out


def _norm_proj(x, mod, g, w_bf16, *, gate, tm):
    b, t, _ = x.shape
    n = w_bf16.shape[1]
    return pl.pallas_call(
        functools.partial(_norm_proj_kernel, gate=gate),
        out_shape=jax.ShapeDtypeStruct((b, t, n), F32),
        grid=(b, t // tm),
        in_specs=[pl.BlockSpec((1, tm, D_MODEL), lambda i, j: (i, j, 0)),
                  _mod_spec(mod, tm, lambda i, j: (i, j)),
                  pl.BlockSpec((1, D_MODEL), lambda i, j: (0, 0)),
                  pl.BlockSpec((D_MODEL, n), lambda i, j: (0, 0))],
        out_specs=pl.BlockSpec((1, tm, n), lambda i, j: (i, j, 0)),
        compiler_params=pltpu.CompilerParams(dimension_semantics=("parallel", "parallel"),
                                             vmem_limit_bytes=VMEM_LIMIT),
        name="norm_proj_gate" if gate else "norm_proj",
    )(x, mod, g, w_bf16)


def _blockdiag(x, lo_mask):
    return jnp.concatenate([jnp.where(lo_mask, x, 0.0), jnp.where(lo_mask, 0.0, x)], axis=0)


def _rwkv_kernel(p_ref, shift_ref, s0_ref, mu_ref, w0_ref, dup_ref, a0_ref, aup_ref, gup_ref,
                 kk_ref, ka_ref, rk_ref, lng_ref, lnb_ref, y_ref, sout_ref, s_scr, carry_scr,
                 *, nb):
    c = pl.program_id(1)
    n_chunks = pl.num_programs(1)
    C = CHUNK
    N = RWKV_HEAD_DIM
    t_blk = p_ref.shape[1]

    @pl.when(c == 0)
    def _():
        for bl in range(nb):
            for pr in range(RWKV_PAIRS):
                s_scr[bl, pr] = jnp.concatenate([s0_ref[bl, 2 * pr], s0_ref[bl, 2 * pr + 1]], axis=1)
        carry_scr[...] = shift_ref[...]

    row = lax.broadcasted_iota(jnp.int32, (C, LANES), 0)
    lane = lax.broadcasted_iota(jnp.int32, (C, LANES), 1)
    lane_t = jnp.where(lane < N, lane, lane - N)
    lo = lane < N
    strict = lane_t < row
    incl = lane_t <= row
    eye = (lane_t == row).astype(F32)
    valid = row < t_blk
    tri = (lax.broadcasted_iota(jnp.int32, (C, C), 1)
           <= lax.broadcasted_iota(jnp.int32, (C, C), 0)).astype(BF16)
    sq_r = lax.broadcasted_iota(jnp.int32, (LANES, LANES), 0)
    sq_c = lax.broadcasted_iota(jnp.int32, (LANES, LANES), 1)
    same_head = (sq_r < N) == (sq_c < N)
    diag = sq_r == sq_c
    row_w = lax.broadcasted_iota(jnp.int32, (C, RWKV_PROJ), 0)
    o3 = 3 * D_MODEL

    def hsum(v):
        s_lo = jnp.sum(jnp.where(lo, v, 0.0), axis=1, keepdims=True)
        s_hi = jnp.sum(jnp.where(lo, 0.0, v), axis=1, keepdims=True)
        return jnp.where(lo, s_lo, s_hi)

    def bd(v):
        return _blockdiag(v, lo)

    def cat0(*a):
        return jnp.concatenate(a, axis=0)

    def cat1(*a):
        return jnp.concatenate(a, axis=1)

    valid_w = lax.broadcasted_iota(jnp.int32, (C, D_MODEL), 0) < t_blk
    xs, logw_all, cum_all, asig_all, g_all = [], [], [], [], []
    for bl in range(nb):
        p = p_ref[bl]
        carry_new = p[t_blk - 1:t_blk, :]
        if t_blk < C:
            p = jnp.concatenate([p, jnp.zeros((C - t_blk, RWKV_PROJ), F32)], axis=0)
        prev = jnp.where(row_w == 0, carry_scr[bl], pltpu.roll(p, 1, axis=0))
        carry_scr[bl] = carry_new
        x = p + (prev - p) * mu_ref[...]
        xw = x[:, o3:o3 + DECAY_LORA]
        xa = x[:, o3 + DECAY_LORA:o3 + DECAY_LORA + AAA_LORA]
        xg = x[:, o3 + DECAY_LORA + AAA_LORA:]
        z = w0_ref[...] + _bdot(jnp.tanh(xw), dup_ref[...])
        logw_full = -math.exp(-0.5) * _sigmoid(z)
        if t_blk < C:
            logw_full = jnp.where(valid_w, logw_full, 0.0)
        xs.append(x)
        logw_all.append(logw_full)
        cum_all.append(_dot_exact_lhs(tri, logw_full))
        asig_all.append(_sigmoid(a0_ref[...] + _bdot(xa, aup_ref[...])))
        g_all.append(_bdot(_sigmoid(xg), gup_ref[...]))

    units = [(bl, pr) for bl in range(nb) for pr in range(RWKV_PAIRS)]
    U = range(len(units))
    cols = [slice(pr * LANES, (pr + 1) * LANES) for _, pr in units]
    r = [xs[bl][:, pr * LANES:(pr + 1) * LANES] for bl, pr in units]
    k = [xs[bl][:, D_MODEL + pr * LANES:D_MODEL + (pr + 1) * LANES] for bl, pr in units]
    v = [xs[bl][:, 2 * D_MODEL + pr * LANES:2 * D_MODEL + (pr + 1) * LANES] for bl, pr in units]
    logw = [logw_all[bl][:, cols[u]] for u, (bl, _) in enumerate(units)]
    asig = [asig_all[bl][:, cols[u]] for u, (bl, _) in enumerate(units)]
    kk = [k[u] * kk_ref[:, cols[u]] for u in U]
    k2 = [k[u] * (1.0 + (asig[u] - 1.0) * ka_ref[:, cols[u]]) for u in U]
    ss = [hsum(kk[u] * kk[u]) for u in U]
    cum = [cum_all[bl][:, cols[u]] for u, (bl, _) in enumerate(units)]
    bsum = [hsum(r[u] * k2[u] * rk_ref[:, cols[u]]) for u in U]
    kk = [kk[u] * lax.rsqrt(jnp.maximum(ss[u], 1e-24)) for u in U]
    bonus = [bsum[u] * v[u] for u in U]
    if t_blk < C:
        kk = [jnp.where(valid, a, 0.0) for a in kk]
        k2 = [jnp.where(valid, a, 0.0) for a in k2]
        v = [jnp.where(valid, a, 0.0) for a in v]
        r = [jnp.where(valid, a, 0.0) for a in r]
    b_vec = [kk[u] * asig[u] for u in U]
    clast = [cum[u][C - 1:C, :] for u in U]
    e_inv = [jnp.exp(-cum[u]) for u in U]
    e_tail = [jnp.exp(clast[u] - cum[u]) for u in U]
    aq = [-kk[u] * jnp.exp(cum[u] - logw[u]) for u in U]
    rq = [r[u] * jnp.exp(cum[u]) for u in U]
    bk = [b_vec[u] * e_inv[u] for u in U]
    kx = [k2[u] * e_inv[u] for u in U]
    bb = [b_vec[u] * e_tail[u] for u in U]
    kb = [k2[u] * e_tail[u] for u in U]

    g4 = [_bdot_nt(cat0(aq[u], rq[u]), cat0(bd(bk[u]), bd(kx[u]))) for u in U]
    m_ab = [jnp.where(strict, g4[u][:C, :LANES], 0.0) for u in U]
    m_ak = [jnp.where(strict, g4[u][:C, LANES:], 0.0) for u in U]
    m_rb = [jnp.where(incl, g4[u][C:, :LANES], 0.0) for u in U]
    m_rk = [jnp.where(incl, g4[u][C:, LANES:], 0.0) for u in U]

    t_inv = [eye + m_ab[u] for u in U]
    l_pow = [_bdot(m_ab[u], bd(m_ab[u])) for u in U]
    mv = [_bdot(cat0(m_ak[u], m_rk[u]), bd(v[u])) for u in U]
    n_steps = C.bit_length() - 1
    for step in range(1, n_steps):
        if step < n_steps - 1:
            both = [_bdot(l_pow[u], cat1(bd(t_inv[u]), bd(l_pow[u]))) for u in U]
            t_inv = [t_inv[u] + both[u][:, :LANES] for u in U]
            l_pow = [both[u][:, LANES:] for u in U]
        else:
            t_inv = [t_inv[u] + _bdot(l_pow[u], bd(t_inv[u])) for u in U]

    au = [_bdot(t_inv[u], cat1(bd(aq[u]), bd(mv[u][:C]))) for u in U]
    a_bar = [au[u][:, :LANES] for u in U]
    u_bar = [au[u][:, LANES:] for u in U]
    mau = [_bdot(m_rb[u], cat1(bd(a_bar[u]), bd(u_bar[u]))) for u in U]
    p_full = [_bdot_tn(a_bar[u], bb[u]) for u in U]
    q_full = [_bdot_tn(cat0(u_bar[u], v[u]), cat0(bb[u], kb[u])) for u in U]
    r_til = [rq[u] + mau[u][:, :LANES] for u in U]
    y_loc = [mau[u][:, LANES:] + mv[u][C:] for u in U]
    p_bd = [jnp.where(same_head, p_full[u], 0.0) + jnp.where(diag, jnp.exp(clast[u]), 0.0) for u in U]
    q_pk = [jnp.where(lo[:N], q_full[u][:N], q_full[u][N:]) for u in U]

    s_old = [s_scr[bl, pr] for bl, pr in units]
    y = [_bdot_nt(r_til[u], _blockdiag(s_old[u], lo[:N])) + y_loc[u] for u in U]
    s_new = [_bdot(s_old[u], p_bd[u]) + q_pk[u] for u in U]
    for u, (bl, pr) in enumerate(units):
        s_scr[bl, pr] = s_new[u]

    mu_y = [hsum(y[u]) * (1.0 / N) for u in U]
    yc = [y[u] - mu_y[u] for u in U]
    var = [hsum(yc[u] * yc[u]) * (1.0 / N) for u in U]
    for u, (bl, pr) in enumerate(units):
        yn = yc[u] * lax.rsqrt(var[u] + LN_X_EPS) * lng_ref[:, cols[u]] + lnb_ref[:, cols[u]]
        y_ref[bl, :, cols[u]] = ((yn + bonus[u]) * g_all[bl][:, cols[u]])[:t_blk]

    @pl.when(c == n_chunks - 1)
    def _():
        for bl in range(nb):
            for pr in range(RWKV_PAIRS):
                sout_ref[bl, 2 * pr] = s_scr[bl, pr][:, :N]
                sout_ref[bl, 2 * pr + 1] = s_scr[bl, pr][:, N:]


def _rwkv(p_rwkv, shift_prev, s0, mu, w0, decay_up, a0, aaa_up, gate_up, k_k, k_a, r_k,
          ln_g, ln_b, *, nb):
    b, t, _ = p_rwkv.shape
    nb = math.gcd(b, nb)
    t_blk = min(t, CHUNK)
    row = lambda a: a.reshape(1, -1)
    full = lambda shape: pl.BlockSpec(shape, lambda i, j: (0,) * len(shape))
    state_spec = pl.BlockSpec((nb, RWKV_HEADS, RWKV_HEAD_DIM, RWKV_HEAD_DIM), lambda i, j: (i, 0, 0, 0))
    return pl.pallas_call(
        functools.partial(_rwkv_kernel, nb=nb),
        out_shape=(jax.ShapeDtypeStruct((b, t, D_MODEL), F32),
                   jax.ShapeDtypeStruct((b, RWKV_HEADS, RWKV_HEAD_DIM, RWKV_HEAD_DIM), F32)),
        grid=(b // nb, t // t_blk),
        in_specs=[pl.BlockSpec((nb, t_blk, RWKV_PROJ), lambda i, j: (i, j, 0)),
                  pl.BlockSpec((nb, 1, RWKV_PROJ), lambda i, j: (i, 0, 0)),
                  state_spec,
                  full((1, RWKV_PROJ)), full((1, D_MODEL)), full((DECAY_LORA, D_MODEL)),
                  full((1, D_MODEL)), full((AAA_LORA, D_MODEL)), full((GATE_LORA, D_MODEL)),
                  full((1, D_MODEL)), full((1, D_MODEL)), full((1, D_MODEL)),
                  full((1, D_MODEL)), full((1, D_MODEL))],
        out_specs=(pl.BlockSpec((nb, t_blk, D_MODEL), lambda i, j: (i, j, 0)), state_spec),
        scratch_shapes=[pltpu.VMEM((nb, RWKV_PAIRS, RWKV_HEAD_DIM, LANES), F32),
                        pltpu.VMEM((nb, 1, RWKV_PROJ), F32)],
        compiler_params=pltpu.CompilerParams(dimension_semantics=("parallel", "arbitrary"),
                                             vmem_limit_bytes=VMEM_LIMIT),
        name="rwkv_scan",
    )(p_rwkv, shift_prev, s0, row(mu), row(w0), decay_up, row(a0), aaa_up, gate_up,
      row(k_k), row(k_a), row(r_k), row(ln_g), row(ln_b))


def _gla_kernel(p_ref, s0_ref, aup_ref, ab_ref, ng_ref, y_ref, sout_ref, s_scr, *, nb):
    c = pl.program_id(1)
    n_chunks = pl.num_programs(1)
    C = CHUNK
    t_blk = p_ref.shape[1]
    units = [(bl, h) for bl in range(nb) for h in range(GLA_HEADS)]
    U = range(len(units))

    @pl.when(c == 0)
    def _():
        for bl, h in units:
            s_scr[bl, h] = s0_ref[bl, h].T

    o_k = GLA_DK_TOTAL
    o_v = 2 * GLA_DK_TOTAL
    o_r = o_v + GLA_DV_TOTAL
    o_a = o_r + GLA_DV_TOTAL
    tri = (lax.broadcasted_iota(jnp.int32, (C, C), 1)
           <= lax.broadcasted_iota(jnp.int32, (C, C), 0))
    tri_bf = tri.astype(BF16)
    valid_g = lax.broadcasted_iota(jnp.int32, (C, GLA_DK_TOTAL), 0) < t_blk
    valid_k = lax.broadcasted_iota(jnp.int32, (C, GLA_DK), 0) < t_blk
    valid_v = lax.broadcasted_iota(jnp.int32, (C, GLA_DV), 0) < t_blk

    ps, cums = [], []
    for bl in range(nb):
        p = p_ref[bl]
        if t_blk < C:
            p = jnp.concatenate([p, jnp.zeros((C - t_blk, GLA_PROJ_PAD), F32)], axis=0)
        logit = _dot3(p[:, o_a:], aup_ref[...]) + ab_ref[...]
        g = (jnp.minimum(logit, 0.0) - jnp.log(1.0 + jnp.exp(-jnp.abs(logit)))) * (1.0 / GLA_GATE_NORM)
        if t_blk < C:
            g = jnp.where(valid_g, g, 0.0)
        ps.append(p)
        cums.append(_dot_exact_lhs(tri_bf, g))

    q = [ps[bl][:, h * GLA_DK:(h + 1) * GLA_DK] * (GLA_DK ** -0.5) for bl, h in units]
    k = [ps[bl][:, o_k + h * GLA_DK:o_k + (h + 1) * GLA_DK] for bl, h in units]
    v = [ps[bl][:, o_v + h * GLA_DV:o_v + (h + 1) * GLA_DV] for bl, h in units]
    if t_blk < C:
        k = [jnp.where(valid_k, a, 0.0) for a in k]
        v = [jnp.where(valid_v, a, 0.0) for a in v]
    b = [cums[bl][:, h * GLA_DK:(h + 1) * GLA_DK] for bl, h in units]
    clast = [b[u][C - 1:C, :] for u in U]
    cmid = [b[u][C // 2 - 1:C // 2, :] for u in U]
    att = [_bdot_nt(q[u] * jnp.exp(b[u] - cmid[u]), k[u] * jnp.exp(cmid[u] - b[u])) for u in U]
    att = [jnp.where(tri, a, 0.0) for a in att]
    s_t = [s_scr[bl, h] for bl, h in units]
    o = [_bdot(att[u], v[u]) + _bdot_nt(q[u] * jnp.exp(b[u]), s_t[u]) for u in U]
    s_new = [s_t[u] * jnp.exp(clast[u]) + _bdot_tn(v[u], k[u] * jnp.exp(clast[u] - b[u])) for u in U]
    for u, (bl, h) in enumerate(units):
        s_scr[bl, h] = s_new[u]
        r = ps[bl][:, o_r + h * GLA_DV:o_r + (h + 1) * GLA_DV]
        on = o[u] * lax.rsqrt(jnp.mean(o[u] * o[u], axis=-1, keepdims=True) + EPS) * ng_ref[...]
        y_ref[bl, :, h * GLA_DV:(h + 1) * GLA_DV] = (on * _silu(r))[:t_blk]

    @pl.when(c == n_chunks - 1)
    def _():
        for bl, h in units:
            sout_ref[bl, h] = s_scr[bl, h].T


def _gla(p_gla, s0, alpha_up_pad, alpha_b, norm_g, *, nb):
    b, t, _ = p_gla.shape
    nb = math.gcd(b, nb)
    t_blk = min(t, CHUNK)
    full = lambda shape: pl.BlockSpec(shape, lambda i, j: (0,) * len(shape))
    state_spec = pl.BlockSpec((nb, GLA_HEADS, GLA_DK, GLA_DV), lambda i, j: (i, 0, 0, 0))
    return pl.pallas_call(
        functools.partial(_gla_kernel, nb=nb),
        out_shape=(jax.ShapeDtypeStruct((b, t, GLA_DV_TOTAL), F32),
                   jax.ShapeDtypeStruct((b, GLA_HEADS, GLA_DK, GLA_DV), F32)),
        grid=(b // nb, t // t_blk),
        in_specs=[pl.BlockSpec((nb, t_blk, GLA_PROJ_PAD), lambda i, j: (i, j, 0)),
                  state_spec,
                  full((LANES, GLA_DK_TOTAL)), full((1, GLA_DK_TOTAL)), full((1, GLA_DV))],
        out_specs=(pl.BlockSpec((nb, t_blk, GLA_DV_TOTAL), lambda i, j: (i, j, 0)), state_spec),
        scratch_shapes=[pltpu.VMEM((nb, GLA_HEADS, GLA_DV, GLA_DK), F32)],
        compiler_params=pltpu.CompilerParams(dimension_semantics=("parallel", "arbitrary"),
                                             vmem_limit_bytes=VMEM_LIMIT),
        name="gla_scan",
    )(p_gla, s0, alpha_up_pad, alpha_b.reshape(1, -1), norm_g.reshape(1, -1))


def _merge_body(yr_ref, yg_ref, gate_ref, x_ref, mod_ref, wbr_ref, wbg_ref, wout_ref, g2_ref,
                router_ref, x1_ref, h2_ref, route_ref):
    mod = mod_ref[0]
    gates = gate_ref[0]
    merged = (gates[:, :D_MODEL] * _bdot(yr_ref[0], wbr_ref[...])
              + gates[:, D_MODEL:] * _bdot(yg_ref[0], wbg_ref[...]))
    x1 = x_ref[0] + mod[2] * _bdot(merged, wout_ref[...])
    x1_ref[0] = x1
    h2 = _rms_scale(x1) * g2_ref[...] * (1.0 + mod[4]) + mod[3]
    h2_ref[:, :D_MODEL] = h2

    tm = h2.shape[0]
    h_hi, h_lo, _ = _split3(h2)
    prod = jnp.dot(jnp.concatenate([h_hi, h_lo], axis=0), router_ref[...], preferred_element_type=F32)
    logits = prod[:tm, :LANES] + prod[tm:, :LANES] + prod[:tm, LANES:]
    lane = lax.broadcasted_iota(jnp.int32, (tm, LANES), 1)
    neg = -jnp.inf
    is_group = (lane >= N_EXPERTS) & (lane < N_EXPERTS + N_GROUPS)
    gl = jnp.where(is_group, logits, neg)
    gmax = jnp.max(gl, axis=1, keepdims=True)
    g_idx = jnp.min(jnp.where(gl == gmax, lane, LANES), axis=1, keepdims=True) - N_EXPERTS
    p_g = 1.0 / jnp.sum(jnp.exp(gl - gmax), axis=1, keepdims=True)
    in_group = (lane >= g_idx * EXPERTS_PER_GROUP) & (lane < (g_idx + 1) * EXPERTS_PER_GROUP)
    el = jnp.where(in_group, logits, neg)
    v1 = jnp.max(el, axis=1, keepdims=True)
    i1 = jnp.min(jnp.where(el == v1, lane, LANES), axis=1, keepdims=True)
    el2 = jnp.where(lane == i1, neg, el)
    v2 = jnp.max(el2, axis=1, keepdims=True)
    i2 = jnp.min(jnp.where(el2 == v2, lane, LANES), axis=1, keepdims=True)
    e21 = jnp.exp(v2 - v1)
    w1 = p_g / (1.0 + e21)
    route = (jnp.where(lane == 0, i1.astype(F32), 0.0) + jnp.where(lane == 1, i2.astype(F32), 0.0)
             + jnp.where(lane == 2, w1, 0.0) + jnp.where(lane == 3, w1 * e21, 0.0))
    route_ref[...] = route
    h2_ref[:, D_MODEL:] = route


def _merge_first_kernel(*refs, n_own):
    step = pl.program_id(0)

    @pl.when(step < n_own)
    def _():
        _merge_body(*refs)

    @pl.when(step >= n_own)
    def _():
        refs[-2][...] = jnp.zeros(refs[-2].shape, F32)
        refs[-1][...] = jnp.zeros(refs[-1].shape, F32)


def _merge_second_kernel(*refs):
    n_in = 10
    _merge_body(*refs[:n_in], *refs[n_in + 2:])


def _merge(y_r, y_g, gates, x, mod, wbr, wbg, wout, norm2_g, router_cat, *, tm, n_tok, row_off, shared=None):
    b, t, _ = x.shape
    per_seq = t // tm
    n_own = b * per_seq
    first = shared is None
    n_steps = n_tok // tm if first else n_own
    assert row_off % tm == 0 and n_tok % tm == 0 and (row_off == 0 or not first)
    blk0 = row_off // tm
    own = lambda s: jnp.minimum(s, n_own - 1)
    full = lambda shape: pl.BlockSpec(shape, lambda s: (0,) * len(shape))
    tile = lambda n: pl.BlockSpec((1, tm, n), lambda s: (own(s) // per_seq, own(s) % per_seq, 0))
    flat = lambda n: pl.BlockSpec((tm, n), lambda s: (blk0 + s, 0))
    in_specs = [tile(D_MODEL), tile(D_MODEL), tile(GATE_PROJ), tile(D_MODEL),
                _mod_spec(mod, tm, lambda s: (own(s) // per_seq, own(s) % per_seq)),
                full((D_MODEL, D_MODEL)), full((D_MODEL, D_MODEL)), full((D_MODEL, D_MODEL)),
                full((1, D_MODEL)), full((D_MODEL, 2 * LANES))]
    args = [y_r, y_g, gates, x, mod, wbr, wbg, wout, norm2_g.reshape(1, -1), router_cat]
    if not first:
        in_specs += [pl.BlockSpec(memory_space=pl.ANY)] * 2
        args += list(shared)
    return pl.pallas_call(
        functools.partial(_merge_first_kernel, n_own=n_own) if first else _merge_second_kernel,
        out_shape=(jax.ShapeDtypeStruct((b, t, D_MODEL), F32),
                   jax.ShapeDtypeStruct((n_tok, MOE_ROW), F32),
                   jax.ShapeDtypeStruct((n_tok, LANES), F32)),
        grid=(n_steps,),
        in_specs=in_specs,
        out_specs=(tile(D_MODEL), flat(MOE_ROW), flat(LANES)),
        input_output_aliases={} if first else {10: 1, 11: 2},
        compiler_params=pltpu.CompilerParams(dimension_semantics=("arbitrary",),
                                             vmem_limit_bytes=VMEM_LIMIT),
        name="merge_router",
    )(*args)


def _moe_plan(expert_ids, n_tok):
    tm = MOE_TILE
    n_tiles = -(-n_tok // tm) + N_GROUPS * (EXPERTS_PER_GROUP * (EXPERTS_PER_GROUP - 1) // 2)
    n_slots = n_tiles * tm
    n_fill = n_slots - n_tok
    row_bits = 15
    assert max(n_tok, n_fill) <= 1 << row_bits and N_PAIR_CLASSES << (row_bits + 1) <= 1 << 24
    e_lo = jnp.minimum(expert_ids[:, 0], expert_ids[:, 1])
    e_hi = jnp.maximum(expert_ids[:, 0], expert_ids[:, 1])
    cls = e_lo * EXPERTS_PER_GROUP + e_hi % EXPERTS_PER_GROUP
    c_iota = jnp.arange(N_PAIR_CLASSES, dtype=jnp.int32)
    counts = jnp.sum((cls[None, :] == c_iota[:, None]).astype(jnp.int32), axis=1)
    tiles_c = (counts + tm - 1) // tm
    n_used = jnp.sum(tiles_c)
    pad_end = jnp.cumsum(tiles_c * tm - counts)
    f_iota = jnp.arange(n_fill, dtype=jnp.int32)
    f_cls = jnp.minimum(jnp.sum((f_iota[:, None] >= pad_end[None, :]).astype(jnp.int32), axis=1), N_PAIR_CLASSES - 1)
    keys = jnp.concatenate([(cls << (row_bits + 1)) + jnp.arange(n_tok, dtype=jnp.int32),
                            (f_cls << (row_bits + 1)) + (1 << row_bits) + f_iota])
    neg_sorted, _ = lax.top_k(-keys.astype(F32), n_slots)
    slots = (-neg_sorted).astype(jnp.int32).reshape(n_tiles, tm)
    valid = (slots & (1 << row_bits)) == 0
    tok = jnp.where(valid, slots & ((1 << row_bits) - 1), 0)
    t_idx = jnp.arange(n_tiles, dtype=jnp.int32)
    j_idx = jnp.arange(tm, dtype=jnp.int32)[None, :]
    dst = jnp.where(valid, tok, n_tok + (t_idx[:, None] % 2) * tm + j_idx)
    tile_c = slots[:, 0] >> (row_bits + 1)
    last_c = lax.dynamic_slice(tile_c, (jnp.maximum(n_used - 1, 0),), (1,))
    tile_c = jnp.where(t_idx < n_used, tile_c, last_c)
    tile_lo = tile_c // EXPERTS_PER_GROUP
    tile_hi = (tile_c // (EXPERTS_PER_GROUP * EXPERTS_PER_GROUP)) * EXPERTS_PER_GROUP + tile_c % EXPERTS_PER_GROUP
    return (tile_lo, tile_hi, n_used.reshape(1).astype(jnp.int32),
            tok.reshape(n_tiles, 1, tm), dst.reshape(n_tiles, 1, tm))


def _moe_pair_kernel(lo_ref, hi_ref, n_used_ref, tok_ref, tok_next_ref, dst_ref, h2_hbm,
                     wg_lo_ref, wu_lo_ref, wd_lo_ref, wg_hi_ref, wu_hi_ref, wd_hi_ref, out_hbm,
                     xbuf, obuf, w_bf, gsem, ssem):
    t = pl.program_id(0)
    n_t = pl.num_programs(0)
    tm = MOE_TILE
    slot = lax.rem(t, 2)
    n_used = n_used_ref[0]

    def gather(idx_ref, s):
        for j in range(tm):
            pltpu.make_async_copy(h2_hbm.at[pl.ds(idx_ref[0, 0, j], 1)], xbuf.at[s, pl.ds(j, 1)],
                                  gsem.at[s]).start()

    def wait_gather(s):
        pltpu.make_async_copy(h2_hbm.at[pl.ds(0, tm)], xbuf.at[s], gsem.at[s]).wait()

    def wait_scatter(s):
        pltpu.make_async_copy(obuf.at[s], out_hbm.at[pl.ds(0, tm)], ssem.at[s]).wait()

    @pl.when(t == 0)
    def _():
        n_rows = out_hbm.shape[0]
        obuf[1] = jnp.zeros((tm, D_MODEL), F32)
        for half in range(2):
            fill = pltpu.make_async_copy(obuf.at[1], out_hbm.at[pl.ds(n_rows - (2 - half) * tm, tm)],
                                         ssem.at[1])
            fill.start()
            fill.wait()

    @pl.when((t == 0) & (n_used > 0))
    def _():
        gather(tok_ref, 0)

    @pl.when((t >= 2) & (t - 2 < n_used))
    def _():
        wait_scatter(slot)

    @pl.when(t < n_used)
    def _():
        t_prev = jnp.maximum(t - 1, 0)
        for which, (e_ref, refs) in enumerate(((lo_ref, (wg_lo_ref, wu_lo_ref, wd_lo_ref)),
                                               (hi_ref, (wg_hi_ref, wu_hi_ref, wd_hi_ref)))):
            @pl.when((t == 0) | (e_ref[t] != e_ref[t_prev]))
            def _():
                for m, ref in enumerate(refs):
                    w_bf[which][m][...] = ref[0].astype(BF16)

        wait_gather(slot)
        gather(tok_next_ref, 1 - slot)
        xe = xbuf[slot]
        x = xe[:, :D_MODEL].astype(BF16)
        route = xe[:, D_MODEL:]
        lane = lax.broadcasted_iota(jnp.int32, route.shape, 1)
        pick = lambda k: jnp.sum(jnp.where(lane == k, route, 0.0), axis=1, keepdims=True)
        i1, i2, w1, w2 = pick(0), pick(1), pick(2), pick(3)
        w_lo = jnp.where(i1 < i2, w1, w2)
        w_hi = jnp.where(i1 < i2, w2, w1)
        acc = None
        for which, wt in enumerate((w_lo, w_hi)):
            wg, wu, wd = w_bf[which]
            hid = _silu(jnp.dot(x, wg[...], preferred_element_type=F32)) * jnp.dot(
                x, wu[...], preferred_element_type=F32)
            part = wt * _bdot(hid, wd[...])
            acc = part if acc is None else acc + part
        obuf[slot] = acc

        for j in range(tm):
            pltpu.make_async_copy(obuf.at[slot, pl.ds(j, 1)], out_hbm.at[pl.ds(dst_ref[0, 0, j], 1)],
                                  ssem.at[slot]).start()

    @pl.when((t < n_used) & (t + 1 >= n_used))
    def _():
        wait_gather(1 - slot)

    @pl.when(t == n_t - 1)
    def _():
        @pl.when((t >= 1) & (t - 1 < n_used))
        def _():
            wait_scatter(1 - slot)

        @pl.when(t < n_used)
        def _():
            wait_scatter(slot)


def _moe_grouped(h2_ext, expert_ids, wg, wu, wd):
    return _moe_call(h2_ext, *_moe_plan(expert_ids, h2_ext.shape[0]), wg, wu, wd)


def _moe_call(h2_ext, tile_lo, tile_hi, n_used, tok, dst, wg, wu, wd):
    n_tok = h2_ext.shape[0]
    tm = MOE_TILE
    n_tiles = tile_lo.shape[0]
    smem_tile = lambda f: pl.BlockSpec((1, 1, tm), f, memory_space=pltpu.SMEM)
    w_lo = lambda shape: pl.BlockSpec(shape, lambda t, lo, hi, nu: (lo[t], 0, 0))
    w_hi = lambda shape: pl.BlockSpec(shape, lambda t, lo, hi, nu: (hi[t], 0, 0))
    up, down = (1, D_MODEL, EXPERT_FF), (1, EXPERT_FF, D_MODEL)
    bf_set = [pltpu.VMEM(up[1:], BF16), pltpu.VMEM(up[1:], BF16), pltpu.VMEM(down[1:], BF16)]
    grid_spec = pltpu.PrefetchScalarGridSpec(
        num_scalar_prefetch=3,
        grid=(n_tiles,),
        in_specs=[smem_tile(lambda t, *_: (t, 0, 0)),
                  smem_tile(lambda t, *_: (jnp.minimum(t + 1, n_tiles - 1), 0, 0)),
                  smem_tile(lambda t, *_: (t, 0, 0)),
                  pl.BlockSpec(memory_space=pl.ANY),
                  w_lo(up), w_lo(up), w_lo(down), w_hi(up), w_hi(up), w_hi(down)],
        out_specs=pl.BlockSpec(memory_space=pl.ANY),
        scratch_shapes=[pltpu.VMEM((2, tm, MOE_ROW), F32), pltpu.VMEM((2, tm, D_MODEL), F32),
                        [bf_set, bf_set],
                        pltpu.SemaphoreType.DMA((2,)), pltpu.SemaphoreType.DMA((2,))],
    )
    return pl.pallas_call(
        _moe_pair_kernel,
        out_shape=jax.ShapeDtypeStruct((n_tok + 2 * tm, D_MODEL), F32),
        grid_spec=grid_spec,
        compiler_params=pltpu.CompilerParams(dimension_semantics=("arbitrary",),
                                             vmem_limit_bytes=VMEM_LIMIT),
        name="moe_pairs",
    )(tile_lo, tile_hi, n_used, tok, tok, dst, h2_ext, wg, wu, wd, wg, wu, wd)


def _final_kernel(x1_ref, moe_ref, mod_ref, fg_ref, y_ref):
    x2 = x1_ref[0] + mod_ref[0][5] * moe_ref[...]
    y_ref[0] = _rms_scale(x2) * fg_ref[...]


def _final(x1, moe_out, mod, final_g, *, row_off, tm):
    b, t, _ = x1.shape
    blk0 = row_off // tm
    per_seq = t // tm
    return pl.pallas_call(
        _final_kernel,
        out_shape=jax.ShapeDtypeStruct((b, t, D_MODEL), F32),
        grid=(b, per_seq),
        in_specs=[pl.BlockSpec((1, tm, D_MODEL), lambda i, j: (i, j, 0)),
                  pl.BlockSpec((tm, D_MODEL), lambda i, j: (blk0 + i * per_seq + j, 0)),
                  _mod_spec(mod, tm, lambda i, j: (i, j)),
                  pl.BlockSpec((1, D_MODEL), lambda i, j: (0, 0))],
        out_specs=pl.BlockSpec((1, tm, D_MODEL), lambda i, j: (i, j, 0)),
        compiler_params=pltpu.CompilerParams(dimension_semantics=("parallel", "parallel"),
                                             vmem_limit_bytes=VMEM_LIMIT),
        name="moe_combine_norm",
    )(x1, moe_out, mod, final_g.reshape(1, -1))


def _mixer(x, mod, shift0, s_rwkv0, s_gla0, wts, *, n_tok, row_off, shared=None):
    b, t, _ = x.shape
    per_token = mod.shape[2] > 1
    rows = x.reshape(1, b * t, D_MODEL) if per_token else x
    tm = min(rows.shape[1], ROW_TILE)
    p_rwkv = _norm_proj(rows, mod, wts["norm1_g"], wts["w_in_rwkv"], gate=False, tm=tm)
    p_gla = _norm_proj(rows, mod, wts["norm1_g"], wts["w_in_gla"], gate=False, tm=tm)
    gates = _norm_proj(rows, mod, wts["norm1_g"], wts["w_in_gate"], gate=True, tm=tm)
    p_rwkv = p_rwkv.reshape(b, t, RWKV_PROJ)

    y_r, s_rwkv = _rwkv(p_rwkv, shift0, s_rwkv0,
                        wts["rwkv_mu"], wts["rwkv_w0"], wts["rwkv_decay_up"], wts["rwkv_a0"],
                        wts["rwkv_aaa_up"], wts["rwkv_gate_up"], wts["rwkv_k_k"], wts["rwkv_k_a"],
                        wts["rwkv_r_k"], wts["rwkv_ln_g"], wts["rwkv_ln_b"], nb=RWKV_SEQS_PER_STEP)
    y_g, s_gla = _gla(p_gla.reshape(b, t, GLA_PROJ_PAD), s_gla0, wts["gla_alpha_up"],
                      wts["gla_alpha_b"], wts["gla_norm_g"], nb=GLA_SEQS_PER_STEP)

    x1, h2_all, route_all = _merge(
        y_r.reshape(rows.shape), y_g.reshape(rows.shape), gates, rows, mod, wts["w_branch_rwkv"],
        wts["w_branch_gla"], wts["w_out"], wts["norm2_g"], wts["router_cat"],
        tm=tm, n_tok=n_tok, row_off=row_off, shared=shared)
    states = (p_rwkv[:, t - 1:t, :][None], s_rwkv[None], s_gla[None])
    return x1, h2_all, route_all, states


def kernel(x_prompt, x_sample, c_prompt, c_sample, state_rwkv_shift, state_rwkv, state_gla, w_ada, b_ada, norm1_g, w_in, rwkv_mu, rwkv_w0, rwkv_decay_up, rwkv_a0, rwkv_aaa_up, rwkv_gate_up, rwkv_k_k, rwkv_k_a, rwkv_r_k, rwkv_ln_g, rwkv_ln_b, gla_alpha_up, gla_alpha_b, gla_norm_g, w_branch_rwkv, w_branch_gla, w_out, norm2_g, router_group, router_expert, expert_w_gate, expert_w_up, expert_w_down, final_norm_g):
    bp, tp = x_prompt.shape[:2]
    bs, ts = x_sample.shape[:2]
    w_in0 = w_in[0]
    g0 = RWKV_PROJ
    o3 = 2 * GLA_DK_TOTAL + GLA_DV_TOTAL
    o4 = o3 + GLA_GATE_RANK
    w_gla = w_in0[:, g0:g0 + GLA_PROJ]
    w_gla = jnp.concatenate([w_gla[:, :o3], w_gla[:, o4:], w_gla[:, o3:o4],
                             jnp.zeros((D_MODEL, GLA_PROJ_PAD - GLA_PROJ), F32)], axis=1)
    wts = dict(
        norm1_g=norm1_g[0].reshape(1, -1),
        w_in_rwkv=w_in0[:, :g0].astype(BF16),
        w_in_gla=w_gla.astype(BF16),
        w_in_gate=w_in0[:, g0 + GLA_PROJ:].astype(BF16),
        rwkv_mu=rwkv_mu[0], rwkv_w0=rwkv_w0[0], rwkv_decay_up=rwkv_decay_up[0], rwkv_a0=rwkv_a0[0],
        rwkv_aaa_up=rwkv_aaa_up[0], rwkv_gate_up=rwkv_gate_up[0], rwkv_k_k=rwkv_k_k[0],
        rwkv_k_a=rwkv_k_a[0], rwkv_r_k=rwkv_r_k[0], rwkv_ln_g=rwkv_ln_g[0], rwkv_ln_b=rwkv_ln_b[0],
        gla_alpha_up=jnp.pad(gla_alpha_up[0], ((0, LANES - GLA_GATE_RANK), (0, 0))),
        gla_alpha_b=gla_alpha_b[0], gla_norm_g=gla_norm_g[0],
        w_branch_rwkv=w_branch_rwkv[0].astype(BF16), w_branch_gla=w_branch_gla[0].astype(BF16),
        w_out=w_out[0].astype(BF16), norm2_g=norm2_g[0],
        router_cat=_split_hi_lo(jnp.pad(jnp.concatenate([router_expert[0], router_group[0]], axis=1),
                                        ((0, 0), (0, LANES - N_EXPERTS - N_GROUPS)))),
        expert_w_gate=expert_w_gate[0], expert_w_up=expert_w_up[0], expert_w_down=expert_w_down[0],
    )
    mod = _ada(jnp.concatenate([c_prompt, c_sample], axis=0), w_ada[0], b_ada[0])
    mod = mod.reshape(bp + bs, N_MOD, D_MODEL)
    n_p = bp * tp
    n_s = bs * ts
    n_tok = n_p + n_s
    mod_p = mod[:bp].reshape(bp, N_MOD, 1, D_MODEL)
    mod_s = jnp.repeat(mod[bp:].transpose(1, 0, 2), ts, axis=1)[None]

    dt = x_prompt.dtype
    x1_p, h2_all, route_all, states_p = _mixer(
        x_prompt, mod_p, jnp.zeros((bp, 1, RWKV_PROJ), dt),
        jnp.zeros((bp, RWKV_HEADS, RWKV_HEAD_DIM, RWKV_HEAD_DIM), state_rwkv.dtype),
        jnp.zeros((bp, GLA_HEADS, GLA_DK, GLA_DV), state_gla.dtype), wts, n_tok=n_tok, row_off=0)
    x1_s, h2_all, route_all, states_s = _mixer(
        x_sample, mod_s, state_rwkv_shift[0], state_rwkv[0], state_gla[0], wts,
        n_tok=n_tok, row_off=n_p, shared=(h2_all, route_all))

    expert_ids = route_all[:, :2].astype(jnp.int32)
    moe_out = _moe_grouped(h2_all, expert_ids, wts["expert_w_gate"], wts["expert_w_up"],
                           wts["expert_w_down"])
    y_p = _final(x1_p, moe_out, mod_p, final_norm_g, row_off=0, tm=min(tp, ROW_TILE))
    y_s = _final(x1_s, moe_out, mod_s, final_norm_g, row_off=n_p, tm=min(n_s, ROW_TILE))
    return (y_p, y_s.reshape(bs, ts, D_MODEL)) + states_p + states_s
```

```python
import functools
import math

import jax
import jax.numpy as jnp
from jax import lax
from jax.experimental import pallas as pl
from jax.experimental.pallas import tpu as pltpu

F32 = jnp.float32
BF16 = jnp.bfloat16

D_MODEL = 1024
N_MOD = 6
EPS = 1e-6
RWKV_HEAD_DIM = 64
RWKV_HEADS = 16
RWKV_PAIRS = RWKV_HEADS // 2
DECAY_LORA = 64
AAA_LORA = 64
GATE_LORA = 128
RWKV_PROJ = 3 * D_MODEL + DECAY_LORA + AAA_LORA + GATE_LORA
LN_X_EPS = 64e-5
GLA_HEADS = 4
GLA_DK = 128
GLA_DV = 256
GLA_DK_TOTAL = GLA_HEADS * GLA_DK
GLA_DV_TOTAL = GLA_HEADS * GLA_DV
GLA_GATE_RANK = 16
GLA_GATE_NORM = 16.0
GLA_PROJ = 2 * GLA_DK_TOTAL + 2 * GLA_DV_TOTAL + GLA_GATE_RANK
LANES = 128
GLA_PROJ_PAD = 2 * GLA_DK_TOTAL + 2 * GLA_DV_TOTAL + LANES
GATE_PROJ = 2 * D_MODEL
N_GROUPS = 4
EXPERTS_PER_GROUP = 8
N_EXPERTS = 32
EXPERT_FF = 512

CHUNK = 64
ROW_TILE = 512
MOE_TILE = 192
DMA_GROUP = 8
MOE_ROW = D_MODEL + LANES
N_PAIR_CLASSES = N_EXPERTS * EXPERTS_PER_GROUP
RWKV_SEQS_PER_STEP = 4
GLA_SEQS_PER_STEP = 8
VMEM_LIMIT = 56 * 1024 * 1024


def _bdot(a, b):
    return jnp.dot(a.astype(BF16), b.astype(BF16), preferred_element_type=F32)


def _bdot_nt(a, b):
    return lax.dot_general(a.astype(BF16), b.astype(BF16), (((1,), (1,)), ((), ())),
                           preferred_element_type=F32)


def _bdot_tn(a, b):
    return jnp.dot(a.T.astype(BF16), b.astype(BF16), preferred_element_type=F32)


def _split3(x):
    h1 = x.astype(BF16)
    r1 = x - h1.astype(F32)
    h2 = r1.astype(BF16)
    h3 = (r1 - h2.astype(F32)).astype(BF16)
    return h1, h2, h3


def _dot3(a, b):
    a1, a2, _ = _split3(a)
    b1, b2, _ = _split3(b)
    return (jnp.dot(a1, b1, preferred_element_type=F32)
            + jnp.dot(a2, b1, preferred_element_type=F32)
            + jnp.dot(a1, b2, preferred_element_type=F32))


def _dot_exact_lhs(a_bf16, x):
    x1, x2, _ = _split3(x)
    return (jnp.dot(a_bf16, x1, preferred_element_type=F32)
            + jnp.dot(a_bf16, x2, preferred_element_type=F32))


def _split_hi_lo(w):
    hi, lo, _ = _split3(w)
    return jnp.concatenate([hi, lo], axis=1)


def _sigmoid(x):
    return 1.0 / (1.0 + jnp.exp(-x))


def _silu(x):
    return x * _sigmoid(x)


def _rms_scale(x):
    return x * lax.rsqrt(jnp.mean(x * x, axis=-1, keepdims=True) + EPS)


def _ada_kernel(c_ref, w_ref, b_ref, o_ref):
    o_ref[...] = _dot3(_silu(c_ref[...]), w_ref[...]) + b_ref[...]


def _ada(c, w_ada, b_ada):
    n_rows = c.shape[0]
    n_out = w_ada.shape[1]
    tn = 1536
    return pl.pallas_call(
        _ada_kernel,
        out_shape=jax.ShapeDtypeStruct((n_rows, n_out), F32),
        grid=(n_out // tn,),
        in_specs=[pl.BlockSpec((n_rows, D_MODEL), lambda j: (0, 0)),
                  pl.BlockSpec((D_MODEL, tn), lambda j: (0, j)),
                  pl.BlockSpec((1, tn), lambda j: (0, j))],
        out_specs=pl.BlockSpec((n_rows, tn), lambda j: (0, j)),
        compiler_params=pltpu.CompilerParams(dimension_semantics=("parallel",),
                                             vmem_limit_bytes=VMEM_LIMIT),
        name="ada",
    )(c, w_ada, b_ada.reshape(1, n_out))


def _mod_spec(mod, tm, seq_tile):
    if mod.shape[2] == 1:
        return pl.BlockSpec((1, N_MOD, 1, D_MODEL), lambda *g: (seq_tile(*g)[0], 0, 0, 0))
    return pl.BlockSpec((1, N_MOD, tm, D_MODEL), lambda *g: (seq_tile(*g)[0], 0, seq_tile(*g)[1], 0))


def _norm_proj_kernel(x_ref, mod_ref, g_ref, w_ref, o_ref, *, gate):
    mod = mod_ref[0]
    h = _rms_scale(x_ref[0]) * g_ref[...] * (1.0 + mod[1]) + mod[0]
    out = _bdot(h, w_ref[...])
    o_ref[0] = _sigmoid(out) if gate else out


def _norm_proj(x, mod, g, w_bf16, *, gate, tm):
    b, t, _ = x.shape
    n = w_bf16.shape[1]
    return pl.pallas_call(
        functools.partial(_norm_proj_kernel, gate=gate),
        out_shape=jax.ShapeDtypeStruct((b, t, n), F32),
        grid=(b, t // tm),
        in_specs=[pl.BlockSpec((1, tm, D_MODEL), lambda i, j: (i, j, 0)),
                  _mod_spec(mod, tm, lambda i, j: (i, j)),
                  pl.BlockSpec((1, D_MODEL), lambda i, j: (0, 0)),
                  pl.BlockSpec((D_MODEL, n), lambda i, j: (0, 0))],
        out_specs=pl.BlockSpec((1, tm, n), lambda i, j: (i, j, 0)),
        compiler_params=pltpu.CompilerParams(dimension_semantics=("parallel", "parallel"),
                                             vmem_limit_bytes=VMEM_LIMIT),
        name="norm_proj_gate" if gate else "norm_proj",
    )(x, mod, g, w_bf16)


def _blockdiag(x, lo_mask):
    return jnp.concatenate([jnp.where(lo_mask, x, 0.0), jnp.where(lo_mask, 0.0, x)], axis=0)


def _rwkv_kernel(p_ref, shift_ref, s0_ref, mu_ref, w0_ref, dup_ref, a0_ref, aup_ref, gup_ref,
                 kk_ref, ka_ref, rk_ref, lng_ref, lnb_ref, y_ref, sout_ref, s_scr, carry_scr,
                 *, nb):
    c = pl.program_id(1)
    n_chunks = pl.num_programs(1)
    C = CHUNK
    N = RWKV_HEAD_DIM
    t_blk = p_ref.shape[1]

    @pl.when(c == 0)
    def _():
        for bl in range(nb):
            for pr in range(RWKV_PAIRS):
                s_scr[bl, pr] = jnp.concatenate([s0_ref[bl, 2 * pr], s0_ref[bl, 2 * pr + 1]], axis=1)
        carry_scr[...] = shift_ref[...]

    row = lax.broadcasted_iota(jnp.int32, (C, LANES), 0)
    lane = lax.broadcasted_iota(jnp.int32, (C, LANES), 1)
    lane_t = jnp.where(lane < N, lane, lane - N)
    lo = lane < N
    strict = lane_t < row
    incl = lane_t <= row
    eye = (lane_t == row).astype(F32)
    valid = row < t_blk
    tri = (lax.broadcasted_iota(jnp.int32, (C, C), 1)
           <= lax.broadcasted_iota(jnp.int32, (C, C), 0)).astype(BF16)
    sq_r = lax.broadcasted_iota(jnp.int32, (LANES, LANES), 0)
    sq_c = lax.broadcasted_iota(jnp.int32, (LANES, LANES), 1)
    same_head = (sq_r < N) == (sq_c < N)
    diag = sq_r == sq_c
    row_w = lax.broadcasted_iota(jnp.int32, (C, RWKV_PROJ), 0)
    o3 = 3 * D_MODEL

    def hsum(v):
        s_lo = jnp.sum(jnp.where(lo, v, 0.0), axis=1, keepdims=True)
        s_hi = jnp.sum(jnp.where(lo, 0.0, v), axis=1, keepdims=True)
        return jnp.where(lo, s_lo, s_hi)

    def bd(v):
        return _blockdiag(v, lo)

    def cat0(*a):
        return jnp.concatenate(a, axis=0)

    def cat1(*a):
        return jnp.concatenate(a, axis=1)

    valid_w = lax.broadcasted_iota(jnp.int32, (C, D_MODEL), 0) < t_blk
    xs, logw_all, cum_all, asig_all, g_all = [], [], [], [], []
    for bl in range(nb):
        p = p_ref[bl]
        carry_new = p[t_blk - 1:t_blk, :]
        if t_blk < C:
            p = jnp.concatenate([p, jnp.zeros((C - t_blk, RWKV_PROJ), F32)], axis=0)
        prev = jnp.where(row_w == 0, carry_scr[bl], pltpu.roll(p, 1, axis=0))
        carry_scr[bl] = carry_new
        x = p + (prev - p) * mu_ref[...]
        xw = x[:, o3:o3 + DECAY_LORA]
        xa = x[:, o3 + DECAY_LORA:o3 + DECAY_LORA + AAA_LORA]
        xg = x[:, o3 + DECAY_LORA + AAA_LORA:]
        z = w0_ref[...] + _bdot(jnp.tanh(xw), dup_ref[...])
        logw_full = -math.exp(-0.5) * _sigmoid(z)
        if t_blk < C:
            logw_full = jnp.where(valid_w, logw_full, 0.0)
        xs.append(x)
        logw_all.append(logw_full)
        cum_all.append(_dot_exact_lhs(tri, logw_full))
        asig_all.append(_sigmoid(a0_ref[...] + _bdot(xa, aup_ref[...])))
        g_all.append(_bdot(_sigmoid(xg), gup_ref[...]))

    units = [(bl, pr) for bl in range(nb) for pr in range(RWKV_PAIRS)]
    U = range(len(units))
    cols = [slice(pr * LANES, (pr + 1) * LANES) for _, pr in units]
    r = [xs[bl][:, pr * LANES:(pr + 1) * LANES] for bl, pr in units]
    k = [xs[bl][:, D_MODEL + pr * LANES:D_MODEL + (pr + 1) * LANES] for bl, pr in units]
    v = [xs[bl][:, 2 * D_MODEL + pr * LANES:2 * D_MODEL + (pr + 1) * LANES] for bl, pr in units]
    logw = [logw_all[bl][:, cols[u]] for u, (bl, _) in enumerate(units)]
    asig = [asig_all[bl][:, cols[u]] for u, (bl, _) in enumerate(units)]
    kk = [k[u] * kk_ref[:, cols[u]] for u in U]
    k2 = [k[u] * (1.0 + (asig[u] - 1.0) * ka_ref[:, cols[u]]) for u in U]
    ss = [hsum(kk[u] * kk[u]) for u in U]
    cum = [cum_all[bl][:, cols[u]] for u, (bl, _) in enumerate(units)]
    bsum = [hsum(r[u] * k2[u] * rk_ref[:, cols[u]]) for u in U]
    kk = [kk[u] * lax.rsqrt(jnp.maximum(ss[u], 1e-24)) for u in U]
    bonus = [bsum[u] * v[u] for u in U]
    if t_blk < C:
        kk = [jnp.where(valid, a, 0.0) for a in kk]
        k2 = [jnp.where(valid, a, 0.0) for a in k2]
        v = [jnp.where(valid, a, 0.0) for a in v]
        r = [jnp.where(valid, a, 0.0) for a in r]
    b_vec = [kk[u] * asig[u] for u in U]
    clast = [cum[u][C - 1:C, :] for u in U]
    e_inv = [jnp.exp(-cum[u]) for u in U]
    e_tail = [jnp.exp(clast[u] - cum[u]) for u in U]
    aq = [-kk[u] * jnp.exp(cum[u] - logw[u]) for u in U]
    rq = [r[u] * jnp.exp(cum[u]) for u in U]
    bk = [b_vec[u] * e_inv[u] for u in U]
    kx = [k2[u] * e_inv[u] for u in U]
    bb = [b_vec[u] * e_tail[u] for u in U]
    kb = [k2[u] * e_tail[u] for u in U]

    g4 = [_bdot_nt(cat0(aq[u], rq[u]), cat0(bd(bk[u]), bd(kx[u]))) for u in U]
    m_ab = [jnp.where(strict, g4[u][:C, :LANES], 0.0) for u in U]
    m_ak = [jnp.where(strict, g4[u][:C, LANES:], 0.0) for u in U]
    m_rb = [jnp.where(incl, g4[u][C:, :LANES], 0.0) for u in U]
    m_rk = [jnp.where(incl, g4[u][C:, LANES:], 0.0) for u in U]

    t_inv = [eye + m_ab[u] for u in U]
    l_pow = [_bdot(m_ab[u], bd(m_ab[u])) for u in U]
    mv = [_bdot(cat0(m_ak[u], m_rk[u]), bd(v[u])) for u in U]
    n_steps = C.bit_length() - 1
    for step in range(1, n_steps):
        if step < n_steps - 1:
            both = [_bdot(l_pow[u], cat1(bd(t_inv[u]), bd(l_pow[u]))) for u in U]
            t_inv = [t_inv[u] + both[u][:, :LANES] for u in U]
            l_pow = [both[u][:, LANES:] for u in U]
        else:
            t_inv = [t_inv[u] + _bdot(l_pow[u], bd(t_inv[u])) for u in U]

    au = [_bdot(t_inv[u], cat1(bd(aq[u]), bd(mv[u][:C]))) for u in U]
    a_bar = [au[u][:, :LANES] for u in U]
    u_bar = [au[u][:, LANES:] for u in U]
    mau = [_bdot(m_rb[u], cat1(bd(a_bar[u]), bd(u_bar[u]))) for u in U]
    p_full = [_bdot_tn(a_bar[u], bb[u]) for u in U]
    q_full = [_bdot_tn(cat0(u_bar[u], v[u]), cat0(bb[u], kb[u])) for u in U]
    r_til = [rq[u] + mau[u][:, :LANES] for u in U]
    y_loc = [mau[u][:, LANES:] + mv[u][C:] for u in U]
    p_bd = [jnp.where(same_head, p_full[u], 0.0) + jnp.where(diag, jnp.exp(clast[u]), 0.0) for u in U]
    q_pk = [jnp.where(lo[:N], q_full[u][:N], q_full[u][N:]) for u in U]

    s_old = [s_scr[bl, pr] for bl, pr in units]
    y = [_bdot_nt(r_til[u], _blockdiag(s_old[u], lo[:N])) + y_loc[u] for u in U]
    s_new = [_bdot(s_old[u], p_bd[u]) + q_pk[u] for u in U]
    for u, (bl, pr) in enumerate(units):
        s_scr[bl, pr] = s_new[u]

    mu_y = [hsum(y[u]) * (1.0 / N) for u in U]
    yc = [y[u] - mu_y[u] for u in U]
    var = [hsum(yc[u] * yc[u]) * (1.0 / N) for u in U]
    for u, (bl, pr) in enumerate(units):
        yn = yc[u] * lax.rsqrt(var[u] + LN_X_EPS) * lng_ref[:, cols[u]] + lnb_ref[:, cols[u]]
        y_ref[bl, :, cols[u]] = ((yn + bonus[u]) * g_all[bl][:, cols[u]])[:t_blk]

    @pl.when(c == n_chunks - 1)
    def _():
        for bl in range(nb):
            for pr in range(RWKV_PAIRS):
                sout_ref[bl, 2 * pr] = s_scr[bl, pr][:, :N]
                sout_ref[bl, 2 * pr + 1] = s_scr[bl, pr][:, N:]


def _rwkv(p_rwkv, shift_prev, s0, mu, w0, decay_up, a0, aaa_up, gate_up, k_k, k_a, r_k,
          ln_g, ln_b, *, nb):
    b, t, _ = p_rwkv.shape
    nb = math.gcd(b, nb)
    t_blk = min(t, CHUNK)
    row = lambda a: a.reshape(1, -1)
    full = lambda shape: pl.BlockSpec(shape, lambda i, j: (0,) * len(shape))
    state_spec = pl.BlockSpec((nb, RWKV_HEADS, RWKV_HEAD_DIM, RWKV_HEAD_DIM), lambda i, j: (i, 0, 0, 0))
    return pl.pallas_call(
        functools.partial(_rwkv_kernel, nb=nb),
        out_shape=(jax.ShapeDtypeStruct((b, t, D_MODEL), F32),
                   jax.ShapeDtypeStruct((b, RWKV_HEADS, RWKV_HEAD_DIM, RWKV_HEAD_DIM), F32)),
        grid=(b // nb, t // t_blk),
        in_specs=[pl.BlockSpec((nb, t_blk, RWKV_PROJ), lambda i, j: (i, j, 0)),
                  pl.BlockSpec((nb, 1, RWKV_PROJ), lambda i, j: (i, 0, 0)),
                  state_spec,
                  full((1, RWKV_PROJ)), full((1, D_MODEL)), full((DECAY_LORA, D_MODEL)),
                  full((1, D_MODEL)), full((AAA_LORA, D_MODEL)), full((GATE_LORA, D_MODEL)),
                  full((1, D_MODEL)), full((1, D_MODEL)), full((1, D_MODEL)),
                  full((1, D_MODEL)), full((1, D_MODEL))],
        out_specs=(pl.BlockSpec((nb, t_blk, D_MODEL), lambda i, j: (i, j, 0)), state_spec),
        scratch_shapes=[pltpu.VMEM((nb, RWKV_PAIRS, RWKV_HEAD_DIM, LANES), F32),
                        pltpu.VMEM((nb, 1, RWKV_PROJ), F32)],
        compiler_params=pltpu.CompilerParams(dimension_semantics=("parallel", "arbitrary"),
                                             vmem_limit_bytes=VMEM_LIMIT),
        name="rwkv_scan",
    )(p_rwkv, shift_prev, s0, row(mu), row(w0), decay_up, row(a0), aaa_up, gate_up,
      row(k_k), row(k_a), row(r_k), row(ln_g), row(ln_b))


def _gla_kernel(p_ref, s0_ref, aup_ref, ab_ref, ng_ref, y_ref, sout_ref, s_scr, *, nb):
    c = pl.program_id(1)
    n_chunks = pl.num_programs(1)
    C = CHUNK
    t_blk = p_ref.shape[1]
    units = [(bl, h) for bl in range(nb) for h in range(GLA_HEADS)]
    U = range(len(units))

    @pl.when(c == 0)
    def _():
        for bl, h in units:
            s_scr[bl, h] = s0_ref[bl, h].T

    o_k = GLA_DK_TOTAL
    o_v = 2 * GLA_DK_TOTAL
    o_r = o_v + GLA_DV_TOTAL
    o_a = o_r + GLA_DV_TOTAL
    tri = (lax.broadcasted_iota(jnp.int32, (C, C), 1)
           <= lax.broadcasted_iota(jnp.int32, (C, C), 0))
    tri_bf = tri.astype(BF16)
    valid_g = lax.broadcasted_iota(jnp.int32, (C, GLA_DK_TOTAL), 0) < t_blk
    valid_k = lax.broadcasted_iota(jnp.int32, (C, GLA_DK), 0) < t_blk
    valid_v = lax.broadcasted_iota(jnp.int32, (C, GLA_DV), 0) < t_blk

    ps, cums = [], []
    for bl in range(nb):
        p = p_ref[bl]
        if t_blk < C:
            p = jnp.concatenate([p, jnp.zeros((C - t_blk, GLA_PROJ_PAD), F32)], axis=0)
        logit = _dot3(p[:, o_a:], aup_ref[...]) + ab_ref[...]
        g = (jnp.minimum(logit, 0.0) - jnp.log(1.0 + jnp.exp(-jnp.abs(logit)))) * (1.0 / GLA_GATE_NORM)
        if t_blk < C:
            g = jnp.where(valid_g, g, 0.0)
        ps.append(p)
        cums.append(_dot_exact_lhs(tri_bf, g))

    q = [ps[bl][:, h * GLA_DK:(h + 1) * GLA_DK] * (GLA_DK ** -0.5) for bl, h in units]
    k = [ps[bl][:, o_k + h * GLA_DK:o_k + (h + 1) * GLA_DK] for bl, h in units]
    v = [ps[bl][:, o_v + h * GLA_DV:o_v + (h + 1) * GLA_DV] for bl, h in units]
    if t_blk < C:
        k = [jnp.where(valid_k, a, 0.0) for a in k]
        v = [jnp.where(valid_v, a, 0.0) for a in v]
    b = [cums[bl][:, h * GLA_DK:(h + 1) * GLA_DK] for bl, h in units]
    clast = [b[u][C - 1:C, :] for u in U]
    cmid = [b[u][C // 2 - 1:C // 2, :] for u in U]
    att = [_bdot_nt(q[u] * jnp.exp(b[u] - cmid[u]), k[u] * jnp.exp(cmid[u] - b[u])) for u in U]
    att = [jnp.where(tri, a, 0.0) for a in att]
    s_t = [s_scr[bl, h] for bl, h in units]
    o = [_bdot(att[u], v[u]) + _bdot_nt(q[u] * jnp.exp(b[u]), s_t[u]) for u in U]
    s_new = [s_t[u] * jnp.exp(clast[u]) + _bdot_tn(v[u], k[u] * jnp.exp(clast[u] - b[u])) for u in U]
    for u, (bl, h) in enumerate(units):
        s_scr[bl, h] = s_new[u]
        r = ps[bl][:, o_r + h * GLA_DV:o_r + (h + 1) * GLA_DV]
        on = o[u] * lax.rsqrt(jnp.mean(o[u] * o[u], axis=-1, keepdims=True) + EPS) * ng_ref[...]
        y_ref[bl, :, h * GLA_DV:(h + 1) * GLA_DV] = (on * _silu(r))[:t_blk]

    @pl.when(c == n_chunks - 1)
    def _():
        for bl, h in units:
            sout_ref[bl, h] = s_scr[bl, h].T


def _gla(p_gla, s0, alpha_up_pad, alpha_b, norm_g, *, nb):
    b, t, _ = p_gla.shape
    nb = math.gcd(b, nb)
    t_blk = min(t, CHUNK)
    full = lambda shape: pl.BlockSpec(shape, lambda i, j: (0,) * len(shape))
    state_spec = pl.BlockSpec((nb, GLA_HEADS, GLA_DK, GLA_DV), lambda i, j: (i, 0, 0, 0))
    return pl.pallas_call(
        functools.partial(_gla_kernel, nb=nb),
        out_shape=(jax.ShapeDtypeStruct((b, t, GLA_DV_TOTAL), F32),
                   jax.ShapeDtypeStruct((b, GLA_HEADS, GLA_DK, GLA_DV), F32)),
        grid=(b // nb, t // t_blk),
        in_specs=[pl.BlockSpec((nb, t_blk, GLA_PROJ_PAD), lambda i, j: (i, j, 0)),
                  state_spec,
                  full((LANES, GLA_DK_TOTAL)), full((1, GLA_DK_TOTAL)), full((1, GLA_DV))],
        out_specs=(pl.BlockSpec((nb, t_blk, GLA_DV_TOTAL), lambda i, j: (i, j, 0)), state_spec),
        scratch_shapes=[pltpu.VMEM((nb, GLA_HEADS, GLA_DV, GLA_DK), F32)],
        compiler_params=pltpu.CompilerParams(dimension_semantics=("parallel", "arbitrary"),
                                             vmem_limit_bytes=VMEM_LIMIT),
        name="gla_scan",
    )(p_gla, s0, alpha_up_pad, alpha_b.reshape(1, -1), norm_g.reshape(1, -1))


def _merge_body(yr_ref, yg_ref, gate_ref, x_ref, mod_ref, wbr_ref, wbg_ref, wout_ref, g2_ref,
                router_ref, x1_ref, h2_ref, route_ref):
    mod = mod_ref[0]
    gates = gate_ref[0]
    merged = (gates[:, :D_MODEL] * _bdot(yr_ref[0], wbr_ref[...])
              + gates[:, D_MODEL:] * _bdot(yg_ref[0], wbg_ref[...]))
    x1 = x_ref[0] + mod[2] * _bdot(merged, wout_ref[...])
    x1_ref[0] = x1
    h2 = _rms_scale(x1) * g2_ref[...] * (1.0 + mod[4]) + mod[3]
    h2_ref[:, :D_MODEL] = h2

    tm = h2.shape[0]
    h_hi, h_lo, _ = _split3(h2)
    prod = jnp.dot(jnp.concatenate([h_hi, h_lo], axis=0), router_ref[...], preferred_element_type=F32)
    logits = prod[:tm, :LANES] + prod[tm:, :LANES] + prod[:tm, LANES:]
    lane = lax.broadcasted_iota(jnp.int32, (tm, LANES), 1)
    neg = -jnp.inf
    is_group = (lane >= N_EXPERTS) & (lane < N_EXPERTS + N_GROUPS)
    gl = jnp.where(is_group, logits, neg)
    gmax = jnp.max(gl, axis=1, keepdims=True)
    g_idx = jnp.min(jnp.where(gl == gmax, lane, LANES), axis=1, keepdims=True) - N_EXPERTS
    p_g = 1.0 / jnp.sum(jnp.exp(gl - gmax), axis=1, keepdims=True)
    in_group = (lane >= g_idx * EXPERTS_PER_GROUP) & (lane < (g_idx + 1) * EXPERTS_PER_GROUP)
    el = jnp.where(in_group, logits, neg)
    v1 = jnp.max(el, axis=1, keepdims=True)
    i1 = jnp.min(jnp.where(el == v1, lane, LANES), axis=1, keepdims=True)
    el2 = jnp.where(lane == i1, neg, el)
    v2 = jnp.max(el2, axis=1, keepdims=True)
    i2 = jnp.min(jnp.where(el2 == v2, lane, LANES), axis=1, keepdims=True)
    e21 = jnp.exp(v2 - v1)
    w1 = p_g / (1.0 + e21)
    route = (jnp.where(lane == 0, i1.astype(F32), 0.0) + jnp.where(lane == 1, i2.astype(F32), 0.0)
             + jnp.where(lane == 2, w1, 0.0) + jnp.where(lane == 3, w1 * e21, 0.0))
    route_ref[...] = route
    h2_ref[:, D_MODEL:] = route


def _merge_first_kernel(*refs, n_own):
    step = pl.program_id(0)

    @pl.when(step < n_own)
    def _():
        _merge_body(*refs)

    @pl.when(step >= n_own)
    def _():
        refs[-2][...] = jnp.zeros(refs[-2].shape, F32)
        refs[-1][...] = jnp.zeros(refs[-1].shape, F32)


def _merge_second_kernel(*refs):
    n_in = 10
    _merge_body(*refs[:n_in], *refs[n_in + 2:])


def _merge(y_r, y_g, gates, x, mod, wbr, wbg, wout, norm2_g, router_cat, *, tm, n_tok, row_off, shared=None):
    b, t, _ = x.shape
    per_seq = t // tm
    n_own = b * per_seq
    first = shared is None
    n_steps = n_tok // tm if first else n_own
    assert row_off % tm == 0 and n_tok % tm == 0 and (row_off == 0 or not first)
    blk0 = row_off // tm
    own = lambda s: jnp.minimum(s, n_own - 1)
    full = lambda shape: pl.BlockSpec(shape, lambda s: (0,) * len(shape))
    tile = lambda n: pl.BlockSpec((1, tm, n), lambda s: (own(s) // per_seq, own(s) % per_seq, 0))
    flat = lambda n: pl.BlockSpec((tm, n), lambda s: (blk0 + s, 0))
    in_specs = [tile(D_MODEL), tile(D_MODEL), tile(GATE_PROJ), tile(D_MODEL),
                _mod_spec(mod, tm, lambda s: (own(s) // per_seq, own(s) % per_seq)),
                full((D_MODEL, D_MODEL)), full((D_MODEL, D_MODEL)), full((D_MODEL, D_MODEL)),
                full((1, D_MODEL)), full((D_MODEL, 2 * LANES))]
    args = [y_r, y_g, gates, x, mod, wbr, wbg, wout, norm2_g.reshape(1, -1), router_cat]
    if not first:
        in_specs += [pl.BlockSpec(memory_space=pl.ANY)] * 2
        args += list(shared)
    return pl.pallas_call(
        functools.partial(_merge_first_kernel, n_own=n_own) if first else _merge_second_kernel,
        out_shape=(jax.ShapeDtypeStruct((b, t, D_MODEL), F32),
                   jax.ShapeDtypeStruct((n_tok, MOE_ROW), F32),
                   jax.ShapeDtypeStruct((n_tok, LANES), F32)),
        grid=(n_steps,),
        in_specs=in_specs,
        out_specs=(tile(D_MODEL), flat(MOE_ROW), flat(LANES)),
        input_output_aliases={} if first else {10: 1, 11: 2},
        compiler_params=pltpu.CompilerParams(dimension_semantics=("arbitrary",),
                                             vmem_limit_bytes=VMEM_LIMIT),
        name="merge_router",
    )(*args)


def _moe_plan(expert_ids, n_tok):
    tm = MOE_TILE
    n_tiles = -(-n_tok // tm) + N_GROUPS * (EXPERTS_PER_GROUP * (EXPERTS_PER_GROUP - 1) // 2)
    n_slots = n_tiles * tm
    n_fill = n_slots - n_tok
    row_bits = 15
    assert max(n_tok, n_fill) <= 1 << row_bits and N_PAIR_CLASSES << (row_bits + 1) <= 1 << 24
    e_lo = jnp.minimum(expert_ids[:, 0], expert_ids[:, 1])
    e_hi = jnp.maximum(expert_ids[:, 0], expert_ids[:, 1])
    cls = e_lo * EXPERTS_PER_GROUP + e_hi % EXPERTS_PER_GROUP
    c_iota = jnp.arange(N_PAIR_CLASSES, dtype=jnp.int32)
    counts = jnp.sum((cls[None, :] == c_iota[:, None]).astype(jnp.int32), axis=1)
    tiles_c = (counts + tm - 1) // tm
    n_used = jnp.sum(tiles_c)
    pad_end = jnp.cumsum(tiles_c * tm - counts)
    f_iota = jnp.arange(n_fill, dtype=jnp.int32)
    f_cls = jnp.minimum(jnp.sum((f_iota[:, None] >= pad_end[None, :]).astype(jnp.int32), axis=1), N_PAIR_CLASSES - 1)
    keys = jnp.concatenate([(cls << (row_bits + 1)) + jnp.arange(n_tok, dtype=jnp.int32),
                            (f_cls << (row_bits + 1)) + (1 << row_bits) + f_iota])
    neg_sorted, _ = lax.top_k(-keys.astype(F32), n_slots)
    slots = (-neg_sorted).astype(jnp.int32).reshape(n_tiles, tm)
    valid = (slots & (1 << row_bits)) == 0
    tok = jnp.where(valid, slots & ((1 << row_bits) - 1), 0)
    t_idx = jnp.arange(n_tiles, dtype=jnp.int32)
    j_idx = jnp.arange(tm, dtype=jnp.int32)[None, :]
    dst = jnp.where(valid, tok, n_tok + (t_idx[:, None] % 2) * DMA_GROUP + j_idx % DMA_GROUP)
    n_valid = jnp.sum(valid.astype(jnp.int32), axis=1)
    n_groups = jnp.where(t_idx < n_used, (n_valid + DMA_GROUP - 1) // DMA_GROUP, 0)
    tile_c = slots[:, 0] >> (row_bits + 1)
    last_c = lax.dynamic_slice(tile_c, (jnp.maximum(n_used - 1, 0),), (1,))
    tile_c = jnp.where(t_idx < n_used, tile_c, last_c)
    tile_lo = tile_c // EXPERTS_PER_GROUP
    tile_hi = (tile_c // (EXPERTS_PER_GROUP * EXPERTS_PER_GROUP)) * EXPERTS_PER_GROUP + tile_c % EXPERTS_PER_GROUP
    return (tile_lo, tile_hi, n_groups.astype(jnp.int32), n_used.reshape(1).astype(jnp.int32),
            tok.reshape(n_tiles, 1, tm), dst.reshape(n_tiles, 1, tm))


def _moe_pair_kernel(lo_ref, hi_ref, ng_ref, n_used_ref, tok_ref, tok_next_ref, dst_ref, h2_hbm,
                     wg_lo_ref, wu_lo_ref, wd_lo_ref, wg_hi_ref, wu_hi_ref, wd_hi_ref, out_hbm,
                     xbuf, obuf, w_bf, gsem, ssem):
    t = pl.program_id(0)
    n_t = pl.num_programs(0)
    tm = MOE_TILE
    slot = lax.rem(t, 2)
    n_used = n_used_ref[0]

    def gather(idx_ref, s, n_groups):
        def body(g, carry):
            for u in range(DMA_GROUP):
                pltpu.make_async_copy(h2_hbm.at[pl.ds(idx_ref[0, 0, g * DMA_GROUP + u], 1)],
                                      xbuf.at[s, g, pl.ds(u, 1)], gsem.at[s]).start()
            return carry
        lax.fori_loop(0, n_groups, body, 0)

    def wait_gather(s, n_groups):
        def body(g, carry):
            pltpu.make_async_copy(h2_hbm.at[pl.ds(0, DMA_GROUP)], xbuf.at[s, 0], gsem.at[s]).wait()
            return carry
        lax.fori_loop(0, n_groups, body, 0)

    def wait_scatter(s, n_groups):
        def body(g, carry):
            pltpu.make_async_copy(obuf.at[s, 0], out_hbm.at[pl.ds(0, DMA_GROUP)], ssem.at[s]).wait()
            return carry
        lax.fori_loop(0, n_groups, body, 0)

    @pl.when(t == 0)
    def _():
        xbuf[...] = jnp.zeros(xbuf.shape, F32)
        obuf[0, 0] = jnp.zeros(obuf.shape[2:], F32)
        n_rows = out_hbm.shape[0]
        for half in range(2):
            fill = pltpu.make_async_copy(obuf.at[0, 0], out_hbm.at[pl.ds(n_rows - (2 - half) * DMA_GROUP, DMA_GROUP)],
                                         ssem.at[0])
            fill.start()
            fill.wait()
        gather(tok_ref, 0, ng_ref[0])

    @pl.when((t >= 2) & (t - 2 < n_used))
    def _():
        wait_scatter(slot, ng_ref[jnp.maximum(t - 2, 0)])

    @pl.when(t < n_used)
    def _():
        n_here = ng_ref[t]
        wait_gather(slot, n_here)

        @pl.when(t + 1 < n_used)
        def _():
            gather(tok_next_ref, 1 - slot, ng_ref[jnp.minimum(t + 1, n_t - 1)])

        t_prev = jnp.maximum(t - 1, 0)
        for which, (e_ref, refs) in enumerate(((lo_ref, (wg_lo_ref, wu_lo_ref, wd_lo_ref)),
                                               (hi_ref, (wg_hi_ref, wu_hi_ref, wd_hi_ref)))):
            @pl.when((t == 0) | (e_ref[t] != e_ref[t_prev]))
            def _():
                for m, ref in enumerate(refs):
                    w_bf[which][m][...] = ref[0].astype(BF16)

        xe = xbuf[slot].reshape(tm, MOE_ROW)
        x = xe[:, :D_MODEL].astype(BF16)
        route = xe[:, D_MODEL:]
        lane = lax.broadcasted_iota(jnp.int32, route.shape, 1)
        pick = lambda k: jnp.sum(jnp.where(lane == k, route, 0.0), axis=1, keepdims=True)
        i1, i2, w1, w2 = pick(0), pick(1), pick(2), pick(3)
        w_lo = jnp.where(i1 < i2, w1, w2)
        w_hi = jnp.where(i1 < i2, w2, w1)
        acc = None
        for which, wt in enumerate((w_lo, w_hi)):
            wg, wu, wd = w_bf[which]
            hid = _silu(jnp.dot(x, wg[...], preferred_element_type=F32)) * jnp.dot(
                x, wu[...], preferred_element_type=F32)
            part = wt * _bdot(hid, wd[...])
            acc = part if acc is None else acc + part
        obuf[slot] = acc.reshape(obuf.shape[1:])

        def body(g, carry):
            for u in range(DMA_GROUP):
                pltpu.make_async_copy(obuf.at[slot, g, pl.ds(u, 1)],
                                      out_hbm.at[pl.ds(dst_ref[0, 0, g * DMA_GROUP + u], 1)], ssem.at[slot]).start()
            return carry
        lax.fori_loop(0, n_here, body, 0)

    @pl.when(t == n_t - 1)
    def _():
        @pl.when((t >= 1) & (t - 1 < n_used))
        def _():
            wait_scatter(1 - slot, ng_ref[jnp.maximum(t - 1, 0)])

        @pl.when(t < n_used)
        def _():
            wait_scatter(slot, ng_ref[t])


def _moe_grouped(h2_ext, expert_ids, wg, wu, wd):
    return _moe_call(h2_ext, *_moe_plan(expert_ids, h2_ext.shape[0]), wg, wu, wd)


def _moe_call(h2_ext, tile_lo, tile_hi, n_groups, n_used, tok, dst, wg, wu, wd):
    n_tok = h2_ext.shape[0]
    tm = MOE_TILE
    n_tiles = tile_lo.shape[0]
    smem_tile = lambda f: pl.BlockSpec((1, 1, tm), f, memory_space=pltpu.SMEM)
    w_lo = lambda shape: pl.BlockSpec(shape, lambda t, lo, hi, nv, nu: (lo[t], 0, 0))
    w_hi = lambda shape: pl.BlockSpec(shape, lambda t, lo, hi, nv, nu: (hi[t], 0, 0))
    up, down = (1, D_MODEL, EXPERT_FF), (1, EXPERT_FF, D_MODEL)
    bf_set = [pltpu.VMEM(up[1:], BF16), pltpu.VMEM(up[1:], BF16), pltpu.VMEM(down[1:], BF16)]
    grid_spec = pltpu.PrefetchScalarGridSpec(
        num_scalar_prefetch=4,
        grid=(n_tiles,),
        in_specs=[smem_tile(lambda t, *_: (t, 0, 0)),
                  smem_tile(lambda t, *_: (jnp.minimum(t + 1, n_tiles - 1), 0, 0)),
                  smem_tile(lambda t, *_: (t, 0, 0)),
                  pl.BlockSpec(memory_space=pl.ANY),
                  w_lo(up), w_lo(up), w_lo(down), w_hi(up), w_hi(up), w_hi(down)],
        out_specs=pl.BlockSpec(memory_space=pl.ANY),
        scratch_shapes=[pltpu.VMEM((2, tm // DMA_GROUP, DMA_GROUP, MOE_ROW), F32),
                        pltpu.VMEM((2, tm // DMA_GROUP, DMA_GROUP, D_MODEL), F32),
                        [bf_set, bf_set],
                        pltpu.SemaphoreType.DMA((2,)), pltpu.SemaphoreType.DMA((2,))],
    )
    return pl.pallas_call(
        _moe_pair_kernel,
        out_shape=jax.ShapeDtypeStruct((n_tok + 2 * DMA_GROUP, D_MODEL), F32),
        grid_spec=grid_spec,
        compiler_params=pltpu.CompilerParams(dimension_semantics=("arbitrary",),
                                             vmem_limit_bytes=VMEM_LIMIT),
        name="moe_pairs",
    )(tile_lo, tile_hi, n_groups, n_used, tok, tok, dst, h2_ext, wg, wu, wd, wg, wu, wd)


def _final_kernel(x1_ref, moe_ref, mod_ref, fg_ref, y_ref):
    x2 = x1_ref[0] + mod_ref[0][5] * moe_ref[...]
    y_ref[0] = _rms_scale(x2) * fg_ref[...]


def _final(x1, moe_out, mod, final_g, *, row_off, tm):
    b, t, _ = x1.shape
    blk0 = row_off // tm
    per_seq = t // tm
    return pl.pallas_call(
        _final_kernel,
        out_shape=jax.ShapeDtypeStruct((b, t, D_MODEL), F32),
        grid=(b, per_seq),
        in_specs=[pl.BlockSpec((1, tm, D_MODEL), lambda i, j: (i, j, 0)),
                  pl.BlockSpec((tm, D_MODEL), lambda i, j: (blk0 + i * per_seq + j, 0)),
                  _mod_spec(mod, tm, lambda i, j: (i, j)),
                  pl.BlockSpec((1, D_MODEL), lambda i, j: (0, 0))],
        out_specs=pl.BlockSpec((1, tm, D_MODEL), lambda i, j: (i, j, 0)),
        compiler_params=pltpu.CompilerParams(dimension_semantics=("parallel", "parallel"),
                                             vmem_limit_bytes=VMEM_LIMIT),
        name="moe_combine_norm",
    )(x1, moe_out, mod, final_g.reshape(1, -1))


def _mixer(x, mod, shift0, s_rwkv0, s_gla0, wts, *, n_tok, row_off, shared=None):
    b, t, _ = x.shape
    per_token = mod.shape[2] > 1
    rows = x.reshape(1, b * t, D_MODEL) if per_token else x
    tm = min(rows.shape[1], ROW_TILE)
    p_rwkv = _norm_proj(rows, mod, wts["norm1_g"], wts["w_in_rwkv"], gate=False, tm=tm)
    p_gla = _norm_proj(rows, mod, wts["norm1_g"], wts["w_in_gla"], gate=False, tm=tm)
    gates = _norm_proj(rows, mod, wts["norm1_g"], wts["w_in_gate"], gate=True, tm=tm)
    p_rwkv = p_rwkv.reshape(b, t, RWKV_PROJ)

    y_r, s_rwkv = _rwkv(p_rwkv, shift0, s_rwkv0,
                        wts["rwkv_mu"], wts["rwkv_w0"], wts["rwkv_decay_up"], wts["rwkv_a0"],
                        wts["rwkv_aaa_up"], wts["rwkv_gate_up"], wts["rwkv_k_k"], wts["rwkv_k_a"],
                        wts["rwkv_r_k"], wts["rwkv_ln_g"], wts["rwkv_ln_b"], nb=RWKV_SEQS_PER_STEP)
    y_g, s_gla = _gla(p_gla.reshape(b, t, GLA_PROJ_PAD), s_gla0, wts["gla_alpha_up"],
                      wts["gla_alpha_b"], wts["gla_norm_g"], nb=GLA_SEQS_PER_STEP)

    x1, h2_all, route_all = _merge(
        y_r.reshape(rows.shape), y_g.reshape(rows.shape), gates, rows, mod, wts["w_branch_rwkv"],
        wts["w_branch_gla"], wts["w_out"], wts["norm2_g"], wts["router_cat"],
        tm=tm, n_tok=n_tok, row_off=row_off, shared=shared)
    states = (p_rwkv[:, t - 1:t, :][None], s_rwkv[None], s_gla[None])
    return x1, h2_all, route_all, states


def kernel(x_prompt, x_sample, c_prompt, c_sample, state_rwkv_shift, state_rwkv, state_gla, w_ada, b_ada, norm1_g, w_in, rwkv_mu, rwkv_w0, rwkv_decay_up, rwkv_a0, rwkv_aaa_up, rwkv_gate_up, rwkv_k_k, rwkv_k_a, rwkv_r_k, rwkv_ln_g, rwkv_ln_b, gla_alpha_up, gla_alpha_b, gla_norm_g, w_branch_rwkv, w_branch_gla, w_out, norm2_g, router_group, router_expert, expert_w_gate, expert_w_up, expert_w_down, final_norm_g):
    bp, tp = x_prompt.shape[:2]
    bs, ts = x_sample.shape[:2]
    w_in0 = w_in[0]
    g0 = RWKV_PROJ
    o3 = 2 * GLA_DK_TOTAL + GLA_DV_TOTAL
    o4 = o3 + GLA_GATE_RANK
    w_gla = w_in0[:, g0:g0 + GLA_PROJ]
    w_gla = jnp.concatenate([w_gla[:, :o3], w_gla[:, o4:], w_gla[:, o3:o4],
                             jnp.zeros((D_MODEL, GLA_PROJ_PAD - GLA_PROJ), F32)], axis=1)
    wts = dict(
        norm1_g=norm1_g[0].reshape(1, -1),
        w_in_rwkv=w_in0[:, :g0].astype(BF16),
        w_in_gla=w_gla.astype(BF16),
        w_in_gate=w_in0[:, g0 + GLA_PROJ:].astype(BF16),
        rwkv_mu=rwkv_mu[0], rwkv_w0=rwkv_w0[0], rwkv_decay_up=rwkv_decay_up[0], rwkv_a0=rwkv_a0[0],
        rwkv_aaa_up=rwkv_aaa_up[0], rwkv_gate_up=rwkv_gate_up[0], rwkv_k_k=rwkv_k_k[0],
        rwkv_k_a=rwkv_k_a[0], rwkv_r_k=rwkv_r_k[0], rwkv_ln_g=rwkv_ln_g[0], rwkv_ln_b=rwkv_ln_b[0],
        gla_alpha_up=jnp.pad(gla_alpha_up[0], ((0, LANES - GLA_GATE_RANK), (0, 0))),
        gla_alpha_b=gla_alpha_b[0], gla_norm_g=gla_norm_g[0],
        w_branch_rwkv=w_branch_rwkv[0].astype(BF16), w_branch_gla=w_branch_gla[0].astype(BF16),
        w_out=w_out[0].astype(BF16), norm2_g=norm2_g[0],
        router_cat=_split_hi_lo(jnp.pad(jnp.concatenate([router_expert[0], router_group[0]], axis=1),
                                        ((0, 0), (0, LANES - N_EXPERTS - N_GROUPS)))),
        expert_w_gate=expert_w_gate[0], expert_w_up=expert_w_up[0], expert_w_down=expert_w_down[0],
    )
    mod = _ada(jnp.concatenate([c_prompt, c_sample], axis=0), w_ada[0], b_ada[0])
    mod = mod.reshape(bp + bs, N_MOD, D_MODEL)
    n_p = bp * tp
    n_s = bs * ts
    n_tok = n_p + n_s
    mod_p = mod[:bp].reshape(bp, N_MOD, 1, D_MODEL)
    mod_s = jnp.repeat(mod[bp:].transpose(1, 0, 2), ts, axis=1)[None]

    dt = x_prompt.dtype
    x1_p, h2_all, route_all, states_p = _mixer(
        x_prompt, mod_p, jnp.zeros((bp, 1, RWKV_PROJ), dt),
        jnp.zeros((bp, RWKV_HEADS, RWKV_HEAD_DIM, RWKV_HEAD_DIM), state_rwkv.dtype),
        jnp.zeros((bp, GLA_HEADS, GLA_DK, GLA_DV), state_gla.dtype), wts, n_tok=n_tok, row_off=0)
    x1_s, h2_all, route_all, states_s = _mixer(
        x_sample, mod_s, state_rwkv_shift[0], state_rwkv[0], state_gla[0], wts,
        n_tok=n_tok, row_off=n_p, shared=(h2_all, route_all))

    expert_ids = route_all[:, :2].astype(jnp.int32)
    moe_out = _moe_grouped(h2_all, expert_ids, wts["expert_w_gate"], wts["expert_w_up"],
                           wts["expert_w_down"])
    y_p = _final(x1_p, moe_out, mod_p, final_norm_g, row_off=0, tm=min(tp, ROW_TILE))
    y_s = _final(x1_s, moe_out, mod_s, final_norm_g, row_off=n_p, tm=min(n_s, ROW_TILE))
    return (y_p, y_s.reshape(bs, ts, D_MODEL)) + states_p + states_s
```

```python
import functools
import math

import jax
import jax.numpy as jnp
from jax import lax
from jax.experimental import pallas as pl
from jax.experimental.pallas import tpu as pltpu

F32 = jnp.float32
BF16 = jnp.bfloat16

D_MODEL = 1024
N_MOD = 6
EPS = 1e-6
RWKV_HEAD_DIM = 64
RWKV_HEADS = 16
RWKV_PAIRS = RWKV_HEADS // 2
DECAY_LORA = 64
AAA_LORA = 64
GATE_LORA = 128
RWKV_PROJ = 3 * D_MODEL + DECAY_LORA + AAA_LORA + GATE_LORA
LN_X_EPS = 64e-5
GLA_HEADS = 4
GLA_DK = 128
GLA_DV = 256
GLA_DK_TOTAL = GLA_HEADS * GLA_DK
GLA_DV_TOTAL = GLA_HEADS * GLA_DV
GLA_GATE_RANK = 16
GLA_GATE_NORM = 16.0
GLA_PROJ = 2 * GLA_DK_TOTAL + 2 * GLA_DV_TOTAL + GLA_GATE_RANK
LANES = 128
GLA_PROJ_PAD = 2 * GLA_DK_TOTAL + 2 * GLA_DV_TOTAL + LANES
GATE_PROJ = 2 * D_MODEL
N_GROUPS = 4
EXPERTS_PER_GROUP = 8
N_EXPERTS = 32
EXPERT_FF = 512

CHUNK = 64
ROW_TILE = 512
MOE_TILE = 192
DMA_GROUP = 8
MOE_ROW = D_MODEL + LANES
N_PAIR_CLASSES = N_EXPERTS * EXPERTS_PER_GROUP
RWKV_SEQS_PER_STEP = 4
GLA_SEQS_PER_STEP = 8
VMEM_LIMIT = 56 * 1024 * 1024


def _bdot(a, b):
    return jnp.dot(a.astype(BF16), b.astype(BF16), preferred_element_type=F32)


def _bdot_nt(a, b):
    return lax.dot_general(a.astype(BF16), b.astype(BF16), (((1,), (1,)), ((), ())),
                           preferred_element_type=F32)


def _bdot_tn(a, b):
    return jnp.dot(a.T.astype(BF16), b.astype(BF16), preferred_element_type=F32)


def _split3(x):
    h1 = x.astype(BF16)
    r1 = x - h1.astype(F32)
    h2 = r1.astype(BF16)
    h3 = (r1 - h2.astype(F32)).astype(BF16)
    return h1, h2, h3


def _dot3(a, b):
    a1, a2, _ = _split3(a)
    b1, b2, _ = _split3(b)
    return (jnp.dot(a1, b1, preferred_element_type=F32)
            + jnp.dot(a2, b1, preferred_element_type=F32)
            + jnp.dot(a1, b2, preferred_element_type=F32))


def _dot_exact_lhs(a_bf16, x):
    x1, x2, _ = _split3(x)
    return (jnp.dot(a_bf16, x1, preferred_element_type=F32)
            + jnp.dot(a_bf16, x2, preferred_element_type=F32))


def _split_hi_lo(w):
    hi, lo, _ = _split3(w)
    return jnp.concatenate([hi, lo], axis=1)


def _sigmoid(x):
    return 1.0 / (1.0 + jnp.exp(-x))


def _silu(x):
    return x * _sigmoid(x)


def _rms_scale(x):
    return x * lax.rsqrt(jnp.mean(x * x, axis=-1, keepdims=True) + EPS)


def _ada_kernel(c_ref, w_ref, b_ref, o_ref):
    o_ref[...] = _dot3(_silu(c_ref[...]), w_ref[...]) + b_ref[...]


def _ada(c, w_ada, b_ada):
    n_rows = c.shape[0]
    n_out = w_ada.shape[1]
    tn = 1536
    return pl.pallas_call(
        _ada_kernel,
        out_shape=jax.ShapeDtypeStruct((n_rows, n_out), F32),
        grid=(n_out // tn,),
        in_specs=[pl.BlockSpec((n_rows, D_MODEL), lambda j: (0, 0)),
                  pl.BlockSpec((D_MODEL, tn), lambda j: (0, j)),
                  pl.BlockSpec((1, tn), lambda j: (0, j))],
        out_specs=pl.BlockSpec((n_rows, tn), lambda j: (0, j)),
        compiler_params=pltpu.CompilerParams(dimension_semantics=("parallel",),
                                             vmem_limit_bytes=VMEM_LIMIT),
        name="ada",
    )(c, w_ada, b_ada.reshape(1, n_out))


def _mod_spec(mod, tm, seq_tile):
    if mod.shape[2] == 1:
        return pl.BlockSpec((1, N_MOD, 1, D_MODEL), lambda *g: (seq_tile(*g)[0], 0, 0, 0))
    return pl.BlockSpec((1, N_MOD, tm, D_MODEL), lambda *g: (seq_tile(*g)[0], 0, seq_tile(*g)[1], 0))


def _norm_proj_kernel(x_ref, mod_ref, g_ref, w_ref, o_ref, *, gate):
    mod = mod_ref[0]
    h = _rms_scale(x_ref[0]) * g_ref[...] * (1.0 + mod[1]) + mod[0]
    out = _bdot(h, w_ref[...])
    o_ref[0] = _sigmoid(out) if gate else out


def _norm_proj(x, mod, g, w_bf16, *, gate, tm):
    b, t, _ = x.shape
    n = w_bf16.shape[1]
    return pl.pallas_call(
        functools.partial(_norm_proj_kernel, gate=gate),
        out_shape=jax.ShapeDtypeStruct((b, t, n), F32),
        grid=(b, t // tm),
        in_specs=[pl.BlockSpec((1, tm, D_MODEL), lambda i, j: (i, j, 0)),
                  _mod_spec(mod, tm, lambda i, j: (i, j)),
                  pl.BlockSpec((1, D_MODEL), lambda i, j: (0, 0)),
                  pl.BlockSpec((D_MODEL, n), lambda i, j: (0, 0))],
        out_specs=pl.BlockSpec((1, tm, n), lambda i, j: (i, j, 0)),
        compiler_params=pltpu.CompilerParams(dimension_semantics=("parallel", "parallel"),
                                             vmem_limit_bytes=VMEM_LIMIT),
        name="norm_proj_gate" if gate else "norm_proj",
    )(x, mod, g, w_bf16)


def _blockdiag(x, lo_mask):
    return jnp.concatenate([jnp.where(lo_mask, x, 0.0), jnp.where(lo_mask, 0.0, x)], axis=0)


def _rwkv_kernel(p_ref, shift_ref, s0_ref, mu_ref, w0_ref, dup_ref, a0_ref, aup_ref, gup_ref,
                 kk_ref, ka_ref, rk_ref, lng_ref, lnb_ref, y_ref, sout_ref, s_scr, carry_scr,
                 *, nb):
    c = pl.program_id(1)
    n_chunks = pl.num_programs(1)
    C = CHUNK
    N = RWKV_HEAD_DIM
    t_blk = p_ref.shape[1]

    @pl.when(c == 0)
    def _():
        for bl in range(nb):
            for pr in range(RWKV_PAIRS):
                s_scr[bl, pr] = jnp.concatenate([s0_ref[bl, 2 * pr], s0_ref[bl, 2 * pr + 1]], axis=1)
        carry_scr[...] = shift_ref[...]

    row = lax.broadcasted_iota(jnp.int32, (C, LANES), 0)
    lane = lax.broadcasted_iota(jnp.int32, (C, LANES), 1)
    lane_t = jnp.where(lane < N, lane, lane - N)
    lo = lane < N
    strict = lane_t < row
    incl = lane_t <= row
    eye = (lane_t == row).astype(F32)
    valid = row < t_blk
    tri = (lax.broadcasted_iota(jnp.int32, (C, C), 1)
           <= lax.broadcasted_iota(jnp.int32, (C, C), 0)).astype(BF16)
    sq_r = lax.broadcasted_iota(jnp.int32, (LANES, LANES), 0)
    sq_c = lax.broadcasted_iota(jnp.int32, (LANES, LANES), 1)
    same_head = (sq_r < N) == (sq_c < N)
    diag = sq_r == sq_c
    row_w = lax.broadcasted_iota(jnp.int32, (C, RWKV_PROJ), 0)
    o3 = 3 * D_MODEL

    def hsum(v):
        s_lo = jnp.sum(jnp.where(lo, v, 0.0), axis=1, keepdims=True)
        s_hi = jnp.sum(jnp.where(lo, 0.0, v), axis=1, keepdims=True)
        return jnp.where(lo, s_lo, s_hi)

    def bd(v):
        return _blockdiag(v, lo)

    def cat0(*a):
        return jnp.concatenate(a, axis=0)

    def cat1(*a):
        return jnp.concatenate(a, axis=1)

    valid_w = lax.broadcasted_iota(jnp.int32, (C, D_MODEL), 0) < t_blk
    xs, logw_all, cum_all, asig_all, g_all = [], [], [], [], []
    for bl in range(nb):
        p = p_ref[bl]
        carry_new = p[t_blk - 1:t_blk, :]
        if t_blk < C:
            p = jnp.concatenate([p, jnp.zeros((C - t_blk, RWKV_PROJ), F32)], axis=0)
        prev = jnp.where(row_w == 0, carry_scr[bl], pltpu.roll(p, 1, axis=0))
        carry_scr[bl] = carry_new
        x = p + (prev - p) * mu_ref[...]
        xw = x[:, o3:o3 + DECAY_LORA]
        xa = x[:, o3 + DECAY_LORA:o3 + DECAY_LORA + AAA_LORA]
        xg = x[:, o3 + DECAY_LORA + AAA_LORA:]
        z = w0_ref[...] + _bdot(jnp.tanh(xw), dup_ref[...])
        logw_full = -math.exp(-0.5) * _sigmoid(z)
        if t_blk < C:
            logw_full = jnp.where(valid_w, logw_full, 0.0)
        xs.append(x)
        logw_all.append(logw_full)
        cum_all.append(_dot_exact_lhs(tri, logw_full))
        asig_all.append(_sigmoid(a0_ref[...] + _bdot(xa, aup_ref[...])))
        g_all.append(_bdot(_sigmoid(xg), gup_ref[...]))

    units = [(bl, pr) for bl in range(nb) for pr in range(RWKV_PAIRS)]
    U = range(len(units))
    cols = [slice(pr * LANES, (pr + 1) * LANES) for _, pr in units]
    r = [xs[bl][:, pr * LANES:(pr + 1) * LANES] for bl, pr in units]
    k = [xs[bl][:, D_MODEL + pr * LANES:D_MODEL + (pr + 1) * LANES] for bl, pr in units]
    v = [xs[bl][:, 2 * D_MODEL + pr * LANES:2 * D_MODEL + (pr + 1) * LANES] for bl, pr in units]
    logw = [logw_all[bl][:, cols[u]] for u, (bl, _) in enumerate(units)]
    asig = [asig_all[bl][:, cols[u]] for u, (bl, _) in enumerate(units)]
    kk = [k[u] * kk_ref[:, cols[u]] for u in U]
    k2 = [k[u] * (1.0 + (asig[u] - 1.0) * ka_ref[:, cols[u]]) for u in U]
    ss = [hsum(kk[u] * kk[u]) for u in U]
    cum = [cum_all[bl][:, cols[u]] for u, (bl, _) in enumerate(units)]
    bsum = [hsum(r[u] * k2[u] * rk_ref[:, cols[u]]) for u in U]
    kk = [kk[u] * lax.rsqrt(jnp.maximum(ss[u], 1e-24)) for u in U]
    bonus = [bsum[u] * v[u] for u in U]
    if t_blk < C:
        kk = [jnp.where(valid, a, 0.0) for a in kk]
        k2 = [jnp.where(valid, a, 0.0) for a in k2]
        v = [jnp.where(valid, a, 0.0) for a in v]
        r = [jnp.where(valid, a, 0.0) for a in r]
    b_vec = [kk[u] * asig[u] for u in U]
    clast = [cum[u][C - 1:C, :] for u in U]
    e_inv = [jnp.exp(-cum[u]) for u in U]
    e_tail = [jnp.exp(clast[u] - cum[u]) for u in U]
    aq = [-kk[u] * jnp.exp(cum[u] - logw[u]) for u in U]
    rq = [r[u] * jnp.exp(cum[u]) for u in U]
    bk = [b_vec[u] * e_inv[u] for u in U]
    kx = [k2[u] * e_inv[u] for u in U]
    bb = [b_vec[u] * e_tail[u] for u in U]
    kb = [k2[u] * e_tail[u] for u in U]

    g4 = [_bdot_nt(cat0(aq[u], rq[u]), cat0(bd(bk[u]), bd(kx[u]))) for u in U]
    m_ab = [jnp.where(strict, g4[u][:C, :LANES], 0.0) for u in U]
    m_ak = [jnp.where(strict, g4[u][:C, LANES:], 0.0) for u in U]
    m_rb = [jnp.where(incl, g4[u][C:, :LANES], 0.0) for u in U]
    m_rk = [jnp.where(incl, g4[u][C:, LANES:], 0.0) for u in U]

    t_inv = [eye + m_ab[u] for u in U]
    l_pow = [_bdot(m_ab[u], bd(m_ab[u])) for u in U]
    mv = [_bdot(cat0(m_ak[u], m_rk[u]), bd(v[u])) for u in U]
    n_steps = C.bit_length() - 1
    for step in range(1, n_steps):
        if step < n_steps - 1:
            both = [_bdot(l_pow[u], cat1(bd(t_inv[u]), bd(l_pow[u]))) for u in U]
            t_inv = [t_inv[u] + both[u][:, :LANES] for u in U]
            l_pow = [both[u][:, LANES:] for u in U]
        else:
            t_inv = [t_inv[u] + _bdot(l_pow[u], bd(t_inv[u])) for u in U]

    au = [_bdot(t_inv[u], cat1(bd(aq[u]), bd(mv[u][:C]))) for u in U]
    a_bar = [au[u][:, :LANES] for u in U]
    u_bar = [au[u][:, LANES:] for u in U]
    mau = [_bdot(m_rb[u], cat1(bd(a_bar[u]), bd(u_bar[u]))) for u in U]
    p_full = [_bdot_tn(a_bar[u], bb[u]) for u in U]
    q_full = [_bdot_tn(cat0(u_bar[u], v[u]), cat0(bb[u], kb[u])) for u in U]
    r_til = [rq[u] + mau[u][:, :LANES] for u in U]
    y_loc = [mau[u][:, LANES:] + mv[u][C:] for u in U]
    p_bd = [jnp.where(same_head, p_full[u], 0.0) + jnp.where(diag, jnp.exp(clast[u]), 0.0) for u in U]
    q_pk = [jnp.where(lo[:N], q_full[u][:N], q_full[u][N:]) for u in U]

    s_old = [s_scr[bl, pr] for bl, pr in units]
    y = [_bdot_nt(r_til[u], _blockdiag(s_old[u], lo[:N])) + y_loc[u] for u in U]
    s_new = [_bdot(s_old[u], p_bd[u]) + q_pk[u] for u in U]
    for u, (bl, pr) in enumerate(units):
        s_scr[bl, pr] = s_new[u]

    mu_y = [hsum(y[u]) * (1.0 / N) for u in U]
    yc = [y[u] - mu_y[u] for u in U]
    var = [hsum(yc[u] * yc[u]) * (1.0 / N) for u in U]
    for u, (bl, pr) in enumerate(units):
        yn = yc[u] * lax.rsqrt(var[u] + LN_X_EPS) * lng_ref[:, cols[u]] + lnb_ref[:, cols[u]]
        y_ref[bl, :, cols[u]] = ((yn + bonus[u]) * g_all[bl][:, cols[u]])[:t_blk]

    @pl.when(c == n_chunks - 1)
    def _():
        for bl in range(nb):
            for pr in range(RWKV_PAIRS):
                sout_ref[bl, 2 * pr] = s_scr[bl, pr][:, :N]
                sout_ref[bl, 2 * pr + 1] = s_scr[bl, pr][:, N:]


def _rwkv(p_rwkv, shift_prev, s0, mu, w0, decay_up, a0, aaa_up, gate_up, k_k, k_a, r_k,
          ln_g, ln_b, *, nb):
    b, t, _ = p_rwkv.shape
    nb = math.gcd(b, nb)
    t_blk = min(t, CHUNK)
    row = lambda a: a.reshape(1, -1)
    full = lambda shape: pl.BlockSpec(shape, lambda i, j: (0,) * len(shape))
    state_spec = pl.BlockSpec((nb, RWKV_HEADS, RWKV_HEAD_DIM, RWKV_HEAD_DIM), lambda i, j: (i, 0, 0, 0))
    return pl.pallas_call(
        functools.partial(_rwkv_kernel, nb=nb),
        out_shape=(jax.ShapeDtypeStruct((b, t, D_MODEL), F32),
                   jax.ShapeDtypeStruct((b, RWKV_HEADS, RWKV_HEAD_DIM, RWKV_HEAD_DIM), F32)),
        grid=(b // nb, t // t_blk),
        in_specs=[pl.BlockSpec((nb, t_blk, RWKV_PROJ), lambda i, j: (i, j, 0)),
                  pl.BlockSpec((nb, 1, RWKV_PROJ), lambda i, j: (i, 0, 0)),
                  state_spec,
                  full((1, RWKV_PROJ)), full((1, D_MODEL)), full((DECAY_LORA, D_MODEL)),
                  full((1, D_MODEL)), full((AAA_LORA, D_MODEL)), full((GATE_LORA, D_MODEL)),
                  full((1, D_MODEL)), full((1, D_MODEL)), full((1, D_MODEL)),
                  full((1, D_MODEL)), full((1, D_MODEL))],
        out_specs=(pl.BlockSpec((nb, t_blk, D_MODEL), lambda i, j: (i, j, 0)), state_spec),
        scratch_shapes=[pltpu.VMEM((nb, RWKV_PAIRS, RWKV_HEAD_DIM, LANES), F32),
                        pltpu.VMEM((nb, 1, RWKV_PROJ), F32)],
        compiler_params=pltpu.CompilerParams(dimension_semantics=("parallel", "arbitrary"),
                                             vmem_limit_bytes=VMEM_LIMIT),
        name="rwkv_scan",
    )(p_rwkv, shift_prev, s0, row(mu), row(w0), decay_up, row(a0), aaa_up, gate_up,
      row(k_k), row(k_a), row(r_k), row(ln_g), row(ln_b))


def _gla_kernel(p_ref, s0_ref, aup_ref, ab_ref, ng_ref, y_ref, sout_ref, s_scr, *, nb):
    c = pl.program_id(1)
    n_chunks = pl.num_programs(1)
    C = CHUNK
    t_blk = p_ref.shape[1]
    units = [(bl, h) for bl in range(nb) for h in range(GLA_HEADS)]
    U = range(len(units))

    @pl.when(c == 0)
    def _():
        for bl, h in units:
            s_scr[bl, h] = s0_ref[bl, h].T

    o_k = GLA_DK_TOTAL
    o_v = 2 * GLA_DK_TOTAL
    o_r = o_v + GLA_DV_TOTAL
    o_a = o_r + GLA_DV_TOTAL
    tri = (lax.broadcasted_iota(jnp.int32, (C, C), 1)
           <= lax.broadcasted_iota(jnp.int32, (C, C), 0))
    tri_bf = tri.astype(BF16)
    valid_g = lax.broadcasted_iota(jnp.int32, (C, GLA_DK_TOTAL), 0) < t_blk
    valid_k = lax.broadcasted_iota(jnp.int32, (C, GLA_DK), 0) < t_blk
    valid_v = lax.broadcasted_iota(jnp.int32, (C, GLA_DV), 0) < t_blk

    ps, cums = [], []
    for bl in range(nb):
        p = p_ref[bl]
        if t_blk < C:
            p = jnp.concatenate([p, jnp.zeros((C - t_blk, GLA_PROJ_PAD), F32)], axis=0)
        logit = _dot3(p[:, o_a:], aup_ref[...]) + ab_ref[...]
        g = (jnp.minimum(logit, 0.0) - jnp.log(1.0 + jnp.exp(-jnp.abs(logit)))) * (1.0 / GLA_GATE_NORM)
        if t_blk < C:
            g = jnp.where(valid_g, g, 0.0)
        ps.append(p)
        cums.append(_dot_exact_lhs(tri_bf, g))

    q = [ps[bl][:, h * GLA_DK:(h + 1) * GLA_DK] * (GLA_DK ** -0.5) for bl, h in units]
    k = [ps[bl][:, o_k + h * GLA_DK:o_k + (h + 1) * GLA_DK] for bl, h in units]
    v = [ps[bl][:, o_v + h * GLA_DV:o_v + (h + 1) * GLA_DV] for bl, h in units]
    if t_blk < C:
        k = [jnp.where(valid_k, a, 0.0) for a in k]
        v = [jnp.where(valid_v, a, 0.0) for a in v]
    b = [cums[bl][:, h * GLA_DK:(h + 1) * GLA_DK] for bl, h in units]
    clast = [b[u][C - 1:C, :] for u in U]
    cmid = [b[u][C // 2 - 1:C // 2, :] for u in U]
    att = [_bdot_nt(q[u] * jnp.exp(b[u] - cmid[u]), k[u] * jnp.exp(cmid[u] - b[u])) for u in U]
    att = [jnp.where(tri, a, 0.0) for a in att]
    s_t = [s_scr[bl, h] for bl, h in units]
    o = [_bdot(att[u], v[u]) + _bdot_nt(q[u] * jnp.exp(b[u]), s_t[u]) for u in U]
    s_new = [s_t[u] * jnp.exp(clast[u]) + _bdot_tn(v[u], k[u] * jnp.exp(clast[u] - b[u])) for u in U]
    for u, (bl, h) in enumerate(units):
        s_scr[bl, h] = s_new[u]
        r = ps[bl][:, o_r + h * GLA_DV:o_r + (h + 1) * GLA_DV]
        on = o[u] * lax.rsqrt(jnp.mean(o[u] * o[u], axis=-1, keepdims=True) + EPS) * ng_ref[...]
        y_ref[bl, :, h * GLA_DV:(h + 1) * GLA_DV] = (on * _silu(r))[:t_blk]

    @pl.when(c == n_chunks - 1)
    def _():
        for bl, h in units:
            sout_ref[bl, h] = s_scr[bl, h].T


def _gla(p_gla, s0, alpha_up_pad, alpha_b, norm_g, *, nb):
    b, t, _ = p_gla.shape
    nb = math.gcd(b, nb)
    t_blk = min(t, CHUNK)
    full = lambda shape: pl.BlockSpec(shape, lambda i, j: (0,) * len(shape))
    state_spec = pl.BlockSpec((nb, GLA_HEADS, GLA_DK, GLA_DV), lambda i, j: (i, 0, 0, 0))
    return pl.pallas_call(
        functools.partial(_gla_kernel, nb=nb),
        out_shape=(jax.ShapeDtypeStruct((b, t, GLA_DV_TOTAL), F32),
                   jax.ShapeDtypeStruct((b, GLA_HEADS, GLA_DK, GLA_DV), F32)),
        grid=(b // nb, t // t_blk),
        in_specs=[pl.BlockSpec((nb, t_blk, GLA_PROJ_PAD), lambda i, j: (i, j, 0)),
                  state_spec,
                  full((LANES, GLA_DK_TOTAL)), full((1, GLA_DK_TOTAL)), full((1, GLA_DV))],
        out_specs=(pl.BlockSpec((nb, t_blk, GLA_DV_TOTAL), lambda i, j: (i, j, 0)), state_spec),
        scratch_shapes=[pltpu.VMEM((nb, GLA_HEADS, GLA_DV, GLA_DK), F32)],
        compiler_params=pltpu.CompilerParams(dimension_semantics=("parallel", "arbitrary"),
                                             vmem_limit_bytes=VMEM_LIMIT),
        name="gla_scan",
    )(p_gla, s0, alpha_up_pad, alpha_b.reshape(1, -1), norm_g.reshape(1, -1))


def _merge_body(yr_ref, yg_ref, gate_ref, x_ref, mod_ref, wbr_ref, wbg_ref, wout_ref, g2_ref,
                router_ref, x1_ref, h2_ref, route_ref):
    mod = mod_ref[0]
    gates = gate_ref[0]
    merged = (gates[:, :D_MODEL] * _bdot(yr_ref[0], wbr_ref[...])
              + gates[:, D_MODEL:] * _bdot(yg_ref[0], wbg_ref[...]))
    x1 = x_ref[0] + mod[2] * _bdot(merged, wout_ref[...])
    x1_ref[0] = x1
    h2 = _rms_scale(x1) * g2_ref[...] * (1.0 + mod[4]) + mod[3]
    h2_ref[:, :D_MODEL] = h2

    tm = h2.shape[0]
    h_hi, h_lo, _ = _split3(h2)
    prod = jnp.dot(jnp.concatenate([h_hi, h_lo], axis=0), router_ref[...], preferred_element_type=F32)
    logits = prod[:tm, :LANES] + prod[tm:, :LANES] + prod[:tm, LANES:]
    lane = lax.broadcasted_iota(jnp.int32, (tm, LANES), 1)
    neg = -jnp.inf
    is_group = (lane >= N_EXPERTS) & (lane < N_EXPERTS + N_GROUPS)
    gl = jnp.where(is_group, logits, neg)
    gmax = jnp.max(gl, axis=1, keepdims=True)
    g_idx = jnp.min(jnp.where(gl == gmax, lane, LANES), axis=1, keepdims=True) - N_EXPERTS
    p_g = 1.0 / jnp.sum(jnp.exp(gl - gmax), axis=1, keepdims=True)
    in_group = (lane >= g_idx * EXPERTS_PER_GROUP) & (lane < (g_idx + 1) * EXPERTS_PER_GROUP)
    el = jnp.where(in_group, logits, neg)
    v1 = jnp.max(el, axis=1, keepdims=True)
    i1 = jnp.min(jnp.where(el == v1, lane, LANES), axis=1, keepdims=True)
    el2 = jnp.where(lane == i1, neg, el)
    v2 = jnp.max(el2, axis=1, keepdims=True)
    i2 = jnp.min(jnp.where(el2 == v2, lane, LANES), axis=1, keepdims=True)
    e21 = jnp.exp(v2 - v1)
    w1 = p_g / (1.0 + e21)
    route = (jnp.where(lane == 0, i1.astype(F32), 0.0) + jnp.where(lane == 1, i2.astype(F32), 0.0)
             + jnp.where(lane == 2, w1, 0.0) + jnp.where(lane == 3, w1 * e21, 0.0))
    route_ref[...] = route
    h2_ref[:, D_MODEL:] = route


def _merge_first_kernel(*refs, n_own):
    step = pl.program_id(0)

    @pl.when(step < n_own)
    def _():
        _merge_body(*refs)

    @pl.when(step >= n_own)
    def _():
        refs[-2][...] = jnp.zeros(refs[-2].shape, F32)
        refs[-1][...] = jnp.zeros(refs[-1].shape, F32)


def _merge_second_kernel(*refs):
    n_in = 10
    _merge_body(*refs[:n_in], *refs[n_in + 2:])


def _merge(y_r, y_g, gates, x, mod, wbr, wbg, wout, norm2_g, router_cat, *, tm, n_tok, row_off, shared=None):
    b, t, _ = x.shape
    per_seq = t // tm
    n_own = b * per_seq
    first = shared is None
    n_steps = n_tok // tm if first else n_own
    assert row_off % tm == 0 and n_tok % tm == 0 and (row_off == 0 or not first)
    blk0 = row_off // tm
    own = lambda s: jnp.minimum(s, n_own - 1)
    full = lambda shape: pl.BlockSpec(shape, lambda s: (0,) * len(shape))
    tile = lambda n: pl.BlockSpec((1, tm, n), lambda s: (own(s) // per_seq, own(s) % per_seq, 0))
    flat = lambda n: pl.BlockSpec((tm, n), lambda s: (blk0 + s, 0))
    in_specs = [tile(D_MODEL), tile(D_MODEL), tile(GATE_PROJ), tile(D_MODEL),
                _mod_spec(mod, tm, lambda s: (own(s) // per_seq, own(s) % per_seq)),
                full((D_MODEL, D_MODEL)), full((D_MODEL, D_MODEL)), full((D_MODEL, D_MODEL)),
                full((1, D_MODEL)), full((D_MODEL, 2 * LANES))]
    args = [y_r, y_g, gates, x, mod, wbr, wbg, wout, norm2_g.reshape(1, -1), router_cat]
    if not first:
        in_specs += [pl.BlockSpec(memory_space=pl.ANY)] * 2
        args += list(shared)
    return pl.pallas_call(
        functools.partial(_merge_first_kernel, n_own=n_own) if first else _merge_second_kernel,
        out_shape=(jax.ShapeDtypeStruct((b, t, D_MODEL), F32),
                   jax.ShapeDtypeStruct((n_tok, MOE_ROW), F32),
                   jax.ShapeDtypeStruct((n_tok, LANES), F32)),
        grid=(n_steps,),
        in_specs=in_specs,
        out_specs=(tile(D_MODEL), flat(MOE_ROW), flat(LANES)),
        input_output_aliases={} if first else {10: 1, 11: 2},
        compiler_params=pltpu.CompilerParams(dimension_semantics=("arbitrary",),
                                             vmem_limit_bytes=VMEM_LIMIT),
        name="merge_router",
    )(*args)


def _moe_plan(expert_ids, n_tok):
    tm = MOE_TILE
    n_tiles = -(-n_tok // tm) + N_GROUPS * (EXPERTS_PER_GROUP * (EXPERTS_PER_GROUP - 1) // 2)
    n_slots = n_tiles * tm
    n_fill = n_slots - n_tok
    row_bits = 15
    assert max(n_tok, n_fill) <= 1 << row_bits and N_PAIR_CLASSES << (row_bits + 1) <= 1 << 24
    e_lo = jnp.minimum(expert_ids[:, 0], expert_ids[:, 1])
    e_hi = jnp.maximum(expert_ids[:, 0], expert_ids[:, 1])
    cls = e_lo * EXPERTS_PER_GROUP + e_hi % EXPERTS_PER_GROUP
    c_iota = jnp.arange(N_PAIR_CLASSES, dtype=jnp.int32)
    counts = jnp.sum((cls[None, :] == c_iota[:, None]).astype(jnp.int32), axis=1)
    tiles_c = (counts + tm - 1) // tm
    n_used = jnp.sum(tiles_c)
    pad_end = jnp.cumsum(tiles_c * tm - counts)
    f_iota = jnp.arange(n_fill, dtype=jnp.int32)
    f_cls = jnp.minimum(jnp.sum((f_iota[:, None] >= pad_end[None, :]).astype(jnp.int32), axis=1), N_PAIR_CLASSES - 1)
    keys = jnp.concatenate([(cls << (row_bits + 1)) + jnp.arange(n_tok, dtype=jnp.int32),
                            (f_cls << (row_bits + 1)) + (1 << row_bits) + f_iota])
    neg_sorted, _ = lax.top_k(-keys.astype(F32), n_slots)
    slots = (-neg_sorted).astype(jnp.int32).reshape(n_tiles, tm)
    valid = (slots & (1 << row_bits)) == 0
    tok = jnp.where(valid, slots & ((1 << row_bits) - 1), 0)
    t_idx = jnp.arange(n_tiles, dtype=jnp.int32)
    j_idx = jnp.arange(tm, dtype=jnp.int32)[None, :]
    dst = jnp.where(valid, tok, n_tok + (t_idx[:, None] % 2) * DMA_GROUP + j_idx % DMA_GROUP)
    n_valid = jnp.sum(valid.astype(jnp.int32), axis=1)
    n_groups = jnp.where(t_idx < n_used, (n_valid + DMA_GROUP - 1) // DMA_GROUP, 0)
    tile_c = slots[:, 0] >> (row_bits + 1)
    last_c = lax.dynamic_slice(tile_c, (jnp.maximum(n_used - 1, 0),), (1,))
    tile_c = jnp.where(t_idx < n_used, tile_c, last_c)
    tile_lo = tile_c // EXPERTS_PER_GROUP
    tile_hi = (tile_c // (EXPERTS_PER_GROUP * EXPERTS_PER_GROUP)) * EXPERTS_PER_GROUP + tile_c % EXPERTS_PER_GROUP
    return (tile_lo, tile_hi, n_groups.astype(jnp.int32), n_used.reshape(1).astype(jnp.int32),
            tok.reshape(n_tiles, 1, tm), dst.reshape(n_tiles, 1, tm))


def _moe_pair_kernel(lo_ref, hi_ref, ng_ref, n_used_ref, tok_ref, tok_next_ref, dst_ref, h2_hbm,
                     wg_hbm, wu_hbm, wd_hbm, out_hbm,
                     xbuf, obuf, w_res, w_stage, gsem, ssem, wsem):
    t = pl.program_id(0)
    n_t = pl.num_programs(0)
    tm = MOE_TILE
    slot = lax.rem(t, 2)
    n_used = n_used_ref[0]

    def gather(idx_ref, s, n_groups):
        def body(g, carry):
            for u in range(DMA_GROUP):
                pltpu.make_async_copy(h2_hbm.at[pl.ds(idx_ref[0, 0, g * DMA_GROUP + u], 1)],
                                      xbuf.at[s, g, pl.ds(u, 1)], gsem.at[s]).start()
            return carry
        lax.fori_loop(0, n_groups, body, 0)

    def wait_gather(s, n_groups):
        def body(g, carry):
            pltpu.make_async_copy(h2_hbm.at[pl.ds(0, DMA_GROUP)], xbuf.at[s, 0], gsem.at[s]).wait()
            return carry
        lax.fori_loop(0, n_groups, body, 0)

    def wait_scatter(s, n_groups):
        def body(g, carry):
            pltpu.make_async_copy(obuf.at[s, 0], out_hbm.at[pl.ds(0, DMA_GROUP)], ssem.at[s]).wait()
            return carry
        lax.fori_loop(0, n_groups, body, 0)

    @pl.when(t == 0)
    def _():
        xbuf[...] = jnp.zeros(xbuf.shape, F32)
        obuf[0, 0] = jnp.zeros(obuf.shape[2:], F32)
        n_rows = out_hbm.shape[0]
        for half in range(2):
            fill = pltpu.make_async_copy(obuf.at[0, 0], out_hbm.at[pl.ds(n_rows - (2 - half) * DMA_GROUP, DMA_GROUP)],
                                         ssem.at[0])
            fill.start()
            fill.wait()
        gather(tok_ref, 0, ng_ref[0])

    @pl.when((t >= 2) & (t - 2 < n_used))
    def _():
        wait_scatter(slot, ng_ref[jnp.maximum(t - 2, 0)])

    @pl.when(t < n_used)
    def _():
        n_here = ng_ref[t]
        wait_gather(slot, n_here)

        @pl.when(t + 1 < n_used)
        def _():
            gather(tok_next_ref, 1 - slot, ng_ref[jnp.minimum(t + 1, n_t - 1)])

        group = lo_ref[t] // EXPERTS_PER_GROUP

        @pl.when((t == 0) | (group != lo_ref[jnp.maximum(t - 1, 0)] // EXPERTS_PER_GROUP))
        def _():
            def fetch(e, buf):
                return [pltpu.make_async_copy(src.at[group * EXPERTS_PER_GROUP + e], w_stage[m].at[buf], wsem.at[buf, m])
                        for m, src in enumerate((wg_hbm, wu_hbm, wd_hbm))]

            for cp in fetch(0, 0):
                cp.start()
            for e in range(EXPERTS_PER_GROUP):
                if e + 1 < EXPERTS_PER_GROUP:
                    for cp in fetch(e + 1, (e + 1) % 2):
                        cp.start()
                for m, cp in enumerate(fetch(e, e % 2)):
                    cp.wait()
                    w_res[m][e] = w_stage[m][e % 2].astype(BF16)

        xe = xbuf[slot].reshape(tm, MOE_ROW)
        x = xe[:, :D_MODEL].astype(BF16)
        route = xe[:, D_MODEL:]
        lane = lax.broadcasted_iota(jnp.int32, route.shape, 1)
        pick = lambda k: jnp.sum(jnp.where(lane == k, route, 0.0), axis=1, keepdims=True)
        i1, i2, w1, w2 = pick(0), pick(1), pick(2), pick(3)
        w_lo = jnp.where(i1 < i2, w1, w2)
        w_hi = jnp.where(i1 < i2, w2, w1)
        acc = None
        for e_ref, wt in ((lo_ref, w_lo), (hi_ref, w_hi)):
            e_local = lax.rem(e_ref[t], EXPERTS_PER_GROUP)
            hid = _silu(jnp.dot(x, w_res[0][e_local], preferred_element_type=F32)) * jnp.dot(
                x, w_res[1][e_local], preferred_element_type=F32)
            part = wt * _bdot(hid, w_res[2][e_local])
            acc = part if acc is None else acc + part
        obuf[slot] = acc.reshape(obuf.shape[1:])

        def body(g, carry):
            for u in range(DMA_GROUP):
                pltpu.make_async_copy(obuf.at[slot, g, pl.ds(u, 1)],
                                      out_hbm.at[pl.ds(dst_ref[0, 0, g * DMA_GROUP + u], 1)], ssem.at[slot]).start()
            return carry
        lax.fori_loop(0, n_here, body, 0)

    @pl.when(t == n_t - 1)
    def _():
        @pl.when((t >= 1) & (t - 1 < n_used))
        def _():
            wait_scatter(1 - slot, ng_ref[jnp.maximum(t - 1, 0)])

        @pl.when(t < n_used)
        def _():
            wait_scatter(slot, ng_ref[t])


def _moe_grouped(h2_ext, expert_ids, wg, wu, wd):
    return _moe_call(h2_ext, *_moe_plan(expert_ids, h2_ext.shape[0]), wg, wu, wd)


def _moe_call(h2_ext, tile_lo, tile_hi, n_groups, n_used, tok, dst, wg, wu, wd):
    n_tok = h2_ext.shape[0]
    tm = MOE_TILE
    n_tiles = tile_lo.shape[0]
    smem_tile = lambda f: pl.BlockSpec((1, 1, tm), f, memory_space=pltpu.SMEM)
    up, down = (D_MODEL, EXPERT_FF), (EXPERT_FF, D_MODEL)
    any_spec = pl.BlockSpec(memory_space=pl.ANY)
    grid_spec = pltpu.PrefetchScalarGridSpec(
        num_scalar_prefetch=4,
        grid=(n_tiles,),
        in_specs=[smem_tile(lambda t, *_: (t, 0, 0)),
                  smem_tile(lambda t, *_: (jnp.minimum(t + 1, n_tiles - 1), 0, 0)),
                  smem_tile(lambda t, *_: (t, 0, 0)),
                  any_spec, any_spec, any_spec, any_spec],
        out_specs=pl.BlockSpec(memory_space=pl.ANY),
        scratch_shapes=[pltpu.VMEM((2, tm // DMA_GROUP, DMA_GROUP, MOE_ROW), F32),
                        pltpu.VMEM((2, tm // DMA_GROUP, DMA_GROUP, D_MODEL), F32),
                        [pltpu.VMEM((EXPERTS_PER_GROUP,) + shape, BF16) for shape in (up, up, down)],
                        [pltpu.VMEM((2,) + shape, F32) for shape in (up, up, down)],
                        pltpu.SemaphoreType.DMA((2,)), pltpu.SemaphoreType.DMA((2,)),
                        pltpu.SemaphoreType.DMA((2, 3))],
    )
    return pl.pallas_call(
        _moe_pair_kernel,
        out_shape=jax.ShapeDtypeStruct((n_tok + 2 * DMA_GROUP, D_MODEL), F32),
        grid_spec=grid_spec,
        compiler_params=pltpu.CompilerParams(dimension_semantics=("arbitrary",),
                                             vmem_limit_bytes=VMEM_LIMIT),
        name="moe_pairs",
    )(tile_lo, tile_hi, n_groups, n_used, tok, tok, dst, h2_ext, wg, wu, wd)


def _final_kernel(x1_ref, moe_ref, mod_ref, fg_ref, y_ref):
    x2 = x1_ref[0] + mod_ref[0][5] * moe_ref[...]
    y_ref[0] = _rms_scale(x2) * fg_ref[...]


def _final(x1, moe_out, mod, final_g, *, row_off, tm):
    b, t, _ = x1.shape
    blk0 = row_off // tm
    per_seq = t // tm
    return pl.pallas_call(
        _final_kernel,
        out_shape=jax.ShapeDtypeStruct((b, t, D_MODEL), F32),
        grid=(b, per_seq),
        in_specs=[pl.BlockSpec((1, tm, D_MODEL), lambda i, j: (i, j, 0)),
                  pl.BlockSpec((tm, D_MODEL), lambda i, j: (blk0 + i * per_seq + j, 0)),
                  _mod_spec(mod, tm, lambda i, j: (i, j)),
                  pl.BlockSpec((1, D_MODEL), lambda i, j: (0, 0))],
        out_specs=pl.BlockSpec((1, tm, D_MODEL), lambda i, j: (i, j, 0)),
        compiler_params=pltpu.CompilerParams(dimension_semantics=("parallel", "parallel"),
                                             vmem_limit_bytes=VMEM_LIMIT),
        name="moe_combine_norm",
    )(x1, moe_out, mod, final_g.reshape(1, -1))


def _mixer(x, mod, shift0, s_rwkv0, s_gla0, wts, *, n_tok, row_off, shared=None):
    b, t, _ = x.shape
    per_token = mod.shape[2] > 1
    rows = x.reshape(1, b * t, D_MODEL) if per_token else x
    tm = min(rows.shape[1], ROW_TILE)
    p_rwkv = _norm_proj(rows, mod, wts["norm1_g"], wts["w_in_rwkv"], gate=False, tm=tm)
    p_gla = _norm_proj(rows, mod, wts["norm1_g"], wts["w_in_gla"], gate=False, tm=tm)
    gates = _norm_proj(rows, mod, wts["norm1_g"], wts["w_in_gate"], gate=True, tm=tm)
    p_rwkv = p_rwkv.reshape(b, t, RWKV_PROJ)

    y_r, s_rwkv = _rwkv(p_rwkv, shift0, s_rwkv0,
                        wts["rwkv_mu"], wts["rwkv_w0"], wts["rwkv_decay_up"], wts["rwkv_a0"],
                        wts["rwkv_aaa_up"], wts["rwkv_gate_up"], wts["rwkv_k_k"], wts["rwkv_k_a"],
                        wts["rwkv_r_k"], wts["rwkv_ln_g"], wts["rwkv_ln_b"], nb=RWKV_SEQS_PER_STEP)
    y_g, s_gla = _gla(p_gla.reshape(b, t, GLA_PROJ_PAD), s_gla0, wts["gla_alpha_up"],
                      wts["gla_alpha_b"], wts["gla_norm_g"], nb=GLA_SEQS_PER_STEP)

    x1, h2_all, route_all = _merge(
        y_r.reshape(rows.shape), y_g.reshape(rows.shape), gates, rows, mod, wts["w_branch_rwkv"],
        wts["w_branch_gla"], wts["w_out"], wts["norm2_g"], wts["router_cat"],
        tm=tm, n_tok=n_tok, row_off=row_off, shared=shared)
    states = (p_rwkv[:, t - 1:t, :][None], s_rwkv[None], s_gla[None])
    return x1, h2_all, route_all, states


def kernel(x_prompt, x_sample, c_prompt, c_sample, state_rwkv_shift, state_rwkv, state_gla, w_ada, b_ada, norm1_g, w_in, rwkv_mu, rwkv_w0, rwkv_decay_up, rwkv_a0, rwkv_aaa_up, rwkv_gate_up, rwkv_k_k, rwkv_k_a, rwkv_r_k, rwkv_ln_g, rwkv_ln_b, gla_alpha_up, gla_alpha_b, gla_norm_g, w_branch_rwkv, w_branch_gla, w_out, norm2_g, router_group, router_expert, expert_w_gate, expert_w_up, expert_w_down, final_norm_g):
    bp, tp = x_prompt.shape[:2]
    bs, ts = x_sample.shape[:2]
    w_in0 = w_in[0]
    g0 = RWKV_PROJ
    o3 = 2 * GLA_DK_TOTAL + GLA_DV_TOTAL
    o4 = o3 + GLA_GATE_RANK
    w_gla = w_in0[:, g0:g0 + GLA_PROJ]
    w_gla = jnp.concatenate([w_gla[:, :o3], w_gla[:, o4:], w_gla[:, o3:o4],
                             jnp.zeros((D_MODEL, GLA_PROJ_PAD - GLA_PROJ), F32)], axis=1)
    wts = dict(
        norm1_g=norm1_g[0].reshape(1, -1),
        w_in_rwkv=w_in0[:, :g0].astype(BF16),
        w_in_gla=w_gla.astype(BF16),
        w_in_gate=w_in0[:, g0 + GLA_PROJ:].astype(BF16),
        rwkv_mu=rwkv_mu[0], rwkv_w0=rwkv_w0[0], rwkv_decay_up=rwkv_decay_up[0], rwkv_a0=rwkv_a0[0],
        rwkv_aaa_up=rwkv_aaa_up[0], rwkv_gate_up=rwkv_gate_up[0], rwkv_k_k=rwkv_k_k[0],
        rwkv_k_a=rwkv_k_a[0], rwkv_r_k=rwkv_r_k[0], rwkv_ln_g=rwkv_ln_g[0], rwkv_ln_b=rwkv_ln_b[0],
        gla_alpha_up=jnp.pad(gla_alpha_up[0], ((0, LANES - GLA_GATE_RANK), (0, 0))),
        gla_alpha_b=gla_alpha_b[0], gla_norm_g=gla_norm_g[0],
        w_branch_rwkv=w_branch_rwkv[0].astype(BF16), w_branch_gla=w_branch_gla[0].astype(BF16),
        w_out=w_out[0].astype(BF16), norm2_g=norm2_g[0],
        router_cat=_split_hi_lo(jnp.pad(jnp.concatenate([router_expert[0], router_group[0]], axis=1),
                                        ((0, 0), (0, LANES - N_EXPERTS - N_GROUPS)))),
        expert_w_gate=expert_w_gate[0], expert_w_up=expert_w_up[0], expert_w_down=expert_w_down[0],
    )
    mod = _ada(jnp.concatenate([c_prompt, c_sample], axis=0), w_ada[0], b_ada[0])
    mod = mod.reshape(bp + bs, N_MOD, D_MODEL)
    n_p = bp * tp
    n_s = bs * ts
    n_tok = n_p + n_s
    mod_p = mod[:bp].reshape(bp, N_MOD, 1, D_MODEL)
    mod_s = jnp.repeat(mod[bp:].transpose(1, 0, 2), ts, axis=1)[None]

    dt = x_prompt.dtype
    x1_p, h2_all, route_all, states_p = _mixer(
        x_prompt, mod_p, jnp.zeros((bp, 1, RWKV_PROJ), dt),
        jnp.zeros((bp, RWKV_HEADS, RWKV_HEAD_DIM, RWKV_HEAD_DIM), state_rwkv.dtype),
        jnp.zeros((bp, GLA_HEADS, GLA_DK, GLA_DV), state_gla.dtype), wts, n_tok=n_tok, row_off=0)
    x1_s, h2_all, route_all, states_s = _mixer(
        x_sample, mod_s, state_rwkv_shift[0], state_rwkv[0], state_gla[0], wts,
        n_tok=n_tok, row_off=n_p, shared=(h2_all, route_all))

    expert_ids = route_all[:, :2].astype(jnp.int32)
    moe_out = _moe_grouped(h2_all, expert_ids, wts["expert_w_gate"], wts["expert_w_up"],
                           wts["expert_w_down"])
    y_p = _final(x1_p, moe_out, mod_p, final_norm_g, row_off=0, tm=min(tp, ROW_TILE))
    y_s = _final(x1_s, moe_out, mod_s, final_norm_g, row_off=n_p, tm=min(n_s, ROW_TILE))
    return (y_p, y_s.reshape(bs, ts, D_MODEL)) + states_p + states_s
```

```python
import functools
import math

import jax
import jax.numpy as jnp
from jax import lax
from jax.experimental import pallas as pl
from jax.experimental.pallas import tpu as pltpu

F32 = jnp.float32
BF16 = jnp.bfloat16

D_MODEL = 1024
N_MOD = 6
EPS = 1e-6
RWKV_HEAD_DIM = 64
RWKV_HEADS = 16
RWKV_PAIRS = RWKV_HEADS // 2
DECAY_LORA = 64
AAA_LORA = 64
GATE_LORA = 128
RWKV_PROJ = 3 * D_MODEL + DECAY_LORA + AAA_LORA + GATE_LORA
LN_X_EPS = 64e-5
GLA_HEADS = 4
GLA_DK = 128
GLA_DV = 256
GLA_DK_TOTAL = GLA_HEADS * GLA_DK
GLA_DV_TOTAL = GLA_HEADS * GLA_DV
GLA_GATE_RANK = 16
GLA_GATE_NORM = 16.0
GLA_PROJ = 2 * GLA_DK_TOTAL + 2 * GLA_DV_TOTAL + GLA_GATE_RANK
LANES = 128
GLA_PROJ_PAD = 2 * GLA_DK_TOTAL + 2 * GLA_DV_TOTAL + LANES
GATE_PROJ = 2 * D_MODEL
N_GROUPS = 4
EXPERTS_PER_GROUP = 8
N_EXPERTS = 32
EXPERT_FF = 512

CHUNK = 64
ROW_TILE = 512
WIDE_ROW_TILE = 1024
MOE_TILE = 192
DMA_GROUP = 8
MOE_ROW = D_MODEL + LANES
N_PAIR_CLASSES = N_EXPERTS * EXPERTS_PER_GROUP
RWKV_SEQS_PER_STEP = 4
GLA_SEQS_PER_STEP = 8
VMEM_LIMIT = 56 * 1024 * 1024


def _bdot(a, b):
    return jnp.dot(a.astype(BF16), b.astype(BF16), preferred_element_type=F32)


def _bdot_nt(a, b):
    return lax.dot_general(a.astype(BF16), b.astype(BF16), (((1,), (1,)), ((), ())),
                           preferred_element_type=F32)


def _bdot_tn(a, b):
    return jnp.dot(a.T.astype(BF16), b.astype(BF16), preferred_element_type=F32)


def _split3(x):
    h1 = x.astype(BF16)
    r1 = x - h1.astype(F32)
    h2 = r1.astype(BF16)
    h3 = (r1 - h2.astype(F32)).astype(BF16)
    return h1, h2, h3


def _dot3(a, b):
    a1, a2, _ = _split3(a)
    b1, b2, _ = _split3(b)
    return (jnp.dot(a1, b1, preferred_element_type=F32)
            + jnp.dot(a2, b1, preferred_element_type=F32)
            + jnp.dot(a1, b2, preferred_element_type=F32))


def _dot_exact_lhs(a_bf16, x):
    x1, x2, _ = _split3(x)
    return (jnp.dot(a_bf16, x1, preferred_element_type=F32)
            + jnp.dot(a_bf16, x2, preferred_element_type=F32))


def _split_hi_lo(w):
    hi, lo, _ = _split3(w)
    return jnp.concatenate([hi, lo], axis=1)


def _sigmoid(x):
    return 1.0 / (1.0 + jnp.exp(-x))


def _silu(x):
    return x * _sigmoid(x)


def _rms_scale(x):
    return x * lax.rsqrt(jnp.mean(x * x, axis=-1, keepdims=True) + EPS)


def _ada_kernel(c_ref, w_ref, b_ref, o_ref):
    o_ref[...] = _dot3(_silu(c_ref[...]), w_ref[...]) + b_ref[...]


def _ada(c, w_ada, b_ada):
    n_rows = c.shape[0]
    n_out = w_ada.shape[1]
    tn = 1536
    return pl.pallas_call(
        _ada_kernel,
        out_shape=jax.ShapeDtypeStruct((n_rows, n_out), F32),
        grid=(n_out // tn,),
        in_specs=[pl.BlockSpec((n_rows, D_MODEL), lambda j: (0, 0)),
                  pl.BlockSpec((D_MODEL, tn), lambda j: (0, j)),
                  pl.BlockSpec((1, tn), lambda j: (0, j))],
        out_specs=pl.BlockSpec((n_rows, tn), lambda j: (0, j)),
        compiler_params=pltpu.CompilerParams(dimension_semantics=("parallel",),
                                             vmem_limit_bytes=VMEM_LIMIT),
        name="ada",
    )(c, w_ada, b_ada.reshape(1, n_out))


def _mod_spec(mod, tm, seq_tile):
    if mod.shape[2] == 1:
        return pl.BlockSpec((1, N_MOD, 1, D_MODEL), lambda *g: (seq_tile(*g)[0], 0, 0, 0))
    return pl.BlockSpec((1, N_MOD, tm, D_MODEL), lambda *g: (seq_tile(*g)[0], 0, seq_tile(*g)[1], 0))


def _norm_proj_kernel(x_ref, mod_ref, g_ref, w_ref, o_ref, *, gate):
    mod = mod_ref[0]
    h = _rms_scale(x_ref[0]) * g_ref[...] * (1.0 + mod[1]) + mod[0]
    out = _bdot(h, w_ref[...])
    o_ref[0] = _sigmoid(out) if gate else out


def _norm_proj(x, mod, g, w_bf16, *, gate, tm):
    b, t, _ = x.shape
    n = w_bf16.shape[1]
    return pl.pallas_call(
        functools.partial(_norm_proj_kernel, gate=gate),
        out_shape=jax.ShapeDtypeStruct((b, t, n), F32),
        grid=(b, t // tm),
        in_specs=[pl.BlockSpec((1, tm, D_MODEL), lambda i, j: (i, j, 0)),
                  _mod_spec(mod, tm, lambda i, j: (i, j)),
                  pl.BlockSpec((1, D_MODEL), lambda i, j: (0, 0)),
                  pl.BlockSpec((D_MODEL, n), lambda i, j: (0, 0))],
        out_specs=pl.BlockSpec((1, tm, n), lambda i, j: (i, j, 0)),
        compiler_params=pltpu.CompilerParams(dimension_semantics=("parallel", "parallel"),
                                             vmem_limit_bytes=VMEM_LIMIT),
        name="norm_proj_gate" if gate else "norm_proj",
    )(x, mod, g, w_bf16)


def _blockdiag(x, lo_mask):
    return jnp.concatenate([jnp.where(lo_mask, x, 0.0), jnp.where(lo_mask, 0.0, x)], axis=0)


def _rwkv_kernel(p_ref, shift_ref, s0_ref, mu_ref, w0_ref, dup_ref, a0_ref, aup_ref, gup_ref,
                 kk_ref, ka_ref, rk_ref, lng_ref, lnb_ref, y_ref, sout_ref, s_scr, carry_scr,
                 *, nb):
    c = pl.program_id(1)
    n_chunks = pl.num_programs(1)
    C = CHUNK
    N = RWKV_HEAD_DIM
    t_blk = p_ref.shape[1]

    @pl.when(c == 0)
    def _():
        for bl in range(nb):
            for pr in range(RWKV_PAIRS):
                s_scr[bl, pr] = jnp.concatenate([s0_ref[bl, 2 * pr], s0_ref[bl, 2 * pr + 1]], axis=1)
        carry_scr[...] = shift_ref[...]

    row = lax.broadcasted_iota(jnp.int32, (C, LANES), 0)
    lane = lax.broadcasted_iota(jnp.int32, (C, LANES), 1)
    lane_t = jnp.where(lane < N, lane, lane - N)
    lo = lane < N
    strict = lane_t < row
    incl = lane_t <= row
    eye = (lane_t == row).astype(F32)
    valid = row < t_blk
    tri = (lax.broadcasted_iota(jnp.int32, (C, C), 1)
           <= lax.broadcasted_iota(jnp.int32, (C, C), 0)).astype(BF16)
    sq_r = lax.broadcasted_iota(jnp.int32, (LANES, LANES), 0)
    sq_c = lax.broadcasted_iota(jnp.int32, (LANES, LANES), 1)
    same_head = (sq_r < N) == (sq_c < N)
    diag = sq_r == sq_c
    row_w = lax.broadcasted_iota(jnp.int32, (C, RWKV_PROJ), 0)
    o3 = 3 * D_MODEL

    def hsum(v):
        s_lo = jnp.sum(jnp.where(lo, v, 0.0), axis=1, keepdims=True)
        s_hi = jnp.sum(jnp.where(lo, 0.0, v), axis=1, keepdims=True)
        return jnp.where(lo, s_lo, s_hi)

    def bd(v):
        return _blockdiag(v, lo)

    def cat0(*a):
        return jnp.concatenate(a, axis=0)

    def cat1(*a):
        return jnp.concatenate(a, axis=1)

    valid_w = lax.broadcasted_iota(jnp.int32, (C, D_MODEL), 0) < t_blk
    xs, logw_all, cum_all, asig_all, g_all = [], [], [], [], []
    for bl in range(nb):
        p = p_ref[bl]
        carry_new = p[t_blk - 1:t_blk, :]
        if t_blk < C:
            p = jnp.concatenate([p, jnp.zeros((C - t_blk, RWKV_PROJ), F32)], axis=0)
        prev = jnp.where(row_w == 0, carry_scr[bl], pltpu.roll(p, 1, axis=0))
        carry_scr[bl] = carry_new
        x = p + (prev - p) * mu_ref[...]
        xw = x[:, o3:o3 + DECAY_LORA]
        xa = x[:, o3 + DECAY_LORA:o3 + DECAY_LORA + AAA_LORA]
        xg = x[:, o3 + DECAY_LORA + AAA_LORA:]
        z = w0_ref[...] + _bdot(jnp.tanh(xw), dup_ref[...])
        logw_full = -math.exp(-0.5) * _sigmoid(z)
        if t_blk < C:
            logw_full = jnp.where(valid_w, logw_full, 0.0)
        xs.append(x)
        logw_all.append(logw_full)
        cum_all.append(_dot_exact_lhs(tri, logw_full))
        asig_all.append(_sigmoid(a0_ref[...] + _bdot(xa, aup_ref[...])))
        g_all.append(_bdot(_sigmoid(xg), gup_ref[...]))

    units = [(bl, pr) for bl in range(nb) for pr in range(RWKV_PAIRS)]
    U = range(len(units))
    cols = [slice(pr * LANES, (pr + 1) * LANES) for _, pr in units]
    r = [xs[bl][:, pr * LANES:(pr + 1) * LANES] for bl, pr in units]
    k = [xs[bl][:, D_MODEL + pr * LANES:D_MODEL + (pr + 1) * LANES] for bl, pr in units]
    v = [xs[bl][:, 2 * D_MODEL + pr * LANES:2 * D_MODEL + (pr + 1) * LANES] for bl, pr in units]
    logw = [logw_all[bl][:, cols[u]] for u, (bl, _) in enumerate(units)]
    asig = [asig_all[bl][:, cols[u]] for u, (bl, _) in enumerate(units)]
    kk = [k[u] * kk_ref[:, cols[u]] for u in U]
    k2 = [k[u] * (1.0 + (asig[u] - 1.0) * ka_ref[:, cols[u]]) for u in U]
    ss = [hsum(kk[u] * kk[u]) for u in U]
    cum = [cum_all[bl][:, cols[u]] for u, (bl, _) in enumerate(units)]
    bsum = [hsum(r[u] * k2[u] * rk_ref[:, cols[u]]) for u in U]
    kk = [kk[u] * lax.rsqrt(jnp.maximum(ss[u], 1e-24)) for u in U]
    bonus = [bsum[u] * v[u] for u in U]
    if t_blk < C:
        kk = [jnp.where(valid, a, 0.0) for a in kk]
        k2 = [jnp.where(valid, a, 0.0) for a in k2]
        v = [jnp.where(valid, a, 0.0) for a in v]
        r = [jnp.where(valid, a, 0.0) for a in r]
    b_vec = [kk[u] * asig[u] for u in U]
    clast = [cum[u][C - 1:C, :] for u in U]
    e_inv = [jnp.exp(-cum[u]) for u in U]
    e_tail = [jnp.exp(clast[u] - cum[u]) for u in U]
    aq = [-kk[u] * jnp.exp(cum[u] - logw[u]) for u in U]
    rq = [r[u] * jnp.exp(cum[u]) for u in U]
    bk = [b_vec[u] * e_inv[u] for u in U]
    kx = [k2[u] * e_inv[u] for u in U]
    bb = [b_vec[u] * e_tail[u] for u in U]
    kb = [k2[u] * e_tail[u] for u in U]

    g4 = [_bdot_nt(cat0(aq[u], rq[u]), cat0(bd(bk[u]), bd(kx[u]))) for u in U]
    m_ab = [jnp.where(strict, g4[u][:C, :LANES], 0.0) for u in U]
    m_ak = [jnp.where(strict, g4[u][:C, LANES:], 0.0) for u in U]
    m_rb = [jnp.where(incl, g4[u][C:, :LANES], 0.0) for u in U]
    m_rk = [jnp.where(incl, g4[u][C:, LANES:], 0.0) for u in U]

    t_inv = [eye + m_ab[u] for u in U]
    l_pow = [_bdot(m_ab[u], bd(m_ab[u])) for u in U]
    mv = [_bdot(cat0(m_ak[u], m_rk[u]), bd(v[u])) for u in U]
    n_steps = C.bit_length() - 1
    for step in range(1, n_steps):
        if step < n_steps - 1:
            both = [_bdot(l_pow[u], cat1(bd(t_inv[u]), bd(l_pow[u]))) for u in U]
            t_inv = [t_inv[u] + both[u][:, :LANES] for u in U]
            l_pow = [both[u][:, LANES:] for u in U]
        else:
            t_inv = [t_inv[u] + _bdot(l_pow[u], bd(t_inv[u])) for u in U]

    au = [_bdot(t_inv[u], cat1(bd(aq[u]), bd(mv[u][:C]))) for u in U]
    a_bar = [au[u][:, :LANES] for u in U]
    u_bar = [au[u][:, LANES:] for u in U]
    mau = [_bdot(m_rb[u], cat1(bd(a_bar[u]), bd(u_bar[u]))) for u in U]
    p_full = [_bdot_tn(a_bar[u], bb[u]) for u in U]
    q_full = [_bdot_tn(cat0(u_bar[u], v[u]), cat0(bb[u], kb[u])) for u in U]
    r_til = [rq[u] + mau[u][:, :LANES] for u in U]
    y_loc = [mau[u][:, LANES:] + mv[u][C:] for u in U]
    p_bd = [jnp.where(same_head, p_full[u], 0.0) + jnp.where(diag, jnp.exp(clast[u]), 0.0) for u in U]
    q_pk = [jnp.where(lo[:N], q_full[u][:N], q_full[u][N:]) for u in U]

    s_old = [s_scr[bl, pr] for bl, pr in units]
    y = [_bdot_nt(r_til[u], _blockdiag(s_old[u], lo[:N])) + y_loc[u] for u in U]
    s_new = [_bdot(s_old[u], p_bd[u]) + q_pk[u] for u in U]
    for u, (bl, pr) in enumerate(units):
        s_scr[bl, pr] = s_new[u]

    mu_y = [hsum(y[u]) * (1.0 / N) for u in U]
    yc = [y[u] - mu_y[u] for u in U]
    var = [hsum(yc[u] * yc[u]) * (1.0 / N) for u in U]
    for u, (bl, pr) in enumerate(units):
        yn = yc[u] * lax.rsqrt(var[u] + LN_X_EPS) * lng_ref[:, cols[u]] + lnb_ref[:, cols[u]]
        y_ref[bl, :, cols[u]] = ((yn + bonus[u]) * g_all[bl][:, cols[u]])[:t_blk]

    @pl.when(c == n_chunks - 1)
    def _():
        for bl in range(nb):
            for pr in range(RWKV_PAIRS):
                sout_ref[bl, 2 * pr] = s_scr[bl, pr][:, :N]
                sout_ref[bl, 2 * pr + 1] = s_scr[bl, pr][:, N:]


def _rwkv(p_rwkv, shift_prev, s0, mu, w0, decay_up, a0, aaa_up, gate_up, k_k, k_a, r_k,
          ln_g, ln_b, *, nb):
    b, t, _ = p_rwkv.shape
    nb = math.gcd(b, nb)
    t_blk = min(t, CHUNK)
    row = lambda a: a.reshape(1, -1)
    full = lambda shape: pl.BlockSpec(shape, lambda i, j: (0,) * len(shape))
    state_spec = pl.BlockSpec((nb, RWKV_HEADS, RWKV_HEAD_DIM, RWKV_HEAD_DIM), lambda i, j: (i, 0, 0, 0))
    return pl.pallas_call(
        functools.partial(_rwkv_kernel, nb=nb),
        out_shape=(jax.ShapeDtypeStruct((b, t, D_MODEL), F32),
                   jax.ShapeDtypeStruct((b, RWKV_HEADS, RWKV_HEAD_DIM, RWKV_HEAD_DIM), F32)),
        grid=(b // nb, t // t_blk),
        in_specs=[pl.BlockSpec((nb, t_blk, RWKV_PROJ), lambda i, j: (i, j, 0)),
                  pl.BlockSpec((nb, 1, RWKV_PROJ), lambda i, j: (i, 0, 0)),
                  state_spec,
                  full((1, RWKV_PROJ)), full((1, D_MODEL)), full((DECAY_LORA, D_MODEL)),
                  full((1, D_MODEL)), full((AAA_LORA, D_MODEL)), full((GATE_LORA, D_MODEL)),
                  full((1, D_MODEL)), full((1, D_MODEL)), full((1, D_MODEL)),
                  full((1, D_MODEL)), full((1, D_MODEL))],
        out_specs=(pl.BlockSpec((nb, t_blk, D_MODEL), lambda i, j: (i, j, 0)), state_spec),
        scratch_shapes=[pltpu.VMEM((nb, RWKV_PAIRS, RWKV_HEAD_DIM, LANES), F32),
                        pltpu.VMEM((nb, 1, RWKV_PROJ), F32)],
        compiler_params=pltpu.CompilerParams(dimension_semantics=("parallel", "arbitrary"),
                                             vmem_limit_bytes=VMEM_LIMIT),
        name="rwkv_scan",
    )(p_rwkv, shift_prev, s0, row(mu), row(w0), decay_up, row(a0), aaa_up, gate_up,
      row(k_k), row(k_a), row(r_k), row(ln_g), row(ln_b))


def _gla_kernel(p_ref, s0_ref, aup_ref, ab_ref, ng_ref, y_ref, sout_ref, s_scr, *, nb):
    c = pl.program_id(1)
    n_chunks = pl.num_programs(1)
    C = CHUNK
    t_blk = p_ref.shape[1]
    units = [(bl, h) for bl in range(nb) for h in range(GLA_HEADS)]
    U = range(len(units))

    @pl.when(c == 0)
    def _():
        for bl, h in units:
            s_scr[bl, h] = s0_ref[bl, h].T

    o_k = GLA_DK_TOTAL
    o_v = 2 * GLA_DK_TOTAL
    o_r = o_v + GLA_DV_TOTAL
    o_a = o_r + GLA_DV_TOTAL
    tri = (lax.broadcasted_iota(jnp.int32, (C, C), 1)
           <= lax.broadcasted_iota(jnp.int32, (C, C), 0))
    tri_bf = tri.astype(BF16)
    valid_g = lax.broadcasted_iota(jnp.int32, (C, GLA_DK_TOTAL), 0) < t_blk
    valid_k = lax.broadcasted_iota(jnp.int32, (C, GLA_DK), 0) < t_blk
    valid_v = lax.broadcasted_iota(jnp.int32, (C, GLA_DV), 0) < t_blk

    ps, cums = [], []
    for bl in range(nb):
        p = p_ref[bl]
        if t_blk < C:
            p = jnp.concatenate([p, jnp.zeros((C - t_blk, GLA_PROJ_PAD), F32)], axis=0)
        logit = _dot3(p[:, o_a:], aup_ref[...]) + ab_ref[...]
        g = (jnp.minimum(logit, 0.0) - jnp.log(1.0 + jnp.exp(-jnp.abs(logit)))) * (1.0 / GLA_GATE_NORM)
        if t_blk < C:
            g = jnp.where(valid_g, g, 0.0)
        ps.append(p)
        cums.append(_dot_exact_lhs(tri_bf, g))

    q = [ps[bl][:, h * GLA_DK:(h + 1) * GLA_DK] * (GLA_DK ** -0.5) for bl, h in units]
    k = [ps[bl][:, o_k + h * GLA_DK:o_k + (h + 1) * GLA_DK] for bl, h in units]
    v = [ps[bl][:, o_v + h * GLA_DV:o_v + (h + 1) * GLA_DV] for bl, h in units]
    if t_blk < C:
        k = [jnp.where(valid_k, a, 0.0) for a in k]
        v = [jnp.where(valid_v, a, 0.0) for a in v]
    b = [cums[bl][:, h * GLA_DK:(h + 1) * GLA_DK] for bl, h in units]
    clast = [b[u][C - 1:C, :] for u in U]
    cmid = [b[u][C // 2 - 1:C // 2, :] for u in U]
    att = [_bdot_nt(q[u] * jnp.exp(b[u] - cmid[u]), k[u] * jnp.exp(cmid[u] - b[u])) for u in U]
    att = [jnp.where(tri, a, 0.0) for a in att]
    s_t = [s_scr[bl, h] for bl, h in units]
    o = [_bdot(att[u], v[u]) + _bdot_nt(q[u] * jnp.exp(b[u]), s_t[u]) for u in U]
    s_new = [s_t[u] * jnp.exp(clast[u]) + _bdot_tn(v[u], k[u] * jnp.exp(clast[u] - b[u])) for u in U]
    for u, (bl, h) in enumerate(units):
        s_scr[bl, h] = s_new[u]
        r = ps[bl][:, o_r + h * GLA_DV:o_r + (h + 1) * GLA_DV]
        on = o[u] * lax.rsqrt(jnp.mean(o[u] * o[u], axis=-1, keepdims=True) + EPS) * ng_ref[...]
        y_ref[bl, :, h * GLA_DV:(h + 1) * GLA_DV] = (on * _silu(r))[:t_blk]

    @pl.when(c == n_chunks - 1)
    def _():
        for bl, h in units:
            sout_ref[bl, h] = s_scr[bl, h].T


def _gla(p_gla, s0, alpha_up_pad, alpha_b, norm_g, *, nb):
    b, t, _ = p_gla.shape
    nb = math.gcd(b, nb)
    t_blk = min(t, CHUNK)
    full = lambda shape: pl.BlockSpec(shape, lambda i, j: (0,) * len(shape))
    state_spec = pl.BlockSpec((nb, GLA_HEADS, GLA_DK, GLA_DV), lambda i, j: (i, 0, 0, 0))
    return pl.pallas_call(
        functools.partial(_gla_kernel, nb=nb),
        out_shape=(jax.ShapeDtypeStruct((b, t, GLA_DV_TOTAL), F32),
                   jax.ShapeDtypeStruct((b, GLA_HEADS, GLA_DK, GLA_DV), F32)),
        grid=(b // nb, t // t_blk),
        in_specs=[pl.BlockSpec((nb, t_blk, GLA_PROJ_PAD), lambda i, j: (i, j, 0)),
                  state_spec,
                  full((LANES, GLA_DK_TOTAL)), full((1, GLA_DK_TOTAL)), full((1, GLA_DV))],
        out_specs=(pl.BlockSpec((nb, t_blk, GLA_DV_TOTAL), lambda i, j: (i, j, 0)), state_spec),
        scratch_shapes=[pltpu.VMEM((nb, GLA_HEADS, GLA_DV, GLA_DK), F32)],
        compiler_params=pltpu.CompilerParams(dimension_semantics=("parallel", "arbitrary"),
                                             vmem_limit_bytes=VMEM_LIMIT),
        name="gla_scan",
    )(p_gla, s0, alpha_up_pad, alpha_b.reshape(1, -1), norm_g.reshape(1, -1))


def _merge_body(yr_ref, yg_ref, gate_ref, x_ref, mod_ref, wbr_ref, wbg_ref, wout_ref, g2_ref,
                router_ref, x1_ref, h2_ref, route_ref):
    mod = mod_ref[0]
    gates = gate_ref[0]
    merged = (gates[:, :D_MODEL] * _bdot(yr_ref[0], wbr_ref[...])
              + gates[:, D_MODEL:] * _bdot(yg_ref[0], wbg_ref[...]))
    x1 = x_ref[0] + mod[2] * _bdot(merged, wout_ref[...])
    x1_ref[0] = x1
    h2 = _rms_scale(x1) * g2_ref[...] * (1.0 + mod[4]) + mod[3]
    h2_ref[:, :D_MODEL] = h2

    tm = h2.shape[0]
    h_hi, h_lo, _ = _split3(h2)
    prod = jnp.dot(jnp.concatenate([h_hi, h_lo], axis=0), router_ref[...], preferred_element_type=F32)
    logits = prod[:tm, :LANES] + prod[tm:, :LANES] + prod[:tm, LANES:]
    lane = lax.broadcasted_iota(jnp.int32, (tm, LANES), 1)
    neg = -jnp.inf
    is_group = (lane >= N_EXPERTS) & (lane < N_EXPERTS + N_GROUPS)
    gl = jnp.where(is_group, logits, neg)
    gmax = jnp.max(gl, axis=1, keepdims=True)
    g_idx = jnp.min(jnp.where(gl == gmax, lane, LANES), axis=1, keepdims=True) - N_EXPERTS
    p_g = 1.0 / jnp.sum(jnp.exp(gl - gmax), axis=1, keepdims=True)
    in_group = (lane >= g_idx * EXPERTS_PER_GROUP) & (lane < (g_idx + 1) * EXPERTS_PER_GROUP)
    el = jnp.where(in_group, logits, neg)
    v1 = jnp.max(el, axis=1, keepdims=True)
    i1 = jnp.min(jnp.where(el == v1, lane, LANES), axis=1, keepdims=True)
    el2 = jnp.where(lane == i1, neg, el)
    v2 = jnp.max(el2, axis=1, keepdims=True)
    i2 = jnp.min(jnp.where(el2 == v2, lane, LANES), axis=1, keepdims=True)
    e21 = jnp.exp(v2 - v1)
    w1 = p_g / (1.0 + e21)
    route = (jnp.where(lane == 0, i1.astype(F32), 0.0) + jnp.where(lane == 1, i2.astype(F32), 0.0)
             + jnp.where(lane == 2, w1, 0.0) + jnp.where(lane == 3, w1 * e21, 0.0))
    route_ref[...] = route
    h2_ref[:, D_MODEL:] = route


def _merge_first_kernel(*refs, n_own):
    step = pl.program_id(0)

    @pl.when(step < n_own)
    def _():
        _merge_body(*refs)

    @pl.when(step >= n_own)
    def _():
        refs[-2][...] = jnp.zeros(refs[-2].shape, F32)
        refs[-1][...] = jnp.zeros(refs[-1].shape, F32)


def _merge_second_kernel(*refs):
    n_in = 10
    _merge_body(*refs[:n_in], *refs[n_in + 2:])


def _merge(y_r, y_g, gates, x, mod, wbr, wbg, wout, norm2_g, router_cat, *, tm, n_tok, row_off, shared=None):
    b, t, _ = x.shape
    per_seq = t // tm
    n_own = b * per_seq
    first = shared is None
    n_steps = n_tok // tm if first else n_own
    assert row_off % tm == 0 and n_tok % tm == 0 and (row_off == 0 or not first)
    blk0 = row_off // tm
    own = lambda s: jnp.minimum(s, n_own - 1)
    full = lambda shape: pl.BlockSpec(shape, lambda s: (0,) * len(shape))
    tile = lambda n: pl.BlockSpec((1, tm, n), lambda s: (own(s) // per_seq, own(s) % per_seq, 0))
    flat = lambda n: pl.BlockSpec((tm, n), lambda s: (blk0 + s, 0))
    in_specs = [tile(D_MODEL), tile(D_MODEL), tile(GATE_PROJ), tile(D_MODEL),
                _mod_spec(mod, tm, lambda s: (own(s) // per_seq, own(s) % per_seq)),
                full((D_MODEL, D_MODEL)), full((D_MODEL, D_MODEL)), full((D_MODEL, D_MODEL)),
                full((1, D_MODEL)), full((D_MODEL, 2 * LANES))]
    args = [y_r, y_g, gates, x, mod, wbr, wbg, wout, norm2_g.reshape(1, -1), router_cat]
    if not first:
        in_specs += [pl.BlockSpec(memory_space=pl.ANY)] * 2
        args += list(shared)
    return pl.pallas_call(
        functools.partial(_merge_first_kernel, n_own=n_own) if first else _merge_second_kernel,
        out_shape=(jax.ShapeDtypeStruct((b, t, D_MODEL), F32),
                   jax.ShapeDtypeStruct((n_tok, MOE_ROW), F32),
                   jax.ShapeDtypeStruct((n_tok, LANES), F32)),
        grid=(n_steps,),
        in_specs=in_specs,
        out_specs=(tile(D_MODEL), flat(MOE_ROW), flat(LANES)),
        input_output_aliases={} if first else {10: 1, 11: 2},
        compiler_params=pltpu.CompilerParams(dimension_semantics=("arbitrary",),
                                             vmem_limit_bytes=VMEM_LIMIT),
        name="merge_router",
    )(*args)


def _moe_plan(expert_ids, n_tok):
    tm = MOE_TILE
    n_tiles = -(-n_tok // tm) + N_GROUPS * (EXPERTS_PER_GROUP * (EXPERTS_PER_GROUP - 1) // 2)
    n_slots = n_tiles * tm
    n_fill = n_slots - n_tok
    row_bits = 15
    assert max(n_tok, n_fill) <= 1 << row_bits and N_PAIR_CLASSES << (row_bits + 1) <= 1 << 24
    e_lo = jnp.minimum(expert_ids[:, 0], expert_ids[:, 1])
    e_hi = jnp.maximum(expert_ids[:, 0], expert_ids[:, 1])
    cls = e_lo * EXPERTS_PER_GROUP + e_hi % EXPERTS_PER_GROUP
    c_iota = jnp.arange(N_PAIR_CLASSES, dtype=jnp.int32)
    counts = jnp.sum((cls[None, :] == c_iota[:, None]).astype(jnp.int32), axis=1)
    tiles_c = (counts + tm - 1) // tm
    n_used = jnp.sum(tiles_c)
    pad_end = jnp.cumsum(tiles_c * tm - counts)
    f_iota = jnp.arange(n_fill, dtype=jnp.int32)
    f_cls = jnp.minimum(jnp.sum((f_iota[:, None] >= pad_end[None, :]).astype(jnp.int32), axis=1), N_PAIR_CLASSES - 1)
    keys = jnp.concatenate([(cls << (row_bits + 1)) + jnp.arange(n_tok, dtype=jnp.int32),
                            (f_cls << (row_bits + 1)) + (1 << row_bits) + f_iota])
    neg_sorted, _ = lax.top_k(-keys.astype(F32), n_slots)
    slots = (-neg_sorted).astype(jnp.int32).reshape(n_tiles, tm)
    valid = (slots & (1 << row_bits)) == 0
    tok = jnp.where(valid, slots & ((1 << row_bits) - 1), 0)
    t_idx = jnp.arange(n_tiles, dtype=jnp.int32)
    j_idx = jnp.arange(tm, dtype=jnp.int32)[None, :]
    dst = jnp.where(valid, tok, n_tok + (t_idx[:, None] % 2) * DMA_GROUP + j_idx % DMA_GROUP)
    n_valid = jnp.sum(valid.astype(jnp.int32), axis=1)
    n_groups = jnp.where(t_idx < n_used, (n_valid + DMA_GROUP - 1) // DMA_GROUP, 0)
    tile_c = slots[:, 0] >> (row_bits + 1)
    last_c = lax.dynamic_slice(tile_c, (jnp.maximum(n_used - 1, 0),), (1,))
    tile_c = jnp.where(t_idx < n_used, tile_c, last_c)
    tile_lo = tile_c // EXPERTS_PER_GROUP
    tile_hi = (tile_c // (EXPERTS_PER_GROUP * EXPERTS_PER_GROUP)) * EXPERTS_PER_GROUP + tile_c % EXPERTS_PER_GROUP
    return (tile_lo, tile_hi, n_groups.astype(jnp.int32), n_used.reshape(1).astype(jnp.int32),
            tok.reshape(n_tiles, 1, tm), dst.reshape(n_tiles, 1, tm))


def _moe_pair_kernel(lo_ref, hi_ref, ng_ref, n_used_ref, tok_ref, tok_next_ref, dst_ref, h2_hbm,
                     wg_hbm, wu_hbm, wd_hbm, out_hbm,
                     xbuf, obuf, w_res, w_stage, gsem, ssem, wsem):
    t = pl.program_id(0)
    n_t = pl.num_programs(0)
    tm = MOE_TILE
    slot = lax.rem(t, 2)
    n_used = n_used_ref[0]

    def gather(idx_ref, s, n_groups):
        def body(g, carry):
            for u in range(DMA_GROUP):
                pltpu.make_async_copy(h2_hbm.at[pl.ds(idx_ref[0, 0, g * DMA_GROUP + u], 1)],
                                      xbuf.at[s, g, pl.ds(u, 1)], gsem.at[s]).start()
            return carry
        lax.fori_loop(0, n_groups, body, 0)

    def wait_gather(s, n_groups):
        def body(g, carry):
            pltpu.make_async_copy(h2_hbm.at[pl.ds(0, DMA_GROUP)], xbuf.at[s, 0], gsem.at[s]).wait()
            return carry
        lax.fori_loop(0, n_groups, body, 0)

    def wait_scatter(s, n_groups):
        def body(g, carry):
            pltpu.make_async_copy(obuf.at[s, 0], out_hbm.at[pl.ds(0, DMA_GROUP)], ssem.at[s]).wait()
            return carry
        lax.fori_loop(0, n_groups, body, 0)

    @pl.when(t == 0)
    def _():
        xbuf[...] = jnp.zeros(xbuf.shape, F32)
        obuf[0, 0] = jnp.zeros(obuf.shape[2:], F32)
        n_rows = out_hbm.shape[0]
        for half in range(2):
            fill = pltpu.make_async_copy(obuf.at[0, 0], out_hbm.at[pl.ds(n_rows - (2 - half) * DMA_GROUP, DMA_GROUP)],
                                         ssem.at[0])
            fill.start()
            fill.wait()
        gather(tok_ref, 0, ng_ref[0])

    @pl.when((t >= 2) & (t - 2 < n_used))
    def _():
        wait_scatter(slot, ng_ref[jnp.maximum(t - 2, 0)])

    @pl.when(t < n_used)
    def _():
        n_here = ng_ref[t]
        wait_gather(slot, n_here)

        @pl.when(t + 1 < n_used)
        def _():
            gather(tok_next_ref, 1 - slot, ng_ref[jnp.minimum(t + 1, n_t - 1)])

        group = lo_ref[t] // EXPERTS_PER_GROUP

        @pl.when((t == 0) | (group != lo_ref[jnp.maximum(t - 1, 0)] // EXPERTS_PER_GROUP))
        def _():
            def fetch(e, buf):
                return [pltpu.make_async_copy(src.at[group * EXPERTS_PER_GROUP + e], w_stage[m].at[buf], wsem.at[buf, m])
                        for m, src in enumerate((wg_hbm, wu_hbm, wd_hbm))]

            for cp in fetch(0, 0):
                cp.start()
            for e in range(EXPERTS_PER_GROUP):
                if e + 1 < EXPERTS_PER_GROUP:
                    for cp in fetch(e + 1, (e + 1) % 2):
                        cp.start()
                for m, cp in enumerate(fetch(e, e % 2)):
                    cp.wait()
                    w_res[m][e] = w_stage[m][e % 2].astype(BF16)

        xe = xbuf[slot].reshape(tm, MOE_ROW)
        x = xe[:, :D_MODEL].astype(BF16)
        route = xe[:, D_MODEL:]
        lane = lax.broadcasted_iota(jnp.int32, route.shape, 1)
        pick = lambda k: jnp.sum(jnp.where(lane == k, route, 0.0), axis=1, keepdims=True)
        i1, i2, w1, w2 = pick(0), pick(1), pick(2), pick(3)
        w_lo = jnp.where(i1 < i2, w1, w2)
        w_hi = jnp.where(i1 < i2, w2, w1)
        acc = None
        for e_ref, wt in ((lo_ref, w_lo), (hi_ref, w_hi)):
            e_local = lax.rem(e_ref[t], EXPERTS_PER_GROUP)
            hid = _silu(jnp.dot(x, w_res[0][e_local], preferred_element_type=F32)) * jnp.dot(
                x, w_res[1][e_local], preferred_element_type=F32)
            part = wt * _bdot(hid, w_res[2][e_local])
            acc = part if acc is None else acc + part
        obuf[slot] = acc.reshape(obuf.shape[1:])

        def body(g, carry):
            for u in range(DMA_GROUP):
                pltpu.make_async_copy(obuf.at[slot, g, pl.ds(u, 1)],
                                      out_hbm.at[pl.ds(dst_ref[0, 0, g * DMA_GROUP + u], 1)], ssem.at[slot]).start()
            return carry
        lax.fori_loop(0, n_here, body, 0)

    @pl.when(t == n_t - 1)
    def _():
        @pl.when((t >= 1) & (t - 1 < n_used))
        def _():
            wait_scatter(1 - slot, ng_ref[jnp.maximum(t - 1, 0)])

        @pl.when(t < n_used)
        def _():
            wait_scatter(slot, ng_ref[t])


def _moe_grouped(h2_ext, expert_ids, wg, wu, wd):
    return _moe_call(h2_ext, *_moe_plan(expert_ids, h2_ext.shape[0]), wg, wu, wd)


def _moe_call(h2_ext, tile_lo, tile_hi, n_groups, n_used, tok, dst, wg, wu, wd):
    n_tok = h2_ext.shape[0]
    tm = MOE_TILE
    n_tiles = tile_lo.shape[0]
    smem_tile = lambda f: pl.BlockSpec((1, 1, tm), f, memory_space=pltpu.SMEM)
    up, down = (D_MODEL, EXPERT_FF), (EXPERT_FF, D_MODEL)
    any_spec = pl.BlockSpec(memory_space=pl.ANY)
    grid_spec = pltpu.PrefetchScalarGridSpec(
        num_scalar_prefetch=4,
        grid=(n_tiles,),
        in_specs=[smem_tile(lambda t, *_: (t, 0, 0)),
                  smem_tile(lambda t, *_: (jnp.minimum(t + 1, n_tiles - 1), 0, 0)),
                  smem_tile(lambda t, *_: (t, 0, 0)),
                  any_spec, any_spec, any_spec, any_spec],
        out_specs=pl.BlockSpec(memory_space=pl.ANY),
        scratch_shapes=[pltpu.VMEM((2, tm // DMA_GROUP, DMA_GROUP, MOE_ROW), F32),
                        pltpu.VMEM((2, tm // DMA_GROUP, DMA_GROUP, D_MODEL), F32),
                        [pltpu.VMEM((EXPERTS_PER_GROUP,) + shape, BF16) for shape in (up, up, down)],
                        [pltpu.VMEM((2,) + shape, F32) for shape in (up, up, down)],
                        pltpu.SemaphoreType.DMA((2,)), pltpu.SemaphoreType.DMA((2,)),
                        pltpu.SemaphoreType.DMA((2, 3))],
    )
    return pl.pallas_call(
        _moe_pair_kernel,
        out_shape=jax.ShapeDtypeStruct((n_tok + 2 * DMA_GROUP, D_MODEL), F32),
        grid_spec=grid_spec,
        compiler_params=pltpu.CompilerParams(dimension_semantics=("arbitrary",),
                                             vmem_limit_bytes=VMEM_LIMIT),
        name="moe_pairs",
    )(tile_lo, tile_hi, n_groups, n_used, tok, tok, dst, h2_ext, wg, wu, wd)


def _final_kernel(x1_ref, moe_ref, mod_ref, fg_ref, y_ref):
    x2 = x1_ref[0] + mod_ref[0][5] * moe_ref[...]
    y_ref[0] = _rms_scale(x2) * fg_ref[...]


def _final(x1, moe_out, mod, final_g, *, row_off, tm):
    b, t, _ = x1.shape
    blk0 = row_off // tm
    per_seq = t // tm
    return pl.pallas_call(
        _final_kernel,
        out_shape=jax.ShapeDtypeStruct((b, t, D_MODEL), F32),
        grid=(b, per_seq),
        in_specs=[pl.BlockSpec((1, tm, D_MODEL), lambda i, j: (i, j, 0)),
                  pl.BlockSpec((tm, D_MODEL), lambda i, j: (blk0 + i * per_seq + j, 0)),
                  _mod_spec(mod, tm, lambda i, j: (i, j)),
                  pl.BlockSpec((1, D_MODEL), lambda i, j: (0, 0))],
        out_specs=pl.BlockSpec((1, tm, D_MODEL), lambda i, j: (i, j, 0)),
        compiler_params=pltpu.CompilerParams(dimension_semantics=("parallel", "parallel"),
                                             vmem_limit_bytes=VMEM_LIMIT),
        name="moe_combine_norm",
    )(x1, moe_out, mod, final_g.reshape(1, -1))


def _mixer(x, mod, shift0, s_rwkv0, s_gla0, wts, *, n_tok, row_off, shared=None):
    b, t, _ = x.shape
    per_token = mod.shape[2] > 1
    rows = x.reshape(1, b * t, D_MODEL) if per_token else x
    tm = min(rows.shape[1], ROW_TILE)
    p_rwkv = _norm_proj(rows, mod, wts["norm1_g"], wts["w_in_rwkv"], gate=False, tm=tm)
    p_gla = _norm_proj(rows, mod, wts["norm1_g"], wts["w_in_gla"], gate=False, tm=tm)
    gates = _norm_proj(rows, mod, wts["norm1_g"], wts["w_in_gate"], gate=True,
                       tm=min(rows.shape[1], WIDE_ROW_TILE))
    p_rwkv = p_rwkv.reshape(b, t, RWKV_PROJ)

    y_r, s_rwkv = _rwkv(p_rwkv, shift0, s_rwkv0,
                        wts["rwkv_mu"], wts["rwkv_w0"], wts["rwkv_decay_up"], wts["rwkv_a0"],
                        wts["rwkv_aaa_up"], wts["rwkv_gate_up"], wts["rwkv_k_k"], wts["rwkv_k_a"],
                        wts["rwkv_r_k"], wts["rwkv_ln_g"], wts["rwkv_ln_b"], nb=RWKV_SEQS_PER_STEP)
    y_g, s_gla = _gla(p_gla.reshape(b, t, GLA_PROJ_PAD), s_gla0, wts["gla_alpha_up"],
                      wts["gla_alpha_b"], wts["gla_norm_g"], nb=GLA_SEQS_PER_STEP)

    x1, h2_all, route_all = _merge(
        y_r.reshape(rows.shape), y_g.reshape(rows.shape), gates, rows, mod, wts["w_branch_rwkv"],
        wts["w_branch_gla"], wts["w_out"], wts["norm2_g"], wts["router_cat"],
        tm=tm, n_tok=n_tok, row_off=row_off, shared=shared)
    states = (p_rwkv[:, t - 1:t, :][None], s_rwkv[None], s_gla[None])
    return x1, h2_all, route_all, states


def kernel(x_prompt, x_sample, c_prompt, c_sample, state_rwkv_shift, state_rwkv, state_gla, w_ada, b_ada, norm1_g, w_in, rwkv_mu, rwkv_w0, rwkv_decay_up, rwkv_a0, rwkv_aaa_up, rwkv_gate_up, rwkv_k_k, rwkv_k_a, rwkv_r_k, rwkv_ln_g, rwkv_ln_b, gla_alpha_up, gla_alpha_b, gla_norm_g, w_branch_rwkv, w_branch_gla, w_out, norm2_g, router_group, router_expert, expert_w_gate, expert_w_up, expert_w_down, final_norm_g):
    bp, tp = x_prompt.shape[:2]
    bs, ts = x_sample.shape[:2]
    w_in0 = w_in[0]
    g0 = RWKV_PROJ
    o3 = 2 * GLA_DK_TOTAL + GLA_DV_TOTAL
    o4 = o3 + GLA_GATE_RANK
    w_gla = w_in0[:, g0:g0 + GLA_PROJ]
    w_gla = jnp.concatenate([w_gla[:, :o3], w_gla[:, o4:], w_gla[:, o3:o4],
                             jnp.zeros((D_MODEL, GLA_PROJ_PAD - GLA_PROJ), F32)], axis=1)
    wts = dict(
        norm1_g=norm1_g[0].reshape(1, -1),
        w_in_rwkv=w_in0[:, :g0].astype(BF16),
        w_in_gla=w_gla.astype(BF16),
        w_in_gate=w_in0[:, g0 + GLA_PROJ:].astype(BF16),
        rwkv_mu=rwkv_mu[0], rwkv_w0=rwkv_w0[0], rwkv_decay_up=rwkv_decay_up[0], rwkv_a0=rwkv_a0[0],
        rwkv_aaa_up=rwkv_aaa_up[0], rwkv_gate_up=rwkv_gate_up[0], rwkv_k_k=rwkv_k_k[0],
        rwkv_k_a=rwkv_k_a[0], rwkv_r_k=rwkv_r_k[0], rwkv_ln_g=rwkv_ln_g[0], rwkv_ln_b=rwkv_ln_b[0],
        gla_alpha_up=jnp.pad(gla_alpha_up[0], ((0, LANES - GLA_GATE_RANK), (0, 0))),
        gla_alpha_b=gla_alpha_b[0], gla_norm_g=gla_norm_g[0],
        w_branch_rwkv=w_branch_rwkv[0].astype(BF16), w_branch_gla=w_branch_gla[0].astype(BF16),
        w_out=w_out[0].astype(BF16), norm2_g=norm2_g[0],
        router_cat=_split_hi_lo(jnp.pad(jnp.concatenate([router_expert[0], router_group[0]], axis=1),
                                        ((0, 0), (0, LANES - N_EXPERTS - N_GROUPS)))),
        expert_w_gate=expert_w_gate[0], expert_w_up=expert_w_up[0], expert_w_down=expert_w_down[0],
    )
    mod = _ada(jnp.concatenate([c_prompt, c_sample], axis=0), w_ada[0], b_ada[0])
    mod = mod.reshape(bp + bs, N_MOD, D_MODEL)
    n_p = bp * tp
    n_s = bs * ts
    n_tok = n_p + n_s
    mod_p = mod[:bp].reshape(bp, N_MOD, 1, D_MODEL)
    mod_s = jnp.repeat(mod[bp:].transpose(1, 0, 2), ts, axis=1)[None]

    dt = x_prompt.dtype
    x1_p, h2_all, route_all, states_p = _mixer(
        x_prompt, mod_p, jnp.zeros((bp, 1, RWKV_PROJ), dt),
        jnp.zeros((bp, RWKV_HEADS, RWKV_HEAD_DIM, RWKV_HEAD_DIM), state_rwkv.dtype),
        jnp.zeros((bp, GLA_HEADS, GLA_DK, GLA_DV), state_gla.dtype), wts, n_tok=n_tok, row_off=0)
    x1_s, h2_all, route_all, states_s = _mixer(
        x_sample, mod_s, state_rwkv_shift[0], state_rwkv[0], state_gla[0], wts,
        n_tok=n_tok, row_off=n_p, shared=(h2_all, route_all))

    expert_ids = route_all[:, :2].astype(jnp.int32)
    moe_out = _moe_grouped(h2_all, expert_ids, wts["expert_w_gate"], wts["expert_w_up"],
                           wts["expert_w_down"])
    y_p = _final(x1_p, moe_out, mod_p, final_norm_g, row_off=0, tm=min(tp, WIDE_ROW_TILE))
    y_s = _final(x1_s, moe_out, mod_s, final_norm_g, row_off=n_p, tm=min(n_s, WIDE_ROW_TILE))
    return (y_p, y_s.reshape(bs, ts, D_MODEL)) + states_p + states_s
```

```python
import functools
import math

import jax
import jax.numpy as jnp
from jax import lax
from jax.experimental import pallas as pl
from jax.experimental.pallas import tpu as pltpu

F32 = jnp.float32
BF16 = jnp.bfloat16

D_MODEL = 1024
N_MOD = 6
EPS = 1e-6
RWKV_HEAD_DIM = 64
RWKV_HEADS = 16
RWKV_PAIRS = RWKV_HEADS // 2
DECAY_LORA = 64
AAA_LORA = 64
GATE_LORA = 128
RWKV_PROJ = 3 * D_MODEL + DECAY_LORA + AAA_LORA + GATE_LORA
LN_X_EPS = 64e-5
GLA_HEADS = 4
GLA_DK = 128
GLA_DV = 256
GLA_DK_TOTAL = GLA_HEADS * GLA_DK
GLA_DV_TOTAL = GLA_HEADS * GLA_DV
GLA_GATE_RANK = 16
GLA_GATE_NORM = 16.0
GLA_PROJ = 2 * GLA_DK_TOTAL + 2 * GLA_DV_TOTAL + GLA_GATE_RANK
LANES = 128
GLA_PROJ_PAD = 2 * GLA_DK_TOTAL + 2 * GLA_DV_TOTAL + LANES
GATE_PROJ = 2 * D_MODEL
N_GROUPS = 4
EXPERTS_PER_GROUP = 8
N_EXPERTS = 32
EXPERT_FF = 512

CHUNK = 64
ROW_TILE = 512
WIDE_ROW_TILE = 1024
MOE_TILE = 192
DMA_GROUP = 8
MOE_ROW = D_MODEL + LANES
N_PAIR_CLASSES = N_EXPERTS * EXPERTS_PER_GROUP
RWKV_SEQS_PER_STEP = 4
GLA_SEQS_PER_STEP = 8
VMEM_LIMIT = 56 * 1024 * 1024


def _bdot(a, b):
    return jnp.dot(a.astype(BF16), b.astype(BF16), preferred_element_type=F32)


def _bdot_nt(a, b):
    return lax.dot_general(a.astype(BF16), b.astype(BF16), (((1,), (1,)), ((), ())),
                           preferred_element_type=F32)


def _bdot_tn(a, b):
    return jnp.dot(a.T.astype(BF16), b.astype(BF16), preferred_element_type=F32)


def _split3(x):
    h1 = x.astype(BF16)
    r1 = x - h1.astype(F32)
    h2 = r1.astype(BF16)
    h3 = (r1 - h2.astype(F32)).astype(BF16)
    return h1, h2, h3


def _dot3(a, b):
    a1, a2, _ = _split3(a)
    b1, b2, _ = _split3(b)
    return (jnp.dot(a1, b1, preferred_element_type=F32)
            + jnp.dot(a2, b1, preferred_element_type=F32)
            + jnp.dot(a1, b2, preferred_element_type=F32))


def _dot_exact_lhs(a_bf16, x):
    x1, x2, _ = _split3(x)
    return (jnp.dot(a_bf16, x1, preferred_element_type=F32)
            + jnp.dot(a_bf16, x2, preferred_element_type=F32))


def _split_hi_lo(w):
    hi, lo, _ = _split3(w)
    return jnp.concatenate([hi, lo], axis=1)


def _sigmoid(x):
    return 1.0 / (1.0 + jnp.exp(-x))


def _silu(x):
    return x * _sigmoid(x)


def _rms_scale(x):
    return x * lax.rsqrt(jnp.mean(x * x, axis=-1, keepdims=True) + EPS)


def _ada_kernel(c_ref, w_ref, b_ref, o_ref):
    o_ref[...] = _dot3(_silu(c_ref[...]), w_ref[...]) + b_ref[...]


def _ada(c, w_ada, b_ada):
    n_rows = c.shape[0]
    n_out = w_ada.shape[1]
    tn = 1536
    return pl.pallas_call(
        _ada_kernel,
        out_shape=jax.ShapeDtypeStruct((n_rows, n_out), F32),
        grid=(n_out // tn,),
        in_specs=[pl.BlockSpec((n_rows, D_MODEL), lambda j: (0, 0)),
                  pl.BlockSpec((D_MODEL, tn), lambda j: (0, j)),
                  pl.BlockSpec((1, tn), lambda j: (0, j))],
        out_specs=pl.BlockSpec((n_rows, tn), lambda j: (0, j)),
        compiler_params=pltpu.CompilerParams(dimension_semantics=("parallel",),
                                             vmem_limit_bytes=VMEM_LIMIT),
        name="ada",
    )(c, w_ada, b_ada.reshape(1, n_out))


def _mod_spec(mod, tm, seq_tile):
    if mod.shape[2] == 1:
        return pl.BlockSpec((1, N_MOD, 1, D_MODEL), lambda *g: (seq_tile(*g)[0], 0, 0, 0))
    return pl.BlockSpec((1, N_MOD, tm, D_MODEL), lambda *g: (seq_tile(*g)[0], 0, seq_tile(*g)[1], 0))


def _norm_proj_kernel(x_ref, mod_ref, g_ref, w_ref, o_ref, *, gate):
    mod = mod_ref[0]
    h = _rms_scale(x_ref[0]) * g_ref[...] * (1.0 + mod[1]) + mod[0]
    out = _bdot(h, w_ref[...])
    o_ref[0] = _sigmoid(out) if gate else out


def _norm_proj(x, mod, g, w_bf16, *, gate, tm):
    b, t, _ = x.shape
    n = w_bf16.shape[1]
    return pl.pallas_call(
        functools.partial(_norm_proj_kernel, gate=gate),
        out_shape=jax.ShapeDtypeStruct((b, t, n), F32),
        grid=(b, t // tm),
        in_specs=[pl.BlockSpec((1, tm, D_MODEL), lambda i, j: (i, j, 0)),
                  _mod_spec(mod, tm, lambda i, j: (i, j)),
                  pl.BlockSpec((1, D_MODEL), lambda i, j: (0, 0)),
                  pl.BlockSpec((D_MODEL, n), lambda i, j: (0, 0))],
        out_specs=pl.BlockSpec((1, tm, n), lambda i, j: (i, j, 0)),
        compiler_params=pltpu.CompilerParams(dimension_semantics=("parallel", "parallel"),
                                             vmem_limit_bytes=VMEM_LIMIT),
        name="norm_proj_gate" if gate else "norm_proj",
    )(x, mod, g, w_bf16)


def _blockdiag(x, lo_mask):
    return jnp.concatenate([jnp.where(lo_mask, x, 0.0), jnp.where(lo_mask, 0.0, x)], axis=0)


def _rwkv_kernel(p_ref, shift_ref, s0_ref, mu_ref, w0_ref, dup_ref, a0_ref, aup_ref, gup_ref,
                 kk_ref, ka_ref, rk_ref, lng_ref, lnb_ref, y_ref, sout_ref, s_scr, carry_scr,
                 *, nb):
    c = pl.program_id(1)
    n_chunks = pl.num_programs(1)
    C = CHUNK
    N = RWKV_HEAD_DIM
    t_blk = p_ref.shape[1]

    @pl.when(c == 0)
    def _():
        for bl in range(nb):
            for pr in range(RWKV_PAIRS):
                s_scr[bl, pr] = jnp.concatenate([s0_ref[bl, 2 * pr], s0_ref[bl, 2 * pr + 1]], axis=1)
        carry_scr[...] = shift_ref[...]

    row = lax.broadcasted_iota(jnp.int32, (C, LANES), 0)
    lane = lax.broadcasted_iota(jnp.int32, (C, LANES), 1)
    lane_t = jnp.where(lane < N, lane, lane - N)
    lo = lane < N
    strict = lane_t < row
    incl = lane_t <= row
    eye = (lane_t == row).astype(F32)
    valid = row < t_blk
    tri = (lax.broadcasted_iota(jnp.int32, (C, C), 1)
           <= lax.broadcasted_iota(jnp.int32, (C, C), 0)).astype(BF16)
    sq_r = lax.broadcasted_iota(jnp.int32, (LANES, LANES), 0)
    sq_c = lax.broadcasted_iota(jnp.int32, (LANES, LANES), 1)
    same_head = (sq_r < N) == (sq_c < N)
    diag = sq_r == sq_c
    row_w = lax.broadcasted_iota(jnp.int32, (C, RWKV_PROJ), 0)
    o3 = 3 * D_MODEL

    def hsum(v):
        s_lo = jnp.sum(jnp.where(lo, v, 0.0), axis=1, keepdims=True)
        s_hi = jnp.sum(jnp.where(lo, 0.0, v), axis=1, keepdims=True)
        return jnp.where(lo, s_lo, s_hi)

    def bd(v):
        return _blockdiag(v, lo)

    def cat0(*a):
        return jnp.concatenate(a, axis=0)

    def cat1(*a):
        return jnp.concatenate(a, axis=1)

    valid_w = lax.broadcasted_iota(jnp.int32, (C, D_MODEL), 0) < t_blk
    xs, logw_all, cum_all, asig_all, g_all = [], [], [], [], []
    for bl in range(nb):
        p = p_ref[bl]
        carry_new = p[t_blk - 1:t_blk, :]
        if t_blk < C:
            p = jnp.concatenate([p, jnp.zeros((C - t_blk, RWKV_PROJ), F32)], axis=0)
        prev = jnp.where(row_w == 0, carry_scr[bl], pltpu.roll(p, 1, axis=0))
        carry_scr[bl] = carry_new
        x = p + (prev - p) * mu_ref[...]
        xw = x[:, o3:o3 + DECAY_LORA]
        xa = x[:, o3 + DECAY_LORA:o3 + DECAY_LORA + AAA_LORA]
        xg = x[:, o3 + DECAY_LORA + AAA_LORA:]
        z = w0_ref[...] + _bdot(jnp.tanh(xw), dup_ref[...])
        logw_full = -math.exp(-0.5) * _sigmoid(z)
        if t_blk < C:
            logw_full = jnp.where(valid_w, logw_full, 0.0)
        xs.append(x)
        logw_all.append(logw_full)
        cum_all.append(_dot_exact_lhs(tri, logw_full))
        asig_all.append(_sigmoid(a0_ref[...] + _bdot(xa, aup_ref[...])))
        g_all.append(_bdot(_sigmoid(xg), gup_ref[...]))

    units = [(bl, pr) for bl in range(nb) for pr in range(RWKV_PAIRS)]
    U = range(len(units))
    cols = [slice(pr * LANES, (pr + 1) * LANES) for _, pr in units]
    r = [xs[bl][:, pr * LANES:(pr + 1) * LANES] for bl, pr in units]
    k = [xs[bl][:, D_MODEL + pr * LANES:D_MODEL + (pr + 1) * LANES] for bl, pr in units]
    v = [xs[bl][:, 2 * D_MODEL + pr * LANES:2 * D_MODEL + (pr + 1) * LANES] for bl, pr in units]
    logw = [logw_all[bl][:, cols[u]] for u, (bl, _) in enumerate(units)]
    asig = [asig_all[bl][:, cols[u]] for u, (bl, _) in enumerate(units)]
    kk = [k[u] * kk_ref[:, cols[u]] for u in U]
    k2 = [k[u] * (1.0 + (asig[u] - 1.0) * ka_ref[:, cols[u]]) for u in U]
    ss = [hsum(kk[u] * kk[u]) for u in U]
    cum = [cum_all[bl][:, cols[u]] for u, (bl, _) in enumerate(units)]
    bsum = [hsum(r[u] * k2[u] * rk_ref[:, cols[u]]) for u in U]
    kk = [kk[u] * lax.rsqrt(jnp.maximum(ss[u], 1e-24)) for u in U]
    bonus = [bsum[u] * v[u] for u in U]
    if t_blk < C:
        kk = [jnp.where(valid, a, 0.0) for a in kk]
        k2 = [jnp.where(valid, a, 0.0) for a in k2]
        v = [jnp.where(valid, a, 0.0) for a in v]
        r = [jnp.where(valid, a, 0.0) for a in r]
    b_vec = [kk[u] * asig[u] for u in U]
    clast = [cum[u][C - 1:C, :] for u in U]
    e_inv = [jnp.exp(-cum[u]) for u in U]
    e_tail = [jnp.exp(clast[u] - cum[u]) for u in U]
    aq = [-kk[u] * jnp.exp(cum[u] - logw[u]) for u in U]
    rq = [r[u] * jnp.exp(cum[u]) for u in U]
    bk = [b_vec[u] * e_inv[u] for u in U]
    kx = [k2[u] * e_inv[u] for u in U]
    bb = [b_vec[u] * e_tail[u] for u in U]
    kb = [k2[u] * e_tail[u] for u in U]

    g4 = [_bdot_nt(cat0(aq[u], rq[u]), cat0(bd(bk[u]), bd(kx[u]))) for u in U]
    m_ab = [jnp.where(strict, g4[u][:C, :LANES], 0.0) for u in U]
    m_ak = [jnp.where(strict, g4[u][:C, LANES:], 0.0) for u in U]
    m_rb = [jnp.where(incl, g4[u][C:, :LANES], 0.0) for u in U]
    m_rk = [jnp.where(incl, g4[u][C:, LANES:], 0.0) for u in U]

    t_inv = [eye + m_ab[u] for u in U]
    l_pow = [_bdot(m_ab[u], bd(m_ab[u])) for u in U]
    mv = [_bdot(cat0(m_ak[u], m_rk[u]), bd(v[u])) for u in U]
    n_steps = C.bit_length() - 1
    for step in range(1, n_steps):
        if step < n_steps - 1:
            both = [_bdot(l_pow[u], cat1(bd(t_inv[u]), bd(l_pow[u]))) for u in U]
            t_inv = [t_inv[u] + both[u][:, :LANES] for u in U]
            l_pow = [both[u][:, LANES:] for u in U]
        else:
            t_inv = [t_inv[u] + _bdot(l_pow[u], bd(t_inv[u])) for u in U]

    au = [_bdot(t_inv[u], cat1(bd(aq[u]), bd(mv[u][:C]))) for u in U]
    a_bar = [au[u][:, :LANES] for u in U]
    u_bar = [au[u][:, LANES:] for u in U]
    mau = [_bdot(m_rb[u], cat1(bd(a_bar[u]), bd(u_bar[u]))) for u in U]
    p_full = [_bdot_tn(a_bar[u], bb[u]) for u in U]
    q_full = [_bdot_tn(cat0(u_bar[u], v[u]), cat0(bb[u], kb[u])) for u in U]
    r_til = [rq[u] + mau[u][:, :LANES] for u in U]
    y_loc = [mau[u][:, LANES:] + mv[u][C:] for u in U]
    p_bd = [jnp.where(same_head, p_full[u], 0.0) + jnp.where(diag, jnp.exp(clast[u]), 0.0) for u in U]
    q_pk = [jnp.where(lo[:N], q_full[u][:N], q_full[u][N:]) for u in U]

    s_old = [s_scr[bl, pr] for bl, pr in units]
    y = [_bdot_nt(r_til[u], _blockdiag(s_old[u], lo[:N])) + y_loc[u] for u in U]
    s_new = [_bdot(s_old[u], p_bd[u]) + q_pk[u] for u in U]
    for u, (bl, pr) in enumerate(units):
        s_scr[bl, pr] = s_new[u]

    mu_y = [hsum(y[u]) * (1.0 / N) for u in U]
    yc = [y[u] - mu_y[u] for u in U]
    var = [hsum(yc[u] * yc[u]) * (1.0 / N) for u in U]
    for u, (bl, pr) in enumerate(units):
        yn = yc[u] * lax.rsqrt(var[u] + LN_X_EPS) * lng_ref[:, cols[u]] + lnb_ref[:, cols[u]]
        y_ref[bl, :, cols[u]] = ((yn + bonus[u]) * g_all[bl][:, cols[u]])[:t_blk]

    @pl.when(c == n_chunks - 1)
    def _():
        for bl in range(nb):
            for pr in range(RWKV_PAIRS):
                sout_ref[bl, 2 * pr] = s_scr[bl, pr][:, :N]
                sout_ref[bl, 2 * pr + 1] = s_scr[bl, pr][:, N:]


def _rwkv(p_rwkv, shift_prev, s0, mu, w0, decay_up, a0, aaa_up, gate_up, k_k, k_a, r_k,
          ln_g, ln_b, *, nb):
    b, t, _ = p_rwkv.shape
    nb = math.gcd(b, nb)
    t_blk = min(t, CHUNK)
    row = lambda a: a.reshape(1, -1)
    full = lambda shape: pl.BlockSpec(shape, lambda i, j: (0,) * len(shape))
    state_spec = pl.BlockSpec((nb, RWKV_HEADS, RWKV_HEAD_DIM, RWKV_HEAD_DIM), lambda i, j: (i, 0, 0, 0))
    return pl.pallas_call(
        functools.partial(_rwkv_kernel, nb=nb),
        out_shape=(jax.ShapeDtypeStruct((b, t, D_MODEL), F32),
                   jax.ShapeDtypeStruct((b, RWKV_HEADS, RWKV_HEAD_DIM, RWKV_HEAD_DIM), F32)),
        grid=(b // nb, t // t_blk),
        in_specs=[pl.BlockSpec((nb, t_blk, RWKV_PROJ), lambda i, j: (i, j, 0)),
                  pl.BlockSpec((nb, 1, RWKV_PROJ), lambda i, j: (i, 0, 0)),
                  state_spec,
                  full((1, RWKV_PROJ)), full((1, D_MODEL)), full((DECAY_LORA, D_MODEL)),
                  full((1, D_MODEL)), full((AAA_LORA, D_MODEL)), full((GATE_LORA, D_MODEL)),
                  full((1, D_MODEL)), full((1, D_MODEL)), full((1, D_MODEL)),
                  full((1, D_MODEL)), full((1, D_MODEL))],
        out_specs=(pl.BlockSpec((nb, t_blk, D_MODEL), lambda i, j: (i, j, 0)), state_spec),
        scratch_shapes=[pltpu.VMEM((nb, RWKV_PAIRS, RWKV_HEAD_DIM, LANES), F32),
                        pltpu.VMEM((nb, 1, RWKV_PROJ), F32)],
        compiler_params=pltpu.CompilerParams(dimension_semantics=("parallel", "arbitrary"),
                                             vmem_limit_bytes=VMEM_LIMIT),
        name="rwkv_scan",
    )(p_rwkv, shift_prev, s0, row(mu), row(w0), decay_up, row(a0), aaa_up, gate_up,
      row(k_k), row(k_a), row(r_k), row(ln_g), row(ln_b))


def _gla_kernel(p_ref, s0_ref, aup_ref, ab_ref, ng_ref, y_ref, sout_ref, s_scr, *, nb):
    c = pl.program_id(1)
    n_chunks = pl.num_programs(1)
    C = CHUNK
    t_blk = p_ref.shape[1]
    units = [(bl, h) for bl in range(nb) for h in range(GLA_HEADS)]
    U = range(len(units))

    @pl.when(c == 0)
    def _():
        for bl, h in units:
            s_scr[bl, h] = s0_ref[bl, h].T

    o_k = GLA_DK_TOTAL
    o_v = 2 * GLA_DK_TOTAL
    o_r = o_v + GLA_DV_TOTAL
    o_a = o_r + GLA_DV_TOTAL
    tri = (lax.broadcasted_iota(jnp.int32, (C, C), 1)
           <= lax.broadcasted_iota(jnp.int32, (C, C), 0))
    tri_bf = tri.astype(BF16)
    valid_g = lax.broadcasted_iota(jnp.int32, (C, GLA_DK_TOTAL), 0) < t_blk
    valid_k = lax.broadcasted_iota(jnp.int32, (C, GLA_DK), 0) < t_blk
    valid_v = lax.broadcasted_iota(jnp.int32, (C, GLA_DV), 0) < t_blk

    ps, cums = [], []
    for bl in range(nb):
        p = p_ref[bl]
        if t_blk < C:
            p = jnp.concatenate([p, jnp.zeros((C - t_blk, GLA_PROJ_PAD), F32)], axis=0)
        logit = _dot3(p[:, o_a:], aup_ref[...]) + ab_ref[...]
        g = (jnp.minimum(logit, 0.0) - jnp.log(1.0 + jnp.exp(-jnp.abs(logit)))) * (1.0 / GLA_GATE_NORM)
        if t_blk < C:
            g = jnp.where(valid_g, g, 0.0)
        ps.append(p)
        cums.append(_dot_exact_lhs(tri_bf, g))

    q = [ps[bl][:, h * GLA_DK:(h + 1) * GLA_DK] * (GLA_DK ** -0.5) for bl, h in units]
    k = [ps[bl][:, o_k + h * GLA_DK:o_k + (h + 1) * GLA_DK] for bl, h in units]
    v = [ps[bl][:, o_v + h * GLA_DV:o_v + (h + 1) * GLA_DV] for bl, h in units]
    if t_blk < C:
        k = [jnp.where(valid_k, a, 0.0) for a in k]
        v = [jnp.where(valid_v, a, 0.0) for a in v]
    b = [cums[bl][:, h * GLA_DK:(h + 1) * GLA_DK] for bl, h in units]
    clast = [b[u][C - 1:C, :] for u in U]
    cmid = [b[u][C // 2 - 1:C // 2, :] for u in U]
    att = [_bdot_nt(q[u] * jnp.exp(b[u] - cmid[u]), k[u] * jnp.exp(cmid[u] - b[u])) for u in U]
    att = [jnp.where(tri, a, 0.0) for a in att]
    s_t = [s_scr[bl, h] for bl, h in units]
    o = [_bdot(att[u], v[u]) + _bdot_nt(q[u] * jnp.exp(b[u]), s_t[u]) for u in U]
    s_new = [s_t[u] * jnp.exp(clast[u]) + _bdot_tn(v[u], k[u] * jnp.exp(clast[u] - b[u])) for u in U]
    for u, (bl, h) in enumerate(units):
        s_scr[bl, h] = s_new[u]
        r = ps[bl][:, o_r + h * GLA_DV:o_r + (h + 1) * GLA_DV]
        on = o[u] * lax.rsqrt(jnp.mean(o[u] * o[u], axis=-1, keepdims=True) + EPS) * ng_ref[...]
        y_ref[bl, :, h * GLA_DV:(h + 1) * GLA_DV] = (on * _silu(r))[:t_blk]

    @pl.when(c == n_chunks - 1)
    def _():
        for bl, h in units:
            sout_ref[bl, h] = s_scr[bl, h].T


def _gla(p_gla, s0, alpha_up_pad, alpha_b, norm_g, *, nb):
    b, t, _ = p_gla.shape
    nb = math.gcd(b, nb)
    t_blk = min(t, CHUNK)
    full = lambda shape: pl.BlockSpec(shape, lambda i, j: (0,) * len(shape))
    state_spec = pl.BlockSpec((nb, GLA_HEADS, GLA_DK, GLA_DV), lambda i, j: (i, 0, 0, 0))
    return pl.pallas_call(
        functools.partial(_gla_kernel, nb=nb),
        out_shape=(jax.ShapeDtypeStruct((b, t, GLA_DV_TOTAL), F32),
                   jax.ShapeDtypeStruct((b, GLA_HEADS, GLA_DK, GLA_DV), F32)),
        grid=(b // nb, t // t_blk),
        in_specs=[pl.BlockSpec((nb, t_blk, GLA_PROJ_PAD), lambda i, j: (i, j, 0)),
                  state_spec,
                  full((LANES, GLA_DK_TOTAL)), full((1, GLA_DK_TOTAL)), full((1, GLA_DV))],
        out_specs=(pl.BlockSpec((nb, t_blk, GLA_DV_TOTAL), lambda i, j: (i, j, 0)), state_spec),
        scratch_shapes=[pltpu.VMEM((nb, GLA_HEADS, GLA_DV, GLA_DK), F32)],
        compiler_params=pltpu.CompilerParams(dimension_semantics=("parallel", "arbitrary"),
                                             vmem_limit_bytes=VMEM_LIMIT),
        name="gla_scan",
    )(p_gla, s0, alpha_up_pad, alpha_b.reshape(1, -1), norm_g.reshape(1, -1))


def _merge_body(yr_ref, yg_ref, gate_ref, x_ref, mod_ref, wbr_ref, wbg_ref, wout_ref, g2_ref,
                router_ref, x1_ref, h2_ref, route_ref):
    mod = mod_ref[0]
    gates = gate_ref[0]
    merged = (gates[:, :D_MODEL] * _bdot(yr_ref[0], wbr_ref[...])
              + gates[:, D_MODEL:] * _bdot(yg_ref[0], wbg_ref[...]))
    x1 = x_ref[0] + mod[2] * _bdot(merged, wout_ref[...])
    x1_ref[0] = x1
    h2 = _rms_scale(x1) * g2_ref[...] * (1.0 + mod[4]) + mod[3]
    h2_ref[:, :D_MODEL] = h2

    tm = h2.shape[0]
    h_hi, h_lo, _ = _split3(h2)
    prod = jnp.dot(jnp.concatenate([h_hi, h_lo], axis=0), router_ref[...], preferred_element_type=F32)
    logits = prod[:tm, :LANES] + prod[tm:, :LANES] + prod[:tm, LANES:]
    lane = lax.broadcasted_iota(jnp.int32, (tm, LANES), 1)
    neg = -jnp.inf
    is_group = (lane >= N_EXPERTS) & (lane < N_EXPERTS + N_GROUPS)
    gl = jnp.where(is_group, logits, neg)
    gmax = jnp.max(gl, axis=1, keepdims=True)
    g_idx = jnp.min(jnp.where(gl == gmax, lane, LANES), axis=1, keepdims=True) - N_EXPERTS
    p_g = 1.0 / jnp.sum(jnp.exp(gl - gmax), axis=1, keepdims=True)
    in_group = (lane >= g_idx * EXPERTS_PER_GROUP) & (lane < (g_idx + 1) * EXPERTS_PER_GROUP)
    el = jnp.where(in_group, logits, neg)
    v1 = jnp.max(el, axis=1, keepdims=True)
    i1 = jnp.min(jnp.where(el == v1, lane, LANES), axis=1, keepdims=True)
    el2 = jnp.where(lane == i1, neg, el)
    v2 = jnp.max(el2, axis=1, keepdims=True)
    i2 = jnp.min(jnp.where(el2 == v2, lane, LANES), axis=1, keepdims=True)
    e21 = jnp.exp(v2 - v1)
    w1 = p_g / (1.0 + e21)
    route = (jnp.where(lane == 0, i1.astype(F32), 0.0) + jnp.where(lane == 1, i2.astype(F32), 0.0)
             + jnp.where(lane == 2, w1, 0.0) + jnp.where(lane == 3, w1 * e21, 0.0))
    route_ref[...] = route
    h2_ref[:, D_MODEL:] = route


def _merge_first_kernel(*refs, n_own):
    step = pl.program_id(0)

    @pl.when(step < n_own)
    def _():
        _merge_body(*refs)

    @pl.when(step >= n_own)
    def _():
        refs[-2][...] = jnp.zeros(refs[-2].shape, F32)
        refs[-1][...] = jnp.zeros(refs[-1].shape, F32)


def _merge_second_kernel(*refs):
    n_in = 10
    _merge_body(*refs[:n_in], *refs[n_in + 2:])


def _merge(y_r, y_g, gates, x, mod, wbr, wbg, wout, norm2_g, router_cat, *, tm, n_tok, row_off, shared=None):
    b, t, _ = x.shape
    per_seq = t // tm
    n_own = b * per_seq
    first = shared is None
    n_steps = n_tok // tm if first else n_own
    assert row_off % tm == 0 and n_tok % tm == 0 and (row_off == 0 or not first)
    blk0 = row_off // tm
    own = lambda s: jnp.minimum(s, n_own - 1)
    full = lambda shape: pl.BlockSpec(shape, lambda s: (0,) * len(shape))
    tile = lambda n: pl.BlockSpec((1, tm, n), lambda s: (own(s) // per_seq, own(s) % per_seq, 0))
    flat = lambda n: pl.BlockSpec((tm, n), lambda s: (blk0 + s, 0))
    in_specs = [tile(D_MODEL), tile(D_MODEL), tile(GATE_PROJ), tile(D_MODEL),
                _mod_spec(mod, tm, lambda s: (own(s) // per_seq, own(s) % per_seq)),
                full((D_MODEL, D_MODEL)), full((D_MODEL, D_MODEL)), full((D_MODEL, D_MODEL)),
                full((1, D_MODEL)), full((D_MODEL, 2 * LANES))]
    args = [y_r, y_g, gates, x, mod, wbr, wbg, wout, norm2_g.reshape(1, -1), router_cat]
    if not first:
        in_specs += [pl.BlockSpec(memory_space=pl.ANY)] * 2
        args += list(shared)
    return pl.pallas_call(
        functools.partial(_merge_first_kernel, n_own=n_own) if first else _merge_second_kernel,
        out_shape=(jax.ShapeDtypeStruct((b, t, D_MODEL), F32),
                   jax.ShapeDtypeStruct((n_tok, MOE_ROW), F32),
                   jax.ShapeDtypeStruct((n_tok, LANES), F32)),
        grid=(n_steps,),
        in_specs=in_specs,
        out_specs=(tile(D_MODEL), flat(MOE_ROW), flat(LANES)),
        input_output_aliases={} if first else {10: 1, 11: 2},
        compiler_params=pltpu.CompilerParams(dimension_semantics=("arbitrary",),
                                             vmem_limit_bytes=VMEM_LIMIT),
        name="merge_router",
    )(*args)


def _moe_plan(expert_ids, n_tok):
    tm = MOE_TILE
    n_tiles = -(-n_tok // tm) + N_GROUPS * (EXPERTS_PER_GROUP * (EXPERTS_PER_GROUP - 1) // 2)
    n_slots = n_tiles * tm
    n_fill = n_slots - n_tok
    row_bits = 15
    assert max(n_tok, n_fill) <= 1 << row_bits and N_PAIR_CLASSES << (row_bits + 1) <= 1 << 24
    e_lo = jnp.minimum(expert_ids[:, 0], expert_ids[:, 1])
    e_hi = jnp.maximum(expert_ids[:, 0], expert_ids[:, 1])
    cls = e_lo * EXPERTS_PER_GROUP + e_hi % EXPERTS_PER_GROUP
    c_iota = jnp.arange(N_PAIR_CLASSES, dtype=jnp.int32)
    counts = jnp.sum((cls[None, :] == c_iota[:, None]).astype(jnp.int32), axis=1)
    tiles_c = (counts + tm - 1) // tm
    n_used = jnp.sum(tiles_c)
    pad_end = jnp.cumsum(tiles_c * tm - counts)
    f_iota = jnp.arange(n_fill, dtype=jnp.int32)
    f_cls = jnp.minimum(jnp.sum((f_iota[:, None] >= pad_end[None, :]).astype(jnp.int32), axis=1), N_PAIR_CLASSES - 1)
    keys = jnp.concatenate([(cls << (row_bits + 1)) + jnp.arange(n_tok, dtype=jnp.int32),
                            (f_cls << (row_bits + 1)) + (1 << row_bits) + f_iota])
    neg_sorted, _ = lax.top_k(-keys.astype(F32), n_slots)
    slots = (-neg_sorted).astype(jnp.int32).reshape(n_tiles, tm)
    valid = (slots & (1 << row_bits)) == 0
    tok = jnp.where(valid, slots & ((1 << row_bits) - 1), 0)
    t_idx = jnp.arange(n_tiles, dtype=jnp.int32)
    j_idx = jnp.arange(tm, dtype=jnp.int32)[None, :]
    dst = jnp.where(valid, tok, n_tok + (t_idx[:, None] % 2) * DMA_GROUP + j_idx % DMA_GROUP)
    n_valid = jnp.sum(valid.astype(jnp.int32), axis=1)
    n_groups = jnp.where(t_idx < n_used, (n_valid + DMA_GROUP - 1) // DMA_GROUP, 0)
    tile_c = slots[:, 0] >> (row_bits + 1)
    last_c = lax.dynamic_slice(tile_c, (jnp.maximum(n_used - 1, 0),), (1,))
    tile_c = jnp.where(t_idx < n_used, tile_c, last_c)
    tile_lo = tile_c // EXPERTS_PER_GROUP
    tile_hi = (tile_c // (EXPERTS_PER_GROUP * EXPERTS_PER_GROUP)) * EXPERTS_PER_GROUP + tile_c % EXPERTS_PER_GROUP
    return (tile_lo, tile_hi, n_groups.astype(jnp.int32), n_used.reshape(1).astype(jnp.int32),
            tok.reshape(n_tiles, 1, tm), dst.reshape(n_tiles, 1, tm))


def _moe_pair_kernel(lo_ref, hi_ref, ng_ref, n_used_ref, tok_ref, tok_next_ref, dst_ref, h2_hbm,
                     wg_hbm, wu_hbm, wd_hbm, out_hbm,
                     xbuf, obuf, w_res, w_stage, gsem, ssem, wsem):
    t = pl.program_id(0)
    n_t = pl.num_programs(0)
    tm = MOE_TILE
    slot = lax.rem(t, 2)
    n_used = n_used_ref[0]

    def gather(idx_ref, s, n_groups):
        def body(g, carry):
            for u in range(DMA_GROUP):
                pltpu.make_async_copy(h2_hbm.at[pl.ds(idx_ref[0, 0, g * DMA_GROUP + u], 1)],
                                      xbuf.at[s, g, pl.ds(u, 1)], gsem.at[s]).start()
            return carry
        lax.fori_loop(0, n_groups, body, 0)

    def wait_gather(s, n_groups):
        def body(g, carry):
            pltpu.make_async_copy(h2_hbm.at[pl.ds(0, DMA_GROUP)], xbuf.at[s, 0], gsem.at[s]).wait()
            return carry
        lax.fori_loop(0, n_groups, body, 0)

    def wait_scatter(s, n_groups):
        def body(g, carry):
            pltpu.make_async_copy(obuf.at[s, 0], out_hbm.at[pl.ds(0, DMA_GROUP)], ssem.at[s]).wait()
            return carry
        lax.fori_loop(0, n_groups, body, 0)

    @pl.when(t == 0)
    def _():
        xbuf[...] = jnp.zeros(xbuf.shape, F32)
        obuf[0, 0] = jnp.zeros(obuf.shape[2:], F32)
        n_rows = out_hbm.shape[0]
        for half in range(2):
            fill = pltpu.make_async_copy(obuf.at[0, 0], out_hbm.at[pl.ds(n_rows - (2 - half) * DMA_GROUP, DMA_GROUP)],
                                         ssem.at[0])
            fill.start()
            fill.wait()
        gather(tok_ref, 0, ng_ref[0])

    @pl.when((t >= 2) & (t - 2 < n_used))
    def _():
        wait_scatter(slot, ng_ref[jnp.maximum(t - 2, 0)])

    @pl.when(t < n_used)
    def _():
        n_here = ng_ref[t]
        wait_gather(slot, n_here)

        @pl.when(t + 1 < n_used)
        def _():
            gather(tok_next_ref, 1 - slot, ng_ref[jnp.minimum(t + 1, n_t - 1)])

        group = lo_ref[t] // EXPERTS_PER_GROUP

        @pl.when((t == 0) | (group != lo_ref[jnp.maximum(t - 1, 0)] // EXPERTS_PER_GROUP))
        def _():
            def fetch(e, buf):
                return [pltpu.make_async_copy(src.at[group * EXPERTS_PER_GROUP + e], w_stage[m].at[buf], wsem.at[buf, m])
                        for m, src in enumerate((wg_hbm, wu_hbm, wd_hbm))]

            for cp in fetch(0, 0):
                cp.start()
            for e in range(EXPERTS_PER_GROUP):
                if e + 1 < EXPERTS_PER_GROUP:
                    for cp in fetch(e + 1, (e + 1) % 2):
                        cp.start()
                for m, cp in enumerate(fetch(e, e % 2)):
                    cp.wait()
                    w_res[m][e] = w_stage[m][e % 2].astype(BF16)

        xe = xbuf[slot].reshape(tm, MOE_ROW)
        x = xe[:, :D_MODEL].astype(BF16)
        route = xe[:, D_MODEL:]
        lane = lax.broadcasted_iota(jnp.int32, route.shape, 1)
        pick = lambda k: jnp.sum(jnp.where(lane == k, route, 0.0), axis=1, keepdims=True)
        i1, i2, w1, w2 = pick(0), pick(1), pick(2), pick(3)
        w_lo = jnp.where(i1 < i2, w1, w2)
        w_hi = jnp.where(i1 < i2, w2, w1)
        acc = None
        for e_ref, wt in ((lo_ref, w_lo), (hi_ref, w_hi)):
            e_local = lax.rem(e_ref[t], EXPERTS_PER_GROUP)
            hid = _silu(jnp.dot(x, w_res[0][e_local], preferred_element_type=F32)) * jnp.dot(
                x, w_res[1][e_local], preferred_element_type=F32)
            part = wt * _bdot(hid, w_res[2][e_local])
            acc = part if acc is None else acc + part
        obuf[slot] = acc.reshape(obuf.shape[1:])

        def body(g, carry):
            for u in range(DMA_GROUP):
                pltpu.make_async_copy(obuf.at[slot, g, pl.ds(u, 1)],
                                      out_hbm.at[pl.ds(dst_ref[0, 0, g * DMA_GROUP + u], 1)], ssem.at[slot]).start()
            return carry
        lax.fori_loop(0, n_here, body, 0)

    @pl.when(t == n_t - 1)
    def _():
        @pl.when((t >= 1) & (t - 1 < n_used))
        def _():
            wait_scatter(1 - slot, ng_ref[jnp.maximum(t - 1, 0)])

        @pl.when(t < n_used)
        def _():
            wait_scatter(slot, ng_ref[t])


def _moe_grouped(h2_ext, expert_ids, wg, wu, wd):
    return _moe_call(h2_ext, *_moe_plan(expert_ids, h2_ext.shape[0]), wg, wu, wd)


def _moe_call(h2_ext, tile_lo, tile_hi, n_groups, n_used, tok, dst, wg, wu, wd):
    n_tok = h2_ext.shape[0]
    tm = MOE_TILE
    n_tiles = tile_lo.shape[0]
    smem_tile = lambda f: pl.BlockSpec((1, 1, tm), f, memory_space=pltpu.SMEM)
    up, down = (D_MODEL, EXPERT_FF), (EXPERT_FF, D_MODEL)
    any_spec = pl.BlockSpec(memory_space=pl.ANY)
    grid_spec = pltpu.PrefetchScalarGridSpec(
        num_scalar_prefetch=4,
        grid=(n_tiles,),
        in_specs=[smem_tile(lambda t, *_: (t, 0, 0)),
                  smem_tile(lambda t, *_: (jnp.minimum(t + 1, n_tiles - 1), 0, 0)),
                  smem_tile(lambda t, *_: (t, 0, 0)),
                  any_spec, any_spec, any_spec, any_spec],
        out_specs=pl.BlockSpec(memory_space=pl.ANY),
        scratch_shapes=[pltpu.VMEM((2, tm // DMA_GROUP, DMA_GROUP, MOE_ROW), F32),
                        pltpu.VMEM((2, tm // DMA_GROUP, DMA_GROUP, D_MODEL), F32),
                        [pltpu.VMEM((EXPERTS_PER_GROUP,) + shape, BF16) for shape in (up, up, down)],
                        [pltpu.VMEM((2,) + shape, F32) for shape in (up, up, down)],
                        pltpu.SemaphoreType.DMA((2,)), pltpu.SemaphoreType.DMA((2,)),
                        pltpu.SemaphoreType.DMA((2, 3))],
    )
    return pl.pallas_call(
        _moe_pair_kernel,
        out_shape=jax.ShapeDtypeStruct((n_tok + 2 * DMA_GROUP, D_MODEL), F32),
        grid_spec=grid_spec,
        compiler_params=pltpu.CompilerParams(dimension_semantics=("arbitrary",),
                                             vmem_limit_bytes=VMEM_LIMIT),
        name="moe_pairs",
    )(tile_lo, tile_hi, n_groups, n_used, tok, tok, dst, h2_ext, wg, wu, wd)


def _final_kernel(x1_ref, moe_ref, mod_ref, fg_ref, y_ref):
    x2 = x1_ref[0] + mod_ref[0][5] * moe_ref[...]
    y_ref[0] = _rms_scale(x2) * fg_ref[...]


def _final(x1, moe_out, mod, final_g, *, row_off, tm):
    b, t, _ = x1.shape
    blk0 = row_off // tm
    per_seq = t // tm
    return pl.pallas_call(
        _final_kernel,
        out_shape=jax.ShapeDtypeStruct((b, t, D_MODEL), F32),
        grid=(b, per_seq),
        in_specs=[pl.BlockSpec((1, tm, D_MODEL), lambda i, j: (i, j, 0)),
                  pl.BlockSpec((tm, D_MODEL), lambda i, j: (blk0 + i * per_seq + j, 0)),
                  _mod_spec(mod, tm, lambda i, j: (i, j)),
                  pl.BlockSpec((1, D_MODEL), lambda i, j: (0, 0))],
        out_specs=pl.BlockSpec((1, tm, D_MODEL), lambda i, j: (i, j, 0)),
        compiler_params=pltpu.CompilerParams(dimension_semantics=("parallel", "parallel"),
                                             vmem_limit_bytes=VMEM_LIMIT),
        name="moe_combine_norm",
    )(x1, moe_out, mod, final_g.reshape(1, -1))


def _mixer(x, mod, shift0, s_rwkv0, s_gla0, wts, *, n_tok, row_off, shared=None):
    b, t, _ = x.shape
    per_token = mod.shape[2] > 1
    rows = x.reshape(1, b * t, D_MODEL) if per_token else x
    tm = min(rows.shape[1], ROW_TILE)
    tm_wide = min(rows.shape[1], WIDE_ROW_TILE)
    p_rwkv = _norm_proj(rows, mod, wts["norm1_g"], wts["w_in_rwkv"], gate=False, tm=tm_wide)
    p_gla = _norm_proj(rows, mod, wts["norm1_g"], wts["w_in_gla"], gate=False, tm=tm_wide)
    gates = _norm_proj(rows, mod, wts["norm1_g"], wts["w_in_gate"], gate=True, tm=tm_wide)
    p_rwkv = p_rwkv.reshape(b, t, RWKV_PROJ)

    y_r, s_rwkv = _rwkv(p_rwkv, shift0, s_rwkv0,
                        wts["rwkv_mu"], wts["rwkv_w0"], wts["rwkv_decay_up"], wts["rwkv_a0"],
                        wts["rwkv_aaa_up"], wts["rwkv_gate_up"], wts["rwkv_k_k"], wts["rwkv_k_a"],
                        wts["rwkv_r_k"], wts["rwkv_ln_g"], wts["rwkv_ln_b"], nb=RWKV_SEQS_PER_STEP)
    y_g, s_gla = _gla(p_gla.reshape(b, t, GLA_PROJ_PAD), s_gla0, wts["gla_alpha_up"],
                      wts["gla_alpha_b"], wts["gla_norm_g"], nb=GLA_SEQS_PER_STEP)

    x1, h2_all, route_all = _merge(
        y_r.reshape(rows.shape), y_g.reshape(rows.shape), gates, rows, mod, wts["w_branch_rwkv"],
        wts["w_branch_gla"], wts["w_out"], wts["norm2_g"], wts["router_cat"],
        tm=tm, n_tok=n_tok, row_off=row_off, shared=shared)
    states = (p_rwkv[:, t - 1:t, :][None], s_rwkv[None], s_gla[None])
    return x1, h2_all, route_all, states


def kernel(x_prompt, x_sample, c_prompt, c_sample, state_rwkv_shift, state_rwkv, state_gla, w_ada, b_ada, norm1_g, w_in, rwkv_mu, rwkv_w0, rwkv_decay_up, rwkv_a0, rwkv_aaa_up, rwkv_gate_up, rwkv_k_k, rwkv_k_a, rwkv_r_k, rwkv_ln_g, rwkv_ln_b, gla_alpha_up, gla_alpha_b, gla_norm_g, w_branch_rwkv, w_branch_gla, w_out, norm2_g, router_group, router_expert, expert_w_gate, expert_w_up, expert_w_down, final_norm_g):
    bp, tp = x_prompt.shape[:2]
    bs, ts = x_sample.shape[:2]
    w_in0 = w_in[0]
    g0 = RWKV_PROJ
    o3 = 2 * GLA_DK_TOTAL + GLA_DV_TOTAL
    o4 = o3 + GLA_GATE_RANK
    w_gla = w_in0[:, g0:g0 + GLA_PROJ]
    w_gla = jnp.concatenate([w_gla[:, :o3], w_gla[:, o4:], w_gla[:, o3:o4],
                             jnp.zeros((D_MODEL, GLA_PROJ_PAD - GLA_PROJ), F32)], axis=1)
    wts = dict(
        norm1_g=norm1_g[0].reshape(1, -1),
        w_in_rwkv=w_in0[:, :g0].astype(BF16),
        w_in_gla=w_gla.astype(BF16),
        w_in_gate=w_in0[:, g0 + GLA_PROJ:].astype(BF16),
        rwkv_mu=rwkv_mu[0], rwkv_w0=rwkv_w0[0], rwkv_decay_up=rwkv_decay_up[0], rwkv_a0=rwkv_a0[0],
        rwkv_aaa_up=rwkv_aaa_up[0], rwkv_gate_up=rwkv_gate_up[0], rwkv_k_k=rwkv_k_k[0],
        rwkv_k_a=rwkv_k_a[0], rwkv_r_k=rwkv_r_k[0], rwkv_ln_g=rwkv_ln_g[0], rwkv_ln_b=rwkv_ln_b[0],
        gla_alpha_up=jnp.pad(gla_alpha_up[0], ((0, LANES - GLA_GATE_RANK), (0, 0))),
        gla_alpha_b=gla_alpha_b[0], gla_norm_g=gla_norm_g[0],
        w_branch_rwkv=w_branch_rwkv[0].astype(BF16), w_branch_gla=w_branch_gla[0].astype(BF16),
        w_out=w_out[0].astype(BF16), norm2_g=norm2_g[0],
        router_cat=_split_hi_lo(jnp.pad(jnp.concatenate([router_expert[0], router_group[0]], axis=1),
                                        ((0, 0), (0, LANES - N_EXPERTS - N_GROUPS)))),
        expert_w_gate=expert_w_gate[0], expert_w_up=expert_w_up[0], expert_w_down=expert_w_down[0],
    )
    mod = _ada(jnp.concatenate([c_prompt, c_sample], axis=0), w_ada[0], b_ada[0])
    mod = mod.reshape(bp + bs, N_MOD, D_MODEL)
    n_p = bp * tp
    n_s = bs * ts
    n_tok = n_p + n_s
    mod_p = mod[:bp].reshape(bp, N_MOD, 1, D_MODEL)
    mod_s = jnp.repeat(mod[bp:].transpose(1, 0, 2), ts, axis=1)[None]

    dt = x_prompt.dtype
    x1_p, h2_all, route_all, states_p = _mixer(
        x_prompt, mod_p, jnp.zeros((bp, 1, RWKV_PROJ), dt),
        jnp.zeros((bp, RWKV_HEADS, RWKV_HEAD_DIM, RWKV_HEAD_DIM), state_rwkv.dtype),
        jnp.zeros((bp, GLA_HEADS, GLA_DK, GLA_DV), state_gla.dtype), wts, n_tok=n_tok, row_off=0)
    x1_s, h2_all, route_all, states_s = _mixer(
        x_sample, mod_s, state_rwkv_shift[0], state_rwkv[0], state_gla[0], wts,
        n_tok=n_tok, row_off=n_p, shared=(h2_all, route_all))

    expert_ids = route_all[:, :2].astype(jnp.int32)
    moe_out = _moe_grouped(h2_all, expert_ids, wts["expert_w_gate"], wts["expert_w_up"],
                           wts["expert_w_down"])
    y_p = _final(x1_p, moe_out, mod_p, final_norm_g, row_off=0, tm=min(tp, WIDE_ROW_TILE))
    y_s = _final(x1_s, moe_out, mod_s, final_norm_g, row_off=n_p, tm=min(n_s, WIDE_ROW_TILE))
    return (y_p, y_s.reshape(bs, ts, D_MODEL)) + states_p + states_s
```

```python
import functools
import math

import jax
import jax.numpy as jnp
from jax import lax
from jax.experimental import pallas as pl
from jax.experimental.pallas import tpu as pltpu

F32 = jnp.float32
BF16 = jnp.bfloat16

D_MODEL = 1024
N_MOD = 6
EPS = 1e-6
RWKV_HEAD_DIM = 64
RWKV_HEADS = 16
RWKV_PAIRS = RWKV_HEADS // 2
DECAY_LORA = 64
AAA_LORA = 64
GATE_LORA = 128
RWKV_PROJ = 3 * D_MODEL + DECAY_LORA + AAA_LORA + GATE_LORA
LN_X_EPS = 64e-5
GLA_HEADS = 4
GLA_DK = 128
GLA_DV = 256
GLA_DK_TOTAL = GLA_HEADS * GLA_DK
GLA_DV_TOTAL = GLA_HEADS * GLA_DV
GLA_GATE_RANK = 16
GLA_GATE_NORM = 16.0
GLA_PROJ = 2 * GLA_DK_TOTAL + 2 * GLA_DV_TOTAL + GLA_GATE_RANK
LANES = 128
GLA_PROJ_PAD = 2 * GLA_DK_TOTAL + 2 * GLA_DV_TOTAL + LANES
GATE_PROJ = 2 * D_MODEL
N_GROUPS = 4
EXPERTS_PER_GROUP = 8
N_EXPERTS = 32
EXPERT_FF = 512

CHUNK = 64
ROW_TILE = 512
WIDE_ROW_TILE = 1024
MOE_TILE = 192
DMA_GROUP = 8
MOE_ROW = D_MODEL + LANES
N_PAIR_CLASSES = N_EXPERTS * EXPERTS_PER_GROUP
RWKV_SEQS_PER_STEP = 4
GLA_SEQS_PER_STEP = 8
VMEM_LIMIT = 56 * 1024 * 1024


def _bdot(a, b):
    return jnp.dot(a.astype(BF16), b.astype(BF16), preferred_element_type=F32)


def _bdot_nt(a, b):
    return lax.dot_general(a.astype(BF16), b.astype(BF16), (((1,), (1,)), ((), ())),
                           preferred_element_type=F32)


def _bdot_tn(a, b):
    return jnp.dot(a.T.astype(BF16), b.astype(BF16), preferred_element_type=F32)


def _split3(x):
    h1 = x.astype(BF16)
    r1 = x - h1.astype(F32)
    h2 = r1.astype(BF16)
    h3 = (r1 - h2.astype(F32)).astype(BF16)
    return h1, h2, h3


def _dot3(a, b):
    a1, a2, _ = _split3(a)
    b1, b2, _ = _split3(b)
    return (jnp.dot(a1, b1, preferred_element_type=F32)
            + jnp.dot(a2, b1, preferred_element_type=F32)
            + jnp.dot(a1, b2, preferred_element_type=F32))


def _dot_exact_lhs(a_bf16, x):
    x1, x2, _ = _split3(x)
    return (jnp.dot(a_bf16, x1, preferred_element_type=F32)
            + jnp.dot(a_bf16, x2, preferred_element_type=F32))


def _split_hi_lo(w):
    hi, lo, _ = _split3(w)
    return jnp.concatenate([hi, lo], axis=1)


def _sigmoid(x):
    return 1.0 / (1.0 + jnp.exp(-x))


def _silu(x):
    return x * _sigmoid(x)


def _rms_scale(x):
    return x * lax.rsqrt(jnp.mean(x * x, axis=-1, keepdims=True) + EPS)


def _ada_kernel(c_ref, w_ref, b_ref, o_ref):
    o_ref[...] = _dot3(_silu(c_ref[...]), w_ref[...]) + b_ref[...]


def _ada(c, w_ada, b_ada):
    n_rows = c.shape[0]
    n_out = w_ada.shape[1]
    tn = 1536
    return pl.pallas_call(
        _ada_kernel,
        out_shape=jax.ShapeDtypeStruct((n_rows, n_out), F32),
        grid=(n_out // tn,),
        in_specs=[pl.BlockSpec((n_rows, D_MODEL), lambda j: (0, 0)),
                  pl.BlockSpec((D_MODEL, tn), lambda j: (0, j)),
                  pl.BlockSpec((1, tn), lambda j: (0, j))],
        out_specs=pl.BlockSpec((n_rows, tn), lambda j: (0, j)),
        compiler_params=pltpu.CompilerParams(dimension_semantics=("parallel",),
                                             vmem_limit_bytes=VMEM_LIMIT),
        name="ada",
    )(c, w_ada, b_ada.reshape(1, n_out))


def _mod_spec(mod, tm, seq_tile):
    if mod.shape[2] == 1:
        return pl.BlockSpec((1, N_MOD, 1, D_MODEL), lambda *g: (seq_tile(*g)[0], 0, 0, 0))
    return pl.BlockSpec((1, N_MOD, tm, D_MODEL), lambda *g: (seq_tile(*g)[0], 0, seq_tile(*g)[1], 0))


def _norm_proj_kernel(x_ref, mod_ref, g_ref, w_ref, o_ref, *, gate):
    mod = mod_ref[0]
    h = _rms_scale(x_ref[0]) * g_ref[...] * (1.0 + mod[1]) + mod[0]
    out = _bdot(h, w_ref[...])
    o_ref[0] = _sigmoid(out) if gate else out


def _norm_proj(x, mod, g, w_bf16, *, gate, tm):
    b, t, _ = x.shape
    n = w_bf16.shape[1]
    return pl.pallas_call(
        functools.partial(_norm_proj_kernel, gate=gate),
        out_shape=jax.ShapeDtypeStruct((b, t, n), F32),
        grid=(b, t // tm),
        in_specs=[pl.BlockSpec((1, tm, D_MODEL), lambda i, j: (i, j, 0)),
                  _mod_spec(mod, tm, lambda i, j: (i, j)),
                  pl.BlockSpec((1, D_MODEL), lambda i, j: (0, 0)),
                  pl.BlockSpec((D_MODEL, n), lambda i, j: (0, 0))],
        out_specs=pl.BlockSpec((1, tm, n), lambda i, j: (i, j, 0)),
        compiler_params=pltpu.CompilerParams(dimension_semantics=("parallel", "parallel"),
                                             vmem_limit_bytes=VMEM_LIMIT),
        name="norm_proj_gate" if gate else "norm_proj",
    )(x, mod, g, w_bf16)


def _blockdiag(x, lo_mask):
    return jnp.concatenate([jnp.where(lo_mask, x, 0.0), jnp.where(lo_mask, 0.0, x)], axis=0)


def _rwkv_kernel(p_ref, shift_ref, s0_ref, mu_ref, w0_ref, dup_ref, a0_ref, aup_ref, gup_ref,
                 kk_ref, ka_ref, rk_ref, lng_ref, lnb_ref, y_ref, sout_ref, s_scr, carry_scr,
                 *, nb):
    c = pl.program_id(1)
    n_chunks = pl.num_programs(1)
    C = CHUNK
    N = RWKV_HEAD_DIM
    t_blk = p_ref.shape[1]

    @pl.when(c == 0)
    def _():
        for bl in range(nb):
            for pr in range(RWKV_PAIRS):
                s_scr[bl, pr] = jnp.concatenate([s0_ref[bl, 2 * pr], s0_ref[bl, 2 * pr + 1]], axis=1)
        carry_scr[...] = shift_ref[...]

    row = lax.broadcasted_iota(jnp.int32, (C, LANES), 0)
    lane = lax.broadcasted_iota(jnp.int32, (C, LANES), 1)
    lane_t = jnp.where(lane < N, lane, lane - N)
    lo = lane < N
    strict = lane_t < row
    incl = lane_t <= row
    eye = (lane_t == row).astype(F32)
    valid = row < t_blk
    tri = (lax.broadcasted_iota(jnp.int32, (C, C), 1)
           <= lax.broadcasted_iota(jnp.int32, (C, C), 0)).astype(BF16)
    sq_r = lax.broadcasted_iota(jnp.int32, (LANES, LANES), 0)
    sq_c = lax.broadcasted_iota(jnp.int32, (LANES, LANES), 1)
    same_head = (sq_r < N) == (sq_c < N)
    diag = sq_r == sq_c
    row_w = lax.broadcasted_iota(jnp.int32, (C, RWKV_PROJ), 0)
    o3 = 3 * D_MODEL

    def hsum(v):
        s_lo = jnp.sum(jnp.where(lo, v, 0.0), axis=1, keepdims=True)
        s_hi = jnp.sum(jnp.where(lo, 0.0, v), axis=1, keepdims=True)
        return jnp.where(lo, s_lo, s_hi)

    def bd(v):
        return _blockdiag(v, lo)

    def cat0(*a):
        return jnp.concatenate(a, axis=0)

    def cat1(*a):
        return jnp.concatenate(a, axis=1)

    valid_w = lax.broadcasted_iota(jnp.int32, (C, D_MODEL), 0) < t_blk
    xs, logw_all, cum_all, asig_all, g_all = [], [], [], [], []
    for bl in range(nb):
        p = p_ref[bl]
        carry_new = p[t_blk - 1:t_blk, :]
        if t_blk < C:
            p = jnp.concatenate([p, jnp.zeros((C - t_blk, RWKV_PROJ), F32)], axis=0)
        prev = jnp.where(row_w == 0, carry_scr[bl], pltpu.roll(p, 1, axis=0))
        carry_scr[bl] = carry_new
        x = p + (prev - p) * mu_ref[...]
        xw = x[:, o3:o3 + DECAY_LORA]
        xa = x[:, o3 + DECAY_LORA:o3 + DECAY_LORA + AAA_LORA]
        xg = x[:, o3 + DECAY_LORA + AAA_LORA:]
        z = w0_ref[...] + _bdot(jnp.tanh(xw), dup_ref[...])
        logw_full = -math.exp(-0.5) * _sigmoid(z)
        if t_blk < C:
            logw_full = jnp.where(valid_w, logw_full, 0.0)
        xs.append(x)
        logw_all.append(logw_full)
        cum_all.append(_dot_exact_lhs(tri, logw_full))
        asig_all.append(_sigmoid(a0_ref[...] + _bdot(xa, aup_ref[...])))
        g_all.append(_bdot(_sigmoid(xg), gup_ref[...]))

    units = [(bl, pr) for bl in range(nb) for pr in range(RWKV_PAIRS)]
    U = range(len(units))
    cols = [slice(pr * LANES, (pr + 1) * LANES) for _, pr in units]
    r = [xs[bl][:, pr * LANES:(pr + 1) * LANES] for bl, pr in units]
    k = [xs[bl][:, D_MODEL + pr * LANES:D_MODEL + (pr + 1) * LANES] for bl, pr in units]
    v = [xs[bl][:, 2 * D_MODEL + pr * LANES:2 * D_MODEL + (pr + 1) * LANES] for bl, pr in units]
    logw = [logw_all[bl][:, cols[u]] for u, (bl, _) in enumerate(units)]
    asig = [asig_all[bl][:, cols[u]] for u, (bl, _) in enumerate(units)]
    kk = [k[u] * kk_ref[:, cols[u]] for u in U]
    k2 = [k[u] * (1.0 + (asig[u] - 1.0) * ka_ref[:, cols[u]]) for u in U]
    ss = [hsum(kk[u] * kk[u]) for u in U]
    cum = [cum_all[bl][:, cols[u]] for u, (bl, _) in enumerate(units)]
    bsum = [hsum(r[u] * k2[u] * rk_ref[:, cols[u]]) for u in U]
    kk = [kk[u] * lax.rsqrt(jnp.maximum(ss[u], 1e-24)) for u in U]
    bonus = [bsum[u] * v[u] for u in U]
    if t_blk < C:
        kk = [jnp.where(valid, a, 0.0) for a in kk]
        k2 = [jnp.where(valid, a, 0.0) for a in k2]
        v = [jnp.where(valid, a, 0.0) for a in v]
        r = [jnp.where(valid, a, 0.0) for a in r]
    b_vec = [kk[u] * asig[u] for u in U]
    clast = [cum[u][C - 1:C, :] for u in U]
    e_inv = [jnp.exp(-cum[u]) for u in U]
    e_tail = [jnp.exp(clast[u] - cum[u]) for u in U]
    aq = [-kk[u] * jnp.exp(cum[u] - logw[u]) for u in U]
    rq = [r[u] * jnp.exp(cum[u]) for u in U]
    bk = [b_vec[u] * e_inv[u] for u in U]
    kx = [k2[u] * e_inv[u] for u in U]
    bb = [b_vec[u] * e_tail[u] for u in U]
    kb = [k2[u] * e_tail[u] for u in U]

    g4 = [_bdot_nt(cat0(aq[u], rq[u]), cat0(bd(bk[u]), bd(kx[u]))) for u in U]
    m_ab = [jnp.where(strict, g4[u][:C, :LANES], 0.0) for u in U]
    m_ak = [jnp.where(strict, g4[u][:C, LANES:], 0.0) for u in U]
    m_rb = [jnp.where(incl, g4[u][C:, :LANES], 0.0) for u in U]
    m_rk = [jnp.where(incl, g4[u][C:, LANES:], 0.0) for u in U]

    t_inv = [eye + m_ab[u] for u in U]
    l_pow = [_bdot(m_ab[u], bd(m_ab[u])) for u in U]
    mv = [_bdot(cat0(m_ak[u], m_rk[u]), bd(v[u])) for u in U]
    n_steps = C.bit_length() - 1
    for step in range(1, n_steps):
        if step < n_steps - 1:
            both = [_bdot(l_pow[u], cat1(bd(t_inv[u]), bd(l_pow[u]))) for u in U]
            t_inv = [t_inv[u] + both[u][:, :LANES] for u in U]
            l_pow = [both[u][:, LANES:] for u in U]
        else:
            t_inv = [t_inv[u] + _bdot(l_pow[u], bd(t_inv[u])) for u in U]

    au = [_bdot(t_inv[u], cat1(bd(aq[u]), bd(mv[u][:C]))) for u in U]
    a_bar = [au[u][:, :LANES] for u in U]
    u_bar = [au[u][:, LANES:] for u in U]
    mau = [_bdot(m_rb[u], cat1(bd(a_bar[u]), bd(u_bar[u]))) for u in U]
    p_full = [_bdot_tn(a_bar[u], bb[u]) for u in U]
    q_full = [_bdot_tn(cat0(u_bar[u], v[u]), cat0(bb[u], kb[u])) for u in U]
    r_til = [rq[u] + mau[u][:, :LANES] for u in U]
    y_loc = [mau[u][:, LANES:] + mv[u][C:] for u in U]
    p_bd = [jnp.where(same_head, p_full[u], 0.0) + jnp.where(diag, jnp.exp(clast[u]), 0.0) for u in U]
    q_pk = [jnp.where(lo[:N], q_full[u][:N], q_full[u][N:]) for u in U]

    s_old = [s_scr[bl, pr] for bl, pr in units]
    y = [_bdot_nt(r_til[u], _blockdiag(s_old[u], lo[:N])) + y_loc[u] for u in U]
    s_new = [_bdot(s_old[u], p_bd[u]) + q_pk[u] for u in U]
    for u, (bl, pr) in enumerate(units):
        s_scr[bl, pr] = s_new[u]

    mu_y = [hsum(y[u]) * (1.0 / N) for u in U]
    yc = [y[u] - mu_y[u] for u in U]
    var = [hsum(yc[u] * yc[u]) * (1.0 / N) for u in U]
    for u, (bl, pr) in enumerate(units):
        yn = yc[u] * lax.rsqrt(var[u] + LN_X_EPS) * lng_ref[:, cols[u]] + lnb_ref[:, cols[u]]
        y_ref[bl, :, cols[u]] = ((yn + bonus[u]) * g_all[bl][:, cols[u]])[:t_blk]

    @pl.when(c == n_chunks - 1)
    def _():
        for bl in range(nb):
            for pr in range(RWKV_PAIRS):
                sout_ref[bl, 2 * pr] = s_scr[bl, pr][:, :N]
                sout_ref[bl, 2 * pr + 1] = s_scr[bl, pr][:, N:]


def _rwkv(p_rwkv, shift_prev, s0, mu, w0, decay_up, a0, aaa_up, gate_up, k_k, k_a, r_k,
          ln_g, ln_b, *, nb):
    b, t, _ = p_rwkv.shape
    nb = math.gcd(b, nb)
    t_blk = min(t, CHUNK)
    row = lambda a: a.reshape(1, -1)
    full = lambda shape: pl.BlockSpec(shape, lambda i, j: (0,) * len(shape))
    state_spec = pl.BlockSpec((nb, RWKV_HEADS, RWKV_HEAD_DIM, RWKV_HEAD_DIM), lambda i, j: (i, 0, 0, 0))
    return pl.pallas_call(
        functools.partial(_rwkv_kernel, nb=nb),
        out_shape=(jax.ShapeDtypeStruct((b, t, D_MODEL), F32),
                   jax.ShapeDtypeStruct((b, RWKV_HEADS, RWKV_HEAD_DIM, RWKV_HEAD_DIM), F32)),
        grid=(b // nb, t // t_blk),
        in_specs=[pl.BlockSpec((nb, t_blk, RWKV_PROJ), lambda i, j: (i, j, 0)),
                  pl.BlockSpec((nb, 1, RWKV_PROJ), lambda i, j: (i, 0, 0)),
                  state_spec,
                  full((1, RWKV_PROJ)), full((1, D_MODEL)), full((DECAY_LORA, D_MODEL)),
                  full((1, D_MODEL)), full((AAA_LORA, D_MODEL)), full((GATE_LORA, D_MODEL)),
                  full((1, D_MODEL)), full((1, D_MODEL)), full((1, D_MODEL)),
                  full((1, D_MODEL)), full((1, D_MODEL))],
        out_specs=(pl.BlockSpec((nb, t_blk, D_MODEL), lambda i, j: (i, j, 0)), state_spec),
        scratch_shapes=[pltpu.VMEM((nb, RWKV_PAIRS, RWKV_HEAD_DIM, LANES), F32),
                        pltpu.VMEM((nb, 1, RWKV_PROJ), F32)],
        compiler_params=pltpu.CompilerParams(dimension_semantics=("parallel", "arbitrary"),
                                             vmem_limit_bytes=VMEM_LIMIT),
        name="rwkv_scan",
    )(p_rwkv, shift_prev, s0, row(mu), row(w0), decay_up, row(a0), aaa_up, gate_up,
      row(k_k), row(k_a), row(r_k), row(ln_g), row(ln_b))


def _gla_kernel(p_ref, s0_ref, aup_ref, ab_ref, ng_ref, y_ref, sout_ref, s_scr, *, nb):
    c = pl.program_id(1)
    n_chunks = pl.num_programs(1)
    C = CHUNK
    t_blk = p_ref.shape[1]
    units = [(bl, h) for bl in range(nb) for h in range(GLA_HEADS)]
    U = range(len(units))

    @pl.when(c == 0)
    def _():
        for bl, h in units:
            s_scr[bl, h] = s0_ref[bl, h].T

    o_k = GLA_DK_TOTAL
    o_v = 2 * GLA_DK_TOTAL
    o_r = o_v + GLA_DV_TOTAL
    o_a = o_r + GLA_DV_TOTAL
    tri = (lax.broadcasted_iota(jnp.int32, (C, C), 1)
           <= lax.broadcasted_iota(jnp.int32, (C, C), 0))
    tri_bf = tri.astype(BF16)
    valid_g = lax.broadcasted_iota(jnp.int32, (C, GLA_DK_TOTAL), 0) < t_blk
    valid_k = lax.broadcasted_iota(jnp.int32, (C, GLA_DK), 0) < t_blk
    valid_v = lax.broadcasted_iota(jnp.int32, (C, GLA_DV), 0) < t_blk

    ps, cums = [], []
    for bl in range(nb):
        p = p_ref[bl]
        if t_blk < C:
            p = jnp.concatenate([p, jnp.zeros((C - t_blk, GLA_PROJ_PAD), F32)], axis=0)
        logit = _dot3(p[:, o_a:], aup_ref[...]) + ab_ref[...]
        g = (jnp.minimum(logit, 0.0) - jnp.log(1.0 + jnp.exp(-jnp.abs(logit)))) * (1.0 / GLA_GATE_NORM)
        if t_blk < C:
            g = jnp.where(valid_g, g, 0.0)
        ps.append(p)
        cums.append(_dot_exact_lhs(tri_bf, g))

    q = [ps[bl][:, h * GLA_DK:(h + 1) * GLA_DK] * (GLA_DK ** -0.5) for bl, h in units]
    k = [ps[bl][:, o_k + h * GLA_DK:o_k + (h + 1) * GLA_DK] for bl, h in units]
    v = [ps[bl][:, o_v + h * GLA_DV:o_v + (h + 1) * GLA_DV] for bl, h in units]
    if t_blk < C:
        k = [jnp.where(valid_k, a, 0.0) for a in k]
        v = [jnp.where(valid_v, a, 0.0) for a in v]
    b = [cums[bl][:, h * GLA_DK:(h + 1) * GLA_DK] for bl, h in units]
    clast = [b[u][C - 1:C, :] for u in U]
    cmid = [b[u][C // 2 - 1:C // 2, :] for u in U]
    att = [_bdot_nt(q[u] * jnp.exp(b[u] - cmid[u]), k[u] * jnp.exp(cmid[u] - b[u])) for u in U]
    att = [jnp.where(tri, a, 0.0) for a in att]
    s_t = [s_scr[bl, h] for bl, h in units]
    o = [_bdot(att[u], v[u]) + _bdot_nt(q[u] * jnp.exp(b[u]), s_t[u]) for u in U]
    s_new = [s_t[u] * jnp.exp(clast[u]) + _bdot_tn(v[u], k[u] * jnp.exp(clast[u] - b[u])) for u in U]
    for u, (bl, h) in enumerate(units):
        s_scr[bl, h] = s_new[u]
        r = ps[bl][:, o_r + h * GLA_DV:o_r + (h + 1) * GLA_DV]
        on = o[u] * lax.rsqrt(jnp.mean(o[u] * o[u], axis=-1, keepdims=True) + EPS) * ng_ref[...]
        y_ref[bl, :, h * GLA_DV:(h + 1) * GLA_DV] = (on * _silu(r))[:t_blk]

    @pl.when(c == n_chunks - 1)
    def _():
        for bl, h in units:
            sout_ref[bl, h] = s_scr[bl, h].T


def _gla(p_gla, s0, alpha_up_pad, alpha_b, norm_g, *, nb):
    b, t, _ = p_gla.shape
    nb = math.gcd(b, nb)
    t_blk = min(t, CHUNK)
    full = lambda shape: pl.BlockSpec(shape, lambda i, j: (0,) * len(shape))
    state_spec = pl.BlockSpec((nb, GLA_HEADS, GLA_DK, GLA_DV), lambda i, j: (i, 0, 0, 0))
    return pl.pallas_call(
        functools.partial(_gla_kernel, nb=nb),
        out_shape=(jax.ShapeDtypeStruct((b, t, GLA_DV_TOTAL), F32),
                   jax.ShapeDtypeStruct((b, GLA_HEADS, GLA_DK, GLA_DV), F32)),
        grid=(b // nb, t // t_blk),
        in_specs=[pl.BlockSpec((nb, t_blk, GLA_PROJ_PAD), lambda i, j: (i, j, 0)),
                  state_spec,
                  full((LANES, GLA_DK_TOTAL)), full((1, GLA_DK_TOTAL)), full((1, GLA_DV))],
        out_specs=(pl.BlockSpec((nb, t_blk, GLA_DV_TOTAL), lambda i, j: (i, j, 0)), state_spec),
        scratch_shapes=[pltpu.VMEM((nb, GLA_HEADS, GLA_DV, GLA_DK), F32)],
        compiler_params=pltpu.CompilerParams(dimension_semantics=("parallel", "arbitrary"),
                                             vmem_limit_bytes=VMEM_LIMIT),
        name="gla_scan",
    )(p_gla, s0, alpha_up_pad, alpha_b.reshape(1, -1), norm_g.reshape(1, -1))


def _merge_body(yr_ref, yg_ref, gate_ref, x_ref, mod_ref, wbr_ref, wbg_ref, wout_ref, g2_ref,
                router_ref, x1_ref, h2_ref, route_ref):
    mod = mod_ref[0]
    gates = gate_ref[0]
    merged = (gates[:, :D_MODEL] * _bdot(yr_ref[0], wbr_ref[...])
              + gates[:, D_MODEL:] * _bdot(yg_ref[0], wbg_ref[...]))
    x1 = x_ref[0] + mod[2] * _bdot(merged, wout_ref[...])
    x1_ref[0] = x1
    h2 = _rms_scale(x1) * g2_ref[...] * (1.0 + mod[4]) + mod[3]
    h2_ref[:, :D_MODEL] = h2

    tm = h2.shape[0]
    h_hi, h_lo, _ = _split3(h2)
    prod = jnp.dot(jnp.concatenate([h_hi, h_lo], axis=0), router_ref[...], preferred_element_type=F32)
    logits = prod[:tm, :LANES] + prod[tm:, :LANES] + prod[:tm, LANES:]
    lane = lax.broadcasted_iota(jnp.int32, (tm, LANES), 1)
    neg = -jnp.inf
    is_group = (lane >= N_EXPERTS) & (lane < N_EXPERTS + N_GROUPS)
    gl = jnp.where(is_group, logits, neg)
    gmax = jnp.max(gl, axis=1, keepdims=True)
    g_idx = jnp.min(jnp.where(gl == gmax, lane, LANES), axis=1, keepdims=True) - N_EXPERTS
    p_g = 1.0 / jnp.sum(jnp.exp(gl - gmax), axis=1, keepdims=True)
    in_group = (lane >= g_idx * EXPERTS_PER_GROUP) & (lane < (g_idx + 1) * EXPERTS_PER_GROUP)
    el = jnp.where(in_group, logits, neg)
    v1 = jnp.max(el, axis=1, keepdims=True)
    i1 = jnp.min(jnp.where(el == v1, lane, LANES), axis=1, keepdims=True)
    el2 = jnp.where(lane == i1, neg, el)
    v2 = jnp.max(el2, axis=1, keepdims=True)
    i2 = jnp.min(jnp.where(el2 == v2, lane, LANES), axis=1, keepdims=True)
    e21 = jnp.exp(v2 - v1)
    w1 = p_g / (1.0 + e21)
    route = (jnp.where(lane == 0, i1.astype(F32), 0.0) + jnp.where(lane == 1, i2.astype(F32), 0.0)
             + jnp.where(lane == 2, w1, 0.0) + jnp.where(lane == 3, w1 * e21, 0.0))
    route_ref[...] = route
    h2_ref[:, D_MODEL:] = route


def _merge_first_kernel(*refs, n_own):
    step = pl.program_id(0)

    @pl.when(step < n_own)
    def _():
        _merge_body(*refs)

    @pl.when(step >= n_own)
    def _():
        refs[-2][...] = jnp.zeros(refs[-2].shape, F32)
        refs[-1][...] = jnp.zeros(refs[-1].shape, F32)


def _merge_second_kernel(*refs):
    n_in = 10
    _merge_body(*refs[:n_in], *refs[n_in + 2:])


def _merge(y_r, y_g, gates, x, mod, wbr, wbg, wout, norm2_g, router_cat, *, tm, n_tok, row_off, shared=None):
    b, t, _ = x.shape
    per_seq = t // tm
    n_own = b * per_seq
    first = shared is None
    n_steps = n_tok // tm if first else n_own
    assert row_off % tm == 0 and n_tok % tm == 0 and (row_off == 0 or not first)
    blk0 = row_off // tm
    own = lambda s: jnp.minimum(s, n_own - 1)
    full = lambda shape: pl.BlockSpec(shape, lambda s: (0,) * len(shape))
    tile = lambda n: pl.BlockSpec((1, tm, n), lambda s: (own(s) // per_seq, own(s) % per_seq, 0))
    flat = lambda n: pl.BlockSpec((tm, n), lambda s: (blk0 + s, 0))
    in_specs = [tile(D_MODEL), tile(D_MODEL), tile(GATE_PROJ), tile(D_MODEL),
                _mod_spec(mod, tm, lambda s: (own(s) // per_seq, own(s) % per_seq)),
                full((D_MODEL, D_MODEL)), full((D_MODEL, D_MODEL)), full((D_MODEL, D_MODEL)),
                full((1, D_MODEL)), full((D_MODEL, 2 * LANES))]
    args = [y_r, y_g, gates, x, mod, wbr, wbg, wout, norm2_g.reshape(1, -1), router_cat]
    if not first:
        in_specs += [pl.BlockSpec(memory_space=pl.ANY)] * 2
        args += list(shared)
    return pl.pallas_call(
        functools.partial(_merge_first_kernel, n_own=n_own) if first else _merge_second_kernel,
        out_shape=(jax.ShapeDtypeStruct((b, t, D_MODEL), F32),
                   jax.ShapeDtypeStruct((n_tok, MOE_ROW), F32),
                   jax.ShapeDtypeStruct((n_tok, LANES), F32)),
        grid=(n_steps,),
        in_specs=in_specs,
        out_specs=(tile(D_MODEL), flat(MOE_ROW), flat(LANES)),
        input_output_aliases={} if first else {10: 1, 11: 2},
        compiler_params=pltpu.CompilerParams(dimension_semantics=("arbitrary",),
                                             vmem_limit_bytes=VMEM_LIMIT),
        name="merge_router",
    )(*args)


def _moe_plan(expert_ids, n_tok):
    tm = MOE_TILE
    n_tiles = -(-n_tok // tm) + N_GROUPS * (EXPERTS_PER_GROUP * (EXPERTS_PER_GROUP - 1) // 2)
    n_slots = n_tiles * tm
    n_fill = n_slots - n_tok
    row_bits = 15
    assert max(n_tok, n_fill) <= 1 << row_bits and N_PAIR_CLASSES << (row_bits + 1) <= 1 << 24
    e_lo = jnp.minimum(expert_ids[:, 0], expert_ids[:, 1])
    e_hi = jnp.maximum(expert_ids[:, 0], expert_ids[:, 1])
    cls = e_lo * EXPERTS_PER_GROUP + e_hi % EXPERTS_PER_GROUP
    c_iota = jnp.arange(N_PAIR_CLASSES, dtype=jnp.int32)
    counts = jnp.sum((cls[None, :] == c_iota[:, None]).astype(jnp.int32), axis=1)
    tiles_c = (counts + tm - 1) // tm
    n_used = jnp.sum(tiles_c)
    pad_end = jnp.cumsum(tiles_c * tm - counts)
    f_iota = jnp.arange(n_fill, dtype=jnp.int32)
    f_cls = jnp.minimum(jnp.sum((f_iota[:, None] >= pad_end[None, :]).astype(jnp.int32), axis=1), N_PAIR_CLASSES - 1)
    keys = jnp.concatenate([(cls << (row_bits + 1)) + jnp.arange(n_tok, dtype=jnp.int32),
                            (f_cls << (row_bits + 1)) + (1 << row_bits) + f_iota])
    neg_sorted, _ = lax.top_k(-keys.astype(F32), n_slots)
    slots = (-neg_sorted).astype(jnp.int32).reshape(n_tiles, tm)
    valid = (slots & (1 << row_bits)) == 0
    tok = jnp.where(valid, slots & ((1 << row_bits) - 1), 0)
    t_idx = jnp.arange(n_tiles, dtype=jnp.int32)
    j_idx = jnp.arange(tm, dtype=jnp.int32)[None, :]
    dst = jnp.where(valid, tok, n_tok + (t_idx[:, None] % 2) * DMA_GROUP + j_idx % DMA_GROUP)
    n_valid = jnp.sum(valid.astype(jnp.int32), axis=1)
    n_groups = jnp.where(t_idx < n_used, (n_valid + DMA_GROUP - 1) // DMA_GROUP, 0)
    tile_c = slots[:, 0] >> (row_bits + 1)
    last_c = lax.dynamic_slice(tile_c, (jnp.maximum(n_used - 1, 0),), (1,))
    tile_c = jnp.where(t_idx < n_used, tile_c, last_c)
    tile_lo = tile_c // EXPERTS_PER_GROUP
    tile_hi = (tile_c // (EXPERTS_PER_GROUP * EXPERTS_PER_GROUP)) * EXPERTS_PER_GROUP + tile_c % EXPERTS_PER_GROUP
    return (tile_lo, tile_hi, n_groups.astype(jnp.int32), n_used.reshape(1).astype(jnp.int32),
            tok.reshape(n_tiles, 1, tm), dst.reshape(n_tiles, 1, tm))


def _moe_pair_kernel(lo_ref, hi_ref, ng_ref, n_used_ref, tok_ref, tok_next_ref, dst_ref, h2_hbm,
                     wg_hbm, wu_hbm, wd_hbm, out_hbm,
                     xbuf, obuf, w_res, w_stage, gsem, ssem, wsem):
    t = pl.program_id(0)
    n_t = pl.num_programs(0)
    tm = MOE_TILE
    slot = lax.rem(t, 2)
    n_used = n_used_ref[0]

    def gather(idx_ref, s, n_groups):
        def body(g, carry):
            for u in range(DMA_GROUP):
                pltpu.make_async_copy(h2_hbm.at[pl.ds(idx_ref[0, 0, g * DMA_GROUP + u], 1)],
                                      xbuf.at[s, g, pl.ds(u, 1)], gsem.at[s]).start()
            return carry
        lax.fori_loop(0, n_groups, body, 0)

    def wait_gather(s, n_groups):
        def body(g, carry):
            pltpu.make_async_copy(h2_hbm.at[pl.ds(0, DMA_GROUP)], xbuf.at[s, 0], gsem.at[s]).wait()
            return carry
        lax.fori_loop(0, n_groups, body, 0)

    def wait_scatter(s, n_groups):
        def body(g, carry):
            pltpu.make_async_copy(obuf.at[s, 0], out_hbm.at[pl.ds(0, DMA_GROUP)], ssem.at[s]).wait()
            return carry
        lax.fori_loop(0, n_groups, body, 0)

    @pl.when(t == 0)
    def _():
        xbuf[...] = jnp.zeros(xbuf.shape, F32)
        obuf[0, 0] = jnp.zeros(obuf.shape[2:], F32)
        n_rows = out_hbm.shape[0]
        for half in range(2):
            fill = pltpu.make_async_copy(obuf.at[0, 0], out_hbm.at[pl.ds(n_rows - (2 - half) * DMA_GROUP, DMA_GROUP)],
                                         ssem.at[0])
            fill.start()
            fill.wait()
        gather(tok_ref, 0, ng_ref[0])

    @pl.when((t >= 2) & (t - 2 < n_used))
    def _():
        wait_scatter(slot, ng_ref[jnp.maximum(t - 2, 0)])

    @pl.when(t < n_used)
    def _():
        n_here = ng_ref[t]
        wait_gather(slot, n_here)

        @pl.when(t + 1 < n_used)
        def _():
            gather(tok_next_ref, 1 - slot, ng_ref[jnp.minimum(t + 1, n_t - 1)])

        group = lo_ref[t] // EXPERTS_PER_GROUP

        @pl.when((t == 0) | (group != lo_ref[jnp.maximum(t - 1, 0)] // EXPERTS_PER_GROUP))
        def _():
            def fetch(e, buf):
                return [pltpu.make_async_copy(src.at[group * EXPERTS_PER_GROUP + e], w_stage[m].at[buf], wsem.at[buf, m])
                        for m, src in enumerate((wg_hbm, wu_hbm, wd_hbm))]

            for cp in fetch(0, 0):
                cp.start()
            for e in range(EXPERTS_PER_GROUP):
                if e + 1 < EXPERTS_PER_GROUP:
                    for cp in fetch(e + 1, (e + 1) % 2):
                        cp.start()
                for m, cp in enumerate(fetch(e, e % 2)):
                    cp.wait()
                    w_res[m][e] = w_stage[m][e % 2].astype(BF16)

        xe = xbuf[slot].reshape(tm, MOE_ROW)
        x = xe[:, :D_MODEL].astype(BF16)
        route = xe[:, D_MODEL:]
        lane = lax.broadcasted_iota(jnp.int32, route.shape, 1)
        pick = lambda k: jnp.sum(jnp.where(lane == k, route, 0.0), axis=1, keepdims=True)
        i1, i2, w1, w2 = pick(0), pick(1), pick(2), pick(3)
        w_lo = jnp.where(i1 < i2, w1, w2)
        w_hi = jnp.where(i1 < i2, w2, w1)
        acc = None
        for e_ref, wt in ((lo_ref, w_lo), (hi_ref, w_hi)):
            e_local = lax.rem(e_ref[t], EXPERTS_PER_GROUP)
            hid = _silu(jnp.dot(x, w_res[0][e_local], preferred_element_type=F32)) * jnp.dot(
                x, w_res[1][e_local], preferred_element_type=F32)
            part = wt * _bdot(hid, w_res[2][e_local])
            acc = part if acc is None else acc + part
        obuf[slot] = acc.reshape(obuf.shape[1:])

        def body(g, carry):
            for u in range(DMA_GROUP):
                pltpu.make_async_copy(obuf.at[slot, g, pl.ds(u, 1)],
                                      out_hbm.at[pl.ds(dst_ref[0, 0, g * DMA_GROUP + u], 1)], ssem.at[slot]).start()
            return carry
        lax.fori_loop(0, n_here, body, 0)

    @pl.when(t == n_t - 1)
    def _():
        @pl.when((t >= 1) & (t - 1 < n_used))
        def _():
            wait_scatter(1 - slot, ng_ref[jnp.maximum(t - 1, 0)])

        @pl.when(t < n_used)
        def _():
            wait_scatter(slot, ng_ref[t])


def _moe_grouped(h2_ext, expert_ids, wg, wu, wd):
    return _moe_call(h2_ext, *_moe_plan(expert_ids, h2_ext.shape[0]), wg, wu, wd)


def _moe_call(h2_ext, tile_lo, tile_hi, n_groups, n_used, tok, dst, wg, wu, wd):
    n_tok = h2_ext.shape[0]
    tm = MOE_TILE
    n_tiles = tile_lo.shape[0]
    smem_tile = lambda f: pl.BlockSpec((1, 1, tm), f, memory_space=pltpu.SMEM)
    last_used = lambda t, n_used_ref: jnp.minimum(t, jnp.maximum(n_used_ref[0] - 1, 0))
    up, down = (D_MODEL, EXPERT_FF), (EXPERT_FF, D_MODEL)
    any_spec = pl.BlockSpec(memory_space=pl.ANY)
    grid_spec = pltpu.PrefetchScalarGridSpec(
        num_scalar_prefetch=4,
        grid=(n_tiles,),
        in_specs=[smem_tile(lambda t, *s: (last_used(t, s[-1]), 0, 0)),
                  smem_tile(lambda t, *s: (last_used(t + 1, s[-1]), 0, 0)),
                  smem_tile(lambda t, *s: (last_used(t, s[-1]), 0, 0)),
                  any_spec, any_spec, any_spec, any_spec],
        out_specs=pl.BlockSpec(memory_space=pl.ANY),
        scratch_shapes=[pltpu.VMEM((2, tm // DMA_GROUP, DMA_GROUP, MOE_ROW), F32),
                        pltpu.VMEM((2, tm // DMA_GROUP, DMA_GROUP, D_MODEL), F32),
                        [pltpu.VMEM((EXPERTS_PER_GROUP,) + shape, BF16) for shape in (up, up, down)],
                        [pltpu.VMEM((2,) + shape, F32) for shape in (up, up, down)],
                        pltpu.SemaphoreType.DMA((2,)), pltpu.SemaphoreType.DMA((2,)),
                        pltpu.SemaphoreType.DMA((2, 3))],
    )
    return pl.pallas_call(
        _moe_pair_kernel,
        out_shape=jax.ShapeDtypeStruct((n_tok + 2 * DMA_GROUP, D_MODEL), F32),
        grid_spec=grid_spec,
        compiler_params=pltpu.CompilerParams(dimension_semantics=("arbitrary",),
                                             vmem_limit_bytes=VMEM_LIMIT),
        name="moe_pairs",
    )(tile_lo, tile_hi, n_groups, n_used, tok, tok, dst, h2_ext, wg, wu, wd)


def _final_kernel(x1_ref, moe_ref, mod_ref, fg_ref, y_ref):
    x2 = x1_ref[0] + mod_ref[0][5] * moe_ref[...]
    y_ref[0] = _rms_scale(x2) * fg_ref[...]


def _final(x1, moe_out, mod, final_g, *, row_off, tm):
    b, t, _ = x1.shape
    blk0 = row_off // tm
    per_seq = t // tm
    return pl.pallas_call(
        _final_kernel,
        out_shape=jax.ShapeDtypeStruct((b, t, D_MODEL), F32),
        grid=(b, per_seq),
        in_specs=[pl.BlockSpec((1, tm, D_MODEL), lambda i, j: (i, j, 0)),
                  pl.BlockSpec((tm, D_MODEL), lambda i, j: (blk0 + i * per_seq + j, 0)),
                  _mod_spec(mod, tm, lambda i, j: (i, j)),
                  pl.BlockSpec((1, D_MODEL), lambda i, j: (0, 0))],
        out_specs=pl.BlockSpec((1, tm, D_MODEL), lambda i, j: (i, j, 0)),
        compiler_params=pltpu.CompilerParams(dimension_semantics=("parallel", "parallel"),
                                             vmem_limit_bytes=VMEM_LIMIT),
        name="moe_combine_norm",
    )(x1, moe_out, mod, final_g.reshape(1, -1))


def _mixer(x, mod, shift0, s_rwkv0, s_gla0, wts, *, n_tok, row_off, shared=None):
    b, t, _ = x.shape
    per_token = mod.shape[2] > 1
    rows = x.reshape(1, b * t, D_MODEL) if per_token else x
    tm = min(rows.shape[1], ROW_TILE)
    tm_wide = min(rows.shape[1], WIDE_ROW_TILE)
    p_rwkv = _norm_proj(rows, mod, wts["norm1_g"], wts["w_in_rwkv"], gate=False, tm=tm_wide)
    p_gla = _norm_proj(rows, mod, wts["norm1_g"], wts["w_in_gla"], gate=False, tm=tm_wide)
    gates = _norm_proj(rows, mod, wts["norm1_g"], wts["w_in_gate"], gate=True, tm=tm_wide)
    p_rwkv = p_rwkv.reshape(b, t, RWKV_PROJ)

    y_r, s_rwkv = _rwkv(p_rwkv, shift0, s_rwkv0,
                        wts["rwkv_mu"], wts["rwkv_w0"], wts["rwkv_decay_up"], wts["rwkv_a0"],
                        wts["rwkv_aaa_up"], wts["rwkv_gate_up"], wts["rwkv_k_k"], wts["rwkv_k_a"],
                        wts["rwkv_r_k"], wts["rwkv_ln_g"], wts["rwkv_ln_b"], nb=RWKV_SEQS_PER_STEP)
    y_g, s_gla = _gla(p_gla.reshape(b, t, GLA_PROJ_PAD), s_gla0, wts["gla_alpha_up"],
                      wts["gla_alpha_b"], wts["gla_norm_g"], nb=GLA_SEQS_PER_STEP)

    x1, h2_all, route_all = _merge(
        y_r.reshape(rows.shape), y_g.reshape(rows.shape), gates, rows, mod, wts["w_branch_rwkv"],
        wts["w_branch_gla"], wts["w_out"], wts["norm2_g"], wts["router_cat"],
        tm=tm, n_tok=n_tok, row_off=row_off, shared=shared)
    states = (p_rwkv[:, t - 1:t, :][None], s_rwkv[None], s_gla[None])
    return x1, h2_all, route_all, states


def kernel(x_prompt, x_sample, c_prompt, c_sample, state_rwkv_shift, state_rwkv, state_gla, w_ada, b_ada, norm1_g, w_in, rwkv_mu, rwkv_w0, rwkv_decay_up, rwkv_a0, rwkv_aaa_up, rwkv_gate_up, rwkv_k_k, rwkv_k_a, rwkv_r_k, rwkv_ln_g, rwkv_ln_b, gla_alpha_up, gla_alpha_b, gla_norm_g, w_branch_rwkv, w_branch_gla, w_out, norm2_g, router_group, router_expert, expert_w_gate, expert_w_up, expert_w_down, final_norm_g):
    bp, tp = x_prompt.shape[:2]
    bs, ts = x_sample.shape[:2]
    w_in0 = w_in[0]
    g0 = RWKV_PROJ
    o3 = 2 * GLA_DK_TOTAL + GLA_DV_TOTAL
    o4 = o3 + GLA_GATE_RANK
    w_gla = w_in0[:, g0:g0 + GLA_PROJ]
    w_gla = jnp.concatenate([w_gla[:, :o3], w_gla[:, o4:], w_gla[:, o3:o4],
                             jnp.zeros((D_MODEL, GLA_PROJ_PAD - GLA_PROJ), F32)], axis=1)
    wts = dict(
        norm1_g=norm1_g[0].reshape(1, -1),
        w_in_rwkv=w_in0[:, :g0].astype(BF16),
        w_in_gla=w_gla.astype(BF16),
        w_in_gate=w_in0[:, g0 + GLA_PROJ:].astype(BF16),
        rwkv_mu=rwkv_mu[0], rwkv_w0=rwkv_w0[0], rwkv_decay_up=rwkv_decay_up[0], rwkv_a0=rwkv_a0[0],
        rwkv_aaa_up=rwkv_aaa_up[0], rwkv_gate_up=rwkv_gate_up[0], rwkv_k_k=rwkv_k_k[0],
        rwkv_k_a=rwkv_k_a[0], rwkv_r_k=rwkv_r_k[0], rwkv_ln_g=rwkv_ln_g[0], rwkv_ln_b=rwkv_ln_b[0],
        gla_alpha_up=jnp.pad(gla_alpha_up[0], ((0, LANES - GLA_GATE_RANK), (0, 0))),
        gla_alpha_b=gla_alpha_b[0], gla_norm_g=gla_norm_g[0],
        w_branch_rwkv=w_branch_rwkv[0].astype(BF16), w_branch_gla=w_branch_gla[0].astype(BF16),
        w_out=w_out[0].astype(BF16), norm2_g=norm2_g[0],
        router_cat=_split_hi_lo(jnp.pad(jnp.concatenate([router_expert[0], router_group[0]], axis=1),
                                        ((0, 0), (0, LANES - N_EXPERTS - N_GROUPS)))),
        expert_w_gate=expert_w_gate[0], expert_w_up=expert_w_up[0], expert_w_down=expert_w_down[0],
    )
    mod = _ada(jnp.concatenate([c_prompt, c_sample], axis=0), w_ada[0], b_ada[0])
    mod = mod.reshape(bp + bs, N_MOD, D_MODEL)
    n_p = bp * tp
    n_s = bs * ts
    n_tok = n_p + n_s
    mod_p = mod[:bp].reshape(bp, N_MOD, 1, D_MODEL)
    mod_s = jnp.repeat(mod[bp:].transpose(1, 0, 2), ts, axis=1)[None]

    dt = x_prompt.dtype
    x1_p, h2_all, route_all, states_p = _mixer(
        x_prompt, mod_p, jnp.zeros((bp, 1, RWKV_PROJ), dt),
        jnp.zeros((bp, RWKV_HEADS, RWKV_HEAD_DIM, RWKV_HEAD_DIM), state_rwkv.dtype),
        jnp.zeros((bp, GLA_HEADS, GLA_DK, GLA_DV), state_gla.dtype), wts, n_tok=n_tok, row_off=0)
    x1_s, h2_all, route_all, states_s = _mixer(
        x_sample, mod_s, state_rwkv_shift[0], state_rwkv[0], state_gla[0], wts,
        n_tok=n_tok, row_off=n_p, shared=(h2_all, route_all))

    expert_ids = route_all[:, :2].astype(jnp.int32)
    moe_out = _moe_grouped(h2_all, expert_ids, wts["expert_w_gate"], wts["expert_w_up"],
                           wts["expert_w_down"])
    y_p = _final(x1_p, moe_out, mod_p, final_norm_g, row_off=0, tm=min(tp, WIDE_ROW_TILE))
    y_s = _final(x1_s, moe_out, mod_s, final_norm_g, row_off=n_p, tm=min(n_s, WIDE_ROW_TILE))
    return (y_p, y_s.reshape(bs, ts, D_MODEL)) + states_p + states_s
```

```python
import functools
import math

import jax
import jax.numpy as jnp
from jax import lax
from jax.experimental import pallas as pl
from jax.experimental.pallas import tpu as pltpu

F32 = jnp.float32
BF16 = jnp.bfloat16

D_MODEL = 1024
N_MOD = 6
EPS = 1e-6
RWKV_HEAD_DIM = 64
RWKV_HEADS = 16
RWKV_PAIRS = RWKV_HEADS // 2
DECAY_LORA = 64
AAA_LORA = 64
GATE_LORA = 128
RWKV_PROJ = 3 * D_MODEL + DECAY_LORA + AAA_LORA + GATE_LORA
LN_X_EPS = 64e-5
GLA_HEADS = 4
GLA_DK = 128
GLA_DV = 256
GLA_DK_TOTAL = GLA_HEADS * GLA_DK
GLA_DV_TOTAL = GLA_HEADS * GLA_DV
GLA_GATE_RANK = 16
GLA_GATE_NORM = 16.0
GLA_PROJ = 2 * GLA_DK_TOTAL + 2 * GLA_DV_TOTAL + GLA_GATE_RANK
LANES = 128
GLA_PROJ_PAD = 2 * GLA_DK_TOTAL + 2 * GLA_DV_TOTAL + LANES
GATE_PROJ = 2 * D_MODEL
N_GROUPS = 4
EXPERTS_PER_GROUP = 8
N_EXPERTS = 32
EXPERT_FF = 512

CHUNK = 64
ROW_TILE = 512
WIDE_ROW_TILE = 1024
MOE_TILE = 192
DMA_GROUP = 8
MOE_ROW = D_MODEL + LANES
N_PAIR_CLASSES = N_EXPERTS * EXPERTS_PER_GROUP
RWKV_SEQS_PER_STEP = 4
GLA_SEQS_PER_STEP = 8
VMEM_LIMIT = 56 * 1024 * 1024


def _bdot(a, b):
    return jnp.dot(a.astype(BF16), b.astype(BF16), preferred_element_type=F32)


def _bdot_nt(a, b):
    return lax.dot_general(a.astype(BF16), b.astype(BF16), (((1,), (1,)), ((), ())),
                           preferred_element_type=F32)


def _bdot_tn(a, b):
    return jnp.dot(a.T.astype(BF16), b.astype(BF16), preferred_element_type=F32)


def _split3(x):
    h1 = x.astype(BF16)
    r1 = x - h1.astype(F32)
    h2 = r1.astype(BF16)
    h3 = (r1 - h2.astype(F32)).astype(BF16)
    return h1, h2, h3


def _dot3(a, b):
    a1, a2, _ = _split3(a)
    b1, b2, _ = _split3(b)
    return (jnp.dot(a1, b1, preferred_element_type=F32)
            + jnp.dot(a2, b1, preferred_element_type=F32)
            + jnp.dot(a1, b2, preferred_element_type=F32))


def _dot_exact_lhs(a_bf16, x):
    x1, x2, _ = _split3(x)
    return (jnp.dot(a_bf16, x1, preferred_element_type=F32)
            + jnp.dot(a_bf16, x2, preferred_element_type=F32))


def _split_hi_lo(w):
    hi, lo, _ = _split3(w)
    return jnp.concatenate([hi, lo], axis=1)


def _sigmoid(x):
    return 1.0 / (1.0 + jnp.exp(-x))


def _silu(x):
    return x * _sigmoid(x)


def _rms_scale(x):
    return x * lax.rsqrt(jnp.mean(x * x, axis=-1, keepdims=True) + EPS)


def _ada_kernel(c_ref, w_ref, b_ref, o_ref):
    o_ref[...] = _dot3(_silu(c_ref[...]), w_ref[...]) + b_ref[...]


def _ada(c, w_ada, b_ada):
    n_rows = c.shape[0]
    n_out = w_ada.shape[1]
    tn = 1536
    return pl.pallas_call(
        _ada_kernel,
        out_shape=jax.ShapeDtypeStruct((n_rows, n_out), F32),
        grid=(n_out // tn,),
        in_specs=[pl.BlockSpec((n_rows, D_MODEL), lambda j: (0, 0)),
                  pl.BlockSpec((D_MODEL, tn), lambda j: (0, j)),
                  pl.BlockSpec((1, tn), lambda j: (0, j))],
        out_specs=pl.BlockSpec((n_rows, tn), lambda j: (0, j)),
        compiler_params=pltpu.CompilerParams(dimension_semantics=("parallel",),
                                             vmem_limit_bytes=VMEM_LIMIT),
        name="ada",
    )(c, w_ada, b_ada.reshape(1, n_out))


def _mod_spec(mod, tm, seq_tile):
    if mod.shape[2] == 1:
        return pl.BlockSpec((1, N_MOD, 1, D_MODEL), lambda *g: (seq_tile(*g)[0], 0, 0, 0))
    return pl.BlockSpec((1, N_MOD, tm, D_MODEL), lambda *g: (seq_tile(*g)[0], 0, seq_tile(*g)[1], 0))


def _norm_proj_kernel(x_ref, mod_ref, g_ref, w_ref, o_ref, *, gate):
    mod = mod_ref[0]
    h = _rms_scale(x_ref[0]) * g_ref[...] * (1.0 + mod[1]) + mod[0]
    out = _bdot(h, w_ref[...])
    o_ref[0] = _sigmoid(out) if gate else out


def _norm_proj(x, mod, g, w_bf16, *, gate, tm):
    b, t, _ = x.shape
    n = w_bf16.shape[1]
    return pl.pallas_call(
        functools.partial(_norm_proj_kernel, gate=gate),
        out_shape=jax.ShapeDtypeStruct((b, t, n), F32),
        grid=(b, t // tm),
        in_specs=[pl.BlockSpec((1, tm, D_MODEL), lambda i, j: (i, j, 0)),
                  _mod_spec(mod, tm, lambda i, j: (i, j)),
                  pl.BlockSpec((1, D_MODEL), lambda i, j: (0, 0)),
                  pl.BlockSpec((D_MODEL, n), lambda i, j: (0, 0))],
        out_specs=pl.BlockSpec((1, tm, n), lambda i, j: (i, j, 0)),
        compiler_params=pltpu.CompilerParams(dimension_semantics=("parallel", "parallel"),
                                             vmem_limit_bytes=VMEM_LIMIT),
        name="norm_proj_gate" if gate else "norm_proj",
    )(x, mod, g, w_bf16)


def _blockdiag(x, lo_mask):
    return jnp.concatenate([jnp.where(lo_mask, x, 0.0), jnp.where(lo_mask, 0.0, x)], axis=0)


def _rwkv_kernel(p_ref, shift_ref, s0_ref, mu_ref, w0_ref, dup_ref, a0_ref, aup_ref, gup_ref,
                 kk_ref, ka_ref, rk_ref, lng_ref, lnb_ref, y_ref, sout_ref, s_scr, carry_scr,
                 *, nb):
    c = pl.program_id(1)
    n_chunks = pl.num_programs(1)
    C = CHUNK
    N = RWKV_HEAD_DIM
    t_blk = p_ref.shape[1]

    @pl.when(c == 0)
    def _():
        for bl in range(nb):
            for pr in range(RWKV_PAIRS):
                s_scr[bl, pr] = jnp.concatenate([s0_ref[bl, 2 * pr], s0_ref[bl, 2 * pr + 1]], axis=1)
        carry_scr[...] = shift_ref[...]

    row = lax.broadcasted_iota(jnp.int32, (C, LANES), 0)
    lane = lax.broadcasted_iota(jnp.int32, (C, LANES), 1)
    lane_t = jnp.where(lane < N, lane, lane - N)
    lo = lane < N
    strict = lane_t < row
    incl = lane_t <= row
    eye = (lane_t == row).astype(F32)
    valid = row < t_blk
    tri = (lax.broadcasted_iota(jnp.int32, (C, C), 1)
           <= lax.broadcasted_iota(jnp.int32, (C, C), 0)).astype(BF16)
    sq_r = lax.broadcasted_iota(jnp.int32, (LANES, LANES), 0)
    sq_c = lax.broadcasted_iota(jnp.int32, (LANES, LANES), 1)
    same_head = (sq_r < N) == (sq_c < N)
    diag = sq_r == sq_c
    row_w = lax.broadcasted_iota(jnp.int32, (C, RWKV_PROJ), 0)
    o3 = 3 * D_MODEL

    def hsum(v):
        s_lo = jnp.sum(jnp.where(lo, v, 0.0), axis=1, keepdims=True)
        s_hi = jnp.sum(jnp.where(lo, 0.0, v), axis=1, keepdims=True)
        return jnp.where(lo, s_lo, s_hi)

    def bd(v):
        return _blockdiag(v, lo)

    def cat0(*a):
        return jnp.concatenate(a, axis=0)

    def cat1(*a):
        return jnp.concatenate(a, axis=1)

    valid_w = lax.broadcasted_iota(jnp.int32, (C, D_MODEL), 0) < t_blk
    xs, logw_all, cum_all, asig_all, g_all = [], [], [], [], []
    for bl in range(nb):
        p = p_ref[bl]
        carry_new = p[t_blk - 1:t_blk, :]
        if t_blk < C:
            p = jnp.concatenate([p, jnp.zeros((C - t_blk, RWKV_PROJ), F32)], axis=0)
        prev = jnp.where(row_w == 0, carry_scr[bl], pltpu.roll(p, 1, axis=0))
        carry_scr[bl] = carry_new
        x = p + (prev - p) * mu_ref[...]
        xw = x[:, o3:o3 + DECAY_LORA]
        xa = x[:, o3 + DECAY_LORA:o3 + DECAY_LORA + AAA_LORA]
        xg = x[:, o3 + DECAY_LORA + AAA_LORA:]
        z = w0_ref[...] + _bdot(jnp.tanh(xw), dup_ref[...])
        logw_full = -math.exp(-0.5) * _sigmoid(z)
        if t_blk < C:
            logw_full = jnp.where(valid_w, logw_full, 0.0)
        xs.append(x)
        logw_all.append(logw_full)
        cum_all.append(_dot_exact_lhs(tri, logw_full))
        asig_all.append(_sigmoid(a0_ref[...] + _bdot(xa, aup_ref[...])))
        g_all.append(_bdot(_sigmoid(xg), gup_ref[...]))

    units = [(bl, pr) for bl in range(nb) for pr in range(RWKV_PAIRS)]
    U = range(len(units))
    cols = [slice(pr * LANES, (pr + 1) * LANES) for _, pr in units]
    r = [xs[bl][:, pr * LANES:(pr + 1) * LANES] for bl, pr in units]
    k = [xs[bl][:, D_MODEL + pr * LANES:D_MODEL + (pr + 1) * LANES] for bl, pr in units]
    v = [xs[bl][:, 2 * D_MODEL + pr * LANES:2 * D_MODEL + (pr + 1) * LANES] for bl, pr in units]
    logw = [logw_all[bl][:, cols[u]] for u, (bl, _) in enumerate(units)]
    asig = [asig_all[bl][:, cols[u]] for u, (bl, _) in enumerate(units)]
    kk = [k[u] * kk_ref[:, cols[u]] for u in U]
    k2 = [k[u] * (1.0 + (asig[u] - 1.0) * ka_ref[:, cols[u]]) for u in U]
    ss = [hsum(kk[u] * kk[u]) for u in U]
    cum = [cum_all[bl][:, cols[u]] for u, (bl, _) in enumerate(units)]
    bsum = [hsum(r[u] * k2[u] * rk_ref[:, cols[u]]) for u in U]
    kk = [kk[u] * lax.rsqrt(jnp.maximum(ss[u], 1e-24)) for u in U]
    bonus = [bsum[u] * v[u] for u in U]
    if t_blk < C:
        kk = [jnp.where(valid, a, 0.0) for a in kk]
        k2 = [jnp.where(valid, a, 0.0) for a in k2]
        v = [jnp.where(valid, a, 0.0) for a in v]
        r = [jnp.where(valid, a, 0.0) for a in r]
    b_vec = [kk[u] * asig[u] for u in U]
    clast = [cum[u][C - 1:C, :] for u in U]
    e_inv = [jnp.exp(-cum[u]) for u in U]
    e_tail = [jnp.exp(clast[u] - cum[u]) for u in U]
    aq = [-kk[u] * jnp.exp(cum[u] - logw[u]) for u in U]
    rq = [r[u] * jnp.exp(cum[u]) for u in U]
    bk = [b_vec[u] * e_inv[u] for u in U]
    kx = [k2[u] * e_inv[u] for u in U]
    bb = [b_vec[u] * e_tail[u] for u in U]
    kb = [k2[u] * e_tail[u] for u in U]

    g4 = [_bdot_nt(cat0(aq[u], rq[u]), cat0(bd(bk[u]), bd(kx[u]))) for u in U]
    m_ab = [jnp.where(strict, g4[u][:C, :LANES], 0.0) for u in U]
    m_ak = [jnp.where(strict, g4[u][:C, LANES:], 0.0) for u in U]
    m_rb = [jnp.where(incl, g4[u][C:, :LANES], 0.0) for u in U]
    m_rk = [jnp.where(incl, g4[u][C:, LANES:], 0.0) for u in U]

    t_inv = [eye + m_ab[u] for u in U]
    l_pow = [_bdot(m_ab[u], bd(m_ab[u])) for u in U]
    mv = [_bdot(cat0(m_ak[u], m_rk[u]), bd(v[u])) for u in U]
    n_steps = C.bit_length() - 1
    for step in range(1, n_steps):
        if step < n_steps - 1:
            both = [_bdot(l_pow[u], cat1(bd(t_inv[u]), bd(l_pow[u]))) for u in U]
            t_inv = [t_inv[u] + both[u][:, :LANES] for u in U]
            l_pow = [both[u][:, LANES:] for u in U]
        else:
            t_inv = [t_inv[u] + _bdot(l_pow[u], bd(t_inv[u])) for u in U]

    au = [_bdot(t_inv[u], cat1(bd(aq[u]), bd(mv[u][:C]))) for u in U]
    a_bar = [au[u][:, :LANES] for u in U]
    u_bar = [au[u][:, LANES:] for u in U]
    mau = [_bdot(m_rb[u], cat1(bd(a_bar[u]), bd(u_bar[u]))) for u in U]
    p_full = [_bdot_tn(a_bar[u], bb[u]) for u in U]
    q_full = [_bdot_tn(cat0(u_bar[u], v[u]), cat0(bb[u], kb[u])) for u in U]
    r_til = [rq[u] + mau[u][:, :LANES] for u in U]
    y_loc = [mau[u][:, LANES:] + mv[u][C:] for u in U]
    p_bd = [jnp.where(same_head, p_full[u], 0.0) + jnp.where(diag, jnp.exp(clast[u]), 0.0) for u in U]
    q_pk = [jnp.where(lo[:N], q_full[u][:N], q_full[u][N:]) for u in U]

    s_old = [s_scr[bl, pr] for bl, pr in units]
    y = [_bdot_nt(r_til[u], _blockdiag(s_old[u], lo[:N])) + y_loc[u] for u in U]
    s_new = [_bdot(s_old[u], p_bd[u]) + q_pk[u] for u in U]
    for u, (bl, pr) in enumerate(units):
        s_scr[bl, pr] = s_new[u]

    mu_y = [hsum(y[u]) * (1.0 / N) for u in U]
    yc = [y[u] - mu_y[u] for u in U]
    var = [hsum(yc[u] * yc[u]) * (1.0 / N) for u in U]
    for u, (bl, pr) in enumerate(units):
        yn = yc[u] * lax.rsqrt(var[u] + LN_X_EPS) * lng_ref[:, cols[u]] + lnb_ref[:, cols[u]]
        y_ref[bl, :, cols[u]] = ((yn + bonus[u]) * g_all[bl][:, cols[u]])[:t_blk]

    @pl.when(c == n_chunks - 1)
    def _():
        for bl in range(nb):
            for pr in range(RWKV_PAIRS):
                sout_ref[bl, 2 * pr] = s_scr[bl, pr][:, :N]
                sout_ref[bl, 2 * pr + 1] = s_scr[bl, pr][:, N:]


def _rwkv(p_rwkv, shift_prev, s0, mu, w0, decay_up, a0, aaa_up, gate_up, k_k, k_a, r_k,
          ln_g, ln_b, *, nb):
    b, t, _ = p_rwkv.shape
    nb = math.gcd(b, nb)
    t_blk = min(t, CHUNK)
    row = lambda a: a.reshape(1, -1)
    full = lambda shape: pl.BlockSpec(shape, lambda i, j: (0,) * len(shape))
    state_spec = pl.BlockSpec((nb, RWKV_HEADS, RWKV_HEAD_DIM, RWKV_HEAD_DIM), lambda i, j: (i, 0, 0, 0))
    return pl.pallas_call(
        functools.partial(_rwkv_kernel, nb=nb),
        out_shape=(jax.ShapeDtypeStruct((b, t, D_MODEL), F32),
                   jax.ShapeDtypeStruct((b, RWKV_HEADS, RWKV_HEAD_DIM, RWKV_HEAD_DIM), F32)),
        grid=(b // nb, t // t_blk),
        in_specs=[pl.BlockSpec((nb, t_blk, RWKV_PROJ), lambda i, j: (i, j, 0)),
                  pl.BlockSpec((nb, 1, RWKV_PROJ), lambda i, j: (i, 0, 0)),
                  state_spec,
                  full((1, RWKV_PROJ)), full((1, D_MODEL)), full((DECAY_LORA, D_MODEL)),
                  full((1, D_MODEL)), full((AAA_LORA, D_MODEL)), full((GATE_LORA, D_MODEL)),
                  full((1, D_MODEL)), full((1, D_MODEL)), full((1, D_MODEL)),
                  full((1, D_MODEL)), full((1, D_MODEL))],
        out_specs=(pl.BlockSpec((nb, t_blk, D_MODEL), lambda i, j: (i, j, 0)), state_spec),
        scratch_shapes=[pltpu.VMEM((nb, RWKV_PAIRS, RWKV_HEAD_DIM, LANES), F32),
                        pltpu.VMEM((nb, 1, RWKV_PROJ), F32)],
        compiler_params=pltpu.CompilerParams(dimension_semantics=("parallel", "arbitrary"),
                                             vmem_limit_bytes=VMEM_LIMIT),
        name="rwkv_scan",
    )(p_rwkv, shift_prev, s0, row(mu), row(w0), decay_up, row(a0), aaa_up, gate_up,
      row(k_k), row(k_a), row(r_k), row(ln_g), row(ln_b))


def _gla_kernel(p_ref, s0_ref, aup_ref, ab_ref, ng_ref, y_ref, sout_ref, s_scr, *, nb):
    c = pl.program_id(1)
    n_chunks = pl.num_programs(1)
    C = CHUNK
    t_blk = p_ref.shape[1]
    units = [(bl, h) for bl in range(nb) for h in range(GLA_HEADS)]
    U = range(len(units))

    @pl.when(c == 0)
    def _():
        for bl, h in units:
            s_scr[bl, h] = s0_ref[bl, h].T

    o_k = GLA_DK_TOTAL
    o_v = 2 * GLA_DK_TOTAL
    o_r = o_v + GLA_DV_TOTAL
    o_a = o_r + GLA_DV_TOTAL
    tri = (lax.broadcasted_iota(jnp.int32, (C, C), 1)
           <= lax.broadcasted_iota(jnp.int32, (C, C), 0))
    tri_bf = tri.astype(BF16)
    valid_g = lax.broadcasted_iota(jnp.int32, (C, GLA_DK_TOTAL), 0) < t_blk
    valid_k = lax.broadcasted_iota(jnp.int32, (C, GLA_DK), 0) < t_blk
    valid_v = lax.broadcasted_iota(jnp.int32, (C, GLA_DV), 0) < t_blk

    ps, cums = [], []
    for bl in range(nb):
        p = p_ref[bl]
        if t_blk < C:
            p = jnp.concatenate([p, jnp.zeros((C - t_blk, GLA_PROJ_PAD), F32)], axis=0)
        logit = _dot3(p[:, o_a:], aup_ref[...]) + ab_ref[...]
        g = (jnp.minimum(logit, 0.0) - jnp.log(1.0 + jnp.exp(-jnp.abs(logit)))) * (1.0 / GLA_GATE_NORM)
        if t_blk < C:
            g = jnp.where(valid_g, g, 0.0)
        ps.append(p)
        cums.append(_dot_exact_lhs(tri_bf, g))

    q = [ps[bl][:, h * GLA_DK:(h + 1) * GLA_DK] * (GLA_DK ** -0.5) for bl, h in units]
    k = [ps[bl][:, o_k + h * GLA_DK:o_k + (h + 1) * GLA_DK] for bl, h in units]
    v = [ps[bl][:, o_v + h * GLA_DV:o_v + (h + 1) * GLA_DV] for bl, h in units]
    if t_blk < C:
        k = [jnp.where(valid_k, a, 0.0) for a in k]
        v = [jnp.where(valid_v, a, 0.0) for a in v]
    b = [cums[bl][:, h * GLA_DK:(h + 1) * GLA_DK] for bl, h in units]
    clast = [b[u][C - 1:C, :] for u in U]
    cmid = [b[u][C // 2 - 1:C // 2, :] for u in U]
    att = [_bdot_nt(q[u] * jnp.exp(b[u] - cmid[u]), k[u] * jnp.exp(cmid[u] - b[u])) for u in U]
    att = [jnp.where(tri, a, 0.0) for a in att]
    s_t = [s_scr[bl, h] for bl, h in units]
    o = [_bdot(att[u], v[u]) + _bdot_nt(q[u] * jnp.exp(b[u]), s_t[u]) for u in U]
    s_new = [s_t[u] * jnp.exp(clast[u]) + _bdot_tn(v[u], k[u] * jnp.exp(clast[u] - b[u])) for u in U]
    for u, (bl, h) in enumerate(units):
        s_scr[bl, h] = s_new[u]
        r = ps[bl][:, o_r + h * GLA_DV:o_r + (h + 1) * GLA_DV]
        on = o[u] * lax.rsqrt(jnp.mean(o[u] * o[u], axis=-1, keepdims=True) + EPS) * ng_ref[...]
        y_ref[bl, :, h * GLA_DV:(h + 1) * GLA_DV] = (on * _silu(r))[:t_blk]

    @pl.when(c == n_chunks - 1)
    def _():
        for bl, h in units:
            sout_ref[bl, h] = s_scr[bl, h].T


def _gla(p_gla, s0, alpha_up_pad, alpha_b, norm_g, *, nb):
    b, t, _ = p_gla.shape
    nb = math.gcd(b, nb)
    t_blk = min(t, CHUNK)
    full = lambda shape: pl.BlockSpec(shape, lambda i, j: (0,) * len(shape))
    state_spec = pl.BlockSpec((nb, GLA_HEADS, GLA_DK, GLA_DV), lambda i, j: (i, 0, 0, 0))
    return pl.pallas_call(
        functools.partial(_gla_kernel, nb=nb),
        out_shape=(jax.ShapeDtypeStruct((b, t, GLA_DV_TOTAL), F32),
                   jax.ShapeDtypeStruct((b, GLA_HEADS, GLA_DK, GLA_DV), F32)),
        grid=(b // nb, t // t_blk),
        in_specs=[pl.BlockSpec((nb, t_blk, GLA_PROJ_PAD), lambda i, j: (i, j, 0)),
                  state_spec,
                  full((LANES, GLA_DK_TOTAL)), full((1, GLA_DK_TOTAL)), full((1, GLA_DV))],
        out_specs=(pl.BlockSpec((nb, t_blk, GLA_DV_TOTAL), lambda i, j: (i, j, 0)), state_spec),
        scratch_shapes=[pltpu.VMEM((nb, GLA_HEADS, GLA_DV, GLA_DK), F32)],
        compiler_params=pltpu.CompilerParams(dimension_semantics=("parallel", "arbitrary"),
                                             vmem_limit_bytes=VMEM_LIMIT),
        name="gla_scan",
    )(p_gla, s0, alpha_up_pad, alpha_b.reshape(1, -1), norm_g.reshape(1, -1))


def _merge_body(yr_ref, yg_ref, gate_ref, x_ref, mod_ref, wbr_ref, wbg_ref, wout_ref, g2_ref,
                router_ref, x1_ref, h2_ref, route_ref):
    mod = mod_ref[0]
    gates = gate_ref[0]
    merged = (gates[:, :D_MODEL] * _bdot(yr_ref[0], wbr_ref[...])
              + gates[:, D_MODEL:] * _bdot(yg_ref[0], wbg_ref[...]))
    x1 = x_ref[0] + mod[2] * _bdot(merged, wout_ref[...])
    x1_ref[0] = x1
    h2 = _rms_scale(x1) * g2_ref[...] * (1.0 + mod[4]) + mod[3]
    h2_ref[:, :D_MODEL] = h2

    tm = h2.shape[0]
    h_hi, h_lo, _ = _split3(h2)
    prod = jnp.dot(jnp.concatenate([h_hi, h_lo], axis=0), router_ref[...], preferred_element_type=F32)
    logits = prod[:tm, :LANES] + prod[tm:, :LANES] + prod[:tm, LANES:]
    lane = lax.broadcasted_iota(jnp.int32, (tm, LANES), 1)
    neg = -jnp.inf
    is_group = (lane >= N_EXPERTS) & (lane < N_EXPERTS + N_GROUPS)
    gl = jnp.where(is_group, logits, neg)
    gmax = jnp.max(gl, axis=1, keepdims=True)
    g_idx = jnp.min(jnp.where(gl == gmax, lane, LANES), axis=1, keepdims=True) - N_EXPERTS
    p_g = 1.0 / jnp.sum(jnp.exp(gl - gmax), axis=1, keepdims=True)
    in_group = (lane >= g_idx * EXPERTS_PER_GROUP) & (lane < (g_idx + 1) * EXPERTS_PER_GROUP)
    el = jnp.where(in_group, logits, neg)
    v1 = jnp.max(el, axis=1, keepdims=True)
    i1 = jnp.min(jnp.where(el == v1, lane, LANES), axis=1, keepdims=True)
    el2 = jnp.where(lane == i1, neg, el)
    v2 = jnp.max(el2, axis=1, keepdims=True)
    i2 = jnp.min(jnp.where(el2 == v2, lane, LANES), axis=1, keepdims=True)
    e21 = jnp.exp(v2 - v1)
    w1 = p_g / (1.0 + e21)
    route = (jnp.where(lane == 0, i1.astype(F32), 0.0) + jnp.where(lane == 1, i2.astype(F32), 0.0)
             + jnp.where(lane == 2, w1, 0.0) + jnp.where(lane == 3, w1 * e21, 0.0))
    route_ref[...] = route
    h2_ref[:, D_MODEL:] = route


def _merge_first_kernel(*refs, n_own):
    step = pl.program_id(0)

    @pl.when(step < n_own)
    def _():
        _merge_body(*refs)

    @pl.when(step >= n_own)
    def _():
        refs[-2][...] = jnp.zeros(refs[-2].shape, F32)
        refs[-1][...] = jnp.zeros(refs[-1].shape, F32)


def _merge_second_kernel(*refs):
    n_in = 10
    _merge_body(*refs[:n_in], *refs[n_in + 2:])


def _merge(y_r, y_g, gates, x, mod, wbr, wbg, wout, norm2_g, router_cat, *, tm, n_tok, row_off, shared=None):
    b, t, _ = x.shape
    per_seq = t // tm
    n_own = b * per_seq
    first = shared is None
    n_steps = n_tok // tm if first else n_own
    assert row_off % tm == 0 and n_tok % tm == 0 and (row_off == 0 or not first)
    blk0 = row_off // tm
    own = lambda s: jnp.minimum(s, n_own - 1)
    full = lambda shape: pl.BlockSpec(shape, lambda s: (0,) * len(shape))
    tile = lambda n: pl.BlockSpec((1, tm, n), lambda s: (own(s) // per_seq, own(s) % per_seq, 0))
    flat = lambda n: pl.BlockSpec((tm, n), lambda s: (blk0 + s, 0))
    in_specs = [tile(D_MODEL), tile(D_MODEL), tile(GATE_PROJ), tile(D_MODEL),
                _mod_spec(mod, tm, lambda s: (own(s) // per_seq, own(s) % per_seq)),
                full((D_MODEL, D_MODEL)), full((D_MODEL, D_MODEL)), full((D_MODEL, D_MODEL)),
                full((1, D_MODEL)), full((D_MODEL, 2 * LANES))]
    args = [y_r, y_g, gates, x, mod, wbr, wbg, wout, norm2_g.reshape(1, -1), router_cat]
    if not first:
        in_specs += [pl.BlockSpec(memory_space=pl.ANY)] * 2
        args += list(shared)
    return pl.pallas_call(
        functools.partial(_merge_first_kernel, n_own=n_own) if first else _merge_second_kernel,
        out_shape=(jax.ShapeDtypeStruct((b, t, D_MODEL), F32),
                   jax.ShapeDtypeStruct((n_tok, MOE_ROW), F32),
                   jax.ShapeDtypeStruct((n_tok, LANES), F32)),
        grid=(n_steps,),
        in_specs=in_specs,
        out_specs=(tile(D_MODEL), flat(MOE_ROW), flat(LANES)),
        input_output_aliases={} if first else {10: 1, 11: 2},
        compiler_params=pltpu.CompilerParams(dimension_semantics=("arbitrary",),
                                             vmem_limit_bytes=VMEM_LIMIT),
        name="merge_router",
    )(*args)


def _moe_plan(expert_ids, n_tok):
    tm = MOE_TILE
    n_tiles = -(-n_tok // tm) + N_GROUPS * (EXPERTS_PER_GROUP * (EXPERTS_PER_GROUP - 1) // 2)
    n_slots = n_tiles * tm
    n_fill = n_slots - n_tok
    row_bits = 15
    assert max(n_tok, n_fill) <= 1 << row_bits and N_PAIR_CLASSES << (row_bits + 1) <= 1 << 24
    e_lo = jnp.minimum(expert_ids[:, 0], expert_ids[:, 1])
    e_hi = jnp.maximum(expert_ids[:, 0], expert_ids[:, 1])
    cls = e_lo * EXPERTS_PER_GROUP + e_hi % EXPERTS_PER_GROUP
    c_iota = jnp.arange(N_PAIR_CLASSES, dtype=jnp.int32)
    counts = jnp.sum((cls[None, :] == c_iota[:, None]).astype(jnp.int32), axis=1)
    tiles_c = (counts + tm - 1) // tm
    n_used = jnp.sum(tiles_c)
    pad_end = jnp.cumsum(tiles_c * tm - counts)
    f_iota = jnp.arange(n_fill, dtype=jnp.int32)
    f_cls = jnp.minimum(jnp.sum((f_iota[:, None] >= pad_end[None, :]).astype(jnp.int32), axis=1), N_PAIR_CLASSES - 1)
    keys = jnp.concatenate([(cls << (row_bits + 1)) + jnp.arange(n_tok, dtype=jnp.int32),
                            (f_cls << (row_bits + 1)) + (1 << row_bits) + f_iota])
    neg_sorted, _ = lax.top_k(-keys.astype(F32), n_slots)
    slots = (-neg_sorted).astype(jnp.int32).reshape(n_tiles, tm)
    valid = (slots & (1 << row_bits)) == 0
    tok = jnp.where(valid, slots & ((1 << row_bits) - 1), 0)
    t_idx = jnp.arange(n_tiles, dtype=jnp.int32)
    j_idx = jnp.arange(tm, dtype=jnp.int32)[None, :]
    dst = jnp.where(valid, tok, n_tok + (t_idx[:, None] % 2) * DMA_GROUP + j_idx % DMA_GROUP)
    n_valid = jnp.sum(valid.astype(jnp.int32), axis=1)
    n_groups = jnp.where(t_idx < n_used, (n_valid + DMA_GROUP - 1) // DMA_GROUP, 0)
    tile_c = slots[:, 0] >> (row_bits + 1)
    last_c = lax.dynamic_slice(tile_c, (jnp.maximum(n_used - 1, 0),), (1,))
    tile_c = jnp.where(t_idx < n_used, tile_c, last_c)
    tile_lo = tile_c // EXPERTS_PER_GROUP
    tile_hi = (tile_c // (EXPERTS_PER_GROUP * EXPERTS_PER_GROUP)) * EXPERTS_PER_GROUP + tile_c % EXPERTS_PER_GROUP
    rows = jnp.stack([tok, jnp.concatenate([tok[1:], tok[-1:]], axis=0), dst], axis=1)
    return tile_lo, tile_hi, n_groups.astype(jnp.int32), n_used.reshape(1).astype(jnp.int32), rows


def _moe_pair_kernel(lo_ref, hi_ref, ng_ref, n_used_ref, rows_ref, h2_hbm,
                     wg_hbm, wu_hbm, wd_hbm, out_hbm,
                     xbuf, obuf, w_res, w_stage, gsem, ssem, wsem):
    t = pl.program_id(0)
    n_t = pl.num_programs(0)
    tm = MOE_TILE
    slot = lax.rem(t, 2)
    n_used = n_used_ref[0]

    def gather(which, s, n_groups):
        def body(g, carry):
            for u in range(DMA_GROUP):
                pltpu.make_async_copy(h2_hbm.at[pl.ds(rows_ref[0, which, g * DMA_GROUP + u], 1)],
                                      xbuf.at[s, g, pl.ds(u, 1)], gsem.at[s]).start()
            return carry
        lax.fori_loop(0, n_groups, body, 0)

    def wait_gather(s, n_groups):
        def body(g, carry):
            pltpu.make_async_copy(h2_hbm.at[pl.ds(0, DMA_GROUP)], xbuf.at[s, 0], gsem.at[s]).wait()
            return carry
        lax.fori_loop(0, n_groups, body, 0)

    def wait_scatter(s, n_groups):
        def body(g, carry):
            pltpu.make_async_copy(obuf.at[s, 0], out_hbm.at[pl.ds(0, DMA_GROUP)], ssem.at[s]).wait()
            return carry
        lax.fori_loop(0, n_groups, body, 0)

    @pl.when(t == 0)
    def _():
        xbuf[...] = jnp.zeros(xbuf.shape, F32)
        obuf[0, 0] = jnp.zeros(obuf.shape[2:], F32)
        n_rows = out_hbm.shape[0]
        for half in range(2):
            fill = pltpu.make_async_copy(obuf.at[0, 0], out_hbm.at[pl.ds(n_rows - (2 - half) * DMA_GROUP, DMA_GROUP)],
                                         ssem.at[0])
            fill.start()
            fill.wait()
        gather(0, 0, ng_ref[0])

    @pl.when((t >= 2) & (t - 2 < n_used))
    def _():
        wait_scatter(slot, ng_ref[jnp.maximum(t - 2, 0)])

    @pl.when(t < n_used)
    def _():
        n_here = ng_ref[t]
        wait_gather(slot, n_here)

        @pl.when(t + 1 < n_used)
        def _():
            gather(1, 1 - slot, ng_ref[jnp.minimum(t + 1, n_t - 1)])

        group = lo_ref[t] // EXPERTS_PER_GROUP

        @pl.when((t == 0) | (group != lo_ref[jnp.maximum(t - 1, 0)] // EXPERTS_PER_GROUP))
        def _():
            def fetch(e, buf):
                return [pltpu.make_async_copy(src.at[group * EXPERTS_PER_GROUP + e], w_stage[m].at[buf], wsem.at[buf, m])
                        for m, src in enumerate((wg_hbm, wu_hbm, wd_hbm))]

            for cp in fetch(0, 0):
                cp.start()
            for e in range(EXPERTS_PER_GROUP):
                if e + 1 < EXPERTS_PER_GROUP:
                    for cp in fetch(e + 1, (e + 1) % 2):
                        cp.start()
                for m, cp in enumerate(fetch(e, e % 2)):
                    cp.wait()
                    w_res[m][e] = w_stage[m][e % 2].astype(BF16)

        xe = xbuf[slot].reshape(tm, MOE_ROW)
        x = xe[:, :D_MODEL].astype(BF16)
        route = xe[:, D_MODEL:]
        lane = lax.broadcasted_iota(jnp.int32, route.shape, 1)
        pick = lambda k: jnp.sum(jnp.where(lane == k, route, 0.0), axis=1, keepdims=True)
        i1, i2, w1, w2 = pick(0), pick(1), pick(2), pick(3)
        w_lo = jnp.where(i1 < i2, w1, w2)
        w_hi = jnp.where(i1 < i2, w2, w1)
        acc = None
        for e_ref, wt in ((lo_ref, w_lo), (hi_ref, w_hi)):
            e_local = lax.rem(e_ref[t], EXPERTS_PER_GROUP)
            hid = _silu(jnp.dot(x, w_res[0][e_local], preferred_element_type=F32)) * jnp.dot(
                x, w_res[1][e_local], preferred_element_type=F32)
            part = wt * _bdot(hid, w_res[2][e_local])
            acc = part if acc is None else acc + part
        obuf[slot] = acc.reshape(obuf.shape[1:])

        def body(g, carry):
            for u in range(DMA_GROUP):
                pltpu.make_async_copy(obuf.at[slot, g, pl.ds(u, 1)],
                                      out_hbm.at[pl.ds(rows_ref[0, 2, g * DMA_GROUP + u], 1)], ssem.at[slot]).start()
            return carry
        lax.fori_loop(0, n_here, body, 0)

    @pl.when(t == n_t - 1)
    def _():
        @pl.when((t >= 1) & (t - 1 < n_used))
        def _():
            wait_scatter(1 - slot, ng_ref[jnp.maximum(t - 1, 0)])

        @pl.when(t < n_used)
        def _():
            wait_scatter(slot, ng_ref[t])


def _moe_grouped(h2_ext, expert_ids, wg, wu, wd):
    return _moe_call(h2_ext, *_moe_plan(expert_ids, h2_ext.shape[0]), wg, wu, wd)


def _moe_call(h2_ext, tile_lo, tile_hi, n_groups, n_used, rows, wg, wu, wd):
    n_tok = h2_ext.shape[0]
    tm = MOE_TILE
    n_tiles = tile_lo.shape[0]
    last_used = lambda t, n_used_ref: jnp.minimum(t, jnp.maximum(n_used_ref[0] - 1, 0))
    up, down = (D_MODEL, EXPERT_FF), (EXPERT_FF, D_MODEL)
    any_spec = pl.BlockSpec(memory_space=pl.ANY)
    grid_spec = pltpu.PrefetchScalarGridSpec(
        num_scalar_prefetch=4,
        grid=(n_tiles,),
        in_specs=[pl.BlockSpec((1, 3, tm), lambda t, *s: (last_used(t, s[-1]), 0, 0), memory_space=pltpu.SMEM),
                  any_spec, any_spec, any_spec, any_spec],
        out_specs=pl.BlockSpec(memory_space=pl.ANY),
        scratch_shapes=[pltpu.VMEM((2, tm // DMA_GROUP, DMA_GROUP, MOE_ROW), F32),
                        pltpu.VMEM((2, tm // DMA_GROUP, DMA_GROUP, D_MODEL), F32),
                        [pltpu.VMEM((EXPERTS_PER_GROUP,) + shape, BF16) for shape in (up, up, down)],
                        [pltpu.VMEM((2,) + shape, F32) for shape in (up, up, down)],
                        pltpu.SemaphoreType.DMA((2,)), pltpu.SemaphoreType.DMA((2,)),
                        pltpu.SemaphoreType.DMA((2, 3))],
    )
    return pl.pallas_call(
        _moe_pair_kernel,
        out_shape=jax.ShapeDtypeStruct((n_tok + 2 * DMA_GROUP, D_MODEL), F32),
        grid_spec=grid_spec,
        compiler_params=pltpu.CompilerParams(dimension_semantics=("arbitrary",),
                                             vmem_limit_bytes=VMEM_LIMIT),
        name="moe_pairs",
    )(tile_lo, tile_hi, n_groups, n_used, rows, h2_ext, wg, wu, wd)


def _final_kernel(x1_ref, moe_ref, mod_ref, fg_ref, y_ref):
    x2 = x1_ref[0] + mod_ref[0][5] * moe_ref[...]
    y_ref[0] = _rms_scale(x2) * fg_ref[...]


def _final(x1, moe_out, mod, final_g, *, row_off, tm):
    b, t, _ = x1.shape
    blk0 = row_off // tm
    per_seq = t // tm
    return pl.pallas_call(
        _final_kernel,
        out_shape=jax.ShapeDtypeStruct((b, t, D_MODEL), F32),
        grid=(b, per_seq),
        in_specs=[pl.BlockSpec((1, tm, D_MODEL), lambda i, j: (i, j, 0)),
                  pl.BlockSpec((tm, D_MODEL), lambda i, j: (blk0 + i * per_seq + j, 0)),
                  _mod_spec(mod, tm, lambda i, j: (i, j)),
                  pl.BlockSpec((1, D_MODEL), lambda i, j: (0, 0))],
        out_specs=pl.BlockSpec((1, tm, D_MODEL), lambda i, j: (i, j, 0)),
        compiler_params=pltpu.CompilerParams(dimension_semantics=("parallel", "parallel"),
                                             vmem_limit_bytes=VMEM_LIMIT),
        name="moe_combine_norm",
    )(x1, moe_out, mod, final_g.reshape(1, -1))


def _mixer(x, mod, shift0, s_rwkv0, s_gla0, wts, *, n_tok, row_off, shared=None):
    b, t, _ = x.shape
    per_token = mod.shape[2] > 1
    rows = x.reshape(1, b * t, D_MODEL) if per_token else x
    tm = min(rows.shape[1], ROW_TILE)
    tm_wide = min(rows.shape[1], WIDE_ROW_TILE)
    p_rwkv = _norm_proj(rows, mod, wts["norm1_g"], wts["w_in_rwkv"], gate=False, tm=tm_wide)
    p_gla = _norm_proj(rows, mod, wts["norm1_g"], wts["w_in_gla"], gate=False, tm=tm_wide)
    gates = _norm_proj(rows, mod, wts["norm1_g"], wts["w_in_gate"], gate=True, tm=tm_wide)
    p_rwkv = p_rwkv.reshape(b, t, RWKV_PROJ)

    y_r, s_rwkv = _rwkv(p_rwkv, shift0, s_rwkv0,
                        wts["rwkv_mu"], wts["rwkv_w0"], wts["rwkv_decay_up"], wts["rwkv_a0"],
                        wts["rwkv_aaa_up"], wts["rwkv_gate_up"], wts["rwkv_k_k"], wts["rwkv_k_a"],
                        wts["rwkv_r_k"], wts["rwkv_ln_g"], wts["rwkv_ln_b"], nb=RWKV_SEQS_PER_STEP)
    y_g, s_gla = _gla(p_gla.reshape(b, t, GLA_PROJ_PAD), s_gla0, wts["gla_alpha_up"],
                      wts["gla_alpha_b"], wts["gla_norm_g"], nb=GLA_SEQS_PER_STEP)

    x1, h2_all, route_all = _merge(
        y_r.reshape(rows.shape), y_g.reshape(rows.shape), gates, rows, mod, wts["w_branch_rwkv"],
        wts["w_branch_gla"], wts["w_out"], wts["norm2_g"], wts["router_cat"],
        tm=tm, n_tok=n_tok, row_off=row_off, shared=shared)
    states = (p_rwkv[:, t - 1:t, :][None], s_rwkv[None], s_gla[None])
    return x1, h2_all, route_all, states


def kernel(x_prompt, x_sample, c_prompt, c_sample, state_rwkv_shift, state_rwkv, state_gla, w_ada, b_ada, norm1_g, w_in, rwkv_mu, rwkv_w0, rwkv_decay_up, rwkv_a0, rwkv_aaa_up, rwkv_gate_up, rwkv_k_k, rwkv_k_a, rwkv_r_k, rwkv_ln_g, rwkv_ln_b, gla_alpha_up, gla_alpha_b, gla_norm_g, w_branch_rwkv, w_branch_gla, w_out, norm2_g, router_group, router_expert, expert_w_gate, expert_w_up, expert_w_down, final_norm_g):
    bp, tp = x_prompt.shape[:2]
    bs, ts = x_sample.shape[:2]
    w_in0 = w_in[0]
    g0 = RWKV_PROJ
    o3 = 2 * GLA_DK_TOTAL + GLA_DV_TOTAL
    o4 = o3 + GLA_GATE_RANK
    w_gla = w_in0[:, g0:g0 + GLA_PROJ]
    w_gla = jnp.concatenate([w_gla[:, :o3], w_gla[:, o4:], w_gla[:, o3:o4],
                             jnp.zeros((D_MODEL, GLA_PROJ_PAD - GLA_PROJ), F32)], axis=1)
    wts = dict(
        norm1_g=norm1_g[0].reshape(1, -1),
        w_in_rwkv=w_in0[:, :g0].astype(BF16),
        w_in_gla=w_gla.astype(BF16),
        w_in_gate=w_in0[:, g0 + GLA_PROJ:].astype(BF16),
        rwkv_mu=rwkv_mu[0], rwkv_w0=rwkv_w0[0], rwkv_decay_up=rwkv_decay_up[0], rwkv_a0=rwkv_a0[0],
        rwkv_aaa_up=rwkv_aaa_up[0], rwkv_gate_up=rwkv_gate_up[0], rwkv_k_k=rwkv_k_k[0],
        rwkv_k_a=rwkv_k_a[0], rwkv_r_k=rwkv_r_k[0], rwkv_ln_g=rwkv_ln_g[0], rwkv_ln_b=rwkv_ln_b[0],
        gla_alpha_up=jnp.pad(gla_alpha_up[0], ((0, LANES - GLA_GATE_RANK), (0, 0))),
        gla_alpha_b=gla_alpha_b[0], gla_norm_g=gla_norm_g[0],
        w_branch_rwkv=w_branch_rwkv[0].astype(BF16), w_branch_gla=w_branch_gla[0].astype(BF16),
        w_out=w_out[0].astype(BF16), norm2_g=norm2_g[0],
        router_cat=_split_hi_lo(jnp.pad(jnp.concatenate([router_expert[0], router_group[0]], axis=1),
                                        ((0, 0), (0, LANES - N_EXPERTS - N_GROUPS)))),
        expert_w_gate=expert_w_gate[0], expert_w_up=expert_w_up[0], expert_w_down=expert_w_down[0],
    )
    mod = _ada(jnp.concatenate([c_prompt, c_sample], axis=0), w_ada[0], b_ada[0])
    mod = mod.reshape(bp + bs, N_MOD, D_MODEL)
    n_p = bp * tp
    n_s = bs * ts
    n_tok = n_p + n_s
    mod_p = mod[:bp].reshape(bp, N_MOD, 1, D_MODEL)
    mod_s = jnp.repeat(mod[bp:].transpose(1, 0, 2), ts, axis=1)[None]

    dt = x_prompt.dtype
    x1_p, h2_all, route_all, states_p = _mixer(
        x_prompt, mod_p, jnp.zeros((bp, 1, RWKV_PROJ), dt),
        jnp.zeros((bp, RWKV_HEADS, RWKV_HEAD_DIM, RWKV_HEAD_DIM), state_rwkv.dtype),
        jnp.zeros((bp, GLA_HEADS, GLA_DK, GLA_DV), state_gla.dtype), wts, n_tok=n_tok, row_off=0)
    x1_s, h2_all, route_all, states_s = _mixer(
        x_sample, mod_s, state_rwkv_shift[0], state_rwkv[0], state_gla[0], wts,
        n_tok=n_tok, row_off=n_p, shared=(h2_all, route_all))

    expert_ids = route_all[:, :2].astype(jnp.int32)
    moe_out = _moe_grouped(h2_all, expert_ids, wts["expert_w_gate"], wts["expert_w_up"],
                           wts["expert_w_down"])
    y_p = _final(x1_p, moe_out, mod_p, final_norm_g, row_off=0, tm=min(tp, WIDE_ROW_TILE))
    y_s = _final(x1_s, moe_out, mod_s, final_norm_g, row_off=n_p, tm=min(n_s, WIDE_ROW_TILE))
    return (y_p, y_s.reshape(bs, ts, D_MODEL)) + states_p + states_s
```

```python
import functools
import math

import jax
import jax.numpy as jnp
from jax import lax
from jax.experimental import pallas as pl
from jax.experimental.pallas import tpu as pltpu

F32 = jnp.float32
BF16 = jnp.bfloat16

D_MODEL = 1024
N_MOD = 6
EPS = 1e-6
RWKV_HEAD_DIM = 64
RWKV_HEADS = 16
RWKV_PAIRS = RWKV_HEADS // 2
DECAY_LORA = 64
AAA_LORA = 64
GATE_LORA = 128
RWKV_PROJ = 3 * D_MODEL + DECAY_LORA + AAA_LORA + GATE_LORA
LN_X_EPS = 64e-5
GLA_HEADS = 4
GLA_DK = 128
GLA_DV = 256
GLA_DK_TOTAL = GLA_HEADS * GLA_DK
GLA_DV_TOTAL = GLA_HEADS * GLA_DV
GLA_GATE_RANK = 16
GLA_GATE_NORM = 16.0
GLA_PROJ = 2 * GLA_DK_TOTAL + 2 * GLA_DV_TOTAL + GLA_GATE_RANK
LANES = 128
GLA_PROJ_PAD = 2 * GLA_DK_TOTAL + 2 * GLA_DV_TOTAL + LANES
GATE_PROJ = 2 * D_MODEL
N_GROUPS = 4
EXPERTS_PER_GROUP = 8
N_EXPERTS = 32
EXPERT_FF = 512

CHUNK = 64
ROW_TILE = 512
WIDE_ROW_TILE = 1024
MOE_TILE = 192
DMA_GROUP = 8
MOE_ROW = D_MODEL + LANES
N_PAIR_CLASSES = N_EXPERTS * EXPERTS_PER_GROUP
RWKV_SEQS_PER_STEP = 4
GLA_SEQS_PER_STEP = 8
VMEM_LIMIT = 56 * 1024 * 1024


def _bdot(a, b):
    return jnp.dot(a.astype(BF16), b.astype(BF16), preferred_element_type=F32)


def _bdot_nt(a, b):
    return lax.dot_general(a.astype(BF16), b.astype(BF16), (((1,), (1,)), ((), ())),
                           preferred_element_type=F32)


def _bdot_tn(a, b):
    return jnp.dot(a.T.astype(BF16), b.astype(BF16), preferred_element_type=F32)


def _split3(x):
    h1 = x.astype(BF16)
    r1 = x - h1.astype(F32)
    h2 = r1.astype(BF16)
    h3 = (r1 - h2.astype(F32)).astype(BF16)
    return h1, h2, h3


def _dot3(a, b):
    a1, a2, _ = _split3(a)
    b1, b2, _ = _split3(b)
    return (jnp.dot(a1, b1, preferred_element_type=F32)
            + jnp.dot(a2, b1, preferred_element_type=F32)
            + jnp.dot(a1, b2, preferred_element_type=F32))


def _dot_exact_lhs(a_bf16, x):
    x1, x2, _ = _split3(x)
    return (jnp.dot(a_bf16, x1, preferred_element_type=F32)
            + jnp.dot(a_bf16, x2, preferred_element_type=F32))


def _split_hi_lo(w):
    hi, lo, _ = _split3(w)
    return jnp.concatenate([hi, lo], axis=1)


def _sigmoid(x):
    return 1.0 / (1.0 + jnp.exp(-x))


def _silu(x):
    return x * _sigmoid(x)


def _rms_scale(x):
    return x * lax.rsqrt(jnp.mean(x * x, axis=-1, keepdims=True) + EPS)


def _ada_kernel(c_ref, w_ref, b_ref, o_ref):
    o_ref[...] = _dot3(_silu(c_ref[...]), w_ref[...]) + b_ref[...]


def _ada(c, w_ada, b_ada):
    n_rows = c.shape[0]
    n_out = w_ada.shape[1]
    tn = 1536
    return pl.pallas_call(
        _ada_kernel,
        out_shape=jax.ShapeDtypeStruct((n_rows, n_out), F32),
        grid=(n_out // tn,),
        in_specs=[pl.BlockSpec((n_rows, D_MODEL), lambda j: (0, 0)),
                  pl.BlockSpec((D_MODEL, tn), lambda j: (0, j)),
                  pl.BlockSpec((1, tn), lambda j: (0, j))],
        out_specs=pl.BlockSpec((n_rows, tn), lambda j: (0, j)),
        compiler_params=pltpu.CompilerParams(dimension_semantics=("parallel",),
                                             vmem_limit_bytes=VMEM_LIMIT),
        name="ada",
    )(c, w_ada, b_ada.reshape(1, n_out))


def _mod_spec(mod, tm, seq_tile):
    if mod.shape[2] == 1:
        return pl.BlockSpec((1, N_MOD, 1, D_MODEL), lambda *g: (seq_tile(*g)[0], 0, 0, 0))
    return pl.BlockSpec((1, N_MOD, tm, D_MODEL), lambda *g: (seq_tile(*g)[0], 0, seq_tile(*g)[1], 0))


def _norm_proj_kernel(x_ref, mod_ref, g_ref, w_ref, o_ref, *, gate):
    mod = mod_ref[0]
    h = _rms_scale(x_ref[0]) * g_ref[...] * (1.0 + mod[1]) + mod[0]
    out = _bdot(h, w_ref[...])
    o_ref[0] = _sigmoid(out) if gate else out


def _norm_proj(x, mod, g, w_bf16, *, gate, tm):
    b, t, _ = x.shape
    n = w_bf16.shape[1]
    return pl.pallas_call(
        functools.partial(_norm_proj_kernel, gate=gate),
        out_shape=jax.ShapeDtypeStruct((b, t, n), F32),
        grid=(b, t // tm),
        in_specs=[pl.BlockSpec((1, tm, D_MODEL), lambda i, j: (i, j, 0)),
                  _mod_spec(mod, tm, lambda i, j: (i, j)),
                  pl.BlockSpec((1, D_MODEL), lambda i, j: (0, 0)),
                  pl.BlockSpec((D_MODEL, n), lambda i, j: (0, 0))],
        out_specs=pl.BlockSpec((1, tm, n), lambda i, j: (i, j, 0)),
        compiler_params=pltpu.CompilerParams(dimension_semantics=("parallel", "parallel"),
                                             vmem_limit_bytes=VMEM_LIMIT),
        name="norm_proj_gate" if gate else "norm_proj",
    )(x, mod, g, w_bf16)


def _blockdiag(x, lo_mask):
    return jnp.concatenate([jnp.where(lo_mask, x, 0.0), jnp.where(lo_mask, 0.0, x)], axis=0)


def _rwkv_kernel(p_ref, shift_ref, s0_ref, mu_ref, w0_ref, dup_ref, a0_ref, aup_ref, gup_ref,
                 kk_ref, ka_ref, rk_ref, lng_ref, lnb_ref, y_ref, sout_ref, s_scr, carry_scr,
                 *, nb):
    c = pl.program_id(1)
    n_chunks = pl.num_programs(1)
    C = CHUNK
    N = RWKV_HEAD_DIM
    t_blk = p_ref.shape[1]

    @pl.when(c == 0)
    def _():
        for bl in range(nb):
            for pr in range(RWKV_PAIRS):
                s_scr[bl, pr] = jnp.concatenate([s0_ref[bl, 2 * pr], s0_ref[bl, 2 * pr + 1]], axis=1)
        carry_scr[...] = shift_ref[...]

    row = lax.broadcasted_iota(jnp.int32, (C, LANES), 0)
    lane = lax.broadcasted_iota(jnp.int32, (C, LANES), 1)
    lane_t = jnp.where(lane < N, lane, lane - N)
    lo = lane < N
    strict = lane_t < row
    incl = lane_t <= row
    eye = (lane_t == row).astype(F32)
    valid = row < t_blk
    tri = (lax.broadcasted_iota(jnp.int32, (C, C), 1)
           <= lax.broadcasted_iota(jnp.int32, (C, C), 0)).astype(BF16)
    sq_r = lax.broadcasted_iota(jnp.int32, (LANES, LANES), 0)
    sq_c = lax.broadcasted_iota(jnp.int32, (LANES, LANES), 1)
    same_head = (sq_r < N) == (sq_c < N)
    diag = sq_r == sq_c
    o3 = 3 * D_MODEL

    def hsum(v):
        s_lo = jnp.sum(jnp.where(lo, v, 0.0), axis=1, keepdims=True)
        s_hi = jnp.sum(jnp.where(lo, 0.0, v), axis=1, keepdims=True)
        return jnp.where(lo, s_lo, s_hi)

    def bd(v):
        return _blockdiag(v, lo)

    def cat0(*a):
        return jnp.concatenate(a, axis=0)

    def cat1(*a):
        return jnp.concatenate(a, axis=1)

    valid_w = lax.broadcasted_iota(jnp.int32, (C, D_MODEL), 0) < t_blk
    def shifted(bl, c0, width):
        p = p_ref[bl, :, c0:c0 + width]
        if t_blk < C:
            p = jnp.concatenate([p, jnp.zeros((C - t_blk, width), F32)], axis=0)
        first = lax.broadcasted_iota(jnp.int32, (C, width), 0) == 0
        prev = jnp.where(first, carry_scr[bl, :, c0:c0 + width], pltpu.roll(p, 1, axis=0))
        return p + (prev - p) * mu_ref[:, c0:c0 + width]

    logw_all, cum_all, asig_all, g_all = [], [], [], []
    for bl in range(nb):
        x = shifted(bl, o3, RWKV_PROJ - o3)
        xw = x[:, :DECAY_LORA]
        xa = x[:, DECAY_LORA:DECAY_LORA + AAA_LORA]
        xg = x[:, DECAY_LORA + AAA_LORA:]
        z = w0_ref[...] + _bdot(jnp.tanh(xw), dup_ref[...])
        logw_full = -math.exp(-0.5) * _sigmoid(z)
        if t_blk < C:
            logw_full = jnp.where(valid_w, logw_full, 0.0)
        logw_all.append(logw_full)
        cum_all.append(_dot_exact_lhs(tri, logw_full))
        asig_all.append(_sigmoid(a0_ref[...] + _bdot(xa, aup_ref[...])))
        g_all.append(_bdot(_sigmoid(xg), gup_ref[...]))

    units = [(bl, pr) for bl in range(nb) for pr in range(RWKV_PAIRS)]
    U = range(len(units))
    cols = [slice(pr * LANES, (pr + 1) * LANES) for _, pr in units]
    r = [shifted(bl, pr * LANES, LANES) for bl, pr in units]
    k = [shifted(bl, D_MODEL + pr * LANES, LANES) for bl, pr in units]
    v = [shifted(bl, 2 * D_MODEL + pr * LANES, LANES) for bl, pr in units]
    logw = [logw_all[bl][:, cols[u]] for u, (bl, _) in enumerate(units)]
    asig = [asig_all[bl][:, cols[u]] for u, (bl, _) in enumerate(units)]
    kk = [k[u] * kk_ref[:, cols[u]] for u in U]
    k2 = [k[u] * (1.0 + (asig[u] - 1.0) * ka_ref[:, cols[u]]) for u in U]
    ss = [hsum(kk[u] * kk[u]) for u in U]
    cum = [cum_all[bl][:, cols[u]] for u, (bl, _) in enumerate(units)]
    bsum = [hsum(r[u] * k2[u] * rk_ref[:, cols[u]]) for u in U]
    kk = [kk[u] * lax.rsqrt(jnp.maximum(ss[u], 1e-24)) for u in U]
    bonus = [bsum[u] * v[u] for u in U]
    if t_blk < C:
        kk = [jnp.where(valid, a, 0.0) for a in kk]
        k2 = [jnp.where(valid, a, 0.0) for a in k2]
        v = [jnp.where(valid, a, 0.0) for a in v]
        r = [jnp.where(valid, a, 0.0) for a in r]
    b_vec = [kk[u] * asig[u] for u in U]
    clast = [cum[u][C - 1:C, :] for u in U]
    e_inv = [jnp.exp(-cum[u]) for u in U]
    e_tail = [jnp.exp(clast[u] - cum[u]) for u in U]
    aq = [-kk[u] * jnp.exp(cum[u] - logw[u]) for u in U]
    rq = [r[u] * jnp.exp(cum[u]) for u in U]
    bk = [b_vec[u] * e_inv[u] for u in U]
    kx = [k2[u] * e_inv[u] for u in U]
    bb = [b_vec[u] * e_tail[u] for u in U]
    kb = [k2[u] * e_tail[u] for u in U]

    g4 = [_bdot_nt(cat0(aq[u], rq[u]), cat0(bd(bk[u]), bd(kx[u]))) for u in U]
    m_ab = [jnp.where(strict, g4[u][:C, :LANES], 0.0) for u in U]
    m_ak = [jnp.where(strict, g4[u][:C, LANES:], 0.0) for u in U]
    m_rb = [jnp.where(incl, g4[u][C:, :LANES], 0.0) for u in U]
    m_rk = [jnp.where(incl, g4[u][C:, LANES:], 0.0) for u in U]

    t_inv = [eye + m_ab[u] for u in U]
    l_pow = [_bdot(m_ab[u], bd(m_ab[u])) for u in U]
    mv = [_bdot(cat0(m_ak[u], m_rk[u]), bd(v[u])) for u in U]
    n_steps = C.bit_length() - 1
    for step in range(1, n_steps):
        if step < n_steps - 1:
            both = [_bdot(l_pow[u], cat1(bd(t_inv[u]), bd(l_pow[u]))) for u in U]
            t_inv = [t_inv[u] + both[u][:, :LANES] for u in U]
            l_pow = [both[u][:, LANES:] for u in U]
        else:
            t_inv = [t_inv[u] + _bdot(l_pow[u], bd(t_inv[u])) for u in U]

    au = [_bdot(t_inv[u], cat1(bd(aq[u]), bd(mv[u][:C]))) for u in U]
    a_bar = [au[u][:, :LANES] for u in U]
    u_bar = [au[u][:, LANES:] for u in U]
    mau = [_bdot(m_rb[u], cat1(bd(a_bar[u]), bd(u_bar[u]))) for u in U]
    p_full = [_bdot_tn(a_bar[u], bb[u]) for u in U]
    q_full = [_bdot_tn(cat0(u_bar[u], v[u]), cat0(bb[u], kb[u])) for u in U]
    r_til = [rq[u] + mau[u][:, :LANES] for u in U]
    y_loc = [mau[u][:, LANES:] + mv[u][C:] for u in U]
    p_bd = [jnp.where(same_head, p_full[u], 0.0) + jnp.where(diag, jnp.exp(clast[u]), 0.0) for u in U]
    q_pk = [jnp.where(lo[:N], q_full[u][:N], q_full[u][N:]) for u in U]

    s_old = [s_scr[bl, pr] for bl, pr in units]
    y = [_bdot_nt(r_til[u], _blockdiag(s_old[u], lo[:N])) + y_loc[u] for u in U]
    s_new = [_bdot(s_old[u], p_bd[u]) + q_pk[u] for u in U]
    for u, (bl, pr) in enumerate(units):
        s_scr[bl, pr] = s_new[u]

    mu_y = [hsum(y[u]) * (1.0 / N) for u in U]
    yc = [y[u] - mu_y[u] for u in U]
    var = [hsum(yc[u] * yc[u]) * (1.0 / N) for u in U]
    for u, (bl, pr) in enumerate(units):
        yn = yc[u] * lax.rsqrt(var[u] + LN_X_EPS) * lng_ref[:, cols[u]] + lnb_ref[:, cols[u]]
        y_ref[bl, :, cols[u]] = ((yn + bonus[u]) * g_all[bl][:, cols[u]])[:t_blk]

    for bl in range(nb):
        carry_scr[bl] = p_ref[bl, t_blk - 1:t_blk, :]

    @pl.when(c == n_chunks - 1)
    def _():
        for bl in range(nb):
            for pr in range(RWKV_PAIRS):
                sout_ref[bl, 2 * pr] = s_scr[bl, pr][:, :N]
                sout_ref[bl, 2 * pr + 1] = s_scr[bl, pr][:, N:]


def _rwkv(p_rwkv, shift_prev, s0, mu, w0, decay_up, a0, aaa_up, gate_up, k_k, k_a, r_k,
          ln_g, ln_b, *, nb):
    b, t, _ = p_rwkv.shape
    nb = math.gcd(b, nb)
    t_blk = min(t, CHUNK)
    row = lambda a: a.reshape(1, -1)
    full = lambda shape: pl.BlockSpec(shape, lambda i, j: (0,) * len(shape))
    state_spec = pl.BlockSpec((nb, RWKV_HEADS, RWKV_HEAD_DIM, RWKV_HEAD_DIM), lambda i, j: (i, 0, 0, 0))
    return pl.pallas_call(
        functools.partial(_rwkv_kernel, nb=nb),
        out_shape=(jax.ShapeDtypeStruct((b, t, D_MODEL), F32),
                   jax.ShapeDtypeStruct((b, RWKV_HEADS, RWKV_HEAD_DIM, RWKV_HEAD_DIM), F32)),
        grid=(b // nb, t // t_blk),
        in_specs=[pl.BlockSpec((nb, t_blk, RWKV_PROJ), lambda i, j: (i, j, 0)),
                  pl.BlockSpec((nb, 1, RWKV_PROJ), lambda i, j: (i, 0, 0)),
                  state_spec,
                  full((1, RWKV_PROJ)), full((1, D_MODEL)), full((DECAY_LORA, D_MODEL)),
                  full((1, D_MODEL)), full((AAA_LORA, D_MODEL)), full((GATE_LORA, D_MODEL)),
                  full((1, D_MODEL)), full((1, D_MODEL)), full((1, D_MODEL)),
                  full((1, D_MODEL)), full((1, D_MODEL))],
        out_specs=(pl.BlockSpec((nb, t_blk, D_MODEL), lambda i, j: (i, j, 0)), state_spec),
        scratch_shapes=[pltpu.VMEM((nb, RWKV_PAIRS, RWKV_HEAD_DIM, LANES), F32),
                        pltpu.VMEM((nb, 1, RWKV_PROJ), F32)],
        compiler_params=pltpu.CompilerParams(dimension_semantics=("parallel", "arbitrary"),
                                             vmem_limit_bytes=VMEM_LIMIT),
        name="rwkv_scan",
    )(p_rwkv, shift_prev, s0, row(mu), row(w0), decay_up, row(a0), aaa_up, gate_up,
      row(k_k), row(k_a), row(r_k), row(ln_g), row(ln_b))


def _gla_kernel(p_ref, s0_ref, aup_ref, ab_ref, ng_ref, y_ref, sout_ref, s_scr, *, nb):
    c = pl.program_id(1)
    n_chunks = pl.num_programs(1)
    C = CHUNK
    t_blk = p_ref.shape[1]
    units = [(bl, h) for bl in range(nb) for h in range(GLA_HEADS)]
    U = range(len(units))

    @pl.when(c == 0)
    def _():
        for bl, h in units:
            s_scr[bl, h] = s0_ref[bl, h].T

    o_k = GLA_DK_TOTAL
    o_v = 2 * GLA_DK_TOTAL
    o_r = o_v + GLA_DV_TOTAL
    o_a = o_r + GLA_DV_TOTAL
    tri = (lax.broadcasted_iota(jnp.int32, (C, C), 1)
           <= lax.broadcasted_iota(jnp.int32, (C, C), 0))
    tri_bf = tri.astype(BF16)
    valid_g = lax.broadcasted_iota(jnp.int32, (C, GLA_DK_TOTAL), 0) < t_blk
    valid_k = lax.broadcasted_iota(jnp.int32, (C, GLA_DK), 0) < t_blk
    valid_v = lax.broadcasted_iota(jnp.int32, (C, GLA_DV), 0) < t_blk

    ps, cums = [], []
    for bl in range(nb):
        p = p_ref[bl]
        if t_blk < C:
            p = jnp.concatenate([p, jnp.zeros((C - t_blk, GLA_PROJ_PAD), F32)], axis=0)
        logit = _dot3(p[:, o_a:], aup_ref[...]) + ab_ref[...]
        g = (jnp.minimum(logit, 0.0) - jnp.log(1.0 + jnp.exp(-jnp.abs(logit)))) * (1.0 / GLA_GATE_NORM)
        if t_blk < C:
            g = jnp.where(valid_g, g, 0.0)
        ps.append(p)
        cums.append(_dot_exact_lhs(tri_bf, g))

    q = [ps[bl][:, h * GLA_DK:(h + 1) * GLA_DK] * (GLA_DK ** -0.5) for bl, h in units]
    k = [ps[bl][:, o_k + h * GLA_DK:o_k + (h + 1) * GLA_DK] for bl, h in units]
    v = [ps[bl][:, o_v + h * GLA_DV:o_v + (h + 1) * GLA_DV] for bl, h in units]
    if t_blk < C:
        k = [jnp.where(valid_k, a, 0.0) for a in k]
        v = [jnp.where(valid_v, a, 0.0) for a in v]
    b = [cums[bl][:, h * GLA_DK:(h + 1) * GLA_DK] for bl, h in units]
    clast = [b[u][C - 1:C, :] for u in U]
    cmid = [b[u][C // 2 - 1:C // 2, :] for u in U]
    att = [_bdot_nt(q[u] * jnp.exp(b[u] - cmid[u]), k[u] * jnp.exp(cmid[u] - b[u])) for u in U]
    att = [jnp.where(tri, a, 0.0) for a in att]
    s_t = [s_scr[bl, h] for bl, h in units]
    o = [_bdot(att[u], v[u]) + _bdot_nt(q[u] * jnp.exp(b[u]), s_t[u]) for u in U]
    s_new = [s_t[u] * jnp.exp(clast[u]) + _bdot_tn(v[u], k[u] * jnp.exp(clast[u] - b[u])) for u in U]
    for u, (bl, h) in enumerate(units):
        s_scr[bl, h] = s_new[u]
        r = ps[bl][:, o_r + h * GLA_DV:o_r + (h + 1) * GLA_DV]
        on = o[u] * lax.rsqrt(jnp.mean(o[u] * o[u], axis=-1, keepdims=True) + EPS) * ng_ref[...]
        y_ref[bl, :, h * GLA_DV:(h + 1) * GLA_DV] = (on * _silu(r))[:t_blk]

    @pl.when(c == n_chunks - 1)
    def _():
        for bl, h in units:
            sout_ref[bl, h] = s_scr[bl, h].T


def _gla(p_gla, s0, alpha_up_pad, alpha_b, norm_g, *, nb):
    b, t, _ = p_gla.shape
    nb = math.gcd(b, nb)
    t_blk = min(t, CHUNK)
    full = lambda shape: pl.BlockSpec(shape, lambda i, j: (0,) * len(shape))
    state_spec = pl.BlockSpec((nb, GLA_HEADS, GLA_DK, GLA_DV), lambda i, j: (i, 0, 0, 0))
    return pl.pallas_call(
        functools.partial(_gla_kernel, nb=nb),
        out_shape=(jax.ShapeDtypeStruct((b, t, GLA_DV_TOTAL), F32),
                   jax.ShapeDtypeStruct((b, GLA_HEADS, GLA_DK, GLA_DV), F32)),
        grid=(b // nb, t // t_blk),
        in_specs=[pl.BlockSpec((nb, t_blk, GLA_PROJ_PAD), lambda i, j: (i, j, 0)),
                  state_spec,
                  full((LANES, GLA_DK_TOTAL)), full((1, GLA_DK_TOTAL)), full((1, GLA_DV))],
        out_specs=(pl.BlockSpec((nb, t_blk, GLA_DV_TOTAL), lambda i, j: (i, j, 0)), state_spec),
        scratch_shapes=[pltpu.VMEM((nb, GLA_HEADS, GLA_DV, GLA_DK), F32)],
        compiler_params=pltpu.CompilerParams(dimension_semantics=("parallel", "arbitrary"),
                                             vmem_limit_bytes=VMEM_LIMIT),
        name="gla_scan",
    )(p_gla, s0, alpha_up_pad, alpha_b.reshape(1, -1), norm_g.reshape(1, -1))


def _merge_body(yr_ref, yg_ref, gate_ref, x_ref, mod_ref, wbr_ref, wbg_ref, wout_ref, g2_ref,
                router_ref, x1_ref, h2_ref, route_ref):
    mod = mod_ref[0]
    gates = gate_ref[0]
    merged = (gates[:, :D_MODEL] * _bdot(yr_ref[0], wbr_ref[...])
              + gates[:, D_MODEL:] * _bdot(yg_ref[0], wbg_ref[...]))
    x1 = x_ref[0] + mod[2] * _bdot(merged, wout_ref[...])
    x1_ref[0] = x1
    h2 = _rms_scale(x1) * g2_ref[...] * (1.0 + mod[4]) + mod[3]
    h2_ref[:, :D_MODEL] = h2

    tm = h2.shape[0]
    h_hi, h_lo, _ = _split3(h2)
    prod = jnp.dot(jnp.concatenate([h_hi, h_lo], axis=0), router_ref[...], preferred_element_type=F32)
    logits = prod[:tm, :LANES] + prod[tm:, :LANES] + prod[:tm, LANES:]
    lane = lax.broadcasted_iota(jnp.int32, (tm, LANES), 1)
    neg = -jnp.inf
    is_group = (lane >= N_EXPERTS) & (lane < N_EXPERTS + N_GROUPS)
    gl = jnp.where(is_group, logits, neg)
    gmax = jnp.max(gl, axis=1, keepdims=True)
    g_idx = jnp.min(jnp.where(gl == gmax, lane, LANES), axis=1, keepdims=True) - N_EXPERTS
    p_g = 1.0 / jnp.sum(jnp.exp(gl - gmax), axis=1, keepdims=True)
    in_group = (lane >= g_idx * EXPERTS_PER_GROUP) & (lane < (g_idx + 1) * EXPERTS_PER_GROUP)
    el = jnp.where(in_group, logits, neg)
    v1 = jnp.max(el, axis=1, keepdims=True)
    i1 = jnp.min(jnp.where(el == v1, lane, LANES), axis=1, keepdims=True)
    el2 = jnp.where(lane == i1, neg, el)
    v2 = jnp.max(el2, axis=1, keepdims=True)
    i2 = jnp.min(jnp.where(el2 == v2, lane, LANES), axis=1, keepdims=True)
    e21 = jnp.exp(v2 - v1)
    w1 = p_g / (1.0 + e21)
    route = (jnp.where(lane == 0, i1.astype(F32), 0.0) + jnp.where(lane == 1, i2.astype(F32), 0.0)
             + jnp.where(lane == 2, w1, 0.0) + jnp.where(lane == 3, w1 * e21, 0.0))
    route_ref[...] = route
    h2_ref[:, D_MODEL:] = route


def _merge_first_kernel(*refs, n_own):
    step = pl.program_id(0)

    @pl.when(step < n_own)
    def _():
        _merge_body(*refs)

    @pl.when(step >= n_own)
    def _():
        refs[-2][...] = jnp.zeros(refs[-2].shape, F32)
        refs[-1][...] = jnp.zeros(refs[-1].shape, F32)


def _merge_second_kernel(*refs):
    n_in = 10
    _merge_body(*refs[:n_in], *refs[n_in + 2:])


def _merge(y_r, y_g, gates, x, mod, wbr, wbg, wout, norm2_g, router_cat, *, tm, n_tok, row_off, shared=None):
    b, t, _ = x.shape
    per_seq = t // tm
    n_own = b * per_seq
    first = shared is None
    n_steps = n_tok // tm if first else n_own
    assert row_off % tm == 0 and n_tok % tm == 0 and (row_off == 0 or not first)
    blk0 = row_off // tm
    own = lambda s: jnp.minimum(s, n_own - 1)
    full = lambda shape: pl.BlockSpec(shape, lambda s: (0,) * len(shape))
    tile = lambda n: pl.BlockSpec((1, tm, n), lambda s: (own(s) // per_seq, own(s) % per_seq, 0))
    flat = lambda n: pl.BlockSpec((tm, n), lambda s: (blk0 + s, 0))
    in_specs = [tile(D_MODEL), tile(D_MODEL), tile(GATE_PROJ), tile(D_MODEL),
                _mod_spec(mod, tm, lambda s: (own(s) // per_seq, own(s) % per_seq)),
                full((D_MODEL, D_MODEL)), full((D_MODEL, D_MODEL)), full((D_MODEL, D_MODEL)),
                full((1, D_MODEL)), full((D_MODEL, 2 * LANES))]
    args = [y_r, y_g, gates, x, mod, wbr, wbg, wout, norm2_g.reshape(1, -1), router_cat]
    if not first:
        in_specs += [pl.BlockSpec(memory_space=pl.ANY)] * 2
        args += list(shared)
    return pl.pallas_call(
        functools.partial(_merge_first_kernel, n_own=n_own) if first else _merge_second_kernel,
        out_shape=(jax.ShapeDtypeStruct((b, t, D_MODEL), F32),
                   jax.ShapeDtypeStruct((n_tok, MOE_ROW), F32),
                   jax.ShapeDtypeStruct((n_tok, LANES), F32)),
        grid=(n_steps,),
        in_specs=in_specs,
        out_specs=(tile(D_MODEL), flat(MOE_ROW), flat(LANES)),
        input_output_aliases={} if first else {10: 1, 11: 2},
        compiler_params=pltpu.CompilerParams(dimension_semantics=("arbitrary",),
                                             vmem_limit_bytes=VMEM_LIMIT),
        name="merge_router",
    )(*args)


def _moe_plan(expert_ids, n_tok):
    tm = MOE_TILE
    n_tiles = -(-n_tok // tm) + N_GROUPS * (EXPERTS_PER_GROUP * (EXPERTS_PER_GROUP - 1) // 2)
    n_slots = n_tiles * tm
    n_fill = n_slots - n_tok
    row_bits = 15
    assert max(n_tok, n_fill) <= 1 << row_bits and N_PAIR_CLASSES << (row_bits + 1) <= 1 << 24
    e_lo = jnp.minimum(expert_ids[:, 0], expert_ids[:, 1])
    e_hi = jnp.maximum(expert_ids[:, 0], expert_ids[:, 1])
    cls = e_lo * EXPERTS_PER_GROUP + e_hi % EXPERTS_PER_GROUP
    c_iota = jnp.arange(N_PAIR_CLASSES, dtype=jnp.int32)
    counts = jnp.sum((cls[None, :] == c_iota[:, None]).astype(jnp.int32), axis=1)
    tiles_c = (counts + tm - 1) // tm
    n_used = jnp.sum(tiles_c)
    pad_end = jnp.cumsum(tiles_c * tm - counts)
    f_iota = jnp.arange(n_fill, dtype=jnp.int32)
    f_cls = jnp.minimum(jnp.sum((f_iota[:, None] >= pad_end[None, :]).astype(jnp.int32), axis=1), N_PAIR_CLASSES - 1)
    keys = jnp.concatenate([(cls << (row_bits + 1)) + jnp.arange(n_tok, dtype=jnp.int32),
                            (f_cls << (row_bits + 1)) + (1 << row_bits) + f_iota])
    neg_sorted, _ = lax.top_k(-keys.astype(F32), n_slots)
    slots = (-neg_sorted).astype(jnp.int32).reshape(n_tiles, tm)
    valid = (slots & (1 << row_bits)) == 0
    tok = jnp.where(valid, slots & ((1 << row_bits) - 1), 0)
    t_idx = jnp.arange(n_tiles, dtype=jnp.int32)
    j_idx = jnp.arange(tm, dtype=jnp.int32)[None, :]
    dst = jnp.where(valid, tok, n_tok + (t_idx[:, None] % 2) * DMA_GROUP + j_idx % DMA_GROUP)
    n_valid = jnp.sum(valid.astype(jnp.int32), axis=1)
    n_groups = jnp.where(t_idx < n_used, (n_valid + DMA_GROUP - 1) // DMA_GROUP, 0)
    tile_c = slots[:, 0] >> (row_bits + 1)
    last_c = lax.dynamic_slice(tile_c, (jnp.maximum(n_used - 1, 0),), (1,))
    tile_c = jnp.where(t_idx < n_used, tile_c, last_c)
    tile_lo = tile_c // EXPERTS_PER_GROUP
    tile_hi = (tile_c // (EXPERTS_PER_GROUP * EXPERTS_PER_GROUP)) * EXPERTS_PER_GROUP + tile_c % EXPERTS_PER_GROUP
    return (tile_lo, tile_hi, n_groups.astype(jnp.int32), n_used.reshape(1).astype(jnp.int32),
            tok.reshape(n_tiles, 1, tm), dst.reshape(n_tiles, 1, tm))


def _moe_pair_kernel(lo_ref, hi_ref, ng_ref, n_used_ref, tok_ref, tok_next_ref, dst_ref, h2_hbm,
                     wg_hbm, wu_hbm, wd_hbm, out_hbm,
                     xbuf, obuf, w_res, w_stage, gsem, ssem, wsem):
    t = pl.program_id(0)
    n_t = pl.num_programs(0)
    tm = MOE_TILE
    slot = lax.rem(t, 2)
    n_used = n_used_ref[0]

    def gather(idx_ref, s, n_groups):
        def body(g, carry):
            for u in range(DMA_GROUP):
                pltpu.make_async_copy(h2_hbm.at[pl.ds(idx_ref[0, 0, g * DMA_GROUP + u], 1)],
                                      xbuf.at[s, g, pl.ds(u, 1)], gsem.at[s]).start()
            return carry
        lax.fori_loop(0, n_groups, body, 0)

    def wait_gather(s, n_groups):
        def body(g, carry):
            pltpu.make_async_copy(h2_hbm.at[pl.ds(0, DMA_GROUP)], xbuf.at[s, 0], gsem.at[s]).wait()
            return carry
        lax.fori_loop(0, n_groups, body, 0)

    def wait_scatter(s, n_groups):
        def body(g, carry):
            pltpu.make_async_copy(obuf.at[s, 0], out_hbm.at[pl.ds(0, DMA_GROUP)], ssem.at[s]).wait()
            return carry
        lax.fori_loop(0, n_groups, body, 0)

    @pl.when(t == 0)
    def _():
        xbuf[...] = jnp.zeros(xbuf.shape, F32)
        obuf[0, 0] = jnp.zeros(obuf.shape[2:], F32)
        n_rows = out_hbm.shape[0]
        for half in range(2):
            fill = pltpu.make_async_copy(obuf.at[0, 0], out_hbm.at[pl.ds(n_rows - (2 - half) * DMA_GROUP, DMA_GROUP)],
                                         ssem.at[0])
            fill.start()
            fill.wait()
        gather(tok_ref, 0, ng_ref[0])

    @pl.when((t >= 2) & (t - 2 < n_used))
    def _():
        wait_scatter(slot, ng_ref[jnp.maximum(t - 2, 0)])

    @pl.when(t < n_used)
    def _():
        n_here = ng_ref[t]
        wait_gather(slot, n_here)

        @pl.when(t + 1 < n_used)
        def _():
            gather(tok_next_ref, 1 - slot, ng_ref[jnp.minimum(t + 1, n_t - 1)])

        group = lo_ref[t] // EXPERTS_PER_GROUP

        @pl.when((t == 0) | (group != lo_ref[jnp.maximum(t - 1, 0)] // EXPERTS_PER_GROUP))
        def _():
            def fetch(e, buf):
                return [pltpu.make_async_copy(src.at[group * EXPERTS_PER_GROUP + e], w_stage[m].at[buf], wsem.at[buf, m])
                        for m, src in enumerate((wg_hbm, wu_hbm, wd_hbm))]

            for cp in fetch(0, 0):
                cp.start()
            for e in range(EXPERTS_PER_GROUP):
                if e + 1 < EXPERTS_PER_GROUP:
                    for cp in fetch(e + 1, (e + 1) % 2):
                        cp.start()
                for m, cp in enumerate(fetch(e, e % 2)):
                    cp.wait()
                    w_res[m][e] = w_stage[m][e % 2].astype(BF16)

        xe = xbuf[slot].reshape(tm, MOE_ROW)
        x = xe[:, :D_MODEL].astype(BF16)
        route = xe[:, D_MODEL:]
        lane = lax.broadcasted_iota(jnp.int32, route.shape, 1)
        pick = lambda k: jnp.sum(jnp.where(lane == k, route, 0.0), axis=1, keepdims=True)
        i1, i2, w1, w2 = pick(0), pick(1), pick(2), pick(3)
        w_lo = jnp.where(i1 < i2, w1, w2)
        w_hi = jnp.where(i1 < i2, w2, w1)
        acc = None
        for e_ref, wt in ((lo_ref, w_lo), (hi_ref, w_hi)):
            e_local = lax.rem(e_ref[t], EXPERTS_PER_GROUP)
            hid = _silu(jnp.dot(x, w_res[0][e_local], preferred_element_type=F32)) * jnp.dot(
                x, w_res[1][e_local], preferred_element_type=F32)
            part = wt * _bdot(hid, w_res[2][e_local])
            acc = part if acc is None else acc + part
        obuf[slot] = acc.reshape(obuf.shape[1:])

        def body(g, carry):
            for u in range(DMA_GROUP):
                pltpu.make_async_copy(obuf.at[slot, g, pl.ds(u, 1)],
                                      out_hbm.at[pl.ds(dst_ref[0, 0, g * DMA_GROUP + u], 1)], ssem.at[slot]).start()
            return carry
        lax.fori_loop(0, n_here, body, 0)

    @pl.when(t == n_t - 1)
    def _():
        @pl.when((t >= 1) & (t - 1 < n_used))
        def _():
            wait_scatter(1 - slot, ng_ref[jnp.maximum(t - 1, 0)])

        @pl.when(t < n_used)
        def _():
            wait_scatter(slot, ng_ref[t])


def _moe_grouped(h2_ext, expert_ids, wg, wu, wd):
    return _moe_call(h2_ext, *_moe_plan(expert_ids, h2_ext.shape[0]), wg, wu, wd)


def _moe_call(h2_ext, tile_lo, tile_hi, n_groups, n_used, tok, dst, wg, wu, wd):
    n_tok = h2_ext.shape[0]
    tm = MOE_TILE
    n_tiles = tile_lo.shape[0]
    smem_tile = lambda f: pl.BlockSpec((1, 1, tm), f, memory_space=pltpu.SMEM)
    last_used = lambda t, n_used_ref: jnp.minimum(t, jnp.maximum(n_used_ref[0] - 1, 0))
    up, down = (D_MODEL, EXPERT_FF), (EXPERT_FF, D_MODEL)
    any_spec = pl.BlockSpec(memory_space=pl.ANY)
    grid_spec = pltpu.PrefetchScalarGridSpec(
        num_scalar_prefetch=4,
        grid=(n_tiles,),
        in_specs=[smem_tile(lambda t, *s: (last_used(t, s[-1]), 0, 0)),
                  smem_tile(lambda t, *s: (last_used(t + 1, s[-1]), 0, 0)),
                  smem_tile(lambda t, *s: (last_used(t, s[-1]), 0, 0)),
                  any_spec, any_spec, any_spec, any_spec],
        out_specs=pl.BlockSpec(memory_space=pl.ANY),
        scratch_shapes=[pltpu.VMEM((2, tm // DMA_GROUP, DMA_GROUP, MOE_ROW), F32),
                        pltpu.VMEM((2, tm // DMA_GROUP, DMA_GROUP, D_MODEL), F32),
                        [pltpu.VMEM((EXPERTS_PER_GROUP,) + shape, BF16) for shape in (up, up, down)],
                        [pltpu.VMEM((2,) + shape, F32) for shape in (up, up, down)],
                        pltpu.SemaphoreType.DMA((2,)), pltpu.SemaphoreType.DMA((2,)),
                        pltpu.SemaphoreType.DMA((2, 3))],
    )
    return pl.pallas_call(
        _moe_pair_kernel,
        out_shape=jax.ShapeDtypeStruct((n_tok + 2 * DMA_GROUP, D_MODEL), F32),
        grid_spec=grid_spec,
        compiler_params=pltpu.CompilerParams(dimension_semantics=("arbitrary",),
                                             vmem_limit_bytes=VMEM_LIMIT),
        name="moe_pairs",
    )(tile_lo, tile_hi, n_groups, n_used, tok, tok, dst, h2_ext, wg, wu, wd)


def _final_kernel(x1_ref, moe_ref, mod_ref, fg_ref, y_ref):
    x2 = x1_ref[0] + mod_ref[0][5] * moe_ref[...]
    y_ref[0] = _rms_scale(x2) * fg_ref[...]


def _final(x1, moe_out, mod, final_g, *, row_off, tm):
    b, t, _ = x1.shape
    blk0 = row_off // tm
    per_seq = t // tm
    return pl.pallas_call(
        _final_kernel,
        out_shape=jax.ShapeDtypeStruct((b, t, D_MODEL), F32),
        grid=(b, per_seq),
        in_specs=[pl.BlockSpec((1, tm, D_MODEL), lambda i, j: (i, j, 0)),
                  pl.BlockSpec((tm, D_MODEL), lambda i, j: (blk0 + i * per_seq + j, 0)),
                  _mod_spec(mod, tm, lambda i, j: (i, j)),
                  pl.BlockSpec((1, D_MODEL), lambda i, j: (0, 0))],
        out_specs=pl.BlockSpec((1, tm, D_MODEL), lambda i, j: (i, j, 0)),
        compiler_params=pltpu.CompilerParams(dimension_semantics=("parallel", "parallel"),
                                             vmem_limit_bytes=VMEM_LIMIT),
        name="moe_combine_norm",
    )(x1, moe_out, mod, final_g.reshape(1, -1))


def _mixer(x, mod, shift0, s_rwkv0, s_gla0, wts, *, n_tok, row_off, shared=None):
    b, t, _ = x.shape
    per_token = mod.shape[2] > 1
    rows = x.reshape(1, b * t, D_MODEL) if per_token else x
    tm = min(rows.shape[1], ROW_TILE)
    tm_wide = min(rows.shape[1], WIDE_ROW_TILE)
    p_rwkv = _norm_proj(rows, mod, wts["norm1_g"], wts["w_in_rwkv"], gate=False, tm=tm_wide)
    p_gla = _norm_proj(rows, mod, wts["norm1_g"], wts["w_in_gla"], gate=False, tm=tm_wide)
    gates = _norm_proj(rows, mod, wts["norm1_g"], wts["w_in_gate"], gate=True, tm=tm_wide)
    p_rwkv = p_rwkv.reshape(b, t, RWKV_PROJ)

    y_r, s_rwkv = _rwkv(p_rwkv, shift0, s_rwkv0,
                        wts["rwkv_mu"], wts["rwkv_w0"], wts["rwkv_decay_up"], wts["rwkv_a0"],
                        wts["rwkv_aaa_up"], wts["rwkv_gate_up"], wts["rwkv_k_k"], wts["rwkv_k_a"],
                        wts["rwkv_r_k"], wts["rwkv_ln_g"], wts["rwkv_ln_b"], nb=RWKV_SEQS_PER_STEP)
    y_g, s_gla = _gla(p_gla.reshape(b, t, GLA_PROJ_PAD), s_gla0, wts["gla_alpha_up"],
                      wts["gla_alpha_b"], wts["gla_norm_g"], nb=GLA_SEQS_PER_STEP)

    x1, h2_all, route_all = _merge(
        y_r.reshape(rows.shape), y_g.reshape(rows.shape), gates, rows, mod, wts["w_branch_rwkv"],
        wts["w_branch_gla"], wts["w_out"], wts["norm2_g"], wts["router_cat"],
        tm=tm, n_tok=n_tok, row_off=row_off, shared=shared)
    states = (p_rwkv[:, t - 1:t, :][None], s_rwkv[None], s_gla[None])
    return x1, h2_all, route_all, states


def kernel(x_prompt, x_sample, c_prompt, c_sample, state_rwkv_shift, state_rwkv, state_gla, w_ada, b_ada, norm1_g, w_in, rwkv_mu, rwkv_w0, rwkv_decay_up, rwkv_a0, rwkv_aaa_up, rwkv_gate_up, rwkv_k_k, rwkv_k_a, rwkv_r_k, rwkv_ln_g, rwkv_ln_b, gla_alpha_up, gla_alpha_b, gla_norm_g, w_branch_rwkv, w_branch_gla, w_out, norm2_g, router_group, router_expert, expert_w_gate, expert_w_up, expert_w_down, final_norm_g):
    bp, tp = x_prompt.shape[:2]
    bs, ts = x_sample.shape[:2]
    w_in0 = w_in[0]
    g0 = RWKV_PROJ
    o3 = 2 * GLA_DK_TOTAL + GLA_DV_TOTAL
    o4 = o3 + GLA_GATE_RANK
    w_gla = w_in0[:, g0:g0 + GLA_PROJ]
    w_gla = jnp.concatenate([w_gla[:, :o3], w_gla[:, o4:], w_gla[:, o3:o4],
                             jnp.zeros((D_MODEL, GLA_PROJ_PAD - GLA_PROJ), F32)], axis=1)
    wts = dict(
        norm1_g=norm1_g[0].reshape(1, -1),
        w_in_rwkv=w_in0[:, :g0].astype(BF16),
        w_in_gla=w_gla.astype(BF16),
        w_in_gate=w_in0[:, g0 + GLA_PROJ:].astype(BF16),
        rwkv_mu=rwkv_mu[0], rwkv_w0=rwkv_w0[0], rwkv_decay_up=rwkv_decay_up[0], rwkv_a0=rwkv_a0[0],
        rwkv_aaa_up=rwkv_aaa_up[0], rwkv_gate_up=rwkv_gate_up[0], rwkv_k_k=rwkv_k_k[0],
        rwkv_k_a=rwkv_k_a[0], rwkv_r_k=rwkv_r_k[0], rwkv_ln_g=rwkv_ln_g[0], rwkv_ln_b=rwkv_ln_b[0],
        gla_alpha_up=jnp.pad(gla_alpha_up[0], ((0, LANES - GLA_GATE_RANK), (0, 0))),
        gla_alpha_b=gla_alpha_b[0], gla_norm_g=gla_norm_g[0],
        w_branch_rwkv=w_branch_rwkv[0].astype(BF16), w_branch_gla=w_branch_gla[0].astype(BF16),
        w_out=w_out[0].astype(BF16), norm2_g=norm2_g[0],
        router_cat=_split_hi_lo(jnp.pad(jnp.concatenate([router_expert[0], router_group[0]], axis=1),
                                        ((0, 0), (0, LANES - N_EXPERTS - N_GROUPS)))),
        expert_w_gate=expert_w_gate[0], expert_w_up=expert_w_up[0], expert_w_down=expert_w_down[0],
    )
    mod = _ada(jnp.concatenate([c_prompt, c_sample], axis=0), w_ada[0], b_ada[0])
    mod = mod.reshape(bp + bs, N_MOD, D_MODEL)
    n_p = bp * tp
    n_s = bs * ts
    n_tok = n_p + n_s
    mod_p = mod[:bp].reshape(bp, N_MOD, 1, D_MODEL)
    mod_s = jnp.repeat(mod[bp:].transpose(1, 0, 2), ts, axis=1)[None]

    dt = x_prompt.dtype
    x1_p, h2_all, route_all, states_p = _mixer(
        x_prompt, mod_p, jnp.zeros((bp, 1, RWKV_PROJ), dt),
        jnp.zeros((bp, RWKV_HEADS, RWKV_HEAD_DIM, RWKV_HEAD_DIM), state_rwkv.dtype),
        jnp.zeros((bp, GLA_HEADS, GLA_DK, GLA_DV), state_gla.dtype), wts, n_tok=n_tok, row_off=0)
    x1_s, h2_all, route_all, states_s = _mixer(
        x_sample, mod_s, state_rwkv_shift[0], state_rwkv[0], state_gla[0], wts,
        n_tok=n_tok, row_off=n_p, shared=(h2_all, route_all))

    expert_ids = route_all[:, :2].astype(jnp.int32)
    moe_out = _moe_grouped(h2_all, expert_ids, wts["expert_w_gate"], wts["expert_w_up"],
                           wts["expert_w_down"])
    y_p = _final(x1_p, moe_out, mod_p, final_norm_g, row_off=0, tm=min(tp, WIDE_ROW_TILE))
    y_s = _final(x1_s, moe_out, mod_s, final_norm_g, row_off=n_p, tm=min(n_s, WIDE_ROW_TILE))
    return (y_p, y_s.reshape(bs, ts, D_MODEL)) + states_p + states_s
```

```python
import functools
import math

import jax
import jax.numpy as jnp
from jax import lax
from jax.experimental import pallas as pl
from jax.experimental.pallas import tpu as pltpu

F32 = jnp.float32
BF16 = jnp.bfloat16

D_MODEL = 1024
N_MOD = 6
EPS = 1e-6
RWKV_HEAD_DIM = 64
RWKV_HEADS = 16
RWKV_PAIRS = RWKV_HEADS // 2
DECAY_LORA = 64
AAA_LORA = 64
GATE_LORA = 128
RWKV_PROJ = 3 * D_MODEL + DECAY_LORA + AAA_LORA + GATE_LORA
LN_X_EPS = 64e-5
GLA_HEADS = 4
GLA_DK = 128
GLA_DV = 256
GLA_DK_TOTAL = GLA_HEADS * GLA_DK
GLA_DV_TOTAL = GLA_HEADS * GLA_DV
GLA_GATE_RANK = 16
GLA_GATE_NORM = 16.0
GLA_PROJ = 2 * GLA_DK_TOTAL + 2 * GLA_DV_TOTAL + GLA_GATE_RANK
LANES = 128
GLA_PROJ_PAD = 2 * GLA_DK_TOTAL + 2 * GLA_DV_TOTAL + LANES
GATE_PROJ = 2 * D_MODEL
N_GROUPS = 4
EXPERTS_PER_GROUP = 8
N_EXPERTS = 32
EXPERT_FF = 512

CHUNK = 64
ROW_TILE = 512
WIDE_ROW_TILE = 1024
MOE_TILE = 192
DMA_GROUP = 8
MOE_ROW = D_MODEL + LANES
N_PAIR_CLASSES = N_EXPERTS * EXPERTS_PER_GROUP
RWKV_SEQS_PER_STEP = 4
GLA_SEQS_PER_STEP = 8
VMEM_LIMIT = 56 * 1024 * 1024


def _bdot(a, b):
    return jnp.dot(a.astype(BF16), b.astype(BF16), preferred_element_type=F32)


def _bdot_nt(a, b):
    return lax.dot_general(a.astype(BF16), b.astype(BF16), (((1,), (1,)), ((), ())),
                           preferred_element_type=F32)


def _bdot_tn(a, b):
    return jnp.dot(a.T.astype(BF16), b.astype(BF16), preferred_element_type=F32)


def _split3(x):
    h1 = x.astype(BF16)
    r1 = x - h1.astype(F32)
    h2 = r1.astype(BF16)
    h3 = (r1 - h2.astype(F32)).astype(BF16)
    return h1, h2, h3


def _dot3(a, b):
    a1, a2, _ = _split3(a)
    b1, b2, _ = _split3(b)
    return (jnp.dot(a1, b1, preferred_element_type=F32)
            + jnp.dot(a2, b1, preferred_element_type=F32)
            + jnp.dot(a1, b2, preferred_element_type=F32))


def _dot_exact_lhs(a_bf16, x):
    x1, x2, _ = _split3(x)
    return (jnp.dot(a_bf16, x1, preferred_element_type=F32)
            + jnp.dot(a_bf16, x2, preferred_element_type=F32))


def _split_hi_lo(w):
    hi, lo, _ = _split3(w)
    return jnp.concatenate([hi, lo], axis=1)


def _sigmoid(x):
    return 1.0 / (1.0 + jnp.exp(-x))


def _silu(x):
    return x * _sigmoid(x)


def _rms_scale(x):
    return x * lax.rsqrt(jnp.mean(x * x, axis=-1, keepdims=True) + EPS)


def _ada_kernel(c_ref, w_ref, b_ref, o_ref):
    o_ref[...] = _dot3(_silu(c_ref[...]), w_ref[...]) + b_ref[...]


def _ada(c, w_ada, b_ada):
    n_rows = c.shape[0]
    n_out = w_ada.shape[1]
    tn = 1536
    return pl.pallas_call(
        _ada_kernel,
        out_shape=jax.ShapeDtypeStruct((n_rows, n_out), F32),
        grid=(n_out // tn,),
        in_specs=[pl.BlockSpec((n_rows, D_MODEL), lambda j: (0, 0)),
                  pl.BlockSpec((D_MODEL, tn), lambda j: (0, j)),
                  pl.BlockSpec((1, tn), lambda j: (0, j))],
        out_specs=pl.BlockSpec((n_rows, tn), lambda j: (0, j)),
        compiler_params=pltpu.CompilerParams(dimension_semantics=("parallel",),
                                             vmem_limit_bytes=VMEM_LIMIT),
        name="ada",
    )(c, w_ada, b_ada.reshape(1, n_out))


def _mod_spec(mod, tm, seq_tile):
    if mod.shape[2] == 1:
        return pl.BlockSpec((1, N_MOD, 1, D_MODEL), lambda *g: (seq_tile(*g)[0], 0, 0, 0))
    return pl.BlockSpec((1, N_MOD, tm, D_MODEL), lambda *g: (seq_tile(*g)[0], 0, seq_tile(*g)[1], 0))


def _norm_proj_kernel(x_ref, mod_ref, g_ref, w_ref, o_ref, *, gate):
    half = x_ref.shape[1] // 2
    for r0 in (0, half):
        rows = slice(0, 1) if mod_ref.shape[2] == 1 else slice(r0, r0 + half)
        h = _rms_scale(x_ref[0, r0:r0 + half]) * g_ref[...] * (1.0 + mod_ref[0, 1, rows]) + mod_ref[0, 0, rows]
        out = _bdot(h, w_ref[...])
        o_ref[0, r0:r0 + half] = _sigmoid(out) if gate else out


def _norm_proj(x, mod, g, w_bf16, *, gate, tm):
    b, t, _ = x.shape
    n = w_bf16.shape[1]
    return pl.pallas_call(
        functools.partial(_norm_proj_kernel, gate=gate),
        out_shape=jax.ShapeDtypeStruct((b, t, n), F32),
        grid=(b, t // tm),
        in_specs=[pl.BlockSpec((1, tm, D_MODEL), lambda i, j: (i, j, 0)),
                  _mod_spec(mod, tm, lambda i, j: (i, j)),
                  pl.BlockSpec((1, D_MODEL), lambda i, j: (0, 0)),
                  pl.BlockSpec((D_MODEL, n), lambda i, j: (0, 0))],
        out_specs=pl.BlockSpec((1, tm, n), lambda i, j: (i, j, 0)),
        compiler_params=pltpu.CompilerParams(dimension_semantics=("parallel", "parallel"),
                                             vmem_limit_bytes=VMEM_LIMIT),
        name="norm_proj_gate" if gate else "norm_proj",
    )(x, mod, g, w_bf16)


def _blockdiag(x, lo_mask):
    return jnp.concatenate([jnp.where(lo_mask, x, 0.0), jnp.where(lo_mask, 0.0, x)], axis=0)


def _rwkv_kernel(p_ref, shift_ref, s0_ref, mu_ref, w0_ref, dup_ref, a0_ref, aup_ref, gup_ref,
                 kk_ref, ka_ref, rk_ref, lng_ref, lnb_ref, y_ref, sout_ref, s_scr, carry_scr,
                 *, nb):
    c = pl.program_id(1)
    n_chunks = pl.num_programs(1)
    C = CHUNK
    N = RWKV_HEAD_DIM
    t_blk = p_ref.shape[1]

    @pl.when(c == 0)
    def _():
        for bl in range(nb):
            for pr in range(RWKV_PAIRS):
                s_scr[bl, pr] = jnp.concatenate([s0_ref[bl, 2 * pr], s0_ref[bl, 2 * pr + 1]], axis=1)
        carry_scr[...] = shift_ref[...]

    row = lax.broadcasted_iota(jnp.int32, (C, LANES), 0)
    lane = lax.broadcasted_iota(jnp.int32, (C, LANES), 1)
    lane_t = jnp.where(lane < N, lane, lane - N)
    lo = lane < N
    strict = lane_t < row
    incl = lane_t <= row
    eye = (lane_t == row).astype(F32)
    valid = row < t_blk
    tri = (lax.broadcasted_iota(jnp.int32, (C, C), 1)
           <= lax.broadcasted_iota(jnp.int32, (C, C), 0)).astype(BF16)
    sq_r = lax.broadcasted_iota(jnp.int32, (LANES, LANES), 0)
    sq_c = lax.broadcasted_iota(jnp.int32, (LANES, LANES), 1)
    same_head = (sq_r < N) == (sq_c < N)
    diag = sq_r == sq_c
    o3 = 3 * D_MODEL

    def hsum(v):
        s_lo = jnp.sum(jnp.where(lo, v, 0.0), axis=1, keepdims=True)
        s_hi = jnp.sum(jnp.where(lo, 0.0, v), axis=1, keepdims=True)
        return jnp.where(lo, s_lo, s_hi)

    def bd(v):
        return _blockdiag(v, lo)

    def cat0(*a):
        return jnp.concatenate(a, axis=0)

    def cat1(*a):
        return jnp.concatenate(a, axis=1)

    valid_w = lax.broadcasted_iota(jnp.int32, (C, D_MODEL), 0) < t_blk
    def shifted(bl, c0, width):
        p = p_ref[bl, :, c0:c0 + width]
        if t_blk < C:
            p = jnp.concatenate([p, jnp.zeros((C - t_blk, width), F32)], axis=0)
        first = lax.broadcasted_iota(jnp.int32, (C, width), 0) == 0
        prev = jnp.where(first, carry_scr[bl, :, c0:c0 + width], pltpu.roll(p, 1, axis=0))
        return p + (prev - p) * mu_ref[:, c0:c0 + width]

    logw_all, cum_all, asig_all, g_all = [], [], [], []
    for bl in range(nb):
        x = shifted(bl, o3, RWKV_PROJ - o3)
        xw = x[:, :DECAY_LORA]
        xa = x[:, DECAY_LORA:DECAY_LORA + AAA_LORA]
        xg = x[:, DECAY_LORA + AAA_LORA:]
        z = w0_ref[...] + _bdot(jnp.tanh(xw), dup_ref[...])
        logw_full = -math.exp(-0.5) * _sigmoid(z)
        if t_blk < C:
            logw_full = jnp.where(valid_w, logw_full, 0.0)
        logw_all.append(logw_full)
        cum_all.append(_dot_exact_lhs(tri, logw_full))
        asig_all.append(_sigmoid(a0_ref[...] + _bdot(xa, aup_ref[...])))
        g_all.append(_bdot(_sigmoid(xg), gup_ref[...]))

    units = [(bl, pr) for bl in range(nb) for pr in range(RWKV_PAIRS)]
    U = range(len(units))
    cols = [slice(pr * LANES, (pr + 1) * LANES) for _, pr in units]
    r = [shifted(bl, pr * LANES, LANES) for bl, pr in units]
    k = [shifted(bl, D_MODEL + pr * LANES, LANES) for bl, pr in units]
    v = [shifted(bl, 2 * D_MODEL + pr * LANES, LANES) for bl, pr in units]
    logw = [logw_all[bl][:, cols[u]] for u, (bl, _) in enumerate(units)]
    asig = [asig_all[bl][:, cols[u]] for u, (bl, _) in enumerate(units)]
    kk = [k[u] * kk_ref[:, cols[u]] for u in U]
    k2 = [k[u] * (1.0 + (asig[u] - 1.0) * ka_ref[:, cols[u]]) for u in U]
    ss = [hsum(kk[u] * kk[u]) for u in U]
    cum = [cum_all[bl][:, cols[u]] for u, (bl, _) in enumerate(units)]
    bsum = [hsum(r[u] * k2[u] * rk_ref[:, cols[u]]) for u in U]
    kk = [kk[u] * lax.rsqrt(jnp.maximum(ss[u], 1e-24)) for u in U]
    bonus = [bsum[u] * v[u] for u in U]
    if t_blk < C:
        kk = [jnp.where(valid, a, 0.0) for a in kk]
        k2 = [jnp.where(valid, a, 0.0) for a in k2]
        v = [jnp.where(valid, a, 0.0) for a in v]
        r = [jnp.where(valid, a, 0.0) for a in r]
    b_vec = [kk[u] * asig[u] for u in U]
    clast = [cum[u][C - 1:C, :] for u in U]
    e_inv = [jnp.exp(-cum[u]) for u in U]
    e_tail = [jnp.exp(clast[u] - cum[u]) for u in U]
    aq = [-kk[u] * jnp.exp(cum[u] - logw[u]) for u in U]
    rq = [r[u] * jnp.exp(cum[u]) for u in U]
    bk = [b_vec[u] * e_inv[u] for u in U]
    kx = [k2[u] * e_inv[u] for u in U]
    bb = [b_vec[u] * e_tail[u] for u in U]
    kb = [k2[u] * e_tail[u] for u in U]

    g4 = [_bdot_nt(cat0(aq[u], rq[u]), cat0(bd(bk[u]), bd(kx[u]))) for u in U]
    m_ab = [jnp.where(strict, g4[u][:C, :LANES], 0.0) for u in U]
    m_ak = [jnp.where(strict, g4[u][:C, LANES:], 0.0) for u in U]
    m_rb = [jnp.where(incl, g4[u][C:, :LANES], 0.0) for u in U]
    m_rk = [jnp.where(incl, g4[u][C:, LANES:], 0.0) for u in U]

    t_inv = [eye + m_ab[u] for u in U]
    l_pow = [_bdot(m_ab[u], bd(m_ab[u])) for u in U]
    mv = [_bdot(cat0(m_ak[u], m_rk[u]), bd(v[u])) for u in U]
    n_steps = C.bit_length() - 1
    for step in range(1, n_steps):
        if step < n_steps - 1:
            both = [_bdot(l_pow[u], cat1(bd(t_inv[u]), bd(l_pow[u]))) for u in U]
            t_inv = [t_inv[u] + both[u][:, :LANES] for u in U]
            l_pow = [both[u][:, LANES:] for u in U]
        else:
            t_inv = [t_inv[u] + _bdot(l_pow[u], bd(t_inv[u])) for u in U]

    au = [_bdot(t_inv[u], cat1(bd(aq[u]), bd(mv[u][:C]))) for u in U]
    a_bar = [au[u][:, :LANES] for u in U]
    u_bar = [au[u][:, LANES:] for u in U]
    mau = [_bdot(m_rb[u], cat1(bd(a_bar[u]), bd(u_bar[u]))) for u in U]
    p_full = [_bdot_tn(a_bar[u], bb[u]) for u in U]
    q_full = [_bdot_tn(cat0(u_bar[u], v[u]), cat0(bb[u], kb[u])) for u in U]
    r_til = [rq[u] + mau[u][:, :LANES] for u in U]
    y_loc = [mau[u][:, LANES:] + mv[u][C:] for u in U]
    p_bd = [jnp.where(same_head, p_full[u], 0.0) + jnp.where(diag, jnp.exp(clast[u]), 0.0) for u in U]
    q_pk = [jnp.where(lo[:N], q_full[u][:N], q_full[u][N:]) for u in U]

    s_old = [s_scr[bl, pr] for bl, pr in units]
    y = [_bdot_nt(r_til[u], _blockdiag(s_old[u], lo[:N])) + y_loc[u] for u in U]
    s_new = [_bdot(s_old[u], p_bd[u]) + q_pk[u] for u in U]
    for u, (bl, pr) in enumerate(units):
        s_scr[bl, pr] = s_new[u]

    mu_y = [hsum(y[u]) * (1.0 / N) for u in U]
    yc = [y[u] - mu_y[u] for u in U]
    var = [hsum(yc[u] * yc[u]) * (1.0 / N) for u in U]
    for u, (bl, pr) in enumerate(units):
        yn = yc[u] * lax.rsqrt(var[u] + LN_X_EPS) * lng_ref[:, cols[u]] + lnb_ref[:, cols[u]]
        y_ref[bl, :, cols[u]] = ((yn + bonus[u]) * g_all[bl][:, cols[u]])[:t_blk]

    for bl in range(nb):
        carry_scr[bl] = p_ref[bl, t_blk - 1:t_blk, :]

    @pl.when(c == n_chunks - 1)
    def _():
        for bl in range(nb):
            for pr in range(RWKV_PAIRS):
                sout_ref[bl, 2 * pr] = s_scr[bl, pr][:, :N]
                sout_ref[bl, 2 * pr + 1] = s_scr[bl, pr][:, N:]


def _rwkv(p_rwkv, shift_prev, s0, mu, w0, decay_up, a0, aaa_up, gate_up, k_k, k_a, r_k,
          ln_g, ln_b, *, nb):
    b, t, _ = p_rwkv.shape
    nb = math.gcd(b, nb)
    t_blk = min(t, CHUNK)
    row = lambda a: a.reshape(1, -1)
    full = lambda shape: pl.BlockSpec(shape, lambda i, j: (0,) * len(shape))
    state_spec = pl.BlockSpec((nb, RWKV_HEADS, RWKV_HEAD_DIM, RWKV_HEAD_DIM), lambda i, j: (i, 0, 0, 0))
    return pl.pallas_call(
        functools.partial(_rwkv_kernel, nb=nb),
        out_shape=(jax.ShapeDtypeStruct((b, t, D_MODEL), F32),
                   jax.ShapeDtypeStruct((b, RWKV_HEADS, RWKV_HEAD_DIM, RWKV_HEAD_DIM), F32)),
        grid=(b // nb, t // t_blk),
        in_specs=[pl.BlockSpec((nb, t_blk, RWKV_PROJ), lambda i, j: (i, j, 0)),
                  pl.BlockSpec((nb, 1, RWKV_PROJ), lambda i, j: (i, 0, 0)),
                  state_spec,
                  full((1, RWKV_PROJ)), full((1, D_MODEL)), full((DECAY_LORA, D_MODEL)),
                  full((1, D_MODEL)), full((AAA_LORA, D_MODEL)), full((GATE_LORA, D_MODEL)),
                  full((1, D_MODEL)), full((1, D_MODEL)), full((1, D_MODEL)),
                  full((1, D_MODEL)), full((1, D_MODEL))],
        out_specs=(pl.BlockSpec((nb, t_blk, D_MODEL), lambda i, j: (i, j, 0)), state_spec),
        scratch_shapes=[pltpu.VMEM((nb, RWKV_PAIRS, RWKV_HEAD_DIM, LANES), F32),
                        pltpu.VMEM((nb, 1, RWKV_PROJ), F32)],
        compiler_params=pltpu.CompilerParams(dimension_semantics=("parallel", "arbitrary"),
                                             vmem_limit_bytes=VMEM_LIMIT),
        name="rwkv_scan",
    )(p_rwkv, shift_prev, s0, row(mu), row(w0), decay_up, row(a0), aaa_up, gate_up,
      row(k_k), row(k_a), row(r_k), row(ln_g), row(ln_b))


def _gla_kernel(p_ref, s0_ref, aup_ref, ab_ref, ng_ref, y_ref, sout_ref, s_scr, *, nb):
    c = pl.program_id(1)
    n_chunks = pl.num_programs(1)
    C = CHUNK
    t_blk = p_ref.shape[1]
    units = [(bl, h) for bl in range(nb) for h in range(GLA_HEADS)]
    U = range(len(units))

    @pl.when(c == 0)
    def _():
        for bl, h in units:
            s_scr[bl, h] = s0_ref[bl, h].T

    o_k = GLA_DK_TOTAL
    o_v = 2 * GLA_DK_TOTAL
    o_r = o_v + GLA_DV_TOTAL
    o_a = o_r + GLA_DV_TOTAL
    tri = (lax.broadcasted_iota(jnp.int32, (C, C), 1)
           <= lax.broadcasted_iota(jnp.int32, (C, C), 0))
    tri_bf = tri.astype(BF16)
    valid_g = lax.broadcasted_iota(jnp.int32, (C, GLA_DK_TOTAL), 0) < t_blk
    valid_k = lax.broadcasted_iota(jnp.int32, (C, GLA_DK), 0) < t_blk
    valid_v = lax.broadcasted_iota(jnp.int32, (C, GLA_DV), 0) < t_blk

    ps, cums = [], []
    for bl in range(nb):
        p = p_ref[bl]
        if t_blk < C:
            p = jnp.concatenate([p, jnp.zeros((C - t_blk, GLA_PROJ_PAD), F32)], axis=0)
        logit = _dot3(p[:, o_a:], aup_ref[...]) + ab_ref[...]
        g = (jnp.minimum(logit, 0.0) - jnp.log(1.0 + jnp.exp(-jnp.abs(logit)))) * (1.0 / GLA_GATE_NORM)
        if t_blk < C:
            g = jnp.where(valid_g, g, 0.0)
        ps.append(p)
        cums.append(_dot_exact_lhs(tri_bf, g))

    q = [ps[bl][:, h * GLA_DK:(h + 1) * GLA_DK] * (GLA_DK ** -0.5) for bl, h in units]
    k = [ps[bl][:, o_k + h * GLA_DK:o_k + (h + 1) * GLA_DK] for bl, h in units]
    v = [ps[bl][:, o_v + h * GLA_DV:o_v + (h + 1) * GLA_DV] for bl, h in units]
    if t_blk < C:
        k = [jnp.where(valid_k, a, 0.0) for a in k]
        v = [jnp.where(valid_v, a, 0.0) for a in v]
    b = [cums[bl][:, h * GLA_DK:(h + 1) * GLA_DK] for bl, h in units]
    clast = [b[u][C - 1:C, :] for u in U]
    cmid = [b[u][C // 2 - 1:C // 2, :] for u in U]
    att = [_bdot_nt(q[u] * jnp.exp(b[u] - cmid[u]), k[u] * jnp.exp(cmid[u] - b[u])) for u in U]
    att = [jnp.where(tri, a, 0.0) for a in att]
    s_t = [s_scr[bl, h] for bl, h in units]
    o = [_bdot(att[u], v[u]) + _bdot_nt(q[u] * jnp.exp(b[u]), s_t[u]) for u in U]
    s_new = [s_t[u] * jnp.exp(clast[u]) + _bdot_tn(v[u], k[u] * jnp.exp(clast[u] - b[u])) for u in U]
    for u, (bl, h) in enumerate(units):
        s_scr[bl, h] = s_new[u]
        r = ps[bl][:, o_r + h * GLA_DV:o_r + (h + 1) * GLA_DV]
        on = o[u] * lax.rsqrt(jnp.mean(o[u] * o[u], axis=-1, keepdims=True) + EPS) * ng_ref[...]
        y_ref[bl, :, h * GLA_DV:(h + 1) * GLA_DV] = (on * _silu(r))[:t_blk]

    @pl.when(c == n_chunks - 1)
    def _():
        for bl, h in units:
            sout_ref[bl, h] = s_scr[bl, h].T


def _gla(p_gla, s0, alpha_up_pad, alpha_b, norm_g, *, nb):
    b, t, _ = p_gla.shape
    nb = math.gcd(b, nb)
    t_blk = min(t, CHUNK)
    full = lambda shape: pl.BlockSpec(shape, lambda i, j: (0,) * len(shape))
    state_spec = pl.BlockSpec((nb, GLA_HEADS, GLA_DK, GLA_DV), lambda i, j: (i, 0, 0, 0))
    return pl.pallas_call(
        functools.partial(_gla_kernel, nb=nb),
        out_shape=(jax.ShapeDtypeStruct((b, t, GLA_DV_TOTAL), F32),
                   jax.ShapeDtypeStruct((b, GLA_HEADS, GLA_DK, GLA_DV), F32)),
        grid=(b // nb, t // t_blk),
        in_specs=[pl.BlockSpec((nb, t_blk, GLA_PROJ_PAD), lambda i, j: (i, j, 0)),
                  state_spec,
                  full((LANES, GLA_DK_TOTAL)), full((1, GLA_DK_TOTAL)), full((1, GLA_DV))],
        out_specs=(pl.BlockSpec((nb, t_blk, GLA_DV_TOTAL), lambda i, j: (i, j, 0)), state_spec),
        scratch_shapes=[pltpu.VMEM((nb, GLA_HEADS, GLA_DV, GLA_DK), F32)],
        compiler_params=pltpu.CompilerParams(dimension_semantics=("parallel", "arbitrary"),
                                             vmem_limit_bytes=VMEM_LIMIT),
        name="gla_scan",
    )(p_gla, s0, alpha_up_pad, alpha_b.reshape(1, -1), norm_g.reshape(1, -1))


def _merge_body(yr_ref, yg_ref, gate_ref, x_ref, mod_ref, wbr_ref, wbg_ref, wout_ref, g2_ref,
                router_ref, x1_ref, h2_ref, route_ref):
    mod = mod_ref[0]
    gates = gate_ref[0]
    merged = (gates[:, :D_MODEL] * _bdot(yr_ref[0], wbr_ref[...])
              + gates[:, D_MODEL:] * _bdot(yg_ref[0], wbg_ref[...]))
    x1 = x_ref[0] + mod[2] * _bdot(merged, wout_ref[...])
    x1_ref[0] = x1
    h2 = _rms_scale(x1) * g2_ref[...] * (1.0 + mod[4]) + mod[3]
    h2_ref[:, :D_MODEL] = h2

    tm = h2.shape[0]
    h_hi, h_lo, _ = _split3(h2)
    prod = jnp.dot(jnp.concatenate([h_hi, h_lo], axis=0), router_ref[...], preferred_element_type=F32)
    logits = prod[:tm, :LANES] + prod[tm:, :LANES] + prod[:tm, LANES:]
    lane = lax.broadcasted_iota(jnp.int32, (tm, LANES), 1)
    neg = -jnp.inf
    is_group = (lane >= N_EXPERTS) & (lane < N_EXPERTS + N_GROUPS)
    gl = jnp.where(is_group, logits, neg)
    gmax = jnp.max(gl, axis=1, keepdims=True)
    g_idx = jnp.min(jnp.where(gl == gmax, lane, LANES), axis=1, keepdims=True) - N_EXPERTS
    p_g = 1.0 / jnp.sum(jnp.exp(gl - gmax), axis=1, keepdims=True)
    in_group = (lane >= g_idx * EXPERTS_PER_GROUP) & (lane < (g_idx + 1) * EXPERTS_PER_GROUP)
    el = jnp.where(in_group, logits, neg)
    v1 = jnp.max(el, axis=1, keepdims=True)
    i1 = jnp.min(jnp.where(el == v1, lane, LANES), axis=1, keepdims=True)
    el2 = jnp.where(lane == i1, neg, el)
    v2 = jnp.max(el2, axis=1, keepdims=True)
    i2 = jnp.min(jnp.where(el2 == v2, lane, LANES), axis=1, keepdims=True)
    e21 = jnp.exp(v2 - v1)
    w1 = p_g / (1.0 + e21)
    route = (jnp.where(lane == 0, i1.astype(F32), 0.0) + jnp.where(lane == 1, i2.astype(F32), 0.0)
             + jnp.where(lane == 2, w1, 0.0) + jnp.where(lane == 3, w1 * e21, 0.0))
    route_ref[...] = route
    h2_ref[:, D_MODEL:] = route


def _merge_first_kernel(*refs, n_own):
    step = pl.program_id(0)

    @pl.when(step < n_own)
    def _():
        _merge_body(*refs)

    @pl.when(step >= n_own)
    def _():
        refs[-2][...] = jnp.zeros(refs[-2].shape, F32)
        refs[-1][...] = jnp.zeros(refs[-1].shape, F32)


def _merge_second_kernel(*refs):
    n_in = 10
    _merge_body(*refs[:n_in], *refs[n_in + 2:])


def _merge(y_r, y_g, gates, x, mod, wbr, wbg, wout, norm2_g, router_cat, *, tm, n_tok, row_off, shared=None):
    b, t, _ = x.shape
    per_seq = t // tm
    n_own = b * per_seq
    first = shared is None
    n_steps = n_tok // tm if first else n_own
    assert row_off % tm == 0 and n_tok % tm == 0 and (row_off == 0 or not first)
    blk0 = row_off // tm
    own = lambda s: jnp.minimum(s, n_own - 1)
    full = lambda shape: pl.BlockSpec(shape, lambda s: (0,) * len(shape))
    tile = lambda n: pl.BlockSpec((1, tm, n), lambda s: (own(s) // per_seq, own(s) % per_seq, 0))
    flat = lambda n: pl.BlockSpec((tm, n), lambda s: (blk0 + s, 0))
    in_specs = [tile(D_MODEL), tile(D_MODEL), tile(GATE_PROJ), tile(D_MODEL),
                _mod_spec(mod, tm, lambda s: (own(s) // per_seq, own(s) % per_seq)),
                full((D_MODEL, D_MODEL)), full((D_MODEL, D_MODEL)), full((D_MODEL, D_MODEL)),
                full((1, D_MODEL)), full((D_MODEL, 2 * LANES))]
    args = [y_r, y_g, gates, x, mod, wbr, wbg, wout, norm2_g.reshape(1, -1), router_cat]
    if not first:
        in_specs += [pl.BlockSpec(memory_space=pl.ANY)] * 2
        args += list(shared)
    return pl.pallas_call(
        functools.partial(_merge_first_kernel, n_own=n_own) if first else _merge_second_kernel,
        out_shape=(jax.ShapeDtypeStruct((b, t, D_MODEL), F32),
                   jax.ShapeDtypeStruct((n_tok, MOE_ROW), F32),
                   jax.ShapeDtypeStruct((n_tok, LANES), F32)),
        grid=(n_steps,),
        in_specs=in_specs,
        out_specs=(tile(D_MODEL), flat(MOE_ROW), flat(LANES)),
        input_output_aliases={} if first else {10: 1, 11: 2},
        compiler_params=pltpu.CompilerParams(dimension_semantics=("arbitrary",),
                                             vmem_limit_bytes=VMEM_LIMIT),
        name="merge_router",
    )(*args)


def _moe_plan(expert_ids, n_tok):
    tm = MOE_TILE
    n_tiles = -(-n_tok // tm) + N_GROUPS * (EXPERTS_PER_GROUP * (EXPERTS_PER_GROUP - 1) // 2)
    n_slots = n_tiles * tm
    n_fill = n_slots - n_tok
    row_bits = 15
    assert max(n_tok, n_fill) <= 1 << row_bits and N_PAIR_CLASSES << (row_bits + 1) <= 1 << 24
    e_lo = jnp.minimum(expert_ids[:, 0], expert_ids[:, 1])
    e_hi = jnp.maximum(expert_ids[:, 0], expert_ids[:, 1])
    cls = e_lo * EXPERTS_PER_GROUP + e_hi % EXPERTS_PER_GROUP
    c_iota = jnp.arange(N_PAIR_CLASSES, dtype=jnp.int32)
    counts = jnp.sum((cls[None, :] == c_iota[:, None]).astype(jnp.int32), axis=1)
    tiles_c = (counts + tm - 1) // tm
    n_used = jnp.sum(tiles_c)
    pad_end = jnp.cumsum(tiles_c * tm - counts)
    f_iota = jnp.arange(n_fill, dtype=jnp.int32)
    f_cls = jnp.minimum(jnp.sum((f_iota[:, None] >= pad_end[None, :]).astype(jnp.int32), axis=1), N_PAIR_CLASSES - 1)
    keys = jnp.concatenate([(cls << (row_bits + 1)) + jnp.arange(n_tok, dtype=jnp.int32),
                            (f_cls << (row_bits + 1)) + (1 << row_bits) + f_iota])
    neg_sorted, _ = lax.top_k(-keys.astype(F32), n_slots)
    slots = (-neg_sorted).astype(jnp.int32).reshape(n_tiles, tm)
    valid = (slots & (1 << row_bits)) == 0
    tok = jnp.where(valid, slots & ((1 << row_bits) - 1), 0)
    t_idx = jnp.arange(n_tiles, dtype=jnp.int32)
    j_idx = jnp.arange(tm, dtype=jnp.int32)[None, :]
    dst = jnp.where(valid, tok, n_tok + (t_idx[:, None] % 2) * DMA_GROUP + j_idx % DMA_GROUP)
    n_valid = jnp.sum(valid.astype(jnp.int32), axis=1)
    n_groups = jnp.where(t_idx < n_used, (n_valid + DMA_GROUP - 1) // DMA_GROUP, 0)
    tile_c = slots[:, 0] >> (row_bits + 1)
    last_c = lax.dynamic_slice(tile_c, (jnp.maximum(n_used - 1, 0),), (1,))
    tile_c = jnp.where(t_idx < n_used, tile_c, last_c)
    tile_lo = tile_c // EXPERTS_PER_GROUP
    tile_hi = (tile_c // (EXPERTS_PER_GROUP * EXPERTS_PER_GROUP)) * EXPERTS_PER_GROUP + tile_c % EXPERTS_PER_GROUP
    return (tile_lo, tile_hi, n_groups.astype(jnp.int32), n_used.reshape(1).astype(jnp.int32),
            tok.reshape(n_tiles, 1, tm), dst.reshape(n_tiles, 1, tm))


def _moe_pair_kernel(lo_ref, hi_ref, ng_ref, n_used_ref, tok_ref, tok_next_ref, dst_ref, h2_hbm,
                     wg_hbm, wu_hbm, wd_hbm, out_hbm,
                     xbuf, obuf, w_res, w_stage, gsem, ssem, wsem):
    t = pl.program_id(0)
    n_t = pl.num_programs(0)
    tm = MOE_TILE
    slot = lax.rem(t, 2)
    n_used = n_used_ref[0]

    def gather(idx_ref, s, n_groups):
        def body(g, carry):
            for u in range(DMA_GROUP):
                pltpu.make_async_copy(h2_hbm.at[pl.ds(idx_ref[0, 0, g * DMA_GROUP + u], 1)],
                                      xbuf.at[s, g, pl.ds(u, 1)], gsem.at[s]).start()
            return carry
        lax.fori_loop(0, n_groups, body, 0)

    def wait_gather(s, n_groups):
        def body(g, carry):
            pltpu.make_async_copy(h2_hbm.at[pl.ds(0, DMA_GROUP)], xbuf.at[s, 0], gsem.at[s]).wait()
            return carry
        lax.fori_loop(0, n_groups, body, 0)

    def wait_scatter(s, n_groups):
        def body(g, carry):
            pltpu.make_async_copy(obuf.at[s, 0], out_hbm.at[pl.ds(0, DMA_GROUP)], ssem.at[s]).wait()
            return carry
        lax.fori_loop(0, n_groups, body, 0)

    @pl.when(t == 0)
    def _():
        xbuf[...] = jnp.zeros(xbuf.shape, F32)
        obuf[0, 0] = jnp.zeros(obuf.shape[2:], F32)
        n_rows = out_hbm.shape[0]
        for half in range(2):
            fill = pltpu.make_async_copy(obuf.at[0, 0], out_hbm.at[pl.ds(n_rows - (2 - half) * DMA_GROUP, DMA_GROUP)],
                                         ssem.at[0])
            fill.start()
            fill.wait()
        gather(tok_ref, 0, ng_ref[0])

    @pl.when((t >= 2) & (t - 2 < n_used))
    def _():
        wait_scatter(slot, ng_ref[jnp.maximum(t - 2, 0)])

    @pl.when(t < n_used)
    def _():
        n_here = ng_ref[t]
        wait_gather(slot, n_here)

        @pl.when(t + 1 < n_used)
        def _():
            gather(tok_next_ref, 1 - slot, ng_ref[jnp.minimum(t + 1, n_t - 1)])

        group = lo_ref[t] // EXPERTS_PER_GROUP

        @pl.when((t == 0) | (group != lo_ref[jnp.maximum(t - 1, 0)] // EXPERTS_PER_GROUP))
        def _():
            def fetch(e, buf):
                return [pltpu.make_async_copy(src.at[group * EXPERTS_PER_GROUP + e], w_stage[m].at[buf], wsem.at[buf, m])
                        for m, src in enumerate((wg_hbm, wu_hbm, wd_hbm))]

            for cp in fetch(0, 0):
                cp.start()
            for e in range(EXPERTS_PER_GROUP):
                if e + 1 < EXPERTS_PER_GROUP:
                    for cp in fetch(e + 1, (e + 1) % 2):
                        cp.start()
                for m, cp in enumerate(fetch(e, e % 2)):
                    cp.wait()
                    w_res[m][e] = w_stage[m][e % 2].astype(BF16)

        xe = xbuf[slot].reshape(tm, MOE_ROW)
        x = xe[:, :D_MODEL].astype(BF16)
        route = xe[:, D_MODEL:]
        lane = lax.broadcasted_iota(jnp.int32, route.shape, 1)
        pick = lambda k: jnp.sum(jnp.where(lane == k, route, 0.0), axis=1, keepdims=True)
        i1, i2, w1, w2 = pick(0), pick(1), pick(2), pick(3)
        w_lo = jnp.where(i1 < i2, w1, w2)
        w_hi = jnp.where(i1 < i2, w2, w1)
        acc = None
        for e_ref, wt in ((lo_ref, w_lo), (hi_ref, w_hi)):
            e_local = lax.rem(e_ref[t], EXPERTS_PER_GROUP)
            hid = _silu(jnp.dot(x, w_res[0][e_local], preferred_element_type=F32)) * jnp.dot(
                x, w_res[1][e_local], preferred_element_type=F32)
            part = wt * _bdot(hid, w_res[2][e_local])
            acc = part if acc is None else acc + part
        obuf[slot] = acc.reshape(obuf.shape[1:])

        def body(g, carry):
            for u in range(DMA_GROUP):
                pltpu.make_async_copy(obuf.at[slot, g, pl.ds(u, 1)],
                                      out_hbm.at[pl.ds(dst_ref[0, 0, g * DMA_GROUP + u], 1)], ssem.at[slot]).start()
            return carry
        lax.fori_loop(0, n_here, body, 0)

    @pl.when(t == n_t - 1)
    def _():
        @pl.when((t >= 1) & (t - 1 < n_used))
        def _():
            wait_scatter(1 - slot, ng_ref[jnp.maximum(t - 1, 0)])

        @pl.when(t < n_used)
        def _():
            wait_scatter(slot, ng_ref[t])


def _moe_grouped(h2_ext, expert_ids, wg, wu, wd):
    return _moe_call(h2_ext, *_moe_plan(expert_ids, h2_ext.shape[0]), wg, wu, wd)


def _moe_call(h2_ext, tile_lo, tile_hi, n_groups, n_used, tok, dst, wg, wu, wd):
    n_tok = h2_ext.shape[0]
    tm = MOE_TILE
    n_tiles = tile_lo.shape[0]
    smem_tile = lambda f: pl.BlockSpec((1, 1, tm), f, memory_space=pltpu.SMEM)
    last_used = lambda t, n_used_ref: jnp.minimum(t, jnp.maximum(n_used_ref[0] - 1, 0))
    up, down = (D_MODEL, EXPERT_FF), (EXPERT_FF, D_MODEL)
    any_spec = pl.BlockSpec(memory_space=pl.ANY)
    grid_spec = pltpu.PrefetchScalarGridSpec(
        num_scalar_prefetch=4,
        grid=(n_tiles,),
        in_specs=[smem_tile(lambda t, *s: (last_used(t, s[-1]), 0, 0)),
                  smem_tile(lambda t, *s: (last_used(t + 1, s[-1]), 0, 0)),
                  smem_tile(lambda t, *s: (last_used(t, s[-1]), 0, 0)),
                  any_spec, any_spec, any_spec, any_spec],
        out_specs=pl.BlockSpec(memory_space=pl.ANY),
        scratch_shapes=[pltpu.VMEM((2, tm // DMA_GROUP, DMA_GROUP, MOE_ROW), F32),
                        pltpu.VMEM((2, tm // DMA_GROUP, DMA_GROUP, D_MODEL), F32),
                        [pltpu.VMEM((EXPERTS_PER_GROUP,) + shape, BF16) for shape in (up, up, down)],
                        [pltpu.VMEM((2,) + shape, F32) for shape in (up, up, down)],
                        pltpu.SemaphoreType.DMA((2,)), pltpu.SemaphoreType.DMA((2,)),
                        pltpu.SemaphoreType.DMA((2, 3))],
    )
    return pl.pallas_call(
        _moe_pair_kernel,
        out_shape=jax.ShapeDtypeStruct((n_tok + 2 * DMA_GROUP, D_MODEL), F32),
        grid_spec=grid_spec,
        compiler_params=pltpu.CompilerParams(dimension_semantics=("arbitrary",),
                                             vmem_limit_bytes=VMEM_LIMIT),
        name="moe_pairs",
    )(tile_lo, tile_hi, n_groups, n_used, tok, tok, dst, h2_ext, wg, wu, wd)


def _final_kernel(x1_ref, moe_ref, mod_ref, fg_ref, y_ref):
    x2 = x1_ref[0] + mod_ref[0][5] * moe_ref[...]
    y_ref[0] = _rms_scale(x2) * fg_ref[...]


def _final(x1, moe_out, mod, final_g, *, row_off, tm):
    b, t, _ = x1.shape
    blk0 = row_off // tm
    per_seq = t // tm
    return pl.pallas_call(
        _final_kernel,
        out_shape=jax.ShapeDtypeStruct((b, t, D_MODEL), F32),
        grid=(b, per_seq),
        in_specs=[pl.BlockSpec((1, tm, D_MODEL), lambda i, j: (i, j, 0)),
                  pl.BlockSpec((tm, D_MODEL), lambda i, j: (blk0 + i * per_seq + j, 0)),
                  _mod_spec(mod, tm, lambda i, j: (i, j)),
                  pl.BlockSpec((1, D_MODEL), lambda i, j: (0, 0))],
        out_specs=pl.BlockSpec((1, tm, D_MODEL), lambda i, j: (i, j, 0)),
        compiler_params=pltpu.CompilerParams(dimension_semantics=("parallel", "parallel"),
                                             vmem_limit_bytes=VMEM_LIMIT),
        name="moe_combine_norm",
    )(x1, moe_out, mod, final_g.reshape(1, -1))


def _mixer(x, mod, shift0, s_rwkv0, s_gla0, wts, *, n_tok, row_off, shared=None):
    b, t, _ = x.shape
    per_token = mod.shape[2] > 1
    rows = x.reshape(1, b * t, D_MODEL) if per_token else x
    tm = min(rows.shape[1], ROW_TILE)
    tm_wide = min(rows.shape[1], WIDE_ROW_TILE)
    p_rwkv = _norm_proj(rows, mod, wts["norm1_g"], wts["w_in_rwkv"], gate=False, tm=tm_wide)
    p_gla = _norm_proj(rows, mod, wts["norm1_g"], wts["w_in_gla"], gate=False, tm=tm_wide)
    gates = _norm_proj(rows, mod, wts["norm1_g"], wts["w_in_gate"], gate=True, tm=tm_wide)
    p_rwkv = p_rwkv.reshape(b, t, RWKV_PROJ)

    y_r, s_rwkv = _rwkv(p_rwkv, shift0, s_rwkv0,
                        wts["rwkv_mu"], wts["rwkv_w0"], wts["rwkv_decay_up"], wts["rwkv_a0"],
                        wts["rwkv_aaa_up"], wts["rwkv_gate_up"], wts["rwkv_k_k"], wts["rwkv_k_a"],
                        wts["rwkv_r_k"], wts["rwkv_ln_g"], wts["rwkv_ln_b"], nb=RWKV_SEQS_PER_STEP)
    y_g, s_gla = _gla(p_gla.reshape(b, t, GLA_PROJ_PAD), s_gla0, wts["gla_alpha_up"],
                      wts["gla_alpha_b"], wts["gla_norm_g"], nb=GLA_SEQS_PER_STEP)

    x1, h2_all, route_all = _merge(
        y_r.reshape(rows.shape), y_g.reshape(rows.shape), gates, rows, mod, wts["w_branch_rwkv"],
        wts["w_branch_gla"], wts["w_out"], wts["norm2_g"], wts["router_cat"],
        tm=tm, n_tok=n_tok, row_off=row_off, shared=shared)
    states = (p_rwkv[:, t - 1:t, :][None], s_rwkv[None], s_gla[None])
    return x1, h2_all, route_all, states


def kernel(x_prompt, x_sample, c_prompt, c_sample, state_rwkv_shift, state_rwkv, state_gla, w_ada, b_ada, norm1_g, w_in, rwkv_mu, rwkv_w0, rwkv_decay_up, rwkv_a0, rwkv_aaa_up, rwkv_gate_up, rwkv_k_k, rwkv_k_a, rwkv_r_k, rwkv_ln_g, rwkv_ln_b, gla_alpha_up, gla_alpha_b, gla_norm_g, w_branch_rwkv, w_branch_gla, w_out, norm2_g, router_group, router_expert, expert_w_gate, expert_w_up, expert_w_down, final_norm_g):
    bp, tp = x_prompt.shape[:2]
    bs, ts = x_sample.shape[:2]
    w_in0 = w_in[0]
    g0 = RWKV_PROJ
    o3 = 2 * GLA_DK_TOTAL + GLA_DV_TOTAL
    o4 = o3 + GLA_GATE_RANK
    w_gla = w_in0[:, g0:g0 + GLA_PROJ]
    w_gla = jnp.concatenate([w_gla[:, :o3], w_gla[:, o4:], w_gla[:, o3:o4],
                             jnp.zeros((D_MODEL, GLA_PROJ_PAD - GLA_PROJ), F32)], axis=1)
    wts = dict(
        norm1_g=norm1_g[0].reshape(1, -1),
        w_in_rwkv=w_in0[:, :g0].astype(BF16),
        w_in_gla=w_gla.astype(BF16),
        w_in_gate=w_in0[:, g0 + GLA_PROJ:].astype(BF16),
        rwkv_mu=rwkv_mu[0], rwkv_w0=rwkv_w0[0], rwkv_decay_up=rwkv_decay_up[0], rwkv_a0=rwkv_a0[0],
        rwkv_aaa_up=rwkv_aaa_up[0], rwkv_gate_up=rwkv_gate_up[0], rwkv_k_k=rwkv_k_k[0],
        rwkv_k_a=rwkv_k_a[0], rwkv_r_k=rwkv_r_k[0], rwkv_ln_g=rwkv_ln_g[0], rwkv_ln_b=rwkv_ln_b[0],
        gla_alpha_up=jnp.pad(gla_alpha_up[0], ((0, LANES - GLA_GATE_RANK), (0, 0))),
        gla_alpha_b=gla_alpha_b[0], gla_norm_g=gla_norm_g[0],
        w_branch_rwkv=w_branch_rwkv[0].astype(BF16), w_branch_gla=w_branch_gla[0].astype(BF16),
        w_out=w_out[0].astype(BF16), norm2_g=norm2_g[0],
        router_cat=_split_hi_lo(jnp.pad(jnp.concatenate([router_expert[0], router_group[0]], axis=1),
                                        ((0, 0), (0, LANES - N_EXPERTS - N_GROUPS)))),
        expert_w_gate=expert_w_gate[0], expert_w_up=expert_w_up[0], expert_w_down=expert_w_down[0],
    )
    mod = _ada(jnp.concatenate([c_prompt, c_sample], axis=0), w_ada[0], b_ada[0])
    mod = mod.reshape(bp + bs, N_MOD, D_MODEL)
    n_p = bp * tp
    n_s = bs * ts
    n_tok = n_p + n_s
    mod_p = mod[:bp].reshape(bp, N_MOD, 1, D_MODEL)
    mod_s = jnp.repeat(mod[bp:].transpose(1, 0, 2), ts, axis=1)[None]

    dt = x_prompt.dtype
    x1_p, h2_all, route_all, states_p = _mixer(
        x_prompt, mod_p, jnp.zeros((bp, 1, RWKV_PROJ), dt),
        jnp.zeros((bp, RWKV_HEADS, RWKV_HEAD_DIM, RWKV_HEAD_DIM), state_rwkv.dtype),
        jnp.zeros((bp, GLA_HEADS, GLA_DK, GLA_DV), state_gla.dtype), wts, n_tok=n_tok, row_off=0)
    x1_s, h2_all, route_all, states_s = _mixer(
        x_sample, mod_s, state_rwkv_shift[0], state_rwkv[0], state_gla[0], wts,
        n_tok=n_tok, row_off=n_p, shared=(h2_all, route_all))

    expert_ids = route_all[:, :2].astype(jnp.int32)
    moe_out = _moe_grouped(h2_all, expert_ids, wts["expert_w_gate"], wts["expert_w_up"],
                           wts["expert_w_down"])
    y_p = _final(x1_p, moe_out, mod_p, final_norm_g, row_off=0, tm=min(tp, WIDE_ROW_TILE))
    y_s = _final(x1_s, moe_out, mod_s, final_norm_g, row_off=n_p, tm=min(n_s, WIDE_ROW_TILE))
    return (y_p, y_s.reshape(bs, ts, D_MODEL)) + states_p + states_s
```
